```python
import math
import jax, jax.numpy as jnp
from jax import lax
import numpy as np

D_MODEL = 1024
BATCH = 1
SEQ = 16384
DEPTH = 1

CHUNK = 64
Q_BLOCK = 128
HEAD_DIM = 64
D_MIX = D_MODEL
SSM_WIDTH = D_MIX // 2
SSM_HEADS = SSM_WIDTH // HEAD_DIM
SSM_GROUPS = 2
SSM_STATE = 128
CONV_WIDTH = 4
SSM_XBC = SSM_WIDTH + 2 * SSM_GROUPS * SSM_STATE
FOX_WIDTH = D_MIX - SSM_WIDTH
FOX_HEADS = FOX_WIDTH // HEAD_DIM
D_FF = ((8 * D_MODEL // 3 + 127) // 128) * 128
IN_COLS = SSM_WIDTH + SSM_XBC + SSM_HEADS + 3 * FOX_WIDTH + FOX_HEADS
EPS = 1e-6

kernel_name = "hybrid_ssd_fox_macaron"


def rmsnorm(x, g):
    xf = x.astype(jnp.float32)
    y = xf * lax.rsqrt(jnp.mean(xf * xf, axis=-1, keepdims=True) + EPS)
    return (y * g.astype(jnp.float32)).astype(x.dtype)


def swiglu(h, w_in, w_out):
    gate, up = jnp.split(h @ w_in, 2, axis=-1)
    return (jax.nn.silu(gate) * up) @ w_out


def causal_depthwise_conv(u, w, b):
    s = u.shape[1]
    up = jnp.pad(u, ((0, 0), (CONV_WIDTH - 1, 0), (0, 0)))
    out = b
    for k in range(CONV_WIDTH):
        out = out + up[:, k:k + s, :] * w[k]
    return out


def ssd_scan(xh, dt, a, bg, cg):
    b, s, h, p = xh.shape
    g, n = bg.shape[2], bg.shape[3]
    nc = s // CHUNK
    r = h // g
    xdt = (xh * dt[..., None]).reshape(b, nc, CHUNK, h, p)
    da = (dt * a).reshape(b, nc, CHUNK, h)
    a_cs = jnp.cumsum(da, axis=2)
    bc = jnp.repeat(bg.reshape(b, nc, CHUNK, g, n), r, axis=3)
    cc = jnp.repeat(cg.reshape(b, nc, CHUNK, g, n), r, axis=3)
    idx = jnp.arange(CHUNK)
    causal = idx[:, None] >= idx[None, :]
    seg = a_cs[:, :, :, None, :] - a_cs[:, :, None, :, :]
    decay = jnp.exp(jnp.where(causal[None, None, :, :, None], seg, -jnp.inf))
    scores = jnp.einsum("bclhn,bcshn->bclsh", cc, bc) * decay
    y_diag = jnp.einsum("bclsh,bcshp->bclhp", scores, xdt)
    decay_end = jnp.exp(a_cs[:, :, -1:, :] - a_cs)
    states = jnp.einsum("bclhn,bclh,bclhp->bchpn", bc, decay_end, xdt)
    chunk_decay = jnp.exp(a_cs[:, :, -1, :])

    def step(carry, inp):
        st, dec = inp
        return carry * dec[:, :, None, None] + st, carry

    init = jnp.zeros((b, h, p, n), states.dtype)
    _, prev = lax.scan(step, init, (jnp.moveaxis(states, 1, 0), jnp.moveaxis(chunk_decay, 1, 0)))
    prev = jnp.moveaxis(prev, 0, 1)
    y_off = jnp.einsum("bclhn,bchpn,bclh->bclhp", cc, prev, jnp.exp(a_cs))
    return (y_diag + y_off).reshape(b, s, h, p)


def ssd_mixer(z, xbc, dt_raw, conv_w, conv_b, dt_bias, a_log, d_skip, norm_g):
    b, s, _ = z.shape
    xbc = jax.nn.silu(causal_depthwise_conv(xbc, conv_w, conv_b))
    xs, bs, cs = jnp.split(xbc, [SSM_WIDTH, SSM_WIDTH + SSM_GROUPS * SSM_STATE], axis=-1)
    xh = xs.reshape(b, s, SSM_HEADS, HEAD_DIM)
    bg = bs.reshape(b, s, SSM_GROUPS, SSM_STATE)
    cg = cs.reshape(b, s, SSM_GROUPS, SSM_STATE)
    dt = jax.nn.softplus(dt_raw.astype(jnp.float32) + dt_bias.astype(jnp.float32))
    a = -jnp.exp(a_log.astype(jnp.float32))
    y = ssd_scan(xh, dt, a, bg, cg) + d_skip[:, None] * xh
    y = y.reshape(b, s, SSM_WIDTH) * jax.nn.silu(z)
    return rmsnorm(y, norm_g).astype(z.dtype)


def fox_mixer(qkv, f_raw, f_bias, norm_g):
    b, s, _ = qkv.shape
    q, k, v = jnp.split(qkv, 3, axis=-1)
    q = q.reshape(b, s, FOX_HEADS, HEAD_DIM).transpose(0, 2, 1, 3) * (HEAD_DIM ** -0.5)
    k = k.reshape(b, s, FOX_HEADS, HEAD_DIM).transpose(0, 2, 1, 3)
    v = v.reshape(b, s, FOX_HEADS, HEAD_DIM).transpose(0, 2, 1, 3)
    log_f = jax.nn.log_sigmoid(f_raw.astype(jnp.float32) + f_bias.astype(jnp.float32))
    cum = jnp.cumsum(log_f, axis=1).transpose(0, 2, 1)
    nb = s // Q_BLOCK
    qb = q.reshape(b, FOX_HEADS, nb, Q_BLOCK, HEAD_DIM).transpose(2, 0, 1, 3, 4)
    cqb = cum.reshape(b, FOX_HEADS, nb, Q_BLOCK).transpose(2, 0, 1, 3)
    pos = jnp.arange(s)
    pqb = pos.reshape(nb, Q_BLOCK)

    def attend(blk):
        qi, ci, pi = blk
        logits = jnp.einsum("bhqd,bhkd->bhqk", qi, k).astype(jnp.float32) + ci[..., None] - cum[:, :, None, :]
        logits = jnp.where(pos[None, :] <= pi[:, None], logits, -jnp.inf)
        pr = jax.nn.softmax(logits, axis=-1).astype(v.dtype)
        return jnp.einsum("bhqk,bhkd->bhqd", pr, v)

    o = lax.map(attend, (qb, cqb, pqb))
    o = o.transpose(1, 0, 3, 2, 4).reshape(b, s, FOX_WIDTH)
    return rmsnorm(o, norm_g).astype(qkv.dtype)


def setup_inputs(seed: int = 0) -> dict:
    key = jax.random.key(seed)
    ks = jax.random.split(key, 20)
    f32 = jnp.float32

    def nrm(k, shape, scale):
        return jax.random.normal(k, shape, f32) * scale

    def gain(k, shape):
        return 1.0 + 0.02 * jax.random.normal(k, shape, f32)

    dt0 = jnp.exp(jax.random.uniform(ks[7], (DEPTH, SSM_HEADS), f32, math.log(1e-3), math.log(1e-1)))
    return {
        "x": jax.random.normal(ks[0], (BATCH, SEQ, D_MODEL), f32),
        "ffn1_norm": gain(ks[1], (DEPTH, D_MODEL)),
        "ffn1_w_in": nrm(ks[2], (DEPTH, D_MODEL, 2 * D_FF), D_MODEL ** -0.5),
        "ffn1_w_out": nrm(ks[3], (DEPTH, D_FF, D_MODEL), D_FF ** -0.5),
        "mix_norm": gain(ks[4], (DEPTH, D_MODEL)),
        "w_in": nrm(ks[5], (DEPTH, D_MODEL, IN_COLS), D_MODEL ** -0.5),
        "conv_w": nrm(ks[6], (DEPTH, CONV_WIDTH, SSM_XBC), CONV_WIDTH ** -0.5),
        "conv_b": nrm(ks[8], (DEPTH, SSM_XBC), 0.01),
        "dt_bias": dt0 + jnp.log(-jnp.expm1(-dt0)),
        "a_log": jnp.log(jax.random.uniform(ks[9], (DEPTH, SSM_HEADS), f32, 1.0, 16.0)),
        "d_skip": gain(ks[10], (DEPTH, SSM_HEADS)),
        "ssm_norm": gain(ks[11], (DEPTH, SSM_WIDTH)),
        "f_bias": 3.0 + 0.5 * jax.random.normal(ks[12], (DEPTH, FOX_HEADS), f32),
        "fox_norm": gain(ks[13], (DEPTH, FOX_WIDTH)),
        "w_out": nrm(ks[14], (DEPTH, D_MIX, D_MODEL), D_MIX ** -0.5),
        "ffn2_norm": gain(ks[15], (DEPTH, D_MODEL)),
        "ffn2_w_in": nrm(ks[16], (DEPTH, D_MODEL, 2 * D_FF), D_MODEL ** -0.5),
        "ffn2_w_out": nrm(ks[17], (DEPTH, D_FF, D_MODEL), D_FF ** -0.5),
        "final_norm": gain(ks[18], (D_MODEL,)),
    }


def reference(x, ffn1_norm, ffn1_w_in, ffn1_w_out, mix_norm, w_in, conv_w, conv_b, dt_bias, a_log,
              d_skip, ssm_norm, f_bias, fox_norm, w_out, ffn2_norm, ffn2_w_in, ffn2_w_out, final_norm):
    splits = np.cumsum([SSM_WIDTH, SSM_XBC, SSM_HEADS, 3 * FOX_WIDTH]).tolist()
    for l in range(DEPTH):
        x = x + 0.5 * swiglu(rmsnorm(x, ffn1_norm[l]), ffn1_w_in[l], ffn1_w_out[l])
        h = rmsnorm(x, mix_norm[l])
        z, xbc, dt_raw, qkv, f_raw = jnp.split(h @ w_in[l], splits, axis=-1)
        y_ssd = ssd_mixer(z, xbc, dt_raw, conv_w[l], conv_b[l], dt_bias[l], a_log[l], d_skip[l], ssm_norm[l])
        y_fox = fox_mixer(qkv, f_raw, f_bias[l], fox_norm[l])
        x = x + jnp.concatenate([y_ssd, y_fox], axis=-1) @ w_out[l]
        x = x + 0.5 * swiglu(rmsnorm(x, ffn2_norm[l]), ffn2_w_in[l], ffn2_w_out[l])
    return rmsnorm(x, final_norm)
```

```python
import functools

import jax
import jax.numpy as jnp
import numpy as np
from jax import lax
from jax.experimental import pallas as pl
from jax.experimental.pallas import tpu as pltpu

F32 = jnp.float32
BF16 = jnp.bfloat16
HIGHEST = lax.Precision.HIGHEST

D_MODEL = 1024
SEQ = 16384
HEAD_DIM = 64
SSM_WIDTH = 512
SSM_HEADS = 8
SSM_GROUPS = 2
SSM_STATE = 128
CONV_WIDTH = 4
SSM_XBC = SSM_WIDTH + 2 * SSM_GROUPS * SSM_STATE
FOX_WIDTH = 512
FOX_HEADS = 8
D_FF = 2816
EPS = 1e-6

LANES = 128
VMEM_LIMIT = 56 * 1024 * 1024

FFN_TM = 1024
FFN_TF = 256
PROJ_TM = 512
SSD_T = 256
ATT_TQ = 512
ATT_TK = 512
NEG_INF = float("-inf")


def _rms(x, g):
    ms = jnp.mean(x * x, axis=-1, keepdims=True)
    return x * lax.rsqrt(ms + EPS) * g


def _silu(x):
    return x * (1.0 / (1.0 + jnp.exp(-x)))


def _softplus(x):
    return jnp.maximum(x, 0.0) + jnp.log1p(jnp.exp(-jnp.abs(x)))


def _ffn_kernel(x_ref, g_ref, wgu_ref, wo_ref, fg_ref, o_ref, h_ref, *, nf, final):
    f = pl.program_id(1)

    @pl.when(f == 0)
    def _():
        h_ref[...] = _rms(x_ref[...], g_ref[...]).astype(BF16)
        o_ref[...] = jnp.zeros_like(o_ref)

    gu = jnp.dot(h_ref[...], wgu_ref[0], preferred_element_type=F32)
    gate = gu[:, :FFN_TF]
    up = gu[:, FFN_TF:]
    act = (_silu(gate) * up).astype(BF16)
    o_ref[...] += jnp.dot(act, wo_ref[...], preferred_element_type=F32)

    @pl.when(f == nf - 1)
    def _():
        r = x_ref[...] + 0.5 * o_ref[...]
        if final:
            r = _rms(r, fg_ref[...])
        o_ref[...] = r


def _ffn(x, g, wgu, wo, fg, *, final):
    s = x.shape[0]
    nf = D_FF // FFN_TF
    return pl.pallas_call(
        functools.partial(_ffn_kernel, nf=nf, final=final),
        grid=(s // FFN_TM, nf),
        in_specs=[
            pl.BlockSpec((FFN_TM, D_MODEL), lambda i, f: (i, 0)),
            pl.BlockSpec((1, D_MODEL), lambda i, f: (0, 0)),
            pl.BlockSpec((1, D_MODEL, 2 * FFN_TF), lambda i, f: (f, 0, 0)),
            pl.BlockSpec((FFN_TF, D_MODEL), lambda i, f: (f, 0)),
            pl.BlockSpec((1, D_MODEL), lambda i, f: (0, 0)),
        ],
        out_specs=pl.BlockSpec((FFN_TM, D_MODEL), lambda i, f: (i, 0)),
        out_shape=jax.ShapeDtypeStruct((s, D_MODEL), F32),
        scratch_shapes=[pltpu.VMEM((FFN_TM, D_MODEL), BF16)],
        compiler_params=pltpu.CompilerParams(
            dimension_semantics=("parallel", "arbitrary"),
            vmem_limit_bytes=VMEM_LIMIT),
        name="ffn_final" if final else "ffn",
    )(x, g, wgu, wo, fg)


def _inproj_kernel(x_ref, g_ref, wz_ref, wxbc_ref, wq_ref, wkt_ref, wv_ref, ws_ref,
                   z_ref, xbc_ref, q_ref, kt_ref, v_ref, small_ref):
    h = _rms(x_ref[...], g_ref[...]).astype(BF16)
    z_ref[...] = jnp.dot(h, wz_ref[...], preferred_element_type=F32)
    xbc_ref[...] = jnp.dot(h, wxbc_ref[...], preferred_element_type=F32)
    q = jnp.dot(h, wq_ref[...], preferred_element_type=F32)
    q_ref[...] = (q * (HEAD_DIM ** -0.5)).astype(BF16)
    kt = lax.dot_general(wkt_ref[...], h, (((1,), (1,)), ((), ())),
                         preferred_element_type=F32)
    kt_ref[0] = kt.astype(BF16)
    v_ref[...] = jnp.dot(h, wv_ref[...], preferred_element_type=F32).astype(BF16)
    small_ref[...] = jnp.dot(h, ws_ref[...], preferred_element_type=F32)


def _inproj(x, g, wz, wxbc, wq, wkt, wv, ws):
    s = x.shape[0]
    tm = PROJ_TM
    nb = s // tm
    const = lambda i: (0, 0)
    row = lambda i: (i, 0)
    return pl.pallas_call(
        _inproj_kernel,
        grid=(nb,),
        in_specs=[
            pl.BlockSpec((tm, D_MODEL), row),
            pl.BlockSpec((1, D_MODEL), const),
            pl.BlockSpec((D_MODEL, SSM_WIDTH), const),
            pl.BlockSpec((D_MODEL, SSM_XBC), const),
            pl.BlockSpec((D_MODEL, FOX_WIDTH), const),
            pl.BlockSpec((FOX_WIDTH, D_MODEL), const),
            pl.BlockSpec((D_MODEL, FOX_WIDTH), const),
            pl.BlockSpec((D_MODEL, LANES), const),
        ],
        out_specs=[
            pl.BlockSpec((tm, SSM_WIDTH), row),
            pl.BlockSpec((tm, SSM_XBC), row),
            pl.BlockSpec((tm, FOX_WIDTH), row),
            pl.BlockSpec((1, FOX_WIDTH, tm), lambda i: (i, 0, 0)),
            pl.BlockSpec((tm, FOX_WIDTH), row),
            pl.BlockSpec((tm, LANES), row),
        ],
        out_shape=[
            jax.ShapeDtypeStruct((s, SSM_WIDTH), F32),
            jax.ShapeDtypeStruct((s, SSM_XBC), F32),
            jax.ShapeDtypeStruct((s, FOX_WIDTH), BF16),
            jax.ShapeDtypeStruct((nb, FOX_WIDTH, tm), BF16),
            jax.ShapeDtypeStruct((s, FOX_WIDTH), BF16),
            jax.ShapeDtypeStruct((s, LANES), F32),
        ],
        compiler_params=pltpu.CompilerParams(
            dimension_semantics=("parallel",),
            vmem_limit_bytes=VMEM_LIMIT),
        name="in_proj",
    )(x, g, wz, wxbc, wq, wkt, wv, ws)


def _ssd_kernel(z_ref, xbc_ref, small_ref, cw_ref, cb_ref, bias_ref, apad_ref,
                dskip_ref, ng_ref, tri_ref, expand_ref,
                y_ref, cum_ref, cumt_ref,
                ext_ref, state_ref, carry_ref):
    t_rows = SSD_T
    i = pl.program_id(0)

    @pl.when(i == 0)
    def _():
        ext_ref[0:8, :] = jnp.zeros((8, SSM_XBC), F32)
        state_ref[...] = jnp.zeros_like(state_ref)
        carry_ref[...] = jnp.zeros_like(carry_ref)

    ext_ref[8:8 + t_rows, :] = xbc_ref[...]
    conv = cb_ref[...] + ext_ref[5:5 + t_rows, :] * cw_ref[0:1, :]
    for k in range(1, CONV_WIDTH):
        conv = conv + ext_ref[5 + k:5 + k + t_rows, :] * cw_ref[k:k + 1, :]
    ext_ref[0:8, :] = ext_ref[t_rows:t_rows + 8, :]
    u = _silu(conv)
    xs = u[:, :SSM_WIDTH]
    bm = u[:, SSM_WIDTH:SSM_WIDTH + SSM_GROUPS * SSM_STATE]
    cm = u[:, SSM_WIDTH + SSM_GROUPS * SSM_STATE:]

    t = small_ref[...] + bias_ref[...]
    lane = lax.broadcasted_iota(jnp.int32, (t_rows, LANES), 1)
    dt = _softplus(t)
    logf = -_softplus(-t)
    is_dt = lane < SSM_HEADS
    v = jnp.where(is_dt, dt * apad_ref[...], logf)
    cs = jnp.dot(tri_ref[...], v, precision=HIGHEST, preferred_element_type=F32)
    cs = cs + carry_ref[...]
    last = cs[t_rows - 1:t_rows, :]
    carry_ref[...] = jnp.where(lane[0:1, :] < SSM_HEADS, 0.0, last)
    cum_ref[...] = cs
    cst = cs.T
    cumt_ref[...] = cst[0:16, :]

    ea = jnp.exp(cs)
    de = jnp.exp(last - cs)
    stacked = jnp.concatenate([dt, ea, de], axis=0)
    exp3 = jnp.dot(stacked, expand_ref[...], precision=HIGHEST,
                   preferred_element_type=F32)
    dt_e = exp3[0:t_rows]
    ea_e = exp3[t_rows:2 * t_rows]
    de_e = exp3[2 * t_rows:3 * t_rows]
    cd_e = ea_e[t_rows - 1:t_rows, :]

    xdt = xs * dt_e
    xdt_b = xdt.astype(BF16)
    wst = (xdt * de_e).astype(BF16)

    r_i = lax.broadcasted_iota(jnp.int32, (t_rows, t_rows), 0)
    c_i = lax.broadcasted_iota(jnp.int32, (t_rows, t_rows), 1)
    causal = r_i >= c_i
    lane_pair = lax.broadcasted_iota(jnp.int32, (t_rows, LANES), 1)
    lo = lane_pair < HEAD_DIM

    gw = SSM_WIDTH // SSM_GROUPS
    hpg = SSM_HEADS // SSM_GROUPS
    y_parts = []
    for g in range(SSM_GROUPS):
        cg = cm[:, g * SSM_STATE:(g + 1) * SSM_STATE].astype(BF16)
        bg_f = bm[:, g * SSM_STATE:(g + 1) * SSM_STATE]
        bg = bg_f.astype(BF16)
        gmat = lax.dot_general(cg, bg, (((1,), (1,)), ((), ())),
                               preferred_element_type=F32)
        for pr in range(hpg // 2):
            c0 = g * gw + pr * LANES
            xpair = xdt_b[:, c0:c0 + LANES]
            acc = None
            for hh in range(2):
                h = g * hpg + pr * 2 + hh
                seg = cs[:, h:h + 1] - cst[h:h + 1, :]
                dec = jnp.exp(jnp.where(causal, seg, NEG_INF))
                m = (gmat * dec).astype(BF16)
                xm = jnp.where(lo if hh == 0 else jnp.logical_not(lo), xpair,
                               jnp.zeros_like(xpair))
                part = jnp.dot(m, xm, preferred_element_type=F32)
                acc = part if acc is None else acc + part
            y_parts.append(acc)
        s_prev = state_ref[g]
        y_off = jnp.dot(cg, s_prev.astype(BF16), preferred_element_type=F32)
        y_parts[-2] = y_parts[-2] + y_off[:, :LANES] * ea_e[:, g * gw:g * gw + LANES]
        y_parts[-1] = y_parts[-1] + y_off[:, LANES:] * ea_e[:, g * gw + LANES:(g + 1) * gw]
        bgt = bg_f.T.astype(BF16)
        upd = jnp.dot(bgt, wst[:, g * gw:(g + 1) * gw], preferred_element_type=F32)
        state_ref[g] = s_prev * cd_e[:, g * gw:(g + 1) * gw] + upd

    y = jnp.concatenate(y_parts, axis=1) + dskip_ref[...] * xs
    y = y * _silu(z_ref[...])
    y_ref[...] = _rms(y, ng_ref[...]).astype(BF16)


def _ssd(z, xbc, small, cw, cb, bias_pad, a_pad, dskip_e, ng, tri, expand):
    s = z.shape[0]
    t = SSD_T
    const = lambda i: (0, 0)
    row = lambda i: (i, 0)
    return pl.pallas_call(
        _ssd_kernel,
        grid=(s // t,),
        in_specs=[
            pl.BlockSpec((t, SSM_WIDTH), row),
            pl.BlockSpec((t, SSM_XBC), row),
            pl.BlockSpec((t, LANES), row),
            pl.BlockSpec((CONV_WIDTH, SSM_XBC), const),
            pl.BlockSpec((1, SSM_XBC), const),
            pl.BlockSpec((1, LANES), const),
            pl.BlockSpec((1, LANES), const),
            pl.BlockSpec((1, SSM_WIDTH), const),
            pl.BlockSpec((1, SSM_WIDTH), const),
            pl.BlockSpec((t, t), const),
            pl.BlockSpec((LANES, SSM_WIDTH), const),
        ],
        out_specs=[
            pl.BlockSpec((t, SSM_WIDTH), row),
            pl.BlockSpec((t, LANES), row),
            pl.BlockSpec((16, t), lambda i: (0, i)),
        ],
        out_shape=[
            jax.ShapeDtypeStruct((s, SSM_WIDTH), BF16),
            jax.ShapeDtypeStruct((s, LANES), F32),
            jax.ShapeDtypeStruct((16, s), F32),
        ],
        scratch_shapes=[
            pltpu.VMEM((t + 8, SSM_XBC), F32),
            pltpu.VMEM((SSM_GROUPS, SSM_STATE, SSM_WIDTH // SSM_GROUPS), F32),
            pltpu.VMEM((1, LANES), F32),
        ],
        compiler_params=pltpu.CompilerParams(
            dimension_semantics=("arbitrary",),
            vmem_limit_bytes=VMEM_LIMIT),
        name="ssd",
    )(z, xbc, small, cw, cb, bias_pad, a_pad, dskip_e, ng, tri, expand)


def _fox_kernel(q_ref, kt_ref, v_ref, cum_ref, cumt_ref, o_ref,
                m_ref, l_ref, acc_ref):
    tq, tk = ATT_TQ, ATT_TK
    p = pl.program_id(0)
    i = pl.program_id(1)
    q = q_ref[...]
    lane = lax.broadcasted_iota(jnp.int32, (tq, LANES), 1)
    lo = lane < HEAD_DIM
    cum_blk = cum_ref[...]
    r_i = lax.broadcasted_iota(jnp.int32, (tq, tk), 0)
    c_i = lax.broadcasted_iota(jnp.int32, (tq, tk), 1)
    causal = r_i >= c_i

    outs = []
    for hh in range(2):
        h = 2 * p + hh
        qm = jnp.where(lo if hh == 0 else jnp.logical_not(lo), q, jnp.zeros_like(q))
        cq = jnp.sum(jnp.where(lane == FOX_HEADS + h, cum_blk, 0.0), axis=1,
                     keepdims=True)

        def logits(j):
            s = jnp.dot(qm, kt_ref[j], preferred_element_type=F32)
            ck = cumt_ref[pl.ds(FOX_HEADS + h, 1), pl.ds(pl.multiple_of(j * tk, tk), tk)]
            return s + (cq - ck)

        s = jnp.where(causal, logits(i), NEG_INF)
        m0 = jnp.max(s, axis=1, keepdims=True)
        p0 = jnp.exp(s - m0)
        m_ref[hh] = jnp.broadcast_to(m0, (tq, LANES))
        l_ref[hh] = jnp.broadcast_to(jnp.sum(p0, axis=1, keepdims=True), (tq, LANES))
        vblk = v_ref[pl.ds(pl.multiple_of(i * tk, tk), tk), :]
        acc_ref[hh] = jnp.dot(p0.astype(BF16), vblk, preferred_element_type=F32)

        def body(jj, carry):
            j = i - 1 - jj
            s = logits(j)
            m_prev = m_ref[hh]
            m_new = jnp.maximum(m_prev, jnp.max(s, axis=1, keepdims=True))
            alpha = jnp.exp(m_prev - m_new)
            pj = jnp.exp(s - m_new[:, 0:1])
            l_ref[hh] = alpha * l_ref[hh] + jnp.sum(pj, axis=1, keepdims=True)
            vj = v_ref[pl.ds(pl.multiple_of(j * tk, tk), tk), :]
            acc_ref[hh] = alpha * acc_ref[hh] + jnp.dot(
                pj.astype(BF16), vj, preferred_element_type=F32)
            m_ref[hh] = m_new
            return carry

        lax.fori_loop(0, i, body, 0)
        outs.append(acc_ref[hh] / l_ref[hh])

    o_ref[...] = jnp.where(lo, outs[0], outs[1])


def _fox(q, kt3, v, cum, cumt):
    s = q.shape[0]
    tq, tk = ATT_TQ, ATT_TK
    nb = s // tk
    return pl.pallas_call(
        _fox_kernel,
        grid=(FOX_HEADS // 2, s // tq),
        in_specs=[
            pl.BlockSpec((tq, LANES), lambda p, i: (i, p)),
            pl.BlockSpec((nb, LANES, tk), lambda p, i: (0, p, 0)),
            pl.BlockSpec((s, LANES), lambda p, i: (0, p)),
            pl.BlockSpec((tq, LANES), lambda p, i: (i, 0)),
            pl.BlockSpec((16, s), lambda p, i: (0, 0)),
        ],
        out_specs=pl.BlockSpec((tq, LANES), lambda p, i: (i, p)),
        out_shape=jax.ShapeDtypeStruct((s, FOX_WIDTH), F32),
        scratch_shapes=[
            pltpu.VMEM((2, tq, LANES), F32),
            pltpu.VMEM((2, tq, LANES), F32),
            pltpu.VMEM((2, tq, LANES), F32),
        ],
        compiler_params=pltpu.CompilerParams(
            dimension_semantics=("arbitrary", "arbitrary"),
            vmem_limit_bytes=VMEM_LIMIT),
        name="fox_attn",
    )(q, kt3, v, cum, cumt)


def _outproj_kernel(x_ref, ys_ref, of_ref, fg_ref, w1_ref, w2_ref, o_ref):
    yf = _rms(of_ref[...], fg_ref[...]).astype(BF16)
    o_ref[...] = (x_ref[...]
                  + jnp.dot(ys_ref[...], w1_ref[...], preferred_element_type=F32)
                  + jnp.dot(yf, w2_ref[...], preferred_element_type=F32))


def _outproj(x, ys, of, fg, w1, w2):
    s = x.shape[0]
    tm = PROJ_TM
    const = lambda i: (0, 0)
    row = lambda i: (i, 0)
    return pl.pallas_call(
        _outproj_kernel,
        grid=(s // tm,),
        in_specs=[
            pl.BlockSpec((tm, D_MODEL), row),
            pl.BlockSpec((tm, SSM_WIDTH), row),
            pl.BlockSpec((tm, FOX_WIDTH), row),
            pl.BlockSpec((1, FOX_WIDTH), const),
            pl.BlockSpec((SSM_WIDTH, D_MODEL), const),
            pl.BlockSpec((FOX_WIDTH, D_MODEL), const),
        ],
        out_specs=pl.BlockSpec((tm, D_MODEL), row),
        out_shape=jax.ShapeDtypeStruct((s, D_MODEL), F32),
        compiler_params=pltpu.CompilerParams(
            dimension_semantics=("parallel",),
            vmem_limit_bytes=VMEM_LIMIT),
        name="out_proj",
    )(x, ys, of, fg, w1, w2)


def _prep_ffn_w(w_in, w_out):
    nf = D_FF // FFN_TF
    gate = w_in[:, :D_FF].reshape(D_MODEL, nf, FFN_TF)
    up = w_in[:, D_FF:].reshape(D_MODEL, nf, FFN_TF)
    wgu = jnp.concatenate([gate, up], axis=2).transpose(1, 0, 2).astype(BF16)
    return wgu, w_out.astype(BF16)


def _pad_lanes(vec_dt, vec_f):
    out = jnp.zeros((1, LANES), F32)
    out = out.at[0, 0:SSM_HEADS].set(vec_dt.astype(F32))
    if vec_f is not None:
        out = out.at[0, SSM_HEADS:SSM_HEADS + FOX_HEADS].set(vec_f.astype(F32))
    return out


def _layer(x, ffn1_norm, ffn1_w_in, ffn1_w_out, mix_norm, w_in, conv_w, conv_b, dt_bias,
           a_log, d_skip, ssm_norm, f_bias, fox_norm, w_out, ffn2_norm, ffn2_w_in,
           ffn2_w_out, final_g, *, final):
    ones = jnp.ones((1, D_MODEL), F32)
    row = lambda a: a.reshape(1, -1).astype(F32)

    wgu1, wo1 = _prep_ffn_w(ffn1_w_in, ffn1_w_out)
    x1 = _ffn(x, row(ffn1_norm), wgu1, wo1, ones, final=False)

    o0 = SSM_WIDTH
    o1 = o0 + SSM_XBC
    o2 = o1 + SSM_HEADS
    o3 = o2 + 3 * FOX_WIDTH
    wz = w_in[:, :o0].astype(BF16)
    wxbc = w_in[:, o0:o1].astype(BF16)
    w_dt = w_in[:, o1:o2]
    wq = w_in[:, o2:o2 + FOX_WIDTH].astype(BF16)
    wkt = w_in[:, o2 + FOX_WIDTH:o2 + 2 * FOX_WIDTH].T.astype(BF16)
    wv = w_in[:, o2 + 2 * FOX_WIDTH:o3].astype(BF16)
    w_f = w_in[:, o3:]
    ws = jnp.concatenate(
        [w_dt, w_f, jnp.zeros((D_MODEL, LANES - SSM_HEADS - FOX_HEADS), w_in.dtype)],
        axis=1).astype(BF16)
    z, xbc, q, kt3, v, small = _inproj(x1, row(mix_norm), wz, wxbc, wq, wkt, wv, ws)

    bias_pad = _pad_lanes(dt_bias, f_bias)
    a_pad = _pad_lanes(-jnp.exp(a_log.astype(F32)), None)
    dskip_e = jnp.repeat(d_skip.astype(F32), HEAD_DIM).reshape(1, SSM_WIDTH)
    tri = jnp.asarray(np.tril(np.ones((SSD_T, SSD_T), np.float32)))
    expand_np = np.zeros((LANES, SSM_WIDTH), np.float32)
    for hd in range(SSM_HEADS):
        expand_np[hd, hd * HEAD_DIM:(hd + 1) * HEAD_DIM] = 1.0
    y_ssd, cum, cumt = _ssd(z, xbc, small, conv_w.astype(F32), row(conv_b), bias_pad, a_pad,
                            dskip_e, row(ssm_norm), tri, jnp.asarray(expand_np))

    o_fox = _fox(q, kt3, v, cum, cumt)

    wo = w_out.astype(BF16)
    x2 = _outproj(x1, y_ssd, o_fox, row(fox_norm), wo[:SSM_WIDTH], wo[SSM_WIDTH:])

    wgu2, wo2 = _prep_ffn_w(ffn2_w_in, ffn2_w_out)
    return _ffn(x2, row(ffn2_norm), wgu2, wo2, row(final_g), final=final)


def kernel(x, ffn1_norm, ffn1_w_in, ffn1_w_out, mix_norm, w_in, conv_w, conv_b, dt_bias, a_log,
           d_skip, ssm_norm, f_bias, fox_norm, w_out, ffn2_norm, ffn2_w_in, ffn2_w_out, final_norm):
    b, s, d = x.shape
    depth = ffn1_norm.shape[0]
    outs = []
    for bi in range(b):
        xb = x[bi]
        for l in range(depth):
            xb = _layer(xb, ffn1_norm[l], ffn1_w_in[l], ffn1_w_out[l], mix_norm[l], w_in[l],
                        conv_w[l], conv_b[l], dt_bias[l], a_log[l], d_skip[l], ssm_norm[l],
                        f_bias[l], fox_norm[l], w_out[l], ffn2_norm[l], ffn2_w_in[l],
                        ffn2_w_out[l], final_norm, final=(l == depth - 1))
        outs.append(xb)
    return jnp.stack(outs, axis=0)
```

```python
import functools

import jax
import jax.numpy as jnp
import numpy as np
from jax import lax
from jax.experimental import pallas as pl
from jax.experimental.pallas import tpu as pltpu

F32 = jnp.float32
BF16 = jnp.bfloat16
HIGHEST = lax.Precision.HIGHEST

D_MODEL = 1024
SEQ = 16384
HEAD_DIM = 64
SSM_WIDTH = 512
SSM_HEADS = 8
SSM_GROUPS = 2
SSM_STATE = 128
CONV_WIDTH = 4
SSM_XBC = SSM_WIDTH + 2 * SSM_GROUPS * SSM_STATE
FOX_WIDTH = 512
FOX_HEADS = 8
D_FF = 2816
EPS = 1e-6

LANES = 128
VMEM_LIMIT = 56 * 1024 * 1024

FFN_TM = 1024
FFN_TF = 256
PROJ_TM = 512
SSD_T = 256
ATT_TQ = 512
ATT_TK = 512
NEG_INF = float("-inf")
LOG2E = 1.4426950408889634
SKIP_LOG2 = -150.0
NORM_MARGIN = 1.05


def _rms(x, g):
    ms = jnp.mean(x * x, axis=-1, keepdims=True)
    return x * lax.rsqrt(ms + EPS) * g


def _silu(x):
    return x * (1.0 / (1.0 + jnp.exp(-x)))


def _softplus(x):
    return jnp.maximum(x, 0.0) + jnp.log1p(jnp.exp(-jnp.abs(x)))


def _ffn_kernel(x_ref, g_ref, wgu_ref, wo_ref, fg_ref, o_ref, h_ref, *, nf, final):
    f = pl.program_id(1)

    @pl.when(f == 0)
    def _():
        h_ref[...] = _rms(x_ref[...], g_ref[...]).astype(BF16)
        o_ref[...] = jnp.zeros_like(o_ref)

    gu = jnp.dot(h_ref[...], wgu_ref[0], preferred_element_type=F32)
    gate = gu[:, :FFN_TF]
    up = gu[:, FFN_TF:]
    act = (_silu(gate) * up).astype(BF16)
    o_ref[...] += jnp.dot(act, wo_ref[...], preferred_element_type=F32)

    @pl.when(f == nf - 1)
    def _():
        r = x_ref[...] + 0.5 * o_ref[...]
        if final:
            r = _rms(r, fg_ref[...])
        o_ref[...] = r


def _ffn(x, g, wgu, wo, fg, *, final):
    s = x.shape[0]
    nf = D_FF // FFN_TF
    return pl.pallas_call(
        functools.partial(_ffn_kernel, nf=nf, final=final),
        grid=(s // FFN_TM, nf),
        in_specs=[
            pl.BlockSpec((FFN_TM, D_MODEL), lambda i, f: (i, 0)),
            pl.BlockSpec((1, D_MODEL), lambda i, f: (0, 0)),
            pl.BlockSpec((1, D_MODEL, 2 * FFN_TF), lambda i, f: (f, 0, 0)),
            pl.BlockSpec((FFN_TF, D_MODEL), lambda i, f: (f, 0)),
            pl.BlockSpec((1, D_MODEL), lambda i, f: (0, 0)),
        ],
        out_specs=pl.BlockSpec((FFN_TM, D_MODEL), lambda i, f: (i, 0)),
        out_shape=jax.ShapeDtypeStruct((s, D_MODEL), F32),
        scratch_shapes=[pltpu.VMEM((FFN_TM, D_MODEL), BF16)],
        compiler_params=pltpu.CompilerParams(
            dimension_semantics=("parallel", "arbitrary"),
            vmem_limit_bytes=VMEM_LIMIT),
        name="ffn_final" if final else "ffn",
    )(x, g, wgu, wo, fg)


def _inproj_kernel(x_ref, g_ref, wz_ref, wxbc_ref, wq_ref, wkt_ref, wv_ref, ws_ref,
                   ind_ref, indt_ref,
                   z_ref, xbc_ref, q_ref, kt_ref, v_ref, small_ref, qn_ref, kn_ref):
    h = _rms(x_ref[...], g_ref[...]).astype(BF16)
    z_ref[...] = jnp.dot(h, wz_ref[...], preferred_element_type=F32)
    xbc_ref[...] = jnp.dot(h, wxbc_ref[...], preferred_element_type=F32)
    q = jnp.dot(h, wq_ref[...], preferred_element_type=F32)
    qb = (q * (HEAD_DIM ** -0.5 * LOG2E)).astype(BF16)
    q_ref[...] = qb
    kt = lax.dot_general(wkt_ref[...], h, (((1,), (1,)), ((), ())),
                         preferred_element_type=F32)
    ktb = kt.astype(BF16)
    kt_ref[0] = ktb
    v_ref[...] = jnp.dot(h, wv_ref[...], preferred_element_type=F32).astype(BF16)
    small_ref[...] = jnp.dot(h, ws_ref[...], preferred_element_type=F32)
    qf = qb.astype(F32)
    qn_ref[...] = jnp.dot((qf * qf).astype(BF16), ind_ref[...], preferred_element_type=F32)
    kf = ktb.astype(F32)
    kn_ref[...] = jnp.dot(indt_ref[...], (kf * kf).astype(BF16), preferred_element_type=F32)


def _inproj(x, g, wz, wxbc, wq, wkt, wv, ws):
    s = x.shape[0]
    tm = PROJ_TM
    nb = s // tm
    const = lambda i: (0, 0)
    row = lambda i: (i, 0)
    ind_np = np.zeros((FOX_WIDTH, LANES), np.float32)
    for hd in range(FOX_HEADS):
        ind_np[hd * HEAD_DIM:(hd + 1) * HEAD_DIM, hd] = 1.0
    ind = jnp.asarray(ind_np, dtype=BF16)
    indt = jnp.asarray(ind_np.T[:16], dtype=BF16)
    return pl.pallas_call(
        _inproj_kernel,
        grid=(nb,),
        in_specs=[
            pl.BlockSpec((tm, D_MODEL), row),
            pl.BlockSpec((1, D_MODEL), const),
            pl.BlockSpec((D_MODEL, SSM_WIDTH), const),
            pl.BlockSpec((D_MODEL, SSM_XBC), const),
            pl.BlockSpec((D_MODEL, FOX_WIDTH), const),
            pl.BlockSpec((FOX_WIDTH, D_MODEL), const),
            pl.BlockSpec((D_MODEL, FOX_WIDTH), const),
            pl.BlockSpec((D_MODEL, LANES), const),
            pl.BlockSpec((FOX_WIDTH, LANES), const),
            pl.BlockSpec((16, FOX_WIDTH), const),
        ],
        out_specs=[
            pl.BlockSpec((tm, SSM_WIDTH), row),
            pl.BlockSpec((tm, SSM_XBC), row),
            pl.BlockSpec((tm, FOX_WIDTH), row),
            pl.BlockSpec((1, FOX_WIDTH, tm), lambda i: (i, 0, 0)),
            pl.BlockSpec((tm, FOX_WIDTH), row),
            pl.BlockSpec((tm, LANES), row),
            pl.BlockSpec((tm, LANES), row),
            pl.BlockSpec((16, tm), lambda i: (0, i)),
        ],
        out_shape=[
            jax.ShapeDtypeStruct((s, SSM_WIDTH), F32),
            jax.ShapeDtypeStruct((s, SSM_XBC), F32),
            jax.ShapeDtypeStruct((s, FOX_WIDTH), BF16),
            jax.ShapeDtypeStruct((nb, FOX_WIDTH, tm), BF16),
            jax.ShapeDtypeStruct((s, FOX_WIDTH), BF16),
            jax.ShapeDtypeStruct((s, LANES), F32),
            jax.ShapeDtypeStruct((s, LANES), F32),
            jax.ShapeDtypeStruct((16, s), F32),
        ],
        compiler_params=pltpu.CompilerParams(
            dimension_semantics=("parallel",),
            vmem_limit_bytes=VMEM_LIMIT),
        name="in_proj",
    )(x, g, wz, wxbc, wq, wkt, wv, ws, ind, indt)


def _ssd_kernel(z_ref, xbc_ref, small_ref, cw_ref, cb_ref, bias_ref, apad_ref,
                dskip_ref, ng_ref, tri_ref, expand_ref,
                y_ref, cum_ref, cumt_ref,
                ext_ref, state_ref, carry_ref):
    t_rows = SSD_T
    i = pl.program_id(0)

    @pl.when(i == 0)
    def _():
        ext_ref[0:8, :] = jnp.zeros((8, SSM_XBC), F32)
        state_ref[...] = jnp.zeros_like(state_ref)
        carry_ref[...] = jnp.zeros_like(carry_ref)

    ext_ref[8:8 + t_rows, :] = xbc_ref[...]
    conv = cb_ref[...] + ext_ref[5:5 + t_rows, :] * cw_ref[0:1, :]
    for k in range(1, CONV_WIDTH):
        conv = conv + ext_ref[5 + k:5 + k + t_rows, :] * cw_ref[k:k + 1, :]
    ext_ref[0:8, :] = ext_ref[t_rows:t_rows + 8, :]
    u = _silu(conv)
    xs = u[:, :SSM_WIDTH]
    bm = u[:, SSM_WIDTH:SSM_WIDTH + SSM_GROUPS * SSM_STATE]
    cm = u[:, SSM_WIDTH + SSM_GROUPS * SSM_STATE:]

    t = small_ref[...] + bias_ref[...]
    lane = lax.broadcasted_iota(jnp.int32, (t_rows, LANES), 1)
    dt = _softplus(t)
    logf = -_softplus(-t)
    is_dt = lane < SSM_HEADS
    v = jnp.where(is_dt, dt * apad_ref[...], logf)
    cs = jnp.dot(tri_ref[...], v, precision=HIGHEST, preferred_element_type=F32)
    cs = cs + carry_ref[...]
    last = cs[t_rows - 1:t_rows, :]
    carry_ref[...] = jnp.where(lane[0:1, :] < SSM_HEADS, 0.0, last)
    cum_ref[...] = cs
    cst = cs.T
    cumt_ref[...] = cst[0:16, :]

    ea = jnp.exp(cs)
    de = jnp.exp(last - cs)
    stacked = jnp.concatenate([dt, ea, de], axis=0)
    exp3 = jnp.dot(stacked, expand_ref[...], precision=HIGHEST,
                   preferred_element_type=F32)
    dt_e = exp3[0:t_rows]
    ea_e = exp3[t_rows:2 * t_rows]
    de_e = exp3[2 * t_rows:3 * t_rows]
    cd_e = ea_e[t_rows - 1:t_rows, :]

    xdt = xs * dt_e
    xdt_b = xdt.astype(BF16)
    wst = (xdt * de_e).astype(BF16)

    r_i = lax.broadcasted_iota(jnp.int32, (t_rows, t_rows), 0)
    c_i = lax.broadcasted_iota(jnp.int32, (t_rows, t_rows), 1)
    causal = r_i >= c_i
    lane_pair = lax.broadcasted_iota(jnp.int32, (t_rows, LANES), 1)
    lo = lane_pair < HEAD_DIM

    gw = SSM_WIDTH // SSM_GROUPS
    hpg = SSM_HEADS // SSM_GROUPS
    y_parts = []
    for g in range(SSM_GROUPS):
        cg = cm[:, g * SSM_STATE:(g + 1) * SSM_STATE].astype(BF16)
        bg_f = bm[:, g * SSM_STATE:(g + 1) * SSM_STATE]
        bg = bg_f.astype(BF16)
        gmat = lax.dot_general(cg, bg, (((1,), (1,)), ((), ())),
                               preferred_element_type=F32)
        for pr in range(hpg // 2):
            c0 = g * gw + pr * LANES
            xpair = xdt_b[:, c0:c0 + LANES]
            acc = None
            for hh in range(2):
                h = g * hpg + pr * 2 + hh
                seg = cs[:, h:h + 1] - cst[h:h + 1, :]
                dec = jnp.exp(jnp.where(causal, seg, NEG_INF))
                m = (gmat * dec).astype(BF16)
                xm = jnp.where(lo if hh == 0 else jnp.logical_not(lo), xpair,
                               jnp.zeros_like(xpair))
                part = jnp.dot(m, xm, preferred_element_type=F32)
                acc = part if acc is None else acc + part
            y_parts.append(acc)
        s_prev = state_ref[g]
        y_off = jnp.dot(cg, s_prev.astype(BF16), preferred_element_type=F32)
        y_parts[-2] = y_parts[-2] + y_off[:, :LANES] * ea_e[:, g * gw:g * gw + LANES]
        y_parts[-1] = y_parts[-1] + y_off[:, LANES:] * ea_e[:, g * gw + LANES:(g + 1) * gw]
        bgt = bg_f.T.astype(BF16)
        upd = jnp.dot(bgt, wst[:, g * gw:(g + 1) * gw], preferred_element_type=F32)
        state_ref[g] = s_prev * cd_e[:, g * gw:(g + 1) * gw] + upd

    y = jnp.concatenate(y_parts, axis=1) + dskip_ref[...] * xs
    y = y * _silu(z_ref[...])
    y_ref[...] = _rms(y, ng_ref[...]).astype(BF16)


def _ssd(z, xbc, small, cw, cb, bias_pad, a_pad, dskip_e, ng, tri, expand):
    s = z.shape[0]
    t = SSD_T
    const = lambda i: (0, 0)
    row = lambda i: (i, 0)
    return pl.pallas_call(
        _ssd_kernel,
        grid=(s // t,),
        in_specs=[
            pl.BlockSpec((t, SSM_WIDTH), row),
            pl.BlockSpec((t, SSM_XBC), row),
            pl.BlockSpec((t, LANES), row),
            pl.BlockSpec((CONV_WIDTH, SSM_XBC), const),
            pl.BlockSpec((1, SSM_XBC), const),
            pl.BlockSpec((1, LANES), const),
            pl.BlockSpec((1, LANES), const),
            pl.BlockSpec((1, SSM_WIDTH), const),
            pl.BlockSpec((1, SSM_WIDTH), const),
            pl.BlockSpec((t, t), const),
            pl.BlockSpec((LANES, SSM_WIDTH), const),
        ],
        out_specs=[
            pl.BlockSpec((t, SSM_WIDTH), row),
            pl.BlockSpec((t, LANES), row),
            pl.BlockSpec((16, t), lambda i: (0, i)),
        ],
        out_shape=[
            jax.ShapeDtypeStruct((s, SSM_WIDTH), BF16),
            jax.ShapeDtypeStruct((s, LANES), F32),
            jax.ShapeDtypeStruct((16, s), F32),
        ],
        scratch_shapes=[
            pltpu.VMEM((t + 8, SSM_XBC), F32),
            pltpu.VMEM((SSM_GROUPS, SSM_STATE, SSM_WIDTH // SSM_GROUPS), F32),
            pltpu.VMEM((1, LANES), F32),
        ],
        compiler_params=pltpu.CompilerParams(
            dimension_semantics=("arbitrary",),
            vmem_limit_bytes=VMEM_LIMIT),
        name="ssd",
    )(z, xbc, small, cw, cb, bias_pad, a_pad, dskip_e, ng, tri, expand)


def _split3(x):
    hi = x.astype(BF16).astype(F32)
    r = x - hi
    mid = r.astype(BF16).astype(F32)
    lo = (r - mid).astype(BF16).astype(F32)
    return hi, mid, lo


def _fox_kernel(nkv_ref, q_ref, kt_ref, v_ref, cum_ref, cumt_ref, o_ref,
                kaug_ref, m_ref, l_ref, acc_ref):
    tq, tk = ATT_TQ, ATT_TK
    nb = kt_ref.shape[0]
    nq = pl.num_programs(1)
    p = pl.program_id(0)
    i = pl.program_id(1)

    @pl.when(i == 0)
    def _():
        row16 = lax.broadcasted_iota(jnp.int32, (16, tk), 0)
        zeros48 = jnp.zeros((48, tk), BF16)

        def build(j, carry):
            off = pl.multiple_of(j * tk, tk)
            kt = kt_ref[j]
            for hh in range(2):
                ck = cumt_ref[pl.ds(FOX_HEADS + 2 * p + hh, 1), pl.ds(off, tk)] * LOG2E
                hi, mid, lo = _split3(ck)
                bias = jnp.where(row16 < 3, 1.0,
                                 jnp.where(row16 == 3, -hi,
                                           jnp.where(row16 == 4, -mid,
                                                     jnp.where(row16 == 5, -lo, 0.0))))
                bias = bias.astype(BF16)
                if hh == 0:
                    kaug_ref[0, j, 0:64, :] = kt[0:64]
                    kaug_ref[0, j, 64:80, :] = bias
                    kaug_ref[0, j, 80:128, :] = zeros48
                else:
                    kaug_ref[1, j, 0:16, :] = bias
                    kaug_ref[1, j, 16:64, :] = zeros48
                    kaug_ref[1, j, 64:128, :] = kt[64:128]
            return carry

        lax.fori_loop(0, nb, build, 0)

    q = q_ref[...].astype(F32)
    lane = lax.broadcasted_iota(jnp.int32, (tq, LANES), 1)
    lo_half = lane < HEAD_DIM
    cum_blk = cum_ref[...]
    r_i = lax.broadcasted_iota(jnp.int32, (tq, tk), 0)
    c_i = lax.broadcasted_iota(jnp.int32, (tq, tk), 1)
    causal = r_i >= c_i

    qas = []
    for hh in range(2):
        h = 2 * p + hh
        cq = jnp.sum(jnp.where(lane == FOX_HEADS + h, cum_blk, 0.0), axis=1,
                     keepdims=True) * LOG2E
        hi, mid, lo = _split3(cq)
        base = HEAD_DIM if hh == 0 else 0
        aug = jnp.where(lane == base, hi,
                        jnp.where(lane == base + 1, mid,
                                  jnp.where(lane == base + 2, lo,
                                            jnp.where((lane >= base + 3) & (lane < base + 6),
                                                      1.0, 0.0))))
        qas.append(jnp.where(lo_half if hh == 0 else jnp.logical_not(lo_half), q, aug)
                   .astype(BF16))

    def logits(hh, j):
        return jnp.dot(qas[hh], kaug_ref[hh, j], preferred_element_type=F32)

    def vblock(j):
        return v_ref[pl.ds(pl.multiple_of(j * tk, tk), tk), :]

    def diag_step(hh):
        s = jnp.where(causal, logits(hh, i), NEG_INF)
        m0 = jnp.max(s, axis=1, keepdims=True)
        p0 = jnp.exp2(s - m0)
        m_ref[hh] = jnp.broadcast_to(m0, (tq, LANES))
        l_ref[hh] = jnp.broadcast_to(jnp.sum(p0, axis=1, keepdims=True), (tq, LANES))
        acc_ref[hh] = jnp.dot(p0.astype(BF16), vblock(i), preferred_element_type=F32)

    def step(hh, j, s=None):
        if s is None:
            s = logits(hh, j)
        m_prev = m_ref[hh]
        m_new = jnp.maximum(m_prev, jnp.max(s, axis=1, keepdims=True))
        alpha = jnp.exp2(m_prev - m_new)
        pj = jnp.exp2(s - m_new[:, 0:1])
        l_ref[hh] = alpha * l_ref[hh] + jnp.sum(pj, axis=1, keepdims=True)
        acc_ref[hh] = alpha * acc_ref[hh] + jnp.dot(
            pj.astype(BF16), vblock(j), preferred_element_type=F32)
        m_ref[hh] = m_new

    diag_step(0)
    diag_step(1)
    n0 = nkv_ref[(2 * p) * nq + i]
    n1 = nkv_ref[(2 * p + 1) * nq + i]
    n_joint = jnp.minimum(n0, n1)

    def joint_body(jj, carry):
        step(0, i - 1 - jj)
        step(1, i - 1 - jj)
        return carry

    lax.fori_loop(0, n_joint, joint_body, 0)
    for hh, n in ((0, n0), (1, n1)):
        def single_body(jj, carry, hh=hh):
            step(hh, i - 1 - jj)
            return carry

        lax.fori_loop(n_joint, n, single_body, 0)

    o_ref[...] = jnp.where(lo_half, acc_ref[0] / l_ref[0], acc_ref[1] / l_ref[1])


def _kv_counts(qn2, kn2, cum):
    tq = ATT_TQ
    nq = cum.shape[0] // tq
    qn = jnp.sqrt(jnp.max(qn2[:, :FOX_HEADS].reshape(nq, tq, FOX_HEADS), axis=1))
    kn = jnp.sqrt(jnp.max(kn2[:FOX_HEADS], axis=1))
    c = (cum[:, FOX_HEADS:2 * FOX_HEADS] * LOG2E).reshape(nq, tq, FOX_HEADS)
    cmax = jnp.max(c, axis=1)
    pmin = lax.cummin(jnp.min(c, axis=1), axis=0)
    top = 2.0 * NORM_MARGIN * qn * kn[None, :] + 1.0 + cmax
    bound = top[:, None, :] - pmin[None, :, :]
    ii = lax.broadcasted_iota(jnp.int32, bound.shape, 0)
    jj = lax.broadcasted_iota(jnp.int32, bound.shape, 1)
    skip = (bound < SKIP_LOG2) & (jj < ii)
    jstar = jnp.max(jnp.where(skip, jj, -1), axis=1)
    n_off = jnp.arange(nq, dtype=jnp.int32)[:, None] - 1 - jstar
    return n_off.T.reshape(-1).astype(jnp.int32)


def _fox(nkv, q, kt3, v, cum, cumt):
    s = q.shape[0]
    tq, tk = ATT_TQ, ATT_TK
    nb = s // tk
    grid_spec = pltpu.PrefetchScalarGridSpec(
        num_scalar_prefetch=1,
        grid=(FOX_HEADS // 2, s // tq),
        in_specs=[
            pl.BlockSpec((tq, LANES), lambda p, i, n: (i, p)),
            pl.BlockSpec((nb, LANES, tk), lambda p, i, n: (0, p, 0)),
            pl.BlockSpec((s, LANES), lambda p, i, n: (0, p)),
            pl.BlockSpec((tq, LANES), lambda p, i, n: (i, 0)),
            pl.BlockSpec((16, s), lambda p, i, n: (0, 0)),
        ],
        out_specs=pl.BlockSpec((tq, LANES), lambda p, i, n: (i, p)),
        scratch_shapes=[
            pltpu.VMEM((2, nb, LANES, tk), BF16),
            pltpu.VMEM((2, tq, LANES), F32),
            pltpu.VMEM((2, tq, LANES), F32),
            pltpu.VMEM((2, tq, LANES), F32),
        ],
    )
    return pl.pallas_call(
        _fox_kernel,
        grid_spec=grid_spec,
        out_shape=jax.ShapeDtypeStruct((s, FOX_WIDTH), F32),
        compiler_params=pltpu.CompilerParams(
            dimension_semantics=("arbitrary", "arbitrary"),
            vmem_limit_bytes=VMEM_LIMIT),
        name="fox_attn",
    )(nkv, q, kt3, v, cum, cumt)


def _outproj_kernel(x_ref, ys_ref, of_ref, fg_ref, w1_ref, w2_ref, o_ref):
    yf = _rms(of_ref[...], fg_ref[...]).astype(BF16)
    o_ref[...] = (x_ref[...]
                  + jnp.dot(ys_ref[...], w1_ref[...], preferred_element_type=F32)
                  + jnp.dot(yf, w2_ref[...], preferred_element_type=F32))


def _outproj(x, ys, of, fg, w1, w2):
    s = x.shape[0]
    tm = PROJ_TM
    const = lambda i: (0, 0)
    row = lambda i: (i, 0)
    return pl.pallas_call(
        _outproj_kernel,
        grid=(s // tm,),
        in_specs=[
            pl.BlockSpec((tm, D_MODEL), row),
            pl.BlockSpec((tm, SSM_WIDTH), row),
            pl.BlockSpec((tm, FOX_WIDTH), row),
            pl.BlockSpec((1, FOX_WIDTH), const),
            pl.BlockSpec((SSM_WIDTH, D_MODEL), const),
            pl.BlockSpec((FOX_WIDTH, D_MODEL), const),
        ],
        out_specs=pl.BlockSpec((tm, D_MODEL), row),
        out_shape=jax.ShapeDtypeStruct((s, D_MODEL), F32),
        compiler_params=pltpu.CompilerParams(
            dimension_semantics=("parallel",),
            vmem_limit_bytes=VMEM_LIMIT),
        name="out_proj",
    )(x, ys, of, fg, w1, w2)


def _prep_ffn_w(w_in, w_out):
    nf = D_FF // FFN_TF
    gate = w_in[:, :D_FF].reshape(D_MODEL, nf, FFN_TF)
    up = w_in[:, D_FF:].reshape(D_MODEL, nf, FFN_TF)
    wgu = jnp.concatenate([gate, up], axis=2).transpose(1, 0, 2).astype(BF16)
    return wgu, w_out.astype(BF16)


def _pad_lanes(vec_dt, vec_f):
    out = jnp.zeros((1, LANES), F32)
    out = out.at[0, 0:SSM_HEADS].set(vec_dt.astype(F32))
    if vec_f is not None:
        out = out.at[0, SSM_HEADS:SSM_HEADS + FOX_HEADS].set(vec_f.astype(F32))
    return out


def _layer(x, ffn1_norm, ffn1_w_in, ffn1_w_out, mix_norm, w_in, conv_w, conv_b, dt_bias,
           a_log, d_skip, ssm_norm, f_bias, fox_norm, w_out, ffn2_norm, ffn2_w_in,
           ffn2_w_out, final_g, *, final):
    ones = jnp.ones((1, D_MODEL), F32)
    row = lambda a: a.reshape(1, -1).astype(F32)

    wgu1, wo1 = _prep_ffn_w(ffn1_w_in, ffn1_w_out)
    x1 = _ffn(x, row(ffn1_norm), wgu1, wo1, ones, final=False)

    o0 = SSM_WIDTH
    o1 = o0 + SSM_XBC
    o2 = o1 + SSM_HEADS
    o3 = o2 + 3 * FOX_WIDTH
    wz = w_in[:, :o0].astype(BF16)
    wxbc = w_in[:, o0:o1].astype(BF16)
    w_dt = w_in[:, o1:o2]
    wq = w_in[:, o2:o2 + FOX_WIDTH].astype(BF16)
    wkt = w_in[:, o2 + FOX_WIDTH:o2 + 2 * FOX_WIDTH].T.astype(BF16)
    wv = w_in[:, o2 + 2 * FOX_WIDTH:o3].astype(BF16)
    w_f = w_in[:, o3:]
    ws = jnp.concatenate(
        [w_dt, w_f, jnp.zeros((D_MODEL, LANES - SSM_HEADS - FOX_HEADS), w_in.dtype)],
        axis=1).astype(BF16)
    z, xbc, q, kt3, v, small, qn2, kn2 = _inproj(x1, row(mix_norm), wz, wxbc, wq, wkt, wv, ws)

    bias_pad = _pad_lanes(dt_bias, f_bias)
    a_pad = _pad_lanes(-jnp.exp(a_log.astype(F32)), None)
    dskip_e = jnp.repeat(d_skip.astype(F32), HEAD_DIM).reshape(1, SSM_WIDTH)
    tri = jnp.asarray(np.tril(np.ones((SSD_T, SSD_T), np.float32)))
    expand_np = np.zeros((LANES, SSM_WIDTH), np.float32)
    for hd in range(SSM_HEADS):
        expand_np[hd, hd * HEAD_DIM:(hd + 1) * HEAD_DIM] = 1.0
    y_ssd, cum, cumt = _ssd(z, xbc, small, conv_w.astype(F32), row(conv_b), bias_pad, a_pad,
                            dskip_e, row(ssm_norm), tri, jnp.asarray(expand_np))

    o_fox = _fox(_kv_counts(qn2, kn2, cum), q, kt3, v, cum, cumt)

    wo = w_out.astype(BF16)
    x2 = _outproj(x1, y_ssd, o_fox, row(fox_norm), wo[:SSM_WIDTH], wo[SSM_WIDTH:])

    wgu2, wo2 = _prep_ffn_w(ffn2_w_in, ffn2_w_out)
    return _ffn(x2, row(ffn2_norm), wgu2, wo2, row(final_g), final=final)


def kernel(x, ffn1_norm, ffn1_w_in, ffn1_w_out, mix_norm, w_in, conv_w, conv_b, dt_bias, a_log,
           d_skip, ssm_norm, f_bias, fox_norm, w_out, ffn2_norm, ffn2_w_in, ffn2_w_out, final_norm):
    b, s, d = x.shape
    depth = ffn1_norm.shape[0]
    outs = []
    for bi in range(b):
        xb = x[bi]
        for l in range(depth):
            xb = _layer(xb, ffn1_norm[l], ffn1_w_in[l], ffn1_w_out[l], mix_norm[l], w_in[l],
                        conv_w[l], conv_b[l], dt_bias[l], a_log[l], d_skip[l], ssm_norm[l],
                        f_bias[l], fox_norm[l], w_out[l], ffn2_norm[l], ffn2_w_in[l],
                        ffn2_w_out[l], final_norm, final=(l == depth - 1))
        outs.append(xb)
    return jnp.stack(outs, axis=0)
```

```python
import functools

import jax
import jax.numpy as jnp
import numpy as np
from jax import lax
from jax.experimental import pallas as pl
from jax.experimental.pallas import tpu as pltpu

F32 = jnp.float32
BF16 = jnp.bfloat16
HIGHEST = lax.Precision.HIGHEST

D_MODEL = 1024
SEQ = 16384
HEAD_DIM = 64
SSM_WIDTH = 512
SSM_HEADS = 8
SSM_GROUPS = 2
SSM_STATE = 128
CONV_WIDTH = 4
SSM_XBC = SSM_WIDTH + 2 * SSM_GROUPS * SSM_STATE
FOX_WIDTH = 512
FOX_HEADS = 8
D_FF = 2816
EPS = 1e-6

LANES = 128
VMEM_LIMIT = 56 * 1024 * 1024

FFN_TM = 1024
FFN_TF = 256
PROJ_TM = 512
SSD_T = 256
ATT_TQ = 512
ATT_TK = 512
NEG_INF = float("-inf")
LOG2E = 1.4426950408889634
SKIP_LOG2 = -150.0
NORM_MARGIN = 1.05
FIXED_MAX_LOG2 = 64.0


def _rms(x, g):
    ms = jnp.mean(x * x, axis=-1, keepdims=True)
    return x * lax.rsqrt(ms + EPS) * g


def _silu(x):
    return x * (1.0 / (1.0 + jnp.exp(-x)))


def _softplus(x):
    return jnp.maximum(x, 0.0) + jnp.log1p(jnp.exp(-jnp.abs(x)))


def _ffn_kernel(x_ref, g_ref, wg_ref, wu_ref, wo_ref, fg_ref, o_ref, h_ref, *, nf, final):
    f = pl.program_id(1)

    @pl.when(f == 0)
    def _():
        h_ref[...] = _rms(x_ref[...], g_ref[...]).astype(BF16)
        o_ref[...] = jnp.zeros_like(o_ref)

    h = h_ref[...]
    gate = jnp.dot(h, wg_ref[...].astype(BF16), preferred_element_type=F32)
    up = jnp.dot(h, wu_ref[...].astype(BF16), preferred_element_type=F32)
    act = (_silu(gate) * up).astype(BF16)
    o_ref[...] += jnp.dot(act, wo_ref[...].astype(BF16), preferred_element_type=F32)

    @pl.when(f == nf - 1)
    def _():
        r = x_ref[...] + 0.5 * o_ref[...]
        if final:
            r = _rms(r, fg_ref[...])
        o_ref[...] = r


def _ffn(x, g, w_in, w_out, fg, *, final):
    s = x.shape[0]
    nf = D_FF // FFN_TF
    return pl.pallas_call(
        functools.partial(_ffn_kernel, nf=nf, final=final),
        grid=(s // FFN_TM, nf),
        in_specs=[
            pl.BlockSpec((FFN_TM, D_MODEL), lambda i, f: (i, 0)),
            pl.BlockSpec((1, D_MODEL), lambda i, f: (0, 0)),
            pl.BlockSpec((D_MODEL, FFN_TF), lambda i, f: (0, f)),
            pl.BlockSpec((D_MODEL, FFN_TF), lambda i, f: (0, nf + f)),
            pl.BlockSpec((FFN_TF, D_MODEL), lambda i, f: (f, 0)),
            pl.BlockSpec((1, D_MODEL), lambda i, f: (0, 0)),
        ],
        out_specs=pl.BlockSpec((FFN_TM, D_MODEL), lambda i, f: (i, 0)),
        out_shape=jax.ShapeDtypeStruct((s, D_MODEL), F32),
        scratch_shapes=[pltpu.VMEM((FFN_TM, D_MODEL), BF16)],
        compiler_params=pltpu.CompilerParams(
            dimension_semantics=("parallel", "arbitrary"),
            vmem_limit_bytes=VMEM_LIMIT),
        name="ffn_final" if final else "ffn",
    )(x, g, w_in, w_in, w_out, fg)


def _inproj_kernel(x_ref, g_ref, wz_ref, wxbc_ref, wq_ref, wkt_ref, wv_ref, ws_ref,
                   ind_ref, indt_ref,
                   z_ref, xbc_ref, q_ref, kt_ref, v_ref, small_ref, qn_ref, kn_ref):
    h = _rms(x_ref[...], g_ref[...]).astype(BF16)
    z_ref[...] = jnp.dot(h, wz_ref[...], preferred_element_type=F32)
    xbc_ref[...] = jnp.dot(h, wxbc_ref[...], preferred_element_type=F32)
    q = jnp.dot(h, wq_ref[...], preferred_element_type=F32)
    qb = (q * (HEAD_DIM ** -0.5 * LOG2E)).astype(BF16)
    q_ref[...] = qb
    kt = lax.dot_general(wkt_ref[...], h, (((1,), (1,)), ((), ())),
                         preferred_element_type=F32)
    ktb = kt.astype(BF16)
    kt_ref[0] = ktb
    v_ref[...] = jnp.dot(h, wv_ref[...], preferred_element_type=F32).astype(BF16)
    small_ref[...] = jnp.dot(h, ws_ref[...], preferred_element_type=F32)
    qf = qb.astype(F32)
    qn_ref[...] = jnp.dot((qf * qf).astype(BF16), ind_ref[...], preferred_element_type=F32)
    kf = ktb.astype(F32)
    kn_ref[...] = jnp.dot(indt_ref[...], (kf * kf).astype(BF16), preferred_element_type=F32)


def _inproj(x, g, wz, wxbc, wq, wkt, wv, ws):
    s = x.shape[0]
    tm = PROJ_TM
    nb = s // tm
    const = lambda i: (0, 0)
    row = lambda i: (i, 0)
    ind_np = np.zeros((FOX_WIDTH, LANES), np.float32)
    for hd in range(FOX_HEADS):
        ind_np[hd * HEAD_DIM:(hd + 1) * HEAD_DIM, hd] = 1.0
    ind = jnp.asarray(ind_np, dtype=BF16)
    indt = jnp.asarray(ind_np.T[:16], dtype=BF16)
    return pl.pallas_call(
        _inproj_kernel,
        grid=(nb,),
        in_specs=[
            pl.BlockSpec((tm, D_MODEL), row),
            pl.BlockSpec((1, D_MODEL), const),
            pl.BlockSpec((D_MODEL, SSM_WIDTH), const),
            pl.BlockSpec((D_MODEL, SSM_XBC), const),
            pl.BlockSpec((D_MODEL, FOX_WIDTH), const),
            pl.BlockSpec((FOX_WIDTH, D_MODEL), const),
            pl.BlockSpec((D_MODEL, FOX_WIDTH), const),
            pl.BlockSpec((D_MODEL, LANES), const),
            pl.BlockSpec((FOX_WIDTH, LANES), const),
            pl.BlockSpec((16, FOX_WIDTH), const),
        ],
        out_specs=[
            pl.BlockSpec((tm, SSM_WIDTH), row),
            pl.BlockSpec((tm, SSM_XBC), row),
            pl.BlockSpec((tm, FOX_WIDTH), row),
            pl.BlockSpec((1, FOX_WIDTH, tm), lambda i: (i, 0, 0)),
            pl.BlockSpec((tm, FOX_WIDTH), row),
            pl.BlockSpec((tm, LANES), row),
            pl.BlockSpec((tm, LANES), row),
            pl.BlockSpec((16, tm), lambda i: (0, i)),
        ],
        out_shape=[
            jax.ShapeDtypeStruct((s, SSM_WIDTH), F32),
            jax.ShapeDtypeStruct((s, SSM_XBC), F32),
            jax.ShapeDtypeStruct((s, FOX_WIDTH), BF16),
            jax.ShapeDtypeStruct((nb, FOX_WIDTH, tm), BF16),
            jax.ShapeDtypeStruct((s, FOX_WIDTH), BF16),
            jax.ShapeDtypeStruct((s, LANES), F32),
            jax.ShapeDtypeStruct((s, LANES), F32),
            jax.ShapeDtypeStruct((16, s), F32),
        ],
        compiler_params=pltpu.CompilerParams(
            dimension_semantics=("parallel",),
            vmem_limit_bytes=VMEM_LIMIT),
        name="in_proj",
    )(x, g, wz, wxbc, wq, wkt, wv, ws, ind, indt)


def _ssd_kernel(z_ref, xbc_ref, small_ref, cw_ref, cb_ref, bias_ref, apad_ref,
                dskip_ref, ng_ref, tri_ref, expand_ref,
                y_ref, cum_ref, cumt_ref,
                ext_ref, state_ref, carry_ref):
    t_rows = SSD_T
    i = pl.program_id(0)

    @pl.when(i == 0)
    def _():
        ext_ref[0:8, :] = jnp.zeros((8, SSM_XBC), F32)
        state_ref[...] = jnp.zeros_like(state_ref)
        carry_ref[...] = jnp.zeros_like(carry_ref)

    ext_ref[8:8 + t_rows, :] = xbc_ref[...]
    conv = cb_ref[...] + ext_ref[5:5 + t_rows, :] * cw_ref[0:1, :]
    for k in range(1, CONV_WIDTH):
        conv = conv + ext_ref[5 + k:5 + k + t_rows, :] * cw_ref[k:k + 1, :]
    ext_ref[0:8, :] = ext_ref[t_rows:t_rows + 8, :]
    u = _silu(conv)
    xs = u[:, :SSM_WIDTH]
    bm = u[:, SSM_WIDTH:SSM_WIDTH + SSM_GROUPS * SSM_STATE]
    cm = u[:, SSM_WIDTH + SSM_GROUPS * SSM_STATE:]

    t = small_ref[...] + bias_ref[...]
    lane = lax.broadcasted_iota(jnp.int32, (t_rows, LANES), 1)
    dt = _softplus(t)
    logf = -_softplus(-t)
    is_dt = lane < SSM_HEADS
    v = jnp.where(is_dt, dt * apad_ref[...], logf)
    cs = jnp.dot(tri_ref[...], v, precision=HIGHEST, preferred_element_type=F32)
    cs = cs + carry_ref[...]
    last = cs[t_rows - 1:t_rows, :]
    carry_ref[...] = jnp.where(lane[0:1, :] < SSM_HEADS, 0.0, last)
    cum_ref[...] = cs
    cst = cs.T
    cumt_ref[...] = cst[0:16, :]

    ea = jnp.exp(cs)
    de = jnp.exp(last - cs)
    stacked = jnp.concatenate([dt, ea, de], axis=0)
    exp3 = jnp.dot(stacked, expand_ref[...], precision=HIGHEST,
                   preferred_element_type=F32)
    dt_e = exp3[0:t_rows]
    ea_e = exp3[t_rows:2 * t_rows]
    de_e = exp3[2 * t_rows:3 * t_rows]
    cd_e = ea_e[t_rows - 1:t_rows, :]

    xdt = xs * dt_e
    xdt_b = xdt.astype(BF16)
    wst = (xdt * de_e).astype(BF16)

    r_i = lax.broadcasted_iota(jnp.int32, (t_rows, t_rows), 0)
    c_i = lax.broadcasted_iota(jnp.int32, (t_rows, t_rows), 1)
    causal = r_i >= c_i
    lane_pair = lax.broadcasted_iota(jnp.int32, (t_rows, LANES), 1)
    lo = lane_pair < HEAD_DIM

    gw = SSM_WIDTH // SSM_GROUPS
    hpg = SSM_HEADS // SSM_GROUPS
    y_parts = []
    for g in range(SSM_GROUPS):
        cg = cm[:, g * SSM_STATE:(g + 1) * SSM_STATE].astype(BF16)
        bg_f = bm[:, g * SSM_STATE:(g + 1) * SSM_STATE]
        bg = bg_f.astype(BF16)
        gmat = lax.dot_general(cg, bg, (((1,), (1,)), ((), ())),
                               preferred_element_type=F32)
        for pr in range(hpg // 2):
            c0 = g * gw + pr * LANES
            xpair = xdt_b[:, c0:c0 + LANES]
            acc = None
            for hh in range(2):
                h = g * hpg + pr * 2 + hh
                seg = cs[:, h:h + 1] - cst[h:h + 1, :]
                dec = jnp.exp(jnp.where(causal, seg, NEG_INF))
                m = (gmat * dec).astype(BF16)
                xm = jnp.where(lo if hh == 0 else jnp.logical_not(lo), xpair,
                               jnp.zeros_like(xpair))
                part = jnp.dot(m, xm, preferred_element_type=F32)
                acc = part if acc is None else acc + part
            y_parts.append(acc)
        s_prev = state_ref[g]
        y_off = jnp.dot(cg, s_prev.astype(BF16), preferred_element_type=F32)
        y_parts[-2] = y_parts[-2] + y_off[:, :LANES] * ea_e[:, g * gw:g * gw + LANES]
        y_parts[-1] = y_parts[-1] + y_off[:, LANES:] * ea_e[:, g * gw + LANES:(g + 1) * gw]
        bgt = bg_f.T.astype(BF16)
        upd = jnp.dot(bgt, wst[:, g * gw:(g + 1) * gw], preferred_element_type=F32)
        state_ref[g] = s_prev * cd_e[:, g * gw:(g + 1) * gw] + upd

    y = jnp.concatenate(y_parts, axis=1) + dskip_ref[...] * xs
    y = y * _silu(z_ref[...])
    y_ref[...] = _rms(y, ng_ref[...]).astype(BF16)


def _ssd(z, xbc, small, cw, cb, bias_pad, a_pad, dskip_e, ng, tri, expand):
    s = z.shape[0]
    t = SSD_T
    const = lambda i: (0, 0)
    row = lambda i: (i, 0)
    return pl.pallas_call(
        _ssd_kernel,
        grid=(s // t,),
        in_specs=[
            pl.BlockSpec((t, SSM_WIDTH), row),
            pl.BlockSpec((t, SSM_XBC), row),
            pl.BlockSpec((t, LANES), row),
            pl.BlockSpec((CONV_WIDTH, SSM_XBC), const),
            pl.BlockSpec((1, SSM_XBC), const),
            pl.BlockSpec((1, LANES), const),
            pl.BlockSpec((1, LANES), const),
            pl.BlockSpec((1, SSM_WIDTH), const),
            pl.BlockSpec((1, SSM_WIDTH), const),
            pl.BlockSpec((t, t), const),
            pl.BlockSpec((LANES, SSM_WIDTH), const),
        ],
        out_specs=[
            pl.BlockSpec((t, SSM_WIDTH), row),
            pl.BlockSpec((t, LANES), row),
            pl.BlockSpec((16, t), lambda i: (0, i)),
        ],
        out_shape=[
            jax.ShapeDtypeStruct((s, SSM_WIDTH), BF16),
            jax.ShapeDtypeStruct((s, LANES), F32),
            jax.ShapeDtypeStruct((16, s), F32),
        ],
        scratch_shapes=[
            pltpu.VMEM((t + 8, SSM_XBC), F32),
            pltpu.VMEM((SSM_GROUPS, SSM_STATE, SSM_WIDTH // SSM_GROUPS), F32),
            pltpu.VMEM((1, LANES), F32),
        ],
        compiler_params=pltpu.CompilerParams(
            dimension_semantics=("arbitrary",),
            vmem_limit_bytes=VMEM_LIMIT),
        name="ssd",
    )(z, xbc, small, cw, cb, bias_pad, a_pad, dskip_e, ng, tri, expand)


def _split3(x):
    hi = x.astype(BF16).astype(F32)
    r = x - hi
    mid = r.astype(BF16).astype(F32)
    lo = (r - mid).astype(BF16).astype(F32)
    return hi, mid, lo


def _fox_kernel(nkv_ref, q_ref, kt_ref, v_ref, cum_ref, cumt_ref, o_ref,
                kaug_ref, m_ref, l_ref, acc_ref):
    tq, tk = ATT_TQ, ATT_TK
    nb = kt_ref.shape[0]
    nq = pl.num_programs(1)
    p = pl.program_id(0)
    i = pl.program_id(1)

    @pl.when(i == 0)
    def _():
        row16 = lax.broadcasted_iota(jnp.int32, (16, tk), 0)
        zeros48 = jnp.zeros((48, tk), BF16)

        def build(j, carry):
            off = pl.multiple_of(j * tk, tk)
            kt = kt_ref[j]
            for hh in range(2):
                ck = cumt_ref[pl.ds(FOX_HEADS + 2 * p + hh, 1), pl.ds(off, tk)] * LOG2E
                hi, mid, lo = _split3(ck)
                bias = jnp.where(row16 < 3, 1.0,
                                 jnp.where(row16 == 3, -hi,
                                           jnp.where(row16 == 4, -mid,
                                                     jnp.where(row16 == 5, -lo, 0.0))))
                bias = bias.astype(BF16)
                if hh == 0:
                    kaug_ref[0, j, 0:64, :] = kt[0:64]
                    kaug_ref[0, j, 64:80, :] = bias
                    kaug_ref[0, j, 80:128, :] = zeros48
                else:
                    kaug_ref[1, j, 0:16, :] = bias
                    kaug_ref[1, j, 16:64, :] = zeros48
                    kaug_ref[1, j, 64:128, :] = kt[64:128]
            return carry

        lax.fori_loop(0, nb, build, 0)

    q = q_ref[...].astype(F32)
    lane = lax.broadcasted_iota(jnp.int32, (tq, LANES), 1)
    lo_half = lane < HEAD_DIM
    cum_blk = cum_ref[...]
    r_i = lax.broadcasted_iota(jnp.int32, (tq, tk), 0)
    c_i = lax.broadcasted_iota(jnp.int32, (tq, tk), 1)
    causal = r_i >= c_i

    qas = []
    for hh in range(2):
        h = 2 * p + hh
        cq = jnp.sum(jnp.where(lane == FOX_HEADS + h, cum_blk, 0.0), axis=1,
                     keepdims=True) * LOG2E
        hi, mid, lo = _split3(cq)
        base = HEAD_DIM if hh == 0 else 0
        aug = jnp.where(lane == base, hi,
                        jnp.where(lane == base + 1, mid,
                                  jnp.where(lane == base + 2, lo,
                                            jnp.where((lane >= base + 3) & (lane < base + 6),
                                                      1.0, 0.0))))
        qas.append(jnp.where(lo_half if hh == 0 else jnp.logical_not(lo_half), q, aug)
                   .astype(BF16))

    def logits(hh, j):
        return jnp.dot(qas[hh], kaug_ref[hh, j], preferred_element_type=F32)

    def vblock(j):
        return v_ref[pl.ds(pl.multiple_of(j * tk, tk), tk), :]

    def lane_fold(x):
        out = x[:, 0:LANES]
        for c in range(1, tk // LANES):
            out = out + x[:, c * LANES:(c + 1) * LANES]
        return out

    def diag_step(hh):
        s = jnp.where(causal, logits(hh, i), NEG_INF)
        m0 = jnp.max(s, axis=1, keepdims=True)
        p0 = jnp.exp2(s - m0)
        m_ref[hh] = jnp.broadcast_to(m0, (tq, LANES))
        l_ref[hh] = lane_fold(p0)
        acc_ref[hh] = jnp.dot(p0.astype(BF16), vblock(i), preferred_element_type=F32)

    def online_step(hh, j):
        s = logits(hh, j)
        m_prev = m_ref[hh]
        m_new = jnp.maximum(m_prev, jnp.max(s, axis=1, keepdims=True))
        alpha = jnp.exp2(m_prev - m_new)
        pj = jnp.exp2(s - m_new[:, 0:1])
        l_ref[hh] = alpha * l_ref[hh] + lane_fold(pj)
        acc_ref[hh] = alpha * acc_ref[hh] + jnp.dot(
            pj.astype(BF16), vblock(j), preferred_element_type=F32)
        m_ref[hh] = m_new

    def fixed_consume(hh, s, j):
        pj = jnp.exp2(s - m_ref[hh][:, 0:1])
        l_ref[hh] += lane_fold(pj)
        acc_ref[hh] += jnp.dot(pj.astype(BF16), vblock(j), preferred_element_type=F32)

    for hh in range(2):
        diag_step(hh)
        code = nkv_ref[(2 * p + hh) * nq + i]
        n_off = code >> 1
        fixed_ok = (code & 1) == 1

        @pl.when(fixed_ok)
        def _(hh=hh, n_off=n_off):
            def body(jj, carry):
                fixed_consume(hh, logits(hh, i - 1 - jj), i - 1 - jj)
                return carry

            lax.fori_loop(0, n_off, body, 0)

        @pl.when(jnp.logical_not(fixed_ok))
        def _(hh=hh, n_off=n_off):
            def body(jj, carry):
                online_step(hh, i - 1 - jj)
                return carry

            lax.fori_loop(0, n_off, body, 0)

    l0 = jnp.sum(l_ref[0], axis=1, keepdims=True)
    l1 = jnp.sum(l_ref[1], axis=1, keepdims=True)
    o_ref[...] = jnp.where(lo_half, acc_ref[0] / l0, acc_ref[1] / l1)


def _kv_counts(qn2, kn2, cum):
    tq = ATT_TQ
    nq = cum.shape[0] // tq
    qn = jnp.sqrt(jnp.max(qn2[:, :FOX_HEADS].reshape(nq, tq, FOX_HEADS), axis=1))
    kn = jnp.sqrt(jnp.max(kn2[:FOX_HEADS], axis=1))
    c = (cum[:, FOX_HEADS:2 * FOX_HEADS] * LOG2E).reshape(nq, tq, FOX_HEADS)
    cmax = jnp.max(c, axis=1)
    pmin = lax.cummin(jnp.min(c, axis=1), axis=0)
    qk = 2.0 * NORM_MARGIN * qn * kn[None, :] + 1.0
    bound = (qk + cmax)[:, None, :] - pmin[None, :, :]
    ii = lax.broadcasted_iota(jnp.int32, bound.shape, 0)
    jj = lax.broadcasted_iota(jnp.int32, bound.shape, 1)
    skip = (bound < SKIP_LOG2) & (jj < ii)
    jstar = jnp.max(jnp.where(skip, jj, -1), axis=1)
    n_off = jnp.arange(nq, dtype=jnp.int32)[:, None] - 1 - jstar
    pmin_prev = jnp.concatenate([jnp.full((1, FOX_HEADS), jnp.inf, F32), pmin[:-1]], axis=0)
    fixed_ok = (qk + jnp.maximum(cmax - pmin_prev, 0.0)) < FIXED_MAX_LOG2
    code = 2 * n_off + fixed_ok.astype(jnp.int32)
    return code.T.reshape(-1).astype(jnp.int32)


def _fox(nkv, q, kt3, v, cum, cumt):
    s = q.shape[0]
    tq, tk = ATT_TQ, ATT_TK
    nb = s // tk
    grid_spec = pltpu.PrefetchScalarGridSpec(
        num_scalar_prefetch=1,
        grid=(FOX_HEADS // 2, s // tq),
        in_specs=[
            pl.BlockSpec((tq, LANES), lambda p, i, n: (i, p)),
            pl.BlockSpec((nb, LANES, tk), lambda p, i, n: (0, p, 0)),
            pl.BlockSpec((s, LANES), lambda p, i, n: (0, p)),
            pl.BlockSpec((tq, LANES), lambda p, i, n: (i, 0)),
            pl.BlockSpec((16, s), lambda p, i, n: (0, 0)),
        ],
        out_specs=pl.BlockSpec((tq, LANES), lambda p, i, n: (i, p)),
        scratch_shapes=[
            pltpu.VMEM((2, nb, LANES, tk), BF16),
            pltpu.VMEM((2, tq, LANES), F32),
            pltpu.VMEM((2, tq, LANES), F32),
            pltpu.VMEM((2, tq, LANES), F32),
        ],
    )
    return pl.pallas_call(
        _fox_kernel,
        grid_spec=grid_spec,
        out_shape=jax.ShapeDtypeStruct((s, FOX_WIDTH), F32),
        compiler_params=pltpu.CompilerParams(
            dimension_semantics=("arbitrary", "arbitrary"),
            vmem_limit_bytes=VMEM_LIMIT),
        name="fox_attn",
    )(nkv, q, kt3, v, cum, cumt)


def _outproj_kernel(x_ref, ys_ref, of_ref, fg_ref, w1_ref, w2_ref, o_ref):
    yf = _rms(of_ref[...], fg_ref[...]).astype(BF16)
    o_ref[...] = (x_ref[...]
                  + jnp.dot(ys_ref[...], w1_ref[...], preferred_element_type=F32)
                  + jnp.dot(yf, w2_ref[...], preferred_element_type=F32))


def _outproj(x, ys, of, fg, w1, w2):
    s = x.shape[0]
    tm = PROJ_TM
    const = lambda i: (0, 0)
    row = lambda i: (i, 0)
    return pl.pallas_call(
        _outproj_kernel,
        grid=(s // tm,),
        in_specs=[
            pl.BlockSpec((tm, D_MODEL), row),
            pl.BlockSpec((tm, SSM_WIDTH), row),
            pl.BlockSpec((tm, FOX_WIDTH), row),
            pl.BlockSpec((1, FOX_WIDTH), const),
            pl.BlockSpec((SSM_WIDTH, D_MODEL), const),
            pl.BlockSpec((FOX_WIDTH, D_MODEL), const),
        ],
        out_specs=pl.BlockSpec((tm, D_MODEL), row),
        out_shape=jax.ShapeDtypeStruct((s, D_MODEL), F32),
        compiler_params=pltpu.CompilerParams(
            dimension_semantics=("parallel",),
            vmem_limit_bytes=VMEM_LIMIT),
        name="out_proj",
    )(x, ys, of, fg, w1, w2)


def _pad_lanes(vec_dt, vec_f):
    out = jnp.zeros((1, LANES), F32)
    out = out.at[0, 0:SSM_HEADS].set(vec_dt.astype(F32))
    if vec_f is not None:
        out = out.at[0, SSM_HEADS:SSM_HEADS + FOX_HEADS].set(vec_f.astype(F32))
    return out


def _layer(x, ffn1_norm, ffn1_w_in, ffn1_w_out, mix_norm, w_in, conv_w, conv_b, dt_bias,
           a_log, d_skip, ssm_norm, f_bias, fox_norm, w_out, ffn2_norm, ffn2_w_in,
           ffn2_w_out, final_g, *, final):
    ones = jnp.ones((1, D_MODEL), F32)
    row = lambda a: a.reshape(1, -1).astype(F32)

    x1 = _ffn(x, row(ffn1_norm), ffn1_w_in, ffn1_w_out, ones, final=False)

    o0 = SSM_WIDTH
    o1 = o0 + SSM_XBC
    o2 = o1 + SSM_HEADS
    o3 = o2 + 3 * FOX_WIDTH
    wz = w_in[:, :o0].astype(BF16)
    wxbc = w_in[:, o0:o1].astype(BF16)
    w_dt = w_in[:, o1:o2]
    wq = w_in[:, o2:o2 + FOX_WIDTH].astype(BF16)
    wkt = w_in[:, o2 + FOX_WIDTH:o2 + 2 * FOX_WIDTH].T.astype(BF16)
    wv = w_in[:, o2 + 2 * FOX_WIDTH:o3].astype(BF16)
    w_f = w_in[:, o3:]
    ws = jnp.concatenate(
        [w_dt, w_f, jnp.zeros((D_MODEL, LANES - SSM_HEADS - FOX_HEADS), w_in.dtype)],
        axis=1).astype(BF16)
    z, xbc, q, kt3, v, small, qn2, kn2 = _inproj(x1, row(mix_norm), wz, wxbc, wq, wkt, wv, ws)

    bias_pad = _pad_lanes(dt_bias, f_bias)
    a_pad = _pad_lanes(-jnp.exp(a_log.astype(F32)), None)
    dskip_e = jnp.repeat(d_skip.astype(F32), HEAD_DIM).reshape(1, SSM_WIDTH)
    tri = jnp.asarray(np.tril(np.ones((SSD_T, SSD_T), np.float32)))
    expand_np = np.zeros((LANES, SSM_WIDTH), np.float32)
    for hd in range(SSM_HEADS):
        expand_np[hd, hd * HEAD_DIM:(hd + 1) * HEAD_DIM] = 1.0
    y_ssd, cum, cumt = _ssd(z, xbc, small, conv_w.astype(F32), row(conv_b), bias_pad, a_pad,
                            dskip_e, row(ssm_norm), tri, jnp.asarray(expand_np))

    o_fox = _fox(_kv_counts(qn2, kn2, cum), q, kt3, v, cum, cumt)

    wo = w_out.astype(BF16)
    x2 = _outproj(x1, y_ssd, o_fox, row(fox_norm), wo[:SSM_WIDTH], wo[SSM_WIDTH:])

    return _ffn(x2, row(ffn2_norm), ffn2_w_in, ffn2_w_out, row(final_g), final=final)


def kernel(x, ffn1_norm, ffn1_w_in, ffn1_w_out, mix_norm, w_in, conv_w, conv_b, dt_bias, a_log,
           d_skip, ssm_norm, f_bias, fox_norm, w_out, ffn2_norm, ffn2_w_in, ffn2_w_out, final_norm):
    b, s, d = x.shape
    depth = ffn1_norm.shape[0]
    outs = []
    for bi in range(b):
        xb = x[bi]
        for l in range(depth):
            xb = _layer(xb, ffn1_norm[l], ffn1_w_in[l], ffn1_w_out[l], mix_norm[l], w_in[l],
                        conv_w[l], conv_b[l], dt_bias[l], a_log[l], d_skip[l], ssm_norm[l],
                        f_bias[l], fox_norm[l], w_out[l], ffn2_norm[l], ffn2_w_in[l],
                        ffn2_w_out[l], final_norm, final=(l == depth - 1))
        outs.append(xb)
    return jnp.stack(outs, axis=0)
```

```python
import functools

import jax
import jax.numpy as jnp
import numpy as np
from jax import lax
from jax.experimental import pallas as pl
from jax.experimental.pallas import tpu as pltpu

F32 = jnp.float32
BF16 = jnp.bfloat16
HIGHEST = lax.Precision.HIGHEST

D_MODEL = 1024
SEQ = 16384
HEAD_DIM = 64
SSM_WIDTH = 512
SSM_HEADS = 8
SSM_GROUPS = 2
SSM_STATE = 128
CONV_WIDTH = 4
SSM_XBC = SSM_WIDTH + 2 * SSM_GROUPS * SSM_STATE
FOX_WIDTH = 512
FOX_HEADS = 8
D_FF = 2816
EPS = 1e-6

LANES = 128
VMEM_LIMIT = 56 * 1024 * 1024

FFN_TM = 1024
FFN_TF = 256
PROJ_TM = 512
SSD_T = 256
ATT_TQ = 512
ATT_TK = 512
NEG_INF = float("-inf")
LOG2E = 1.4426950408889634
SKIP_LOG2 = -150.0
NORM_MARGIN = 1.05
FIXED_MAX_LOG2 = 64.0


def _rms(x, g):
    ms = jnp.mean(x * x, axis=-1, keepdims=True)
    return x * lax.rsqrt(ms + EPS) * g


def _silu(x):
    return x * (1.0 / (1.0 + jnp.exp(-x)))


def _softplus(x):
    return jnp.maximum(x, 0.0) + jnp.log1p(jnp.exp(-jnp.abs(x)))


def _split3(x):
    hi = x.astype(BF16).astype(F32)
    r = x - hi
    mid = r.astype(BF16).astype(F32)
    lo = (r - mid).astype(BF16).astype(F32)
    return hi, mid, lo


def _dot_f32_rhs(a_bf16, x):
    hi, mid, lo = _split3(x)
    d = lambda piece: jnp.dot(a_bf16, piece.astype(BF16), preferred_element_type=F32)
    return (d(lo) + d(mid)) + d(hi)


def _dot_f32_lhs(x, b_bf16):
    hi, mid, lo = _split3(x)
    d = lambda piece: jnp.dot(piece.astype(BF16), b_bf16, preferred_element_type=F32)
    return (d(lo) + d(mid)) + d(hi)


def _ffn_kernel(x_ref, g_ref, wg_ref, wu_ref, wo_ref, fg_ref, o_ref, h_ref, *, nf, final):
    f = pl.program_id(1)

    @pl.when(f == 0)
    def _():
        h_ref[...] = _rms(x_ref[...], g_ref[...]).astype(BF16)
        o_ref[...] = jnp.zeros_like(o_ref)

    h = h_ref[...]
    gate = jnp.dot(h, wg_ref[...].astype(BF16), preferred_element_type=F32)
    up = jnp.dot(h, wu_ref[...].astype(BF16), preferred_element_type=F32)
    act = (_silu(gate) * up).astype(BF16)
    o_ref[...] += jnp.dot(act, wo_ref[...].astype(BF16), preferred_element_type=F32)

    @pl.when(f == nf - 1)
    def _():
        r = x_ref[...] + 0.5 * o_ref[...]
        if final:
            r = _rms(r, fg_ref[...])
        o_ref[...] = r


def _ffn(x, g, w_in, w_out, fg, *, final):
    s = x.shape[0]
    nf = D_FF // FFN_TF
    return pl.pallas_call(
        functools.partial(_ffn_kernel, nf=nf, final=final),
        grid=(s // FFN_TM, nf),
        in_specs=[
            pl.BlockSpec((FFN_TM, D_MODEL), lambda i, f: (i, 0)),
            pl.BlockSpec((1, D_MODEL), lambda i, f: (0, 0)),
            pl.BlockSpec((D_MODEL, FFN_TF), lambda i, f: (0, f)),
            pl.BlockSpec((D_MODEL, FFN_TF), lambda i, f: (0, nf + f)),
            pl.BlockSpec((FFN_TF, D_MODEL), lambda i, f: (f, 0)),
            pl.BlockSpec((1, D_MODEL), lambda i, f: (0, 0)),
        ],
        out_specs=pl.BlockSpec((FFN_TM, D_MODEL), lambda i, f: (i, 0)),
        out_shape=jax.ShapeDtypeStruct((s, D_MODEL), F32),
        scratch_shapes=[pltpu.VMEM((FFN_TM, D_MODEL), BF16)],
        compiler_params=pltpu.CompilerParams(
            dimension_semantics=("parallel", "arbitrary"),
            vmem_limit_bytes=VMEM_LIMIT),
        name="ffn_final" if final else "ffn",
    )(x, g, w_in, w_in, w_out, fg)


def _inproj_kernel(x_ref, g_ref, wz_ref, wxbc_ref, wq_ref, wkt_ref, wv_ref, ws_ref,
                   ind_ref, indt_ref, cw_ref, cb_ref,
                   zs_ref, xs_ref, bc_ref, q_ref, kt_ref, v_ref, small_ref, qn_ref, kn_ref,
                   ext_ref):
    tm = PROJ_TM
    i = pl.program_id(0)

    @pl.when(i == 0)
    def _():
        ext_ref[0:8, :] = jnp.zeros((8, SSM_XBC), F32)

    h = _rms(x_ref[...], g_ref[...]).astype(BF16)
    ext_ref[8:8 + tm, :] = jnp.dot(h, wxbc_ref[...], preferred_element_type=F32)
    conv = cb_ref[...] + ext_ref[5:5 + tm, :] * cw_ref[0:1, :]
    for k in range(1, CONV_WIDTH):
        conv = conv + ext_ref[5 + k:5 + k + tm, :] * cw_ref[k:k + 1, :]
    ext_ref[0:8, :] = ext_ref[tm:tm + 8, :]
    u = _silu(conv)
    xs_ref[...] = u[:, :SSM_WIDTH]
    bc_ref[...] = u[:, SSM_WIDTH:].astype(BF16)
    zs_ref[...] = _silu(jnp.dot(h, wz_ref[...], preferred_element_type=F32))
    q = jnp.dot(h, wq_ref[...], preferred_element_type=F32)
    qb = (q * (HEAD_DIM ** -0.5 * LOG2E)).astype(BF16)
    q_ref[...] = qb
    kt = lax.dot_general(wkt_ref[...], h, (((1,), (1,)), ((), ())),
                         preferred_element_type=F32)
    ktb = kt.astype(BF16)
    kt_ref[0] = ktb
    v_ref[...] = jnp.dot(h, wv_ref[...], preferred_element_type=F32).astype(BF16)
    small_ref[...] = jnp.dot(h, ws_ref[...], preferred_element_type=F32)
    qf = qb.astype(F32)
    qn_ref[...] = jnp.dot((qf * qf).astype(BF16), ind_ref[...], preferred_element_type=F32)
    kf = ktb.astype(F32)
    kn_ref[...] = jnp.dot(indt_ref[...], (kf * kf).astype(BF16), preferred_element_type=F32)


def _inproj(x, g, wz, wxbc, wq, wkt, wv, ws, cw, cb):
    s = x.shape[0]
    tm = PROJ_TM
    nb = s // tm
    const = lambda i: (0, 0)
    row = lambda i: (i, 0)
    ind_np = np.zeros((FOX_WIDTH, LANES), np.float32)
    for hd in range(FOX_HEADS):
        ind_np[hd * HEAD_DIM:(hd + 1) * HEAD_DIM, hd] = 1.0
    ind = jnp.asarray(ind_np, dtype=BF16)
    indt = jnp.asarray(ind_np.T[:16], dtype=BF16)
    return pl.pallas_call(
        _inproj_kernel,
        grid=(nb,),
        in_specs=[
            pl.BlockSpec((tm, D_MODEL), row),
            pl.BlockSpec((1, D_MODEL), const),
            pl.BlockSpec((D_MODEL, SSM_WIDTH), const),
            pl.BlockSpec((D_MODEL, SSM_XBC), const),
            pl.BlockSpec((D_MODEL, FOX_WIDTH), const),
            pl.BlockSpec((FOX_WIDTH, D_MODEL), const),
            pl.BlockSpec((D_MODEL, FOX_WIDTH), const),
            pl.BlockSpec((D_MODEL, LANES), const),
            pl.BlockSpec((FOX_WIDTH, LANES), const),
            pl.BlockSpec((16, FOX_WIDTH), const),
            pl.BlockSpec((CONV_WIDTH, SSM_XBC), const),
            pl.BlockSpec((1, SSM_XBC), const),
        ],
        out_specs=[
            pl.BlockSpec((tm, SSM_WIDTH), row),
            pl.BlockSpec((tm, SSM_WIDTH), row),
            pl.BlockSpec((tm, SSM_XBC - SSM_WIDTH), row),
            pl.BlockSpec((tm, FOX_WIDTH), row),
            pl.BlockSpec((1, FOX_WIDTH, tm), lambda i: (i, 0, 0)),
            pl.BlockSpec((tm, FOX_WIDTH), row),
            pl.BlockSpec((tm, LANES), row),
            pl.BlockSpec((tm, LANES), row),
            pl.BlockSpec((16, tm), lambda i: (0, i)),
        ],
        out_shape=[
            jax.ShapeDtypeStruct((s, SSM_WIDTH), F32),
            jax.ShapeDtypeStruct((s, SSM_WIDTH), F32),
            jax.ShapeDtypeStruct((s, SSM_XBC - SSM_WIDTH), BF16),
            jax.ShapeDtypeStruct((s, FOX_WIDTH), BF16),
            jax.ShapeDtypeStruct((nb, FOX_WIDTH, tm), BF16),
            jax.ShapeDtypeStruct((s, FOX_WIDTH), BF16),
            jax.ShapeDtypeStruct((s, LANES), F32),
            jax.ShapeDtypeStruct((s, LANES), F32),
            jax.ShapeDtypeStruct((16, s), F32),
        ],
        scratch_shapes=[pltpu.VMEM((tm + 8, SSM_XBC), F32)],
        compiler_params=pltpu.CompilerParams(
            dimension_semantics=("arbitrary",),
            vmem_limit_bytes=VMEM_LIMIT),
        name="in_proj",
    )(x, g, wz, wxbc, wq, wkt, wv, ws, ind, indt, cw, cb)


def _ssd_kernel(zs_ref, xs_ref, bc_ref, small_ref, bias_ref, apad_ref,
                dskip_ref, ng_ref, tri_ref, expand_ref,
                y_ref, cum_ref, cumt_ref,
                state_ref, carry_ref):
    t_rows = SSD_T
    i = pl.program_id(0)

    @pl.when(i == 0)
    def _():
        state_ref[...] = jnp.zeros_like(state_ref)
        carry_ref[...] = jnp.zeros_like(carry_ref)

    xs = xs_ref[...]
    bm = bc_ref[:, 0:SSM_GROUPS * SSM_STATE]
    cm = bc_ref[:, SSM_GROUPS * SSM_STATE:]

    t = small_ref[...] + bias_ref[...]
    lane = lax.broadcasted_iota(jnp.int32, (t_rows, LANES), 1)
    is_dt = lane < SSM_HEADS
    sp = _softplus(jnp.where(is_dt, t, -t))
    v = jnp.where(is_dt, sp * apad_ref[...], -sp)
    cs = _dot_f32_rhs(tri_ref[...], v)
    cs = cs + carry_ref[...]
    last = cs[t_rows - 1:t_rows, :]
    carry_ref[...] = jnp.where(lane[0:1, :] < SSM_HEADS, 0.0, last)
    cum_ref[...] = cs
    cst = cs.T
    cumt_ref[...] = cst[0:16, :]

    stacked = jnp.concatenate([sp, cs], axis=0)
    exp2 = _dot_f32_lhs(stacked, expand_ref[...])
    dt_e = exp2[0:t_rows]
    cs_e = exp2[t_rows:2 * t_rows]
    ea_e = jnp.exp(cs_e)
    de_e = jnp.exp(cs_e[t_rows - 1:t_rows, :] - cs_e)
    cd_e = ea_e[t_rows - 1:t_rows, :]

    xdt = xs * dt_e
    xdt_b = xdt.astype(BF16)
    wst = (xdt * de_e).astype(BF16)

    r_i = lax.broadcasted_iota(jnp.int32, (t_rows, t_rows), 0)
    c_i = lax.broadcasted_iota(jnp.int32, (t_rows, t_rows), 1)
    causal = r_i >= c_i
    lane_pair = lax.broadcasted_iota(jnp.int32, (t_rows, LANES), 1)
    lo = lane_pair < HEAD_DIM

    gw = SSM_WIDTH // SSM_GROUPS
    hpg = SSM_HEADS // SSM_GROUPS
    y_parts = []
    for g in range(SSM_GROUPS):
        cg = cm[:, g * SSM_STATE:(g + 1) * SSM_STATE]
        bg = bm[:, g * SSM_STATE:(g + 1) * SSM_STATE]
        gmat = lax.dot_general(cg, bg, (((1,), (1,)), ((), ())),
                               preferred_element_type=F32)
        for pr in range(hpg // 2):
            c0 = g * gw + pr * LANES
            xpair = xdt_b[:, c0:c0 + LANES]
            acc = None
            for hh in range(2):
                h = g * hpg + pr * 2 + hh
                seg = cs[:, h:h + 1] - cst[h:h + 1, :]
                dec = jnp.exp(jnp.where(causal, seg, NEG_INF))
                m = (gmat * dec).astype(BF16)
                xm = jnp.where(lo if hh == 0 else jnp.logical_not(lo), xpair,
                               jnp.zeros_like(xpair))
                part = jnp.dot(m, xm, preferred_element_type=F32)
                acc = part if acc is None else acc + part
            y_parts.append(acc)
        s_prev = state_ref[g]
        y_off = jnp.dot(cg, s_prev.astype(BF16), preferred_element_type=F32)
        y_parts[-2] = y_parts[-2] + y_off[:, :LANES] * ea_e[:, g * gw:g * gw + LANES]
        y_parts[-1] = y_parts[-1] + y_off[:, LANES:] * ea_e[:, g * gw + LANES:(g + 1) * gw]
        bgt = bg.astype(F32).T.astype(BF16)
        upd = jnp.dot(bgt, wst[:, g * gw:(g + 1) * gw], preferred_element_type=F32)
        state_ref[g] = s_prev * cd_e[:, g * gw:(g + 1) * gw] + upd

    y = jnp.concatenate(y_parts, axis=1) + dskip_ref[...] * xs
    y = y * zs_ref[...]
    y_ref[...] = _rms(y, ng_ref[...]).astype(BF16)


def _ssd(zs, xs, bc, small, bias_pad, a_pad, dskip_e, ng, tri, expand):
    s = zs.shape[0]
    t = SSD_T
    const = lambda i: (0, 0)
    row = lambda i: (i, 0)
    return pl.pallas_call(
        _ssd_kernel,
        grid=(s // t,),
        in_specs=[
            pl.BlockSpec((t, SSM_WIDTH), row),
            pl.BlockSpec((t, SSM_WIDTH), row),
            pl.BlockSpec((t, SSM_XBC - SSM_WIDTH), row),
            pl.BlockSpec((t, LANES), row),
            pl.BlockSpec((1, LANES), const),
            pl.BlockSpec((1, LANES), const),
            pl.BlockSpec((1, SSM_WIDTH), const),
            pl.BlockSpec((1, SSM_WIDTH), const),
            pl.BlockSpec((t, t), const),
            pl.BlockSpec((LANES, SSM_WIDTH), const),
        ],
        out_specs=[
            pl.BlockSpec((t, SSM_WIDTH), row),
            pl.BlockSpec((t, LANES), row),
            pl.BlockSpec((16, t), lambda i: (0, i)),
        ],
        out_shape=[
            jax.ShapeDtypeStruct((s, SSM_WIDTH), BF16),
            jax.ShapeDtypeStruct((s, LANES), F32),
            jax.ShapeDtypeStruct((16, s), F32),
        ],
        scratch_shapes=[
            pltpu.VMEM((SSM_GROUPS, SSM_STATE, SSM_WIDTH // SSM_GROUPS), F32),
            pltpu.VMEM((1, LANES), F32),
        ],
        compiler_params=pltpu.CompilerParams(
            dimension_semantics=("arbitrary",),
            vmem_limit_bytes=VMEM_LIMIT),
        name="ssd",
    )(zs, xs, bc, small, bias_pad, a_pad, dskip_e, ng, tri, expand)


def _fox_kernel(nkv_ref, q_ref, kt_ref, v_ref, cum_ref, cumt_ref, o_ref,
                kaug_ref, m_ref, l_ref, acc_ref):
    tq, tk = ATT_TQ, ATT_TK
    nb = kt_ref.shape[0]
    nq = pl.num_programs(1)
    p = pl.program_id(0)
    i = pl.program_id(1)

    @pl.when(i == 0)
    def _():
        row16 = lax.broadcasted_iota(jnp.int32, (16, tk), 0)
        zeros48 = jnp.zeros((48, tk), BF16)

        def build(j, carry):
            off = pl.multiple_of(j * tk, tk)
            kt = kt_ref[j]
            for hh in range(2):
                ck = cumt_ref[pl.ds(FOX_HEADS + 2 * p + hh, 1), pl.ds(off, tk)] * LOG2E
                hi, mid, lo = _split3(ck)
                bias = jnp.where(row16 < 3, 1.0,
                                 jnp.where(row16 == 3, -hi,
                                           jnp.where(row16 == 4, -mid,
                                                     jnp.where(row16 == 5, -lo, 0.0))))
                bias = bias.astype(BF16)
                if hh == 0:
                    kaug_ref[0, j, 0:64, :] = kt[0:64]
                    kaug_ref[0, j, 64:80, :] = bias
                    kaug_ref[0, j, 80:128, :] = zeros48
                else:
                    kaug_ref[1, j, 0:16, :] = bias
                    kaug_ref[1, j, 16:64, :] = zeros48
                    kaug_ref[1, j, 64:128, :] = kt[64:128]
            return carry

        lax.fori_loop(0, nb, build, 0)

    q = q_ref[...].astype(F32)
    lane = lax.broadcasted_iota(jnp.int32, (tq, LANES), 1)
    lo_half = lane < HEAD_DIM
    cum_blk = cum_ref[...]
    r_i = lax.broadcasted_iota(jnp.int32, (tq, tk), 0)
    c_i = lax.broadcasted_iota(jnp.int32, (tq, tk), 1)
    causal = r_i >= c_i

    qas = []
    for hh in range(2):
        h = 2 * p + hh
        cq = jnp.sum(jnp.where(lane == FOX_HEADS + h, cum_blk, 0.0), axis=1,
                     keepdims=True) * LOG2E
        hi, mid, lo = _split3(cq)
        base = HEAD_DIM if hh == 0 else 0
        aug = jnp.where(lane == base, hi,
                        jnp.where(lane == base + 1, mid,
                                  jnp.where(lane == base + 2, lo,
                                            jnp.where((lane >= base + 3) & (lane < base + 6),
                                                      1.0, 0.0))))
        qas.append(jnp.where(lo_half if hh == 0 else jnp.logical_not(lo_half), q, aug)
                   .astype(BF16))

    def logits(hh, j):
        return jnp.dot(qas[hh], kaug_ref[hh, j], preferred_element_type=F32)

    def vblock(j):
        return v_ref[pl.ds(pl.multiple_of(j * tk, tk), tk), :]

    def lane_fold(x):
        out = x[:, 0:LANES]
        for c in range(1, tk // LANES):
            out = out + x[:, c * LANES:(c + 1) * LANES]
        return out

    def diag_step(hh):
        s = jnp.where(causal, logits(hh, i), NEG_INF)
        m0 = jnp.max(s, axis=1, keepdims=True)
        p0 = jnp.exp2(s - m0)
        m_ref[hh] = jnp.broadcast_to(m0, (tq, LANES))
        l_ref[hh] = lane_fold(p0)
        acc_ref[hh] = jnp.dot(p0.astype(BF16), vblock(i), preferred_element_type=F32)

    def online_step(hh, j):
        s = logits(hh, j)
        m_prev = m_ref[hh]
        m_new = jnp.maximum(m_prev, jnp.max(s, axis=1, keepdims=True))
        alpha = jnp.exp2(m_prev - m_new)
        pj = jnp.exp2(s - m_new[:, 0:1])
        l_ref[hh] = alpha * l_ref[hh] + lane_fold(pj)
        acc_ref[hh] = alpha * acc_ref[hh] + jnp.dot(
            pj.astype(BF16), vblock(j), preferred_element_type=F32)
        m_ref[hh] = m_new

    def fixed_consume(hh, s, j):
        pj = jnp.exp2(s - m_ref[hh][:, 0:1])
        l_ref[hh] += lane_fold(pj)
        acc_ref[hh] += jnp.dot(pj.astype(BF16), vblock(j), preferred_element_type=F32)

    for hh in range(2):
        diag_step(hh)
        code = nkv_ref[(2 * p + hh) * nq + i]
        n_off = code >> 1
        fixed_ok = (code & 1) == 1

        @pl.when(fixed_ok)
        def _(hh=hh, n_off=n_off):
            def body(jj, carry):
                fixed_consume(hh, logits(hh, i - 1 - jj), i - 1 - jj)
                return carry

            lax.fori_loop(0, n_off, body, 0)

        @pl.when(jnp.logical_not(fixed_ok))
        def _(hh=hh, n_off=n_off):
            def body(jj, carry):
                online_step(hh, i - 1 - jj)
                return carry

            lax.fori_loop(0, n_off, body, 0)

    l0 = jnp.sum(l_ref[0], axis=1, keepdims=True)
    l1 = jnp.sum(l_ref[1], axis=1, keepdims=True)
    o_ref[...] = jnp.where(lo_half, acc_ref[0] / l0, acc_ref[1] / l1)


def _kv_counts(qn2, kn2, cum):
    tq = ATT_TQ
    nq = cum.shape[0] // tq
    qn = jnp.sqrt(jnp.max(qn2[:, :FOX_HEADS].reshape(nq, tq, FOX_HEADS), axis=1))
    kn = jnp.sqrt(jnp.max(kn2[:FOX_HEADS], axis=1))
    c = (cum[:, FOX_HEADS:2 * FOX_HEADS] * LOG2E).reshape(nq, tq, FOX_HEADS)
    cmax = jnp.max(c, axis=1)
    pmin = lax.cummin(jnp.min(c, axis=1), axis=0)
    qk = 2.0 * NORM_MARGIN * qn * kn[None, :] + 1.0
    bound = (qk + cmax)[:, None, :] - pmin[None, :, :]
    ii = lax.broadcasted_iota(jnp.int32, bound.shape, 0)
    jj = lax.broadcasted_iota(jnp.int32, bound.shape, 1)
    skip = (bound < SKIP_LOG2) & (jj < ii)
    jstar = jnp.max(jnp.where(skip, jj, -1), axis=1)
    n_off = jnp.arange(nq, dtype=jnp.int32)[:, None] - 1 - jstar
    pmin_prev = jnp.concatenate([jnp.full((1, FOX_HEADS), jnp.inf, F32), pmin[:-1]], axis=0)
    fixed_ok = (qk + jnp.maximum(cmax - pmin_prev, 0.0)) < FIXED_MAX_LOG2
    code = 2 * n_off + fixed_ok.astype(jnp.int32)
    return code.T.reshape(-1).astype(jnp.int32)


def _fox(nkv, q, kt3, v, cum, cumt):
    s = q.shape[0]
    tq, tk = ATT_TQ, ATT_TK
    nb = s // tk
    grid_spec = pltpu.PrefetchScalarGridSpec(
        num_scalar_prefetch=1,
        grid=(FOX_HEADS // 2, s // tq),
        in_specs=[
            pl.BlockSpec((tq, LANES), lambda p, i, n: (i, p)),
            pl.BlockSpec((nb, LANES, tk), lambda p, i, n: (0, p, 0)),
            pl.BlockSpec((s, LANES), lambda p, i, n: (0, p)),
            pl.BlockSpec((tq, LANES), lambda p, i, n: (i, 0)),
            pl.BlockSpec((16, s), lambda p, i, n: (0, 0)),
        ],
        out_specs=pl.BlockSpec((tq, LANES), lambda p, i, n: (i, p)),
        scratch_shapes=[
            pltpu.VMEM((2, nb, LANES, tk), BF16),
            pltpu.VMEM((2, tq, LANES), F32),
            pltpu.VMEM((2, tq, LANES), F32),
            pltpu.VMEM((2, tq, LANES), F32),
        ],
    )
    return pl.pallas_call(
        _fox_kernel,
        grid_spec=grid_spec,
        out_shape=jax.ShapeDtypeStruct((s, FOX_WIDTH), F32),
        compiler_params=pltpu.CompilerParams(
            dimension_semantics=("arbitrary", "arbitrary"),
            vmem_limit_bytes=VMEM_LIMIT),
        name="fox_attn",
    )(nkv, q, kt3, v, cum, cumt)


def _outproj_kernel(x_ref, ys_ref, of_ref, fg_ref, w1_ref, w2_ref, o_ref):
    yf = _rms(of_ref[...], fg_ref[...]).astype(BF16)
    o_ref[...] = (x_ref[...]
                  + jnp.dot(ys_ref[...], w1_ref[...], preferred_element_type=F32)
                  + jnp.dot(yf, w2_ref[...], preferred_element_type=F32))


def _outproj(x, ys, of, fg, w1, w2):
    s = x.shape[0]
    tm = PROJ_TM
    const = lambda i: (0, 0)
    row = lambda i: (i, 0)
    return pl.pallas_call(
        _outproj_kernel,
        grid=(s // tm,),
        in_specs=[
            pl.BlockSpec((tm, D_MODEL), row),
            pl.BlockSpec((tm, SSM_WIDTH), row),
            pl.BlockSpec((tm, FOX_WIDTH), row),
            pl.BlockSpec((1, FOX_WIDTH), const),
            pl.BlockSpec((SSM_WIDTH, D_MODEL), const),
            pl.BlockSpec((FOX_WIDTH, D_MODEL), const),
        ],
        out_specs=pl.BlockSpec((tm, D_MODEL), row),
        out_shape=jax.ShapeDtypeStruct((s, D_MODEL), F32),
        compiler_params=pltpu.CompilerParams(
            dimension_semantics=("parallel",),
            vmem_limit_bytes=VMEM_LIMIT),
        name="out_proj",
    )(x, ys, of, fg, w1, w2)


def _pad_lanes(vec_dt, vec_f):
    out = jnp.zeros((1, LANES), F32)
    out = out.at[0, 0:SSM_HEADS].set(vec_dt.astype(F32))
    if vec_f is not None:
        out = out.at[0, SSM_HEADS:SSM_HEADS + FOX_HEADS].set(vec_f.astype(F32))
    return out


def _layer(x, ffn1_norm, ffn1_w_in, ffn1_w_out, mix_norm, w_in, conv_w, conv_b, dt_bias,
           a_log, d_skip, ssm_norm, f_bias, fox_norm, w_out, ffn2_norm, ffn2_w_in,
           ffn2_w_out, final_g, *, final):
    ones = jnp.ones((1, D_MODEL), F32)
    row = lambda a: a.reshape(1, -1).astype(F32)

    x1 = _ffn(x, row(ffn1_norm), ffn1_w_in, ffn1_w_out, ones, final=False)

    o0 = SSM_WIDTH
    o1 = o0 + SSM_XBC
    o2 = o1 + SSM_HEADS
    o3 = o2 + 3 * FOX_WIDTH
    wz = w_in[:, :o0].astype(BF16)
    wxbc = w_in[:, o0:o1].astype(BF16)
    w_dt = w_in[:, o1:o2]
    wq = w_in[:, o2:o2 + FOX_WIDTH].astype(BF16)
    wkt = w_in[:, o2 + FOX_WIDTH:o2 + 2 * FOX_WIDTH].T.astype(BF16)
    wv = w_in[:, o2 + 2 * FOX_WIDTH:o3].astype(BF16)
    w_f = w_in[:, o3:]
    ws = jnp.concatenate(
        [w_dt, w_f, jnp.zeros((D_MODEL, LANES - SSM_HEADS - FOX_HEADS), w_in.dtype)],
        axis=1).astype(BF16)
    zs, xs, bc, q, kt3, v, small, qn2, kn2 = _inproj(
        x1, row(mix_norm), wz, wxbc, wq, wkt, wv, ws, conv_w.astype(F32), row(conv_b))

    bias_pad = _pad_lanes(dt_bias, f_bias)
    a_pad = _pad_lanes(-jnp.exp(a_log.astype(F32)), None)
    dskip_e = jnp.repeat(d_skip.astype(F32), HEAD_DIM).reshape(1, SSM_WIDTH)
    tri = jnp.asarray(np.tril(np.ones((SSD_T, SSD_T), np.float32)), dtype=BF16)
    expand_np = np.zeros((LANES, SSM_WIDTH), np.float32)
    for hd in range(SSM_HEADS):
        expand_np[hd, hd * HEAD_DIM:(hd + 1) * HEAD_DIM] = 1.0
    y_ssd, cum, cumt = _ssd(zs, xs, bc, small, bias_pad, a_pad,
                            dskip_e, row(ssm_norm), tri, jnp.asarray(expand_np, dtype=BF16))

    o_fox = _fox(_kv_counts(qn2, kn2, cum), q, kt3, v, cum, cumt)

    wo = w_out.astype(BF16)
    x2 = _outproj(x1, y_ssd, o_fox, row(fox_norm), wo[:SSM_WIDTH], wo[SSM_WIDTH:])

    return _ffn(x2, row(ffn2_norm), ffn2_w_in, ffn2_w_out, row(final_g), final=final)


def kernel(x, ffn1_norm, ffn1_w_in, ffn1_w_out, mix_norm, w_in, conv_w, conv_b, dt_bias, a_log,
           d_skip, ssm_norm, f_bias, fox_norm, w_out, ffn2_norm, ffn2_w_in, ffn2_w_out, final_norm):
    b, s, d = x.shape
    depth = ffn1_norm.shape[0]
    outs = []
    for bi in range(b):
        xb = x[bi]
        for l in range(depth):
            xb = _layer(xb, ffn1_norm[l], ffn1_w_in[l], ffn1_w_out[l], mix_norm[l], w_in[l],
                        conv_w[l], conv_b[l], dt_bias[l], a_log[l], d_skip[l], ssm_norm[l],
                        f_bias[l], fox_norm[l], w_out[l], ffn2_norm[l], ffn2_w_in[l],
                        ffn2_w_out[l], final_norm, final=(l == depth - 1))
        outs.append(xb)
    return jnp.stack(outs, axis=0)
```

```python
import functools

import jax
import jax.numpy as jnp
import numpy as np
from jax import lax
from jax.experimental import pallas as pl
from jax.experimental.pallas import tpu as pltpu

F32 = jnp.float32
BF16 = jnp.bfloat16
HIGHEST = lax.Precision.HIGHEST

D_MODEL = 1024
SEQ = 16384
HEAD_DIM = 64
SSM_WIDTH = 512
SSM_HEADS = 8
SSM_GROUPS = 2
SSM_STATE = 128
CONV_WIDTH = 4
SSM_XBC = SSM_WIDTH + 2 * SSM_GROUPS * SSM_STATE
FOX_WIDTH = 512
FOX_HEADS = 8
D_FF = 2816
EPS = 1e-6

LANES = 128
VMEM_LIMIT = 56 * 1024 * 1024

FFN_TM = 1024
FFN_TF = 256
PROJ_TM = 512
SSD_T = 256
ATT_TQ = 512
ATT_TK = 512
NEG_INF = float("-inf")
LOG2E = 1.4426950408889634
SKIP_LOG2 = -150.0
NORM_MARGIN = 1.05
FIXED_MAX_LOG2 = 64.0


def _rms(x, g):
    ms = jnp.mean(x * x, axis=-1, keepdims=True)
    return x * lax.rsqrt(ms + EPS) * g


def _silu(x):
    return x * (1.0 / (1.0 + jnp.exp(-x)))


def _softplus(x):
    return jnp.maximum(x, 0.0) + jnp.log1p(jnp.exp(-jnp.abs(x)))


def _split3(x):
    hi = x.astype(BF16).astype(F32)
    r = x - hi
    mid = r.astype(BF16).astype(F32)
    lo = (r - mid).astype(BF16).astype(F32)
    return hi, mid, lo


def _dot_f32_rhs(a_bf16, x):
    hi, mid, lo = _split3(x)
    d = lambda piece: jnp.dot(a_bf16, piece.astype(BF16), preferred_element_type=F32)
    return (d(lo) + d(mid)) + d(hi)


def _dot_f32_lhs(x, b_bf16):
    hi, mid, lo = _split3(x)
    d = lambda piece: jnp.dot(piece.astype(BF16), b_bf16, preferred_element_type=F32)
    return (d(lo) + d(mid)) + d(hi)


def _ffn_kernel(x_ref, g_ref, wg_ref, wu_ref, wo_ref, fg_ref, o_ref,
                h_ref, wg_res, wu_res, wo_res, *, nf, final):
    i = pl.program_id(0)
    f = pl.program_id(1)

    @pl.when(i == 0)
    def _():
        wg_res[f] = wg_ref[...].astype(BF16)
        wu_res[f] = wu_ref[...].astype(BF16)
        wo_res[f] = wo_ref[...].astype(BF16)

    @pl.when(f == 0)
    def _():
        h_ref[...] = _rms(x_ref[...], g_ref[...]).astype(BF16)
        o_ref[...] = jnp.zeros_like(o_ref)

    h = h_ref[...]
    gate = jnp.dot(h, wg_res[f], preferred_element_type=F32)
    up = jnp.dot(h, wu_res[f], preferred_element_type=F32)
    act = (_silu(gate) * up).astype(BF16)
    o_ref[...] += jnp.dot(act, wo_res[f], preferred_element_type=F32)

    @pl.when(f == nf - 1)
    def _():
        r = x_ref[...] + 0.5 * o_ref[...]
        if final:
            r = _rms(r, fg_ref[...])
        o_ref[...] = r


def _ffn(x, g, w_in, w_out, fg, *, final):
    s = x.shape[0]
    nf = D_FF // FFN_TF

    def wtile(i, f):
        return jnp.where(i == 0, f, nf - 1)

    return pl.pallas_call(
        functools.partial(_ffn_kernel, nf=nf, final=final),
        grid=(s // FFN_TM, nf),
        in_specs=[
            pl.BlockSpec((FFN_TM, D_MODEL), lambda i, f: (i, 0)),
            pl.BlockSpec((1, D_MODEL), lambda i, f: (0, 0)),
            pl.BlockSpec((D_MODEL, FFN_TF), lambda i, f: (0, wtile(i, f))),
            pl.BlockSpec((D_MODEL, FFN_TF), lambda i, f: (0, nf + wtile(i, f))),
            pl.BlockSpec((FFN_TF, D_MODEL), lambda i, f: (wtile(i, f), 0)),
            pl.BlockSpec((1, D_MODEL), lambda i, f: (0, 0)),
        ],
        out_specs=pl.BlockSpec((FFN_TM, D_MODEL), lambda i, f: (i, 0)),
        out_shape=jax.ShapeDtypeStruct((s, D_MODEL), F32),
        scratch_shapes=[
            pltpu.VMEM((FFN_TM, D_MODEL), BF16),
            pltpu.VMEM((nf, D_MODEL, FFN_TF), BF16),
            pltpu.VMEM((nf, D_MODEL, FFN_TF), BF16),
            pltpu.VMEM((nf, FFN_TF, D_MODEL), BF16),
        ],
        compiler_params=pltpu.CompilerParams(
            dimension_semantics=("arbitrary", "arbitrary"),
            vmem_limit_bytes=VMEM_LIMIT),
        name="ffn_final" if final else "ffn",
    )(x, g, w_in, w_in, w_out, fg)


def _inproj_kernel(x_ref, g_ref, wz_ref, wxbc_ref, wq_ref, wkt_ref, wv_ref, ws_ref,
                   ind_ref, indt_ref, cw_ref, cb_ref,
                   zs_ref, xs_ref, bc_ref, q_ref, kt_ref, v_ref, small_ref, qn_ref, kn_ref,
                   ext_ref):
    tm = PROJ_TM
    i = pl.program_id(0)

    @pl.when(i == 0)
    def _():
        ext_ref[0:8, :] = jnp.zeros((8, SSM_XBC), F32)

    h = _rms(x_ref[...], g_ref[...]).astype(BF16)
    ext_ref[8:8 + tm, :] = jnp.dot(h, wxbc_ref[...], preferred_element_type=F32)
    conv = cb_ref[...] + ext_ref[5:5 + tm, :] * cw_ref[0:1, :]
    for k in range(1, CONV_WIDTH):
        conv = conv + ext_ref[5 + k:5 + k + tm, :] * cw_ref[k:k + 1, :]
    ext_ref[0:8, :] = ext_ref[tm:tm + 8, :]
    u = _silu(conv)
    xs_ref[...] = u[:, :SSM_WIDTH]
    bc_ref[...] = u[:, SSM_WIDTH:].astype(BF16)
    zs_ref[...] = _silu(jnp.dot(h, wz_ref[...], preferred_element_type=F32))
    q = jnp.dot(h, wq_ref[...], preferred_element_type=F32)
    qb = (q * (HEAD_DIM ** -0.5 * LOG2E)).astype(BF16)
    q_ref[...] = qb
    kt = lax.dot_general(wkt_ref[...], h, (((1,), (1,)), ((), ())),
                         preferred_element_type=F32)
    ktb = kt.astype(BF16)
    kt_ref[0] = ktb
    v_ref[...] = jnp.dot(h, wv_ref[...], preferred_element_type=F32).astype(BF16)
    small_ref[...] = jnp.dot(h, ws_ref[...], preferred_element_type=F32)
    qf = qb.astype(F32)
    qn_ref[...] = jnp.dot((qf * qf).astype(BF16), ind_ref[...], preferred_element_type=F32)
    kf = ktb.astype(F32)
    kn_ref[...] = jnp.dot(indt_ref[...], (kf * kf).astype(BF16), preferred_element_type=F32)


def _inproj(x, g, wz, wxbc, wq, wkt, wv, ws, cw, cb):
    s = x.shape[0]
    tm = PROJ_TM
    nb = s // tm
    const = lambda i: (0, 0)
    row = lambda i: (i, 0)
    ind_np = np.zeros((FOX_WIDTH, LANES), np.float32)
    for hd in range(FOX_HEADS):
        ind_np[hd * HEAD_DIM:(hd + 1) * HEAD_DIM, hd] = 1.0
    ind = jnp.asarray(ind_np, dtype=BF16)
    indt = jnp.asarray(ind_np.T[:16], dtype=BF16)
    return pl.pallas_call(
        _inproj_kernel,
        grid=(nb,),
        in_specs=[
            pl.BlockSpec((tm, D_MODEL), row),
            pl.BlockSpec((1, D_MODEL), const),
            pl.BlockSpec((D_MODEL, SSM_WIDTH), const),
            pl.BlockSpec((D_MODEL, SSM_XBC), const),
            pl.BlockSpec((D_MODEL, FOX_WIDTH), const),
            pl.BlockSpec((FOX_WIDTH, D_MODEL), const),
            pl.BlockSpec((D_MODEL, FOX_WIDTH), const),
            pl.BlockSpec((D_MODEL, LANES), const),
            pl.BlockSpec((FOX_WIDTH, LANES), const),
            pl.BlockSpec((16, FOX_WIDTH), const),
            pl.BlockSpec((CONV_WIDTH, SSM_XBC), const),
            pl.BlockSpec((1, SSM_XBC), const),
        ],
        out_specs=[
            pl.BlockSpec((tm, SSM_WIDTH), row),
            pl.BlockSpec((tm, SSM_WIDTH), row),
            pl.BlockSpec((tm, SSM_XBC - SSM_WIDTH), row),
            pl.BlockSpec((tm, FOX_WIDTH), row),
            pl.BlockSpec((1, FOX_WIDTH, tm), lambda i: (i, 0, 0)),
            pl.BlockSpec((tm, FOX_WIDTH), row),
            pl.BlockSpec((tm, LANES), row),
            pl.BlockSpec((tm, LANES), row),
            pl.BlockSpec((16, tm), lambda i: (0, i)),
        ],
        out_shape=[
            jax.ShapeDtypeStruct((s, SSM_WIDTH), F32),
            jax.ShapeDtypeStruct((s, SSM_WIDTH), F32),
            jax.ShapeDtypeStruct((s, SSM_XBC - SSM_WIDTH), BF16),
            jax.ShapeDtypeStruct((s, FOX_WIDTH), BF16),
            jax.ShapeDtypeStruct((nb, FOX_WIDTH, tm), BF16),
            jax.ShapeDtypeStruct((s, FOX_WIDTH), BF16),
            jax.ShapeDtypeStruct((s, LANES), F32),
            jax.ShapeDtypeStruct((s, LANES), F32),
            jax.ShapeDtypeStruct((16, s), F32),
        ],
        scratch_shapes=[pltpu.VMEM((tm + 8, SSM_XBC), F32)],
        compiler_params=pltpu.CompilerParams(
            dimension_semantics=("arbitrary",),
            vmem_limit_bytes=VMEM_LIMIT),
        name="in_proj",
    )(x, g, wz, wxbc, wq, wkt, wv, ws, ind, indt, cw, cb)


def _ssd_kernel(zs_ref, xs_ref, bc_ref, small_ref, bias_ref, apad_ref,
                dskip_ref, ng_ref, tri_ref, expand_ref,
                y_ref, cum_ref, cumt_ref,
                state_ref, carry_ref):
    t_rows = SSD_T
    i = pl.program_id(0)

    @pl.when(i == 0)
    def _():
        state_ref[...] = jnp.zeros_like(state_ref)
        carry_ref[...] = jnp.zeros_like(carry_ref)

    xs = xs_ref[...]
    bm = bc_ref[:, 0:SSM_GROUPS * SSM_STATE]
    cm = bc_ref[:, SSM_GROUPS * SSM_STATE:]

    t = small_ref[...] + bias_ref[...]
    lane = lax.broadcasted_iota(jnp.int32, (t_rows, LANES), 1)
    is_dt = lane < SSM_HEADS
    sp = _softplus(jnp.where(is_dt, t, -t))
    v = jnp.where(is_dt, sp * apad_ref[...], -sp)
    cs = _dot_f32_rhs(tri_ref[...], v)
    cs = cs + carry_ref[...]
    last = cs[t_rows - 1:t_rows, :]
    carry_ref[...] = jnp.where(lane[0:1, :] < SSM_HEADS, 0.0, last)
    cum_ref[...] = cs
    cst = cs.T
    cumt_ref[...] = cst[0:16, :]

    stacked = jnp.concatenate([sp, cs], axis=0)
    exp2 = _dot_f32_lhs(stacked, expand_ref[...])
    dt_e = exp2[0:t_rows]
    cs_e = exp2[t_rows:2 * t_rows]
    ea_e = jnp.exp(cs_e)
    de_e = jnp.exp(cs_e[t_rows - 1:t_rows, :] - cs_e)
    cd_e = ea_e[t_rows - 1:t_rows, :]

    xdt = xs * dt_e
    xdt_b = xdt.astype(BF16)
    wst = (xdt * de_e).astype(BF16)

    r_i = lax.broadcasted_iota(jnp.int32, (t_rows, t_rows), 0)
    c_i = lax.broadcasted_iota(jnp.int32, (t_rows, t_rows), 1)
    causal = r_i >= c_i
    lane_pair = lax.broadcasted_iota(jnp.int32, (t_rows, LANES), 1)
    lo = lane_pair < HEAD_DIM

    gw = SSM_WIDTH // SSM_GROUPS
    hpg = SSM_HEADS // SSM_GROUPS
    y_parts = []
    for g in range(SSM_GROUPS):
        cg = cm[:, g * SSM_STATE:(g + 1) * SSM_STATE]
        bg = bm[:, g * SSM_STATE:(g + 1) * SSM_STATE]
        gmat = lax.dot_general(cg, bg, (((1,), (1,)), ((), ())),
                               preferred_element_type=F32)
        for pr in range(hpg // 2):
            c0 = g * gw + pr * LANES
            xpair = xdt_b[:, c0:c0 + LANES]
            acc = None
            for hh in range(2):
                h = g * hpg + pr * 2 + hh
                seg = cs[:, h:h + 1] - cst[h:h + 1, :]
                dec = jnp.exp(jnp.where(causal, seg, NEG_INF))
                m = (gmat * dec).astype(BF16)
                xm = jnp.where(lo if hh == 0 else jnp.logical_not(lo), xpair,
                               jnp.zeros_like(xpair))
                part = jnp.dot(m, xm, preferred_element_type=F32)
                acc = part if acc is None else acc + part
            y_parts.append(acc)
        s_prev = state_ref[g]
        y_off = jnp.dot(cg, s_prev.astype(BF16), preferred_element_type=F32)
        y_parts[-2] = y_parts[-2] + y_off[:, :LANES] * ea_e[:, g * gw:g * gw + LANES]
        y_parts[-1] = y_parts[-1] + y_off[:, LANES:] * ea_e[:, g * gw + LANES:(g + 1) * gw]
        bgt = bg.astype(F32).T.astype(BF16)
        upd = jnp.dot(bgt, wst[:, g * gw:(g + 1) * gw], preferred_element_type=F32)
        state_ref[g] = s_prev * cd_e[:, g * gw:(g + 1) * gw] + upd

    y = jnp.concatenate(y_parts, axis=1) + dskip_ref[...] * xs
    y = y * zs_ref[...]
    y_ref[...] = _rms(y, ng_ref[...]).astype(BF16)


def _ssd(zs, xs, bc, small, bias_pad, a_pad, dskip_e, ng, tri, expand):
    s = zs.shape[0]
    t = SSD_T
    const = lambda i: (0, 0)
    row = lambda i: (i, 0)
    return pl.pallas_call(
        _ssd_kernel,
        grid=(s // t,),
        in_specs=[
            pl.BlockSpec((t, SSM_WIDTH), row),
            pl.BlockSpec((t, SSM_WIDTH), row),
            pl.BlockSpec((t, SSM_XBC - SSM_WIDTH), row),
            pl.BlockSpec((t, LANES), row),
            pl.BlockSpec((1, LANES), const),
            pl.BlockSpec((1, LANES), const),
            pl.BlockSpec((1, SSM_WIDTH), const),
            pl.BlockSpec((1, SSM_WIDTH), const),
            pl.BlockSpec((t, t), const),
            pl.BlockSpec((LANES, SSM_WIDTH), const),
        ],
        out_specs=[
            pl.BlockSpec((t, SSM_WIDTH), row),
            pl.BlockSpec((t, LANES), row),
            pl.BlockSpec((16, t), lambda i: (0, i)),
        ],
        out_shape=[
            jax.ShapeDtypeStruct((s, SSM_WIDTH), BF16),
            jax.ShapeDtypeStruct((s, LANES), F32),
            jax.ShapeDtypeStruct((16, s), F32),
        ],
        scratch_shapes=[
            pltpu.VMEM((SSM_GROUPS, SSM_STATE, SSM_WIDTH // SSM_GROUPS), F32),
            pltpu.VMEM((1, LANES), F32),
        ],
        compiler_params=pltpu.CompilerParams(
            dimension_semantics=("arbitrary",),
            vmem_limit_bytes=VMEM_LIMIT),
        name="ssd",
    )(zs, xs, bc, small, bias_pad, a_pad, dskip_e, ng, tri, expand)


def _fox_kernel(nkv_ref, q_ref, kt_ref, v_ref, cum_ref, cumt_ref, o_ref,
                kaug_ref, m_ref, l_ref, acc_ref):
    tq, tk = ATT_TQ, ATT_TK
    nb = kt_ref.shape[0]
    nq = pl.num_programs(1)
    p = pl.program_id(0)
    i = pl.program_id(1)

    @pl.when(i == 0)
    def _():
        row16 = lax.broadcasted_iota(jnp.int32, (16, tk), 0)
        zeros48 = jnp.zeros((48, tk), BF16)

        def build(j, carry):
            off = pl.multiple_of(j * tk, tk)
            kt = kt_ref[j]
            for hh in range(2):
                ck = cumt_ref[pl.ds(FOX_HEADS + 2 * p + hh, 1), pl.ds(off, tk)] * LOG2E
                hi, mid, lo = _split3(ck)
                bias = jnp.where(row16 < 3, 1.0,
                                 jnp.where(row16 == 3, -hi,
                                           jnp.where(row16 == 4, -mid,
                                                     jnp.where(row16 == 5, -lo, 0.0))))
                bias = bias.astype(BF16)
                if hh == 0:
                    kaug_ref[0, j, 0:64, :] = kt[0:64]
                    kaug_ref[0, j, 64:80, :] = bias
                    kaug_ref[0, j, 80:128, :] = zeros48
                else:
                    kaug_ref[1, j, 0:16, :] = bias
                    kaug_ref[1, j, 16:64, :] = zeros48
                    kaug_ref[1, j, 64:128, :] = kt[64:128]
            return carry

        lax.fori_loop(0, nb, build, 0)

    q = q_ref[...]
    lane = lax.broadcasted_iota(jnp.int32, (tq, LANES), 1)
    lo_half = lane < HEAD_DIM
    r_i = lax.broadcasted_iota(jnp.int32, (tq, tk), 0)
    c_i = lax.broadcasted_iota(jnp.int32, (tq, tk), 1)
    causal = r_i >= c_i

    pr_i = lax.broadcasted_iota(jnp.int32, (LANES, LANES), 0)
    pc_i = lax.broadcasted_iota(jnp.int32, (LANES, LANES), 1)
    src_lane = jnp.where(pc_i >= HEAD_DIM, FOX_HEADS + 2 * p, FOX_HEADS + 2 * p + 1)
    place = jnp.where(pr_i == src_lane, 1.0, 0.0).astype(BF16)
    pieces = [jnp.dot(piece.astype(BF16), place, preferred_element_type=F32)
              for piece in _split3(cum_ref[...] * LOG2E)]
    lm = lane & (HEAD_DIM - 1)
    aug = jnp.where(lm == 0, pieces[0],
                    jnp.where(lm == 1, pieces[1],
                              jnp.where(lm == 2, pieces[2],
                                        jnp.where(lm < 6, 1.0, 0.0)))).astype(BF16)
    qas = [jnp.where(lo_half, q, aug), jnp.where(lo_half, aug, q)]

    def logits(hh, j):
        return jnp.dot(qas[hh], kaug_ref[hh, j], preferred_element_type=F32)

    def vblock(j):
        return v_ref[pl.ds(pl.multiple_of(j * tk, tk), tk), :]

    def lane_fold(x):
        out = x[:, 0:LANES]
        for c in range(1, x.shape[1] // LANES):
            out = out + x[:, c * LANES:(c + 1) * LANES]
        return out

    def online_diag(hh):
        s = jnp.where(causal, logits(hh, i), NEG_INF)
        m0 = jnp.max(s, axis=1, keepdims=True)
        p0 = jnp.exp2(s - m0)
        m_ref[hh] = jnp.broadcast_to(m0, (tq, LANES))
        l_ref[hh] = lane_fold(p0)
        acc_ref[hh] = jnp.dot(p0.astype(BF16), vblock(i), preferred_element_type=F32)

    def online_step(hh, j):
        s = logits(hh, j)
        m_prev = m_ref[hh]
        m_new = jnp.maximum(m_prev, jnp.max(s, axis=1, keepdims=True))
        alpha = jnp.exp2(m_prev - m_new)
        pj = jnp.exp2(s - m_new[:, 0:1])
        l_ref[hh] = alpha * l_ref[hh] + lane_fold(pj)
        acc_ref[hh] = alpha * acc_ref[hh] + jnp.dot(
            pj.astype(BF16), vblock(j), preferred_element_type=F32)
        m_ref[hh] = m_new

    half = tq // 2

    def fixed_diag(hh):
        vb = vblock(i)
        s_top = jnp.dot(qas[hh][0:half], kaug_ref[hh, i, :, 0:half],
                        preferred_element_type=F32)
        p_top = jnp.exp2(jnp.where(causal[0:half, 0:half], s_top, NEG_INF))
        s_bot = jnp.dot(qas[hh][half:], kaug_ref[hh, i], preferred_element_type=F32)
        p_bot = jnp.exp2(jnp.where(causal[half:, :], s_bot, NEG_INF))
        l_ref[hh, 0:half] = lane_fold(p_top)
        l_ref[hh, half:] = lane_fold(p_bot)
        acc_ref[hh, 0:half] = jnp.dot(p_top.astype(BF16), vb[0:half],
                                      preferred_element_type=F32)
        acc_ref[hh, half:] = jnp.dot(p_bot.astype(BF16), vb, preferred_element_type=F32)

    def fixed_step(hh, j):
        pj = jnp.exp2(logits(hh, j))
        l_ref[hh] += lane_fold(pj)
        acc_ref[hh] += jnp.dot(pj.astype(BF16), vblock(j), preferred_element_type=F32)

    for hh in range(2):
        code = nkv_ref[(2 * p + hh) * nq + i]
        n_off = code >> 1
        fixed_ok = (code & 1) == 1

        @pl.when(fixed_ok)
        def _(hh=hh, n_off=n_off):
            fixed_diag(hh)

            def body(jj, carry):
                fixed_step(hh, i - 1 - jj)
                return carry

            lax.fori_loop(0, n_off, body, 0)

        @pl.when(jnp.logical_not(fixed_ok))
        def _(hh=hh, n_off=n_off):
            online_diag(hh)

            def body(jj, carry):
                online_step(hh, i - 1 - jj)
                return carry

            lax.fori_loop(0, n_off, body, 0)

    l0 = jnp.sum(l_ref[0], axis=1, keepdims=True)
    l1 = jnp.sum(l_ref[1], axis=1, keepdims=True)
    o_ref[...] = jnp.where(lo_half, acc_ref[0] / l0, acc_ref[1] / l1)


def _kv_counts(qn2, kn2, cum):
    tq = ATT_TQ
    nq = cum.shape[0] // tq
    qn = jnp.sqrt(jnp.max(qn2[:, :FOX_HEADS].reshape(nq, tq, FOX_HEADS), axis=1))
    kn = jnp.sqrt(jnp.max(kn2[:FOX_HEADS], axis=1))
    c = (cum[:, FOX_HEADS:2 * FOX_HEADS] * LOG2E).reshape(nq, tq, FOX_HEADS)
    cmax = jnp.max(c, axis=1)
    pmin = lax.cummin(jnp.min(c, axis=1), axis=0)
    qk = NORM_MARGIN * qn * kn[None, :] + 1.0
    bound = (2.0 * qk + cmax)[:, None, :] - pmin[None, :, :]
    ii = lax.broadcasted_iota(jnp.int32, bound.shape, 0)
    jj = lax.broadcasted_iota(jnp.int32, bound.shape, 1)
    skip = (bound < SKIP_LOG2) & (jj < ii)
    jstar = jnp.max(jnp.where(skip, jj, -1), axis=1)
    n_off = jnp.arange(nq, dtype=jnp.int32)[:, None] - 1 - jstar
    pmin_prev = jnp.concatenate([jnp.full((1, FOX_HEADS), jnp.inf, F32), pmin[:-1]], axis=0)
    fixed_ok = (qk + jnp.maximum(cmax - pmin_prev, 0.0)) < FIXED_MAX_LOG2
    code = 2 * n_off + fixed_ok.astype(jnp.int32)
    return code.T.reshape(-1).astype(jnp.int32)


def _fox(nkv, q, kt3, v, cum, cumt):
    s = q.shape[0]
    tq, tk = ATT_TQ, ATT_TK
    nb = s // tk
    grid_spec = pltpu.PrefetchScalarGridSpec(
        num_scalar_prefetch=1,
        grid=(FOX_HEADS // 2, s // tq),
        in_specs=[
            pl.BlockSpec((tq, LANES), lambda p, i, n: (i, p)),
            pl.BlockSpec((nb, LANES, tk), lambda p, i, n: (0, p, 0)),
            pl.BlockSpec((s, LANES), lambda p, i, n: (0, p)),
            pl.BlockSpec((tq, LANES), lambda p, i, n: (i, 0)),
            pl.BlockSpec((16, s), lambda p, i, n: (0, 0)),
        ],
        out_specs=pl.BlockSpec((tq, LANES), lambda p, i, n: (i, p)),
        scratch_shapes=[
            pltpu.VMEM((2, nb, LANES, tk), BF16),
            pltpu.VMEM((2, tq, LANES), F32),
            pltpu.VMEM((2, tq, LANES), F32),
            pltpu.VMEM((2, tq, LANES), F32),
        ],
    )
    return pl.pallas_call(
        _fox_kernel,
        grid_spec=grid_spec,
        out_shape=jax.ShapeDtypeStruct((s, FOX_WIDTH), F32),
        compiler_params=pltpu.CompilerParams(
            dimension_semantics=("arbitrary", "arbitrary"),
            vmem_limit_bytes=VMEM_LIMIT),
        name="fox_attn",
    )(nkv, q, kt3, v, cum, cumt)


def _outproj_kernel(x_ref, ys_ref, of_ref, fg_ref, w1_ref, w2_ref, o_ref):
    yf = _rms(of_ref[...], fg_ref[...]).astype(BF16)
    o_ref[...] = (x_ref[...]
                  + jnp.dot(ys_ref[...], w1_ref[...], preferred_element_type=F32)
                  + jnp.dot(yf, w2_ref[...], preferred_element_type=F32))


def _outproj(x, ys, of, fg, w1, w2):
    s = x.shape[0]
    tm = PROJ_TM
    const = lambda i: (0, 0)
    row = lambda i: (i, 0)
    return pl.pallas_call(
        _outproj_kernel,
        grid=(s // tm,),
        in_specs=[
            pl.BlockSpec((tm, D_MODEL), row),
            pl.BlockSpec((tm, SSM_WIDTH), row),
            pl.BlockSpec((tm, FOX_WIDTH), row),
            pl.BlockSpec((1, FOX_WIDTH), const),
            pl.BlockSpec((SSM_WIDTH, D_MODEL), const),
            pl.BlockSpec((FOX_WIDTH, D_MODEL), const),
        ],
        out_specs=pl.BlockSpec((tm, D_MODEL), row),
        out_shape=jax.ShapeDtypeStruct((s, D_MODEL), F32),
        compiler_params=pltpu.CompilerParams(
            dimension_semantics=("parallel",),
            vmem_limit_bytes=VMEM_LIMIT),
        name="out_proj",
    )(x, ys, of, fg, w1, w2)


def _pad_lanes(vec_dt, vec_f):
    out = jnp.zeros((1, LANES), F32)
    out = out.at[0, 0:SSM_HEADS].set(vec_dt.astype(F32))
    if vec_f is not None:
        out = out.at[0, SSM_HEADS:SSM_HEADS + FOX_HEADS].set(vec_f.astype(F32))
    return out


def _layer(x, ffn1_norm, ffn1_w_in, ffn1_w_out, mix_norm, w_in, conv_w, conv_b, dt_bias,
           a_log, d_skip, ssm_norm, f_bias, fox_norm, w_out, ffn2_norm, ffn2_w_in,
           ffn2_w_out, final_g, *, final):
    ones = jnp.ones((1, D_MODEL), F32)
    row = lambda a: a.reshape(1, -1).astype(F32)

    x1 = _ffn(x, row(ffn1_norm), ffn1_w_in, ffn1_w_out, ones, final=False)

    o0 = SSM_WIDTH
    o1 = o0 + SSM_XBC
    o2 = o1 + SSM_HEADS
    o3 = o2 + 3 * FOX_WIDTH
    wz = w_in[:, :o0].astype(BF16)
    wxbc = w_in[:, o0:o1].astype(BF16)
    w_dt = w_in[:, o1:o2]
    wq = w_in[:, o2:o2 + FOX_WIDTH].astype(BF16)
    wkt = w_in[:, o2 + FOX_WIDTH:o2 + 2 * FOX_WIDTH].T.astype(BF16)
    wv = w_in[:, o2 + 2 * FOX_WIDTH:o3].astype(BF16)
    w_f = w_in[:, o3:]
    ws = jnp.concatenate(
        [w_dt, w_f, jnp.zeros((D_MODEL, LANES - SSM_HEADS - FOX_HEADS), w_in.dtype)],
        axis=1).astype(BF16)
    zs, xs, bc, q, kt3, v, small, qn2, kn2 = _inproj(
        x1, row(mix_norm), wz, wxbc, wq, wkt, wv, ws, conv_w.astype(F32), row(conv_b))

    bias_pad = _pad_lanes(dt_bias, f_bias)
    a_pad = _pad_lanes(-jnp.exp(a_log.astype(F32)), None)
    dskip_e = jnp.repeat(d_skip.astype(F32), HEAD_DIM).reshape(1, SSM_WIDTH)
    tri = jnp.asarray(np.tril(np.ones((SSD_T, SSD_T), np.float32)), dtype=BF16)
    expand_np = np.zeros((LANES, SSM_WIDTH), np.float32)
    for hd in range(SSM_HEADS):
        expand_np[hd, hd * HEAD_DIM:(hd + 1) * HEAD_DIM] = 1.0
    y_ssd, cum, cumt = _ssd(zs, xs, bc, small, bias_pad, a_pad,
                            dskip_e, row(ssm_norm), tri, jnp.asarray(expand_np, dtype=BF16))

    o_fox = _fox(_kv_counts(qn2, kn2, cum), q, kt3, v, cum, cumt)

    wo = w_out.astype(BF16)
    x2 = _outproj(x1, y_ssd, o_fox, row(fox_norm), wo[:SSM_WIDTH], wo[SSM_WIDTH:])

    return _ffn(x2, row(ffn2_norm), ffn2_w_in, ffn2_w_out, row(final_g), final=final)


def kernel(x, ffn1_norm, ffn1_w_in, ffn1_w_out, mix_norm, w_in, conv_w, conv_b, dt_bias, a_log,
           d_skip, ssm_norm, f_bias, fox_norm, w_out, ffn2_norm, ffn2_w_in, ffn2_w_out, final_norm):
    b, s, d = x.shape
    depth = ffn1_norm.shape[0]
    outs = []
    for bi in range(b):
        xb = x[bi]
        for l in range(depth):
            xb = _layer(xb, ffn1_norm[l], ffn1_w_in[l], ffn1_w_out[l], mix_norm[l], w_in[l],
                        conv_w[l], conv_b[l], dt_bias[l], a_log[l], d_skip[l], ssm_norm[l],
                        f_bias[l], fox_norm[l], w_out[l], ffn2_norm[l], ffn2_w_in[l],
                        ffn2_w_out[l], final_norm, final=(l == depth - 1))
        outs.append(xb)
    return jnp.stack(outs, axis=0)
```

```python
import functools

import jax
import jax.numpy as jnp
import numpy as np
from jax import lax
from jax.experimental import pallas as pl
from jax.experimental.pallas import tpu as pltpu

F32 = jnp.float32
BF16 = jnp.bfloat16
HIGHEST = lax.Precision.HIGHEST

D_MODEL = 1024
SEQ = 16384
HEAD_DIM = 64
SSM_WIDTH = 512
SSM_HEADS = 8
SSM_GROUPS = 2
SSM_STATE = 128
CONV_WIDTH = 4
SSM_XBC = SSM_WIDTH + 2 * SSM_GROUPS * SSM_STATE
FOX_WIDTH = 512
FOX_HEADS = 8
D_FF = 2816
EPS = 1e-6

LANES = 128
VMEM_LIMIT = 56 * 1024 * 1024

FFN_TM = 1024
FFN_TF = 256
PROJ_TM = 512
SSD_T = 256
ATT_TQ = 512
ATT_TK = 512
NEG_INF = float("-inf")
LOG2E = 1.4426950408889634
SKIP_LOG2 = -150.0
NORM_MARGIN = 1.05
FIXED_MAX_LOG2 = 64.0


def _rms(x, g):
    ms = jnp.mean(x * x, axis=-1, keepdims=True)
    return x * lax.rsqrt(ms + EPS) * g


def _silu(x):
    return x * (1.0 / (1.0 + jnp.exp(-x)))


def _softplus(x):
    return jnp.maximum(x, 0.0) + jnp.log1p(jnp.exp(-jnp.abs(x)))


def _split3(x):
    hi = x.astype(BF16).astype(F32)
    r = x - hi
    mid = r.astype(BF16).astype(F32)
    lo = (r - mid).astype(BF16).astype(F32)
    return hi, mid, lo


def _dot_f32_rhs(a_bf16, x):
    hi, mid, lo = _split3(x)
    d = lambda piece: jnp.dot(a_bf16, piece.astype(BF16), preferred_element_type=F32)
    return (d(lo) + d(mid)) + d(hi)


def _dot_f32_lhs(x, b_bf16):
    hi, mid, lo = _split3(x)
    d = lambda piece: jnp.dot(piece.astype(BF16), b_bf16, preferred_element_type=F32)
    return (d(lo) + d(mid)) + d(hi)


def _ffn_kernel(*refs, mix, final):
    if mix:
        (x_ref, ys_ref, of_ref, mg_ref, wm1_ref, wm2_ref,
         g_ref, wg_ref, wu_ref, wo_ref, fg_ref, o_ref) = refs
        yf = _rms(of_ref[...], mg_ref[...]).astype(BF16)
        x = (x_ref[...]
             + jnp.dot(ys_ref[...], wm1_ref[...], preferred_element_type=F32)
             + jnp.dot(yf, wm2_ref[...], preferred_element_type=F32))
    else:
        x_ref, g_ref, wg_ref, wu_ref, wo_ref, fg_ref, o_ref = refs
        x = x_ref[...]
    h = _rms(x, g_ref[...]).astype(BF16)
    o_ref[...] = 2.0 * x
    for f in range(D_FF // FFN_TF):
        cols = slice(f * FFN_TF, (f + 1) * FFN_TF)
        gate = jnp.dot(h, wg_ref[:, cols], preferred_element_type=F32)
        up = jnp.dot(h, wu_ref[:, cols], preferred_element_type=F32)
        act = (_silu(gate) * up).astype(BF16)
        o_ref[...] += jnp.dot(act, wo_ref[cols, :], preferred_element_type=F32)
    r = 0.5 * o_ref[...]
    if final:
        r = _rms(r, fg_ref[...])
    o_ref[...] = r


def _ffn(x, g, w_in, w_out, fg, *, final, mix=None):
    s = x.shape[0]
    tm = FFN_TM
    row = lambda i: (i, 0)
    const = lambda i: (0, 0)
    resident = functools.partial(pl.BlockSpec, pipeline_mode=pl.Buffered(1))
    x_spec = pl.BlockSpec((tm, D_MODEL), row)
    vec_spec = pl.BlockSpec((1, D_MODEL), const)
    ffn_specs = [
        vec_spec,
        resident((D_MODEL, D_FF), lambda i: (0, 0)),
        resident((D_MODEL, D_FF), lambda i: (0, 1)),
        resident((D_FF, D_MODEL), const),
        vec_spec,
    ]
    ffn_args = (g, w_in, w_in, w_out, fg)
    if mix is None:
        in_specs = [x_spec] + ffn_specs
        args = (x,) + ffn_args
    else:
        ys, of, mg, wm1, wm2 = mix
        in_specs = [
            x_spec,
            pl.BlockSpec((tm, SSM_WIDTH), row),
            pl.BlockSpec((tm, FOX_WIDTH), row),
            pl.BlockSpec((1, FOX_WIDTH), const),
            resident((SSM_WIDTH, D_MODEL), const),
            resident((FOX_WIDTH, D_MODEL), const),
        ] + ffn_specs
        args = (x, ys, of, mg, wm1, wm2) + ffn_args
    return pl.pallas_call(
        functools.partial(_ffn_kernel, mix=mix is not None, final=final),
        grid=(s // tm,),
        in_specs=in_specs,
        out_specs=pl.BlockSpec((tm, D_MODEL), row),
        out_shape=jax.ShapeDtypeStruct((s, D_MODEL), F32),
        compiler_params=pltpu.CompilerParams(
            dimension_semantics=("parallel",),
            vmem_limit_bytes=VMEM_LIMIT),
        name="ffn_final" if final else "ffn",
    )(*args)


def _inproj_kernel(x_ref, g_ref, wz_ref, wxbc_ref, wq_ref, wkt_ref, wv_ref, ws_ref,
                   ind_ref, indt_ref, cw_ref, cb_ref,
                   zs_ref, xs_ref, bc_ref, q_ref, kt_ref, v_ref, small_ref, qn_ref, kn_ref,
                   ext_ref):
    tm = PROJ_TM
    i = pl.program_id(0)

    @pl.when(i == 0)
    def _():
        ext_ref[0:8, :] = jnp.zeros((8, SSM_XBC), F32)

    h = _rms(x_ref[...], g_ref[...]).astype(BF16)
    ext_ref[8:8 + tm, :] = jnp.dot(h, wxbc_ref[...], preferred_element_type=F32)
    conv = cb_ref[...] + ext_ref[5:5 + tm, :] * cw_ref[0:1, :]
    for k in range(1, CONV_WIDTH):
        conv = conv + ext_ref[5 + k:5 + k + tm, :] * cw_ref[k:k + 1, :]
    ext_ref[0:8, :] = ext_ref[tm:tm + 8, :]
    u = _silu(conv)
    xs_ref[...] = u[:, :SSM_WIDTH]
    bc_ref[...] = u[:, SSM_WIDTH:].astype(BF16)
    zs_ref[...] = _silu(jnp.dot(h, wz_ref[...], preferred_element_type=F32))
    q = jnp.dot(h, wq_ref[...], preferred_element_type=F32)
    qb = (q * (HEAD_DIM ** -0.5 * LOG2E)).astype(BF16)
    q_ref[...] = qb
    kt = lax.dot_general(wkt_ref[...], h, (((1,), (1,)), ((), ())),
                         preferred_element_type=F32)
    ktb = kt.astype(BF16)
    kt_ref[0] = ktb
    v_ref[...] = jnp.dot(h, wv_ref[...], preferred_element_type=F32).astype(BF16)
    small_ref[...] = jnp.dot(h, ws_ref[...], preferred_element_type=F32)
    qf = qb.astype(F32)
    qn_ref[...] = jnp.dot((qf * qf).astype(BF16), ind_ref[...], preferred_element_type=F32)
    kf = ktb.astype(F32)
    kn_ref[...] = jnp.dot(indt_ref[...], (kf * kf).astype(BF16), preferred_element_type=F32)


def _inproj(x, g, wz, wxbc, wq, wkt, wv, ws, cw, cb):
    s = x.shape[0]
    tm = PROJ_TM
    nb = s // tm
    const = lambda i: (0, 0)
    row = lambda i: (i, 0)
    ind_np = np.zeros((FOX_WIDTH, LANES), np.float32)
    for hd in range(FOX_HEADS):
        ind_np[hd * HEAD_DIM:(hd + 1) * HEAD_DIM, hd] = 1.0
    ind = jnp.asarray(ind_np, dtype=BF16)
    indt = jnp.asarray(ind_np.T[:16], dtype=BF16)
    return pl.pallas_call(
        _inproj_kernel,
        grid=(nb,),
        in_specs=[
            pl.BlockSpec((tm, D_MODEL), row),
            pl.BlockSpec((1, D_MODEL), const),
            pl.BlockSpec((D_MODEL, SSM_WIDTH), const),
            pl.BlockSpec((D_MODEL, SSM_XBC), const),
            pl.BlockSpec((D_MODEL, FOX_WIDTH), const),
            pl.BlockSpec((FOX_WIDTH, D_MODEL), const),
            pl.BlockSpec((D_MODEL, FOX_WIDTH), const),
            pl.BlockSpec((D_MODEL, LANES), const),
            pl.BlockSpec((FOX_WIDTH, LANES), const),
            pl.BlockSpec((16, FOX_WIDTH), const),
            pl.BlockSpec((CONV_WIDTH, SSM_XBC), const),
            pl.BlockSpec((1, SSM_XBC), const),
        ],
        out_specs=[
            pl.BlockSpec((tm, SSM_WIDTH), row),
            pl.BlockSpec((tm, SSM_WIDTH), row),
            pl.BlockSpec((tm, SSM_XBC - SSM_WIDTH), row),
            pl.BlockSpec((tm, FOX_WIDTH), row),
            pl.BlockSpec((1, FOX_WIDTH, tm), lambda i: (i, 0, 0)),
            pl.BlockSpec((tm, FOX_WIDTH), row),
            pl.BlockSpec((tm, LANES), row),
            pl.BlockSpec((tm, LANES), row),
            pl.BlockSpec((16, tm), lambda i: (0, i)),
        ],
        out_shape=[
            jax.ShapeDtypeStruct((s, SSM_WIDTH), F32),
            jax.ShapeDtypeStruct((s, SSM_WIDTH), F32),
            jax.ShapeDtypeStruct((s, SSM_XBC - SSM_WIDTH), BF16),
            jax.ShapeDtypeStruct((s, FOX_WIDTH), BF16),
            jax.ShapeDtypeStruct((nb, FOX_WIDTH, tm), BF16),
            jax.ShapeDtypeStruct((s, FOX_WIDTH), BF16),
            jax.ShapeDtypeStruct((s, LANES), F32),
            jax.ShapeDtypeStruct((s, LANES), F32),
            jax.ShapeDtypeStruct((16, s), F32),
        ],
        scratch_shapes=[pltpu.VMEM((tm + 8, SSM_XBC), F32)],
        compiler_params=pltpu.CompilerParams(
            dimension_semantics=("arbitrary",),
            vmem_limit_bytes=VMEM_LIMIT),
        name="in_proj",
    )(x, g, wz, wxbc, wq, wkt, wv, ws, ind, indt, cw, cb)


def _ssd_kernel(zs_ref, xs_ref, bc_ref, small_ref, bias_ref, apad_ref,
                dskip_ref, ng_ref, tri_ref, expand_ref,
                y_ref, cum_ref, cumt_ref,
                state_ref, carry_ref):
    t_rows = SSD_T
    i = pl.program_id(0)

    @pl.when(i == 0)
    def _():
        state_ref[...] = jnp.zeros_like(state_ref)
        carry_ref[...] = jnp.zeros_like(carry_ref)

    xs = xs_ref[...]
    bm = bc_ref[:, 0:SSM_GROUPS * SSM_STATE]
    cm = bc_ref[:, SSM_GROUPS * SSM_STATE:]

    t = small_ref[...] + bias_ref[...]
    lane = lax.broadcasted_iota(jnp.int32, (t_rows, LANES), 1)
    is_dt = lane < SSM_HEADS
    sp = _softplus(jnp.where(is_dt, t, -t))
    v = jnp.where(is_dt, sp * apad_ref[...], -sp)
    cs = _dot_f32_rhs(tri_ref[...], v)
    cs = cs + carry_ref[...]
    last = cs[t_rows - 1:t_rows, :]
    carry_ref[...] = jnp.where(lane[0:1, :] < SSM_HEADS, 0.0, last)
    cum_ref[...] = cs
    cst = cs.T
    cumt_ref[...] = cst[0:16, :]

    stacked = jnp.concatenate([sp, cs], axis=0)
    exp2 = _dot_f32_lhs(stacked, expand_ref[...])
    dt_e = exp2[0:t_rows]
    cs_e = exp2[t_rows:2 * t_rows]
    ea_e = jnp.exp(cs_e)
    de_e = jnp.exp(cs_e[t_rows - 1:t_rows, :] - cs_e)
    cd_e = ea_e[t_rows - 1:t_rows, :]

    xdt = xs * dt_e
    xdt_b = xdt.astype(BF16)
    wst = (xdt * de_e).astype(BF16)

    r_i = lax.broadcasted_iota(jnp.int32, (t_rows, t_rows), 0)
    c_i = lax.broadcasted_iota(jnp.int32, (t_rows, t_rows), 1)
    causal = r_i >= c_i
    lane_pair = lax.broadcasted_iota(jnp.int32, (t_rows, LANES), 1)
    lo = lane_pair < HEAD_DIM

    gw = SSM_WIDTH // SSM_GROUPS
    hpg = SSM_HEADS // SSM_GROUPS
    y_parts = []
    for g in range(SSM_GROUPS):
        cg = cm[:, g * SSM_STATE:(g + 1) * SSM_STATE]
        bg = bm[:, g * SSM_STATE:(g + 1) * SSM_STATE]
        gmat = lax.dot_general(cg, bg, (((1,), (1,)), ((), ())),
                               preferred_element_type=F32)
        for pr in range(hpg // 2):
            c0 = g * gw + pr * LANES
            xpair = xdt_b[:, c0:c0 + LANES]
            acc = None
            for hh in range(2):
                h = g * hpg + pr * 2 + hh
                seg = cs[:, h:h + 1] - cst[h:h + 1, :]
                dec = jnp.exp(jnp.where(causal, seg, NEG_INF))
                m = (gmat * dec).astype(BF16)
                xm = jnp.where(lo if hh == 0 else jnp.logical_not(lo), xpair,
                               jnp.zeros_like(xpair))
                part = jnp.dot(m, xm, preferred_element_type=F32)
                acc = part if acc is None else acc + part
            y_parts.append(acc)
        s_prev = state_ref[g]
        y_off = jnp.dot(cg, s_prev.astype(BF16), preferred_element_type=F32)
        y_parts[-2] = y_parts[-2] + y_off[:, :LANES] * ea_e[:, g * gw:g * gw + LANES]
        y_parts[-1] = y_parts[-1] + y_off[:, LANES:] * ea_e[:, g * gw + LANES:(g + 1) * gw]
        bgt = bg.astype(F32).T.astype(BF16)
        upd = jnp.dot(bgt, wst[:, g * gw:(g + 1) * gw], preferred_element_type=F32)
        state_ref[g] = s_prev * cd_e[:, g * gw:(g + 1) * gw] + upd

    y = jnp.concatenate(y_parts, axis=1) + dskip_ref[...] * xs
    y = y * zs_ref[...]
    y_ref[...] = _rms(y, ng_ref[...]).astype(BF16)


def _ssd(zs, xs, bc, small, bias_pad, a_pad, dskip_e, ng, tri, expand):
    s = zs.shape[0]
    t = SSD_T
    const = lambda i: (0, 0)
    row = lambda i: (i, 0)
    return pl.pallas_call(
        _ssd_kernel,
        grid=(s // t,),
        in_specs=[
            pl.BlockSpec((t, SSM_WIDTH), row),
            pl.BlockSpec((t, SSM_WIDTH), row),
            pl.BlockSpec((t, SSM_XBC - SSM_WIDTH), row),
            pl.BlockSpec((t, LANES), row),
            pl.BlockSpec((1, LANES), const),
            pl.BlockSpec((1, LANES), const),
            pl.BlockSpec((1, SSM_WIDTH), const),
            pl.BlockSpec((1, SSM_WIDTH), const),
            pl.BlockSpec((t, t), const),
            pl.BlockSpec((LANES, SSM_WIDTH), const),
        ],
        out_specs=[
            pl.BlockSpec((t, SSM_WIDTH), row),
            pl.BlockSpec((t, LANES), row),
            pl.BlockSpec((16, t), lambda i: (0, i)),
        ],
        out_shape=[
            jax.ShapeDtypeStruct((s, SSM_WIDTH), BF16),
            jax.ShapeDtypeStruct((s, LANES), F32),
            jax.ShapeDtypeStruct((16, s), F32),
        ],
        scratch_shapes=[
            pltpu.VMEM((SSM_GROUPS, SSM_STATE, SSM_WIDTH // SSM_GROUPS), F32),
            pltpu.VMEM((1, LANES), F32),
        ],
        compiler_params=pltpu.CompilerParams(
            dimension_semantics=("arbitrary",),
            vmem_limit_bytes=VMEM_LIMIT),
        name="ssd",
    )(zs, xs, bc, small, bias_pad, a_pad, dskip_e, ng, tri, expand)


def _fox_kernel(nkv_ref, q_ref, kt_ref, v_ref, cum_ref, cumt_ref, o_ref,
                kaug_ref, m_ref, l_ref, acc_ref):
    tq, tk = ATT_TQ, ATT_TK
    nb = kt_ref.shape[0]
    nq = pl.num_programs(1)
    p = pl.program_id(0)
    i = pl.program_id(1)

    @pl.when(i == 0)
    def _():
        row16 = lax.broadcasted_iota(jnp.int32, (16, tk), 0)
        zeros48 = jnp.zeros((48, tk), BF16)

        def build(j, carry):
            off = pl.multiple_of(j * tk, tk)
            kt = kt_ref[j]
            for hh in range(2):
                ck = cumt_ref[pl.ds(FOX_HEADS + 2 * p + hh, 1), pl.ds(off, tk)] * LOG2E
                hi, mid, lo = _split3(ck)
                bias = jnp.where(row16 < 3, 1.0,
                                 jnp.where(row16 == 3, -hi,
                                           jnp.where(row16 == 4, -mid,
                                                     jnp.where(row16 == 5, -lo, 0.0))))
                bias = bias.astype(BF16)
                if hh == 0:
                    kaug_ref[0, j, 0:64, :] = kt[0:64]
                    kaug_ref[0, j, 64:80, :] = bias
                    kaug_ref[0, j, 80:128, :] = zeros48
                else:
                    kaug_ref[1, j, 0:16, :] = bias
                    kaug_ref[1, j, 16:64, :] = zeros48
                    kaug_ref[1, j, 64:128, :] = kt[64:128]
            return carry

        lax.fori_loop(0, nb, build, 0)

    q = q_ref[...]
    lane = lax.broadcasted_iota(jnp.int32, (tq, LANES), 1)
    lo_half = lane < HEAD_DIM
    r_i = lax.broadcasted_iota(jnp.int32, (tq, tk), 0)
    c_i = lax.broadcasted_iota(jnp.int32, (tq, tk), 1)
    causal = r_i >= c_i

    pr_i = lax.broadcasted_iota(jnp.int32, (LANES, LANES), 0)
    pc_i = lax.broadcasted_iota(jnp.int32, (LANES, LANES), 1)
    src_lane = jnp.where(pc_i >= HEAD_DIM, FOX_HEADS + 2 * p, FOX_HEADS + 2 * p + 1)
    place = jnp.where(pr_i == src_lane, 1.0, 0.0).astype(BF16)
    pieces = [jnp.dot(piece.astype(BF16), place, preferred_element_type=F32)
              for piece in _split3(cum_ref[...] * LOG2E)]
    lm = lane & (HEAD_DIM - 1)
    aug = jnp.where(lm == 0, pieces[0],
                    jnp.where(lm == 1, pieces[1],
                              jnp.where(lm == 2, pieces[2],
                                        jnp.where(lm < 6, 1.0, 0.0)))).astype(BF16)
    qas = [jnp.where(lo_half, q, aug), jnp.where(lo_half, aug, q)]

    def logits(hh, j):
        return jnp.dot(qas[hh], kaug_ref[hh, j], preferred_element_type=F32)

    def vblock(j):
        return v_ref[pl.ds(pl.multiple_of(j * tk, tk), tk), :]

    def lane_fold(x):
        out = x[:, 0:LANES]
        for c in range(1, x.shape[1] // LANES):
            out = out + x[:, c * LANES:(c + 1) * LANES]
        return out

    def online_diag(hh):
        s = jnp.where(causal, logits(hh, i), NEG_INF)
        m0 = jnp.max(s, axis=1, keepdims=True)
        p0 = jnp.exp2(s - m0)
        m_ref[hh] = jnp.broadcast_to(m0, (tq, LANES))
        l_ref[hh] = lane_fold(p0)
        acc_ref[hh] = jnp.dot(p0.astype(BF16), vblock(i), preferred_element_type=F32)

    def online_step(hh, j):
        s = logits(hh, j)
        m_prev = m_ref[hh]
        m_new = jnp.maximum(m_prev, jnp.max(s, axis=1, keepdims=True))
        alpha = jnp.exp2(m_prev - m_new)
        pj = jnp.exp2(s - m_new[:, 0:1])
        l_ref[hh] = alpha * l_ref[hh] + lane_fold(pj)
        acc_ref[hh] = alpha * acc_ref[hh] + jnp.dot(
            pj.astype(BF16), vblock(j), preferred_element_type=F32)
        m_ref[hh] = m_new

    half = tq // 2

    def fixed_diag(hh):
        vb = vblock(i)
        s_top = jnp.dot(qas[hh][0:half], kaug_ref[hh, i, :, 0:half],
                        preferred_element_type=F32)
        p_top = jnp.exp2(jnp.where(causal[0:half, 0:half], s_top, NEG_INF))
        s_bot = jnp.dot(qas[hh][half:], kaug_ref[hh, i], preferred_element_type=F32)
        p_bot = jnp.exp2(jnp.where(causal[half:, :], s_bot, NEG_INF))
        l_ref[hh, 0:half] = lane_fold(p_top)
        l_ref[hh, half:] = lane_fold(p_bot)
        acc_ref[hh, 0:half] = jnp.dot(p_top.astype(BF16), vb[0:half],
                                      preferred_element_type=F32)
        acc_ref[hh, half:] = jnp.dot(p_bot.astype(BF16), vb, preferred_element_type=F32)

    def fixed_step(hh, j):
        pj = jnp.exp2(logits(hh, j))
        l_ref[hh] += lane_fold(pj)
        acc_ref[hh] += jnp.dot(pj.astype(BF16), vblock(j), preferred_element_type=F32)

    def fixed_step2(hh, j):
        pj = jnp.exp2(jnp.concatenate([logits(hh, j - 1), logits(hh, j)], axis=1))
        l_ref[hh] += lane_fold(pj)
        v2 = v_ref[pl.ds(pl.multiple_of((j - 1) * tk, tk), 2 * tk), :]
        acc_ref[hh] += jnp.dot(pj.astype(BF16), v2, preferred_element_type=F32)

    for hh in range(2):
        code = nkv_ref[(2 * p + hh) * nq + i]
        n_off = code >> 1
        fixed_ok = (code & 1) == 1

        @pl.when(fixed_ok)
        def _(hh=hh, n_off=n_off):
            fixed_diag(hh)

            def body(u, carry):
                fixed_step2(hh, i - 1 - 2 * u)
                return carry

            pairs = n_off >> 1
            lax.fori_loop(0, pairs, body, 0)

            @pl.when((n_off & 1) == 1)
            def _():
                fixed_step(hh, i - n_off)

        @pl.when(jnp.logical_not(fixed_ok))
        def _(hh=hh, n_off=n_off):
            online_diag(hh)

            def body(jj, carry):
                online_step(hh, i - 1 - jj)
                return carry

            lax.fori_loop(0, n_off, body, 0)

    l0 = jnp.sum(l_ref[0], axis=1, keepdims=True)
    l1 = jnp.sum(l_ref[1], axis=1, keepdims=True)
    o_ref[...] = jnp.where(lo_half, acc_ref[0] / l0, acc_ref[1] / l1)


def _kv_counts(qn2, kn2, cum):
    tq = ATT_TQ
    nq = cum.shape[0] // tq
    qn = jnp.sqrt(jnp.max(qn2[:, :FOX_HEADS].reshape(nq, tq, FOX_HEADS), axis=1))
    kn = jnp.sqrt(jnp.max(kn2[:FOX_HEADS], axis=1))
    c = (cum[:, FOX_HEADS:2 * FOX_HEADS] * LOG2E).reshape(nq, tq, FOX_HEADS)
    cmax = jnp.max(c, axis=1)
    pmin = lax.cummin(jnp.min(c, axis=1), axis=0)
    qk = NORM_MARGIN * qn * kn[None, :] + 1.0
    bound = (2.0 * qk + cmax)[:, None, :] - pmin[None, :, :]
    ii = lax.broadcasted_iota(jnp.int32, bound.shape, 0)
    jj = lax.broadcasted_iota(jnp.int32, bound.shape, 1)
    skip = (bound < SKIP_LOG2) & (jj < ii)
    jstar = jnp.max(jnp.where(skip, jj, -1), axis=1)
    n_off = jnp.arange(nq, dtype=jnp.int32)[:, None] - 1 - jstar
    pmin_prev = jnp.concatenate([jnp.full((1, FOX_HEADS), jnp.inf, F32), pmin[:-1]], axis=0)
    fixed_ok = (qk + jnp.maximum(cmax - pmin_prev, 0.0)) < FIXED_MAX_LOG2
    code = 2 * n_off + fixed_ok.astype(jnp.int32)
    return code.T.reshape(-1).astype(jnp.int32)


def _fox(nkv, q, kt3, v, cum, cumt):
    s = q.shape[0]
    tq, tk = ATT_TQ, ATT_TK
    nb = s // tk
    grid_spec = pltpu.PrefetchScalarGridSpec(
        num_scalar_prefetch=1,
        grid=(FOX_HEADS // 2, s // tq),
        in_specs=[
            pl.BlockSpec((tq, LANES), lambda p, i, n: (i, p)),
            pl.BlockSpec((nb, LANES, tk), lambda p, i, n: (0, p, 0)),
            pl.BlockSpec((s, LANES), lambda p, i, n: (0, p)),
            pl.BlockSpec((tq, LANES), lambda p, i, n: (i, 0)),
            pl.BlockSpec((16, s), lambda p, i, n: (0, 0)),
        ],
        out_specs=pl.BlockSpec((tq, LANES), lambda p, i, n: (i, p)),
        scratch_shapes=[
            pltpu.VMEM((2, nb, LANES, tk), BF16),
            pltpu.VMEM((2, tq, LANES), F32),
            pltpu.VMEM((2, tq, LANES), F32),
            pltpu.VMEM((2, tq, LANES), F32),
        ],
    )
    return pl.pallas_call(
        _fox_kernel,
        grid_spec=grid_spec,
        out_shape=jax.ShapeDtypeStruct((s, FOX_WIDTH), F32),
        compiler_params=pltpu.CompilerParams(
            dimension_semantics=("arbitrary", "arbitrary"),
            vmem_limit_bytes=VMEM_LIMIT),
        name="fox_attn",
    )(nkv, q, kt3, v, cum, cumt)


def _pad_lanes(vec_dt, vec_f):
    out = jnp.zeros((1, LANES), F32)
    out = out.at[0, 0:SSM_HEADS].set(vec_dt.astype(F32))
    if vec_f is not None:
        out = out.at[0, SSM_HEADS:SSM_HEADS + FOX_HEADS].set(vec_f.astype(F32))
    return out


def _layer(x, ffn1_norm, ffn1_w_in, ffn1_w_out, mix_norm, w_in, conv_w, conv_b, dt_bias,
           a_log, d_skip, ssm_norm, f_bias, fox_norm, w_out, ffn2_norm, ffn2_w_in,
           ffn2_w_out, final_g, *, final):
    ones = jnp.ones((1, D_MODEL), F32)
    row = lambda a: a.reshape(1, -1).astype(F32)

    x1 = _ffn(x, row(ffn1_norm), ffn1_w_in.astype(BF16), ffn1_w_out.astype(BF16), ones,
              final=False)

    o0 = SSM_WIDTH
    o1 = o0 + SSM_XBC
    o2 = o1 + SSM_HEADS
    o3 = o2 + 3 * FOX_WIDTH
    wz = w_in[:, :o0].astype(BF16)
    wxbc = w_in[:, o0:o1].astype(BF16)
    w_dt = w_in[:, o1:o2]
    wq = w_in[:, o2:o2 + FOX_WIDTH].astype(BF16)
    wkt = w_in[:, o2 + FOX_WIDTH:o2 + 2 * FOX_WIDTH].T.astype(BF16)
    wv = w_in[:, o2 + 2 * FOX_WIDTH:o3].astype(BF16)
    w_f = w_in[:, o3:]
    ws = jnp.concatenate(
        [w_dt, w_f, jnp.zeros((D_MODEL, LANES - SSM_HEADS - FOX_HEADS), w_in.dtype)],
        axis=1).astype(BF16)
    zs, xs, bc, q, kt3, v, small, qn2, kn2 = _inproj(
        x1, row(mix_norm), wz, wxbc, wq, wkt, wv, ws, conv_w.astype(F32), row(conv_b))

    bias_pad = _pad_lanes(dt_bias, f_bias)
    a_pad = _pad_lanes(-jnp.exp(a_log.astype(F32)), None)
    dskip_e = jnp.repeat(d_skip.astype(F32), HEAD_DIM).reshape(1, SSM_WIDTH)
    tri = jnp.asarray(np.tril(np.ones((SSD_T, SSD_T), np.float32)), dtype=BF16)
    expand_np = np.zeros((LANES, SSM_WIDTH), np.float32)
    for hd in range(SSM_HEADS):
        expand_np[hd, hd * HEAD_DIM:(hd + 1) * HEAD_DIM] = 1.0
    y_ssd, cum, cumt = _ssd(zs, xs, bc, small, bias_pad, a_pad,
                            dskip_e, row(ssm_norm), tri, jnp.asarray(expand_np, dtype=BF16))

    o_fox = _fox(_kv_counts(qn2, kn2, cum), q, kt3, v, cum, cumt)

    wo = w_out.astype(BF16)
    mix = (y_ssd, o_fox, row(fox_norm), wo[:SSM_WIDTH], wo[SSM_WIDTH:])
    return _ffn(x1, row(ffn2_norm), ffn2_w_in.astype(BF16), ffn2_w_out.astype(BF16),
                row(final_g), final=final, mix=mix)


def kernel(x, ffn1_norm, ffn1_w_in, ffn1_w_out, mix_norm, w_in, conv_w, conv_b, dt_bias, a_log,
           d_skip, ssm_norm, f_bias, fox_norm, w_out, ffn2_norm, ffn2_w_in, ffn2_w_out, final_norm):
    b, s, d = x.shape
    depth = ffn1_norm.shape[0]
    outs = []
    for bi in range(b):
        xb = x[bi]
        for l in range(depth):
            xb = _layer(xb, ffn1_norm[l], ffn1_w_in[l], ffn1_w_out[l], mix_norm[l], w_in[l],
                        conv_w[l], conv_b[l], dt_bias[l], a_log[l], d_skip[l], ssm_norm[l],
                        f_bias[l], fox_norm[l], w_out[l], ffn2_norm[l], ffn2_w_in[l],
                        ffn2_w_out[l], final_norm, final=(l == depth - 1))
        outs.append(xb)
    return jnp.stack(outs, axis=0)
```

```python
import functools

import jax
import jax.numpy as jnp
import numpy as np
from jax import lax
from jax.experimental import pallas as pl
from jax.experimental.pallas import tpu as pltpu

F32 = jnp.float32
BF16 = jnp.bfloat16
HIGHEST = lax.Precision.HIGHEST

D_MODEL = 1024
SEQ = 16384
HEAD_DIM = 64
SSM_WIDTH = 512
SSM_HEADS = 8
SSM_GROUPS = 2
SSM_STATE = 128
CONV_WIDTH = 4
SSM_XBC = SSM_WIDTH + 2 * SSM_GROUPS * SSM_STATE
FOX_WIDTH = 512
FOX_HEADS = 8
D_FF = 2816
EPS = 1e-6

LANES = 128
VMEM_LIMIT = 56 * 1024 * 1024

FFN_TM = 1024
FFN_TF = 256
PROJ_TM = 512
SSD_T = 256
ATT_TQ = 512
ATT_TK = 512
NEG_INF = float("-inf")
LOG2E = 1.4426950408889634
SKIP_LOG2 = -150.0
NORM_MARGIN = 1.05
FIXED_MAX_LOG2 = 64.0


def _rms(x, g):
    ms = jnp.mean(x * x, axis=-1, keepdims=True)
    return x * lax.rsqrt(ms + EPS) * g


def _silu(x):
    return x * (1.0 / (1.0 + jnp.exp(-x)))


def _softplus(x):
    return jnp.maximum(x, 0.0) + jnp.log1p(jnp.exp(-jnp.abs(x)))


def _split3(x):
    hi = x.astype(BF16).astype(F32)
    r = x - hi
    mid = r.astype(BF16).astype(F32)
    lo = (r - mid).astype(BF16).astype(F32)
    return hi, mid, lo


def _dot_f32_rhs(a_bf16, x):
    hi, mid, lo = _split3(x)
    d = lambda piece: jnp.dot(a_bf16, piece.astype(BF16), preferred_element_type=F32)
    return (d(lo) + d(mid)) + d(hi)


def _dot_f32_lhs(x, b_bf16):
    hi, mid, lo = _split3(x)
    d = lambda piece: jnp.dot(piece.astype(BF16), b_bf16, preferred_element_type=F32)
    return (d(lo) + d(mid)) + d(hi)


def _ffn_kernel(*refs, mix, final):
    if mix:
        (x_ref, ys_ref, of_ref, mg_ref, wm1_ref, wm2_ref,
         g_ref, wg_ref, wu_ref, wo_ref, fg_ref, o_ref) = refs
        yf = _rms(of_ref[...], mg_ref[...]).astype(BF16)
        x = (x_ref[...]
             + jnp.dot(ys_ref[...], wm1_ref[...], preferred_element_type=F32)
             + jnp.dot(yf, wm2_ref[...], preferred_element_type=F32))
    else:
        x_ref, g_ref, wg_ref, wu_ref, wo_ref, fg_ref, o_ref = refs
        x = x_ref[...]
    h = _rms(x, g_ref[...]).astype(BF16)
    o_ref[...] = 2.0 * x
    for f in range(D_FF // FFN_TF):
        cols = slice(f * FFN_TF, (f + 1) * FFN_TF)
        gate = jnp.dot(h, wg_ref[:, cols], preferred_element_type=F32)
        up = jnp.dot(h, wu_ref[:, cols], preferred_element_type=F32)
        act = (_silu(gate) * up).astype(BF16)
        o_ref[...] += jnp.dot(act, wo_ref[cols, :], preferred_element_type=F32)
    r = 0.5 * o_ref[...]
    if final:
        r = _rms(r, fg_ref[...])
    o_ref[...] = r


def _ffn(x, g, w_in, w_out, fg, *, final, mix=None):
    s = x.shape[0]
    tm = FFN_TM
    row = lambda i: (i, 0)
    const = lambda i: (0, 0)
    resident = functools.partial(pl.BlockSpec, pipeline_mode=pl.Buffered(1))
    x_spec = pl.BlockSpec((tm, D_MODEL), row)
    vec_spec = pl.BlockSpec((1, D_MODEL), const)
    ffn_specs = [
        vec_spec,
        resident((D_MODEL, D_FF), lambda i: (0, 0)),
        resident((D_MODEL, D_FF), lambda i: (0, 1)),
        resident((D_FF, D_MODEL), const),
        vec_spec,
    ]
    ffn_args = (g, w_in, w_in, w_out, fg)
    if mix is None:
        in_specs = [x_spec] + ffn_specs
        args = (x,) + ffn_args
    else:
        ys, of, mg, wm1, wm2 = mix
        in_specs = [
            x_spec,
            pl.BlockSpec((tm, SSM_WIDTH), row),
            pl.BlockSpec((tm, FOX_WIDTH), row),
            pl.BlockSpec((1, FOX_WIDTH), const),
            resident((SSM_WIDTH, D_MODEL), const),
            resident((FOX_WIDTH, D_MODEL), const),
        ] + ffn_specs
        args = (x, ys, of, mg, wm1, wm2) + ffn_args
    return pl.pallas_call(
        functools.partial(_ffn_kernel, mix=mix is not None, final=final),
        grid=(s // tm,),
        in_specs=in_specs,
        out_specs=pl.BlockSpec((tm, D_MODEL), row),
        out_shape=jax.ShapeDtypeStruct((s, D_MODEL), F32),
        compiler_params=pltpu.CompilerParams(
            dimension_semantics=("parallel",),
            vmem_limit_bytes=VMEM_LIMIT),
        name="ffn_final" if final else "ffn",
    )(*args)


def _inproj_kernel(x_ref, g_ref, wz_ref, wxbc_ref, wq_ref, wkt_ref, wv_ref, ws_ref,
                   ind_ref, indt_ref, cw_ref, cb_ref,
                   zs_ref, xs_ref, bc_ref, q_ref, kt_ref, v_ref, small_ref, qn_ref, kn_ref,
                   halo_ref):
    tm = PROJ_TM
    i = pl.program_id(0)

    @pl.when(i == 0)
    def _():
        halo_ref[...] = jnp.zeros_like(halo_ref)

    h = _rms(x_ref[...], g_ref[...]).astype(BF16)
    xb = jnp.dot(h, wxbc_ref[...], preferred_element_type=F32)
    zs_ref[...] = _silu(jnp.dot(h, wz_ref[...], preferred_element_type=F32))
    q = jnp.dot(h, wq_ref[...], preferred_element_type=F32)
    qb = (q * (HEAD_DIM ** -0.5 * LOG2E)).astype(BF16)
    q_ref[...] = qb
    kt = lax.dot_general(wkt_ref[...], h, (((0,), (1,)), ((), ())),
                         preferred_element_type=F32)
    ktb = kt.astype(BF16)
    kt_ref[0] = ktb
    v_ref[...] = jnp.dot(h, wv_ref[...], preferred_element_type=F32).astype(BF16)
    small_ref[...] = jnp.dot(h, ws_ref[...], preferred_element_type=F32)
    qf = qb.astype(F32)
    qn = jnp.dot((qf * qf).astype(BF16), ind_ref[...], preferred_element_type=F32)
    qn_ref[0] = jnp.broadcast_to(jnp.max(qn, axis=0, keepdims=True), (8, LANES))
    kf = ktb.astype(F32)
    kn = jnp.dot(indt_ref[...], (kf * kf).astype(BF16), preferred_element_type=F32)
    kn_ref[0] = jnp.broadcast_to(jnp.max(kn, axis=1, keepdims=True), (16, LANES))
    prev = halo_ref[...]
    row8 = lax.broadcasted_iota(jnp.int32, (8, SSM_XBC), 0)
    conv = cb_ref[...] + xb * cw_ref[CONV_WIDTH - 1:CONV_WIDTH, :]
    for s in range(1, CONV_WIDTH):
        rolled = pltpu.roll(xb, s, axis=0)
        head = jnp.where(row8 < s, pltpu.roll(prev, s, axis=0), rolled[0:8])
        shifted = jnp.concatenate([head, rolled[8:]], axis=0)
        k = CONV_WIDTH - 1 - s
        conv = conv + shifted * cw_ref[k:k + 1, :]
    halo_ref[...] = xb[tm - 8:tm]
    u = _silu(conv)
    xs_ref[...] = u[:, :SSM_WIDTH]
    bc_ref[...] = u[:, SSM_WIDTH:].astype(BF16)


def _inproj(x, g, wz, wxbc, wq, wkt, wv, ws, cw, cb):
    s = x.shape[0]
    tm = PROJ_TM
    nb = s // tm
    const = lambda i: (0, 0)
    row = lambda i: (i, 0)
    ind_np = np.zeros((FOX_WIDTH, LANES), np.float32)
    for hd in range(FOX_HEADS):
        ind_np[hd * HEAD_DIM:(hd + 1) * HEAD_DIM, hd] = 1.0
    ind = jnp.asarray(ind_np, dtype=BF16)
    indt = jnp.asarray(ind_np.T[:16], dtype=BF16)
    return pl.pallas_call(
        _inproj_kernel,
        grid=(nb,),
        in_specs=[
            pl.BlockSpec((tm, D_MODEL), row),
            pl.BlockSpec((1, D_MODEL), const),
            pl.BlockSpec((D_MODEL, SSM_WIDTH), const),
            pl.BlockSpec((D_MODEL, SSM_XBC), const),
            pl.BlockSpec((D_MODEL, FOX_WIDTH), const),
            pl.BlockSpec((D_MODEL, FOX_WIDTH), const),
            pl.BlockSpec((D_MODEL, FOX_WIDTH), const),
            pl.BlockSpec((D_MODEL, LANES), const),
            pl.BlockSpec((FOX_WIDTH, LANES), const),
            pl.BlockSpec((16, FOX_WIDTH), const),
            pl.BlockSpec((CONV_WIDTH, SSM_XBC), const),
            pl.BlockSpec((1, SSM_XBC), const),
        ],
        out_specs=[
            pl.BlockSpec((tm, SSM_WIDTH), row),
            pl.BlockSpec((tm, SSM_WIDTH), row),
            pl.BlockSpec((tm, SSM_XBC - SSM_WIDTH), row),
            pl.BlockSpec((tm, FOX_WIDTH), row),
            pl.BlockSpec((1, FOX_WIDTH, tm), lambda i: (i, 0, 0)),
            pl.BlockSpec((tm, FOX_WIDTH), row),
            pl.BlockSpec((tm, LANES), row),
            pl.BlockSpec((1, 8, LANES), lambda i: (i, 0, 0)),
            pl.BlockSpec((1, 16, LANES), lambda i: (i, 0, 0)),
        ],
        out_shape=[
            jax.ShapeDtypeStruct((s, SSM_WIDTH), F32),
            jax.ShapeDtypeStruct((s, SSM_WIDTH), F32),
            jax.ShapeDtypeStruct((s, SSM_XBC - SSM_WIDTH), BF16),
            jax.ShapeDtypeStruct((s, FOX_WIDTH), BF16),
            jax.ShapeDtypeStruct((nb, FOX_WIDTH, tm), BF16),
            jax.ShapeDtypeStruct((s, FOX_WIDTH), BF16),
            jax.ShapeDtypeStruct((s, LANES), F32),
            jax.ShapeDtypeStruct((nb, 8, LANES), F32),
            jax.ShapeDtypeStruct((nb, 16, LANES), F32),
        ],
        scratch_shapes=[pltpu.VMEM((8, SSM_XBC), F32)],
        compiler_params=pltpu.CompilerParams(
            dimension_semantics=("arbitrary",),
            vmem_limit_bytes=VMEM_LIMIT),
        name="in_proj",
    )(x, g, wz, wxbc, wq, wkt, wv, ws, ind, indt, cw, cb)


def _ssd_kernel(zs_ref, xs_ref, bc_ref, small_ref, bias_ref, apad_ref,
                dskip_ref, ng_ref, tri_ref, expand_ref,
                y_ref, cum_ref, cumt_ref, cmax_ref, cmin_ref,
                state_ref, carry_ref):
    t_rows = SSD_T
    i = pl.program_id(0)

    @pl.when(i == 0)
    def _():
        state_ref[...] = jnp.zeros_like(state_ref)
        carry_ref[...] = jnp.zeros_like(carry_ref)

    xs = xs_ref[...]
    bm = bc_ref[:, 0:SSM_GROUPS * SSM_STATE]
    cm = bc_ref[:, SSM_GROUPS * SSM_STATE:]

    t = small_ref[...] + bias_ref[...]
    lane = lax.broadcasted_iota(jnp.int32, (t_rows, LANES), 1)
    is_dt = lane < SSM_HEADS
    sp = _softplus(jnp.where(is_dt, t, -t))
    v = jnp.where(is_dt, sp * apad_ref[...], -sp)
    cs = _dot_f32_rhs(tri_ref[...], v)
    cs = cs + carry_ref[...]
    last = cs[t_rows - 1:t_rows, :]
    carry_ref[...] = jnp.where(lane[0:1, :] < SSM_HEADS, 0.0, last)
    cum_ref[...] = cs
    cst = cs.T
    cumt_ref[...] = cst[0:16, :]
    cmax_ref[0] = jnp.broadcast_to(jnp.max(cs, axis=0, keepdims=True), (8, LANES))
    cmin_ref[0] = jnp.broadcast_to(jnp.min(cs, axis=0, keepdims=True), (8, LANES))

    stacked = jnp.concatenate([sp, cs], axis=0)
    exp2 = _dot_f32_lhs(stacked, expand_ref[...])
    dt_e = exp2[0:t_rows]
    cs_e = exp2[t_rows:2 * t_rows]
    ea_e = jnp.exp(cs_e)
    de_e = jnp.exp(cs_e[t_rows - 1:t_rows, :] - cs_e)
    cd_e = ea_e[t_rows - 1:t_rows, :]

    xdt = xs * dt_e
    xdt_b = xdt.astype(BF16)
    wst = (xdt * de_e).astype(BF16)

    r_i = lax.broadcasted_iota(jnp.int32, (t_rows, t_rows), 0)
    c_i = lax.broadcasted_iota(jnp.int32, (t_rows, t_rows), 1)
    causal = r_i >= c_i
    lane_pair = lax.broadcasted_iota(jnp.int32, (t_rows, LANES), 1)
    lo = lane_pair < HEAD_DIM

    gw = SSM_WIDTH // SSM_GROUPS
    hpg = SSM_HEADS // SSM_GROUPS
    y_parts = []
    for g in range(SSM_GROUPS):
        cg = cm[:, g * SSM_STATE:(g + 1) * SSM_STATE]
        bg = bm[:, g * SSM_STATE:(g + 1) * SSM_STATE]
        gmat = lax.dot_general(cg, bg, (((1,), (1,)), ((), ())),
                               preferred_element_type=F32)
        for pr in range(hpg // 2):
            c0 = g * gw + pr * LANES
            xpair = xdt_b[:, c0:c0 + LANES]
            acc = None
            for hh in range(2):
                h = g * hpg + pr * 2 + hh
                seg = cs[:, h:h + 1] - cst[h:h + 1, :]
                dec = jnp.exp(jnp.where(causal, seg, NEG_INF))
                m = (gmat * dec).astype(BF16)
                xm = jnp.where(lo if hh == 0 else jnp.logical_not(lo), xpair,
                               jnp.zeros_like(xpair))
                part = jnp.dot(m, xm, preferred_element_type=F32)
                acc = part if acc is None else acc + part
            y_parts.append(acc)
        s_prev = state_ref[g]
        y_off = jnp.dot(cg, s_prev.astype(BF16), preferred_element_type=F32)
        y_parts[-2] = y_parts[-2] + y_off[:, :LANES] * ea_e[:, g * gw:g * gw + LANES]
        y_parts[-1] = y_parts[-1] + y_off[:, LANES:] * ea_e[:, g * gw + LANES:(g + 1) * gw]
        bgt = bg.astype(F32).T.astype(BF16)
        upd = jnp.dot(bgt, wst[:, g * gw:(g + 1) * gw], preferred_element_type=F32)
        state_ref[g] = s_prev * cd_e[:, g * gw:(g + 1) * gw] + upd

    y = jnp.concatenate(y_parts, axis=1) + dskip_ref[...] * xs
    y = y * zs_ref[...]
    y_ref[...] = _rms(y, ng_ref[...]).astype(BF16)


def _ssd(zs, xs, bc, small, bias_pad, a_pad, dskip_e, ng, tri, expand):
    s = zs.shape[0]
    t = SSD_T
    const = lambda i: (0, 0)
    row = lambda i: (i, 0)
    return pl.pallas_call(
        _ssd_kernel,
        grid=(s // t,),
        in_specs=[
            pl.BlockSpec((t, SSM_WIDTH), row),
            pl.BlockSpec((t, SSM_WIDTH), row),
            pl.BlockSpec((t, SSM_XBC - SSM_WIDTH), row),
            pl.BlockSpec((t, LANES), row),
            pl.BlockSpec((1, LANES), const),
            pl.BlockSpec((1, LANES), const),
            pl.BlockSpec((1, SSM_WIDTH), const),
            pl.BlockSpec((1, SSM_WIDTH), const),
            pl.BlockSpec((t, t), const),
            pl.BlockSpec((LANES, SSM_WIDTH), const),
        ],
        out_specs=[
            pl.BlockSpec((t, SSM_WIDTH), row),
            pl.BlockSpec((t, LANES), row),
            pl.BlockSpec((16, t), lambda i: (0, i)),
            pl.BlockSpec((1, 8, LANES), lambda i: (i, 0, 0)),
            pl.BlockSpec((1, 8, LANES), lambda i: (i, 0, 0)),
        ],
        out_shape=[
            jax.ShapeDtypeStruct((s, SSM_WIDTH), BF16),
            jax.ShapeDtypeStruct((s, LANES), F32),
            jax.ShapeDtypeStruct((16, s), F32),
            jax.ShapeDtypeStruct((s // t, 8, LANES), F32),
            jax.ShapeDtypeStruct((s // t, 8, LANES), F32),
        ],
        scratch_shapes=[
            pltpu.VMEM((SSM_GROUPS, SSM_STATE, SSM_WIDTH // SSM_GROUPS), F32),
            pltpu.VMEM((1, LANES), F32),
        ],
        compiler_params=pltpu.CompilerParams(
            dimension_semantics=("arbitrary",),
            vmem_limit_bytes=VMEM_LIMIT),
        name="ssd",
    )(zs, xs, bc, small, bias_pad, a_pad, dskip_e, ng, tri, expand)


def _fox_kernel(nkv_ref, q_ref, kt_ref, v_ref, cum_ref, cumt_ref, o_ref,
                kaug_ref, m_ref, l_ref, acc_ref):
    tq, tk = ATT_TQ, ATT_TK
    nb = kt_ref.shape[0]
    nq = pl.num_programs(1)
    p = pl.program_id(0)
    i = pl.program_id(1)

    @pl.when(i == 0)
    def _():
        row16 = lax.broadcasted_iota(jnp.int32, (16, tk), 0)
        zeros48 = jnp.zeros((48, tk), BF16)

        def build(j, carry):
            off = pl.multiple_of(j * tk, tk)
            kt = kt_ref[j]
            for hh in range(2):
                ck = cumt_ref[pl.ds(FOX_HEADS + 2 * p + hh, 1), pl.ds(off, tk)] * LOG2E
                hi, mid, lo = _split3(ck)
                bias = jnp.where(row16 < 3, 1.0,
                                 jnp.where(row16 == 3, -hi,
                                           jnp.where(row16 == 4, -mid,
                                                     jnp.where(row16 == 5, -lo, 0.0))))
                bias = bias.astype(BF16)
                if hh == 0:
                    kaug_ref[0, j, 0:64, :] = kt[0:64]
                    kaug_ref[0, j, 64:80, :] = bias
                    kaug_ref[0, j, 80:128, :] = zeros48
                else:
                    kaug_ref[1, j, 0:16, :] = bias
                    kaug_ref[1, j, 16:64, :] = zeros48
                    kaug_ref[1, j, 64:128, :] = kt[64:128]
            return carry

        lax.fori_loop(0, nb, build, 0)

    q = q_ref[...]
    lane = lax.broadcasted_iota(jnp.int32, (tq, LANES), 1)
    lo_half = lane < HEAD_DIM
    r_i = lax.broadcasted_iota(jnp.int32, (tq, tk), 0)
    c_i = lax.broadcasted_iota(jnp.int32, (tq, tk), 1)
    causal = r_i >= c_i

    pr_i = lax.broadcasted_iota(jnp.int32, (LANES, LANES), 0)
    pc_i = lax.broadcasted_iota(jnp.int32, (LANES, LANES), 1)
    src_lane = jnp.where(pc_i >= HEAD_DIM, FOX_HEADS + 2 * p, FOX_HEADS + 2 * p + 1)
    place = jnp.where(pr_i == src_lane, 1.0, 0.0).astype(BF16)
    pieces = [jnp.dot(piece.astype(BF16), place, preferred_element_type=F32)
              for piece in _split3(cum_ref[...] * LOG2E)]
    lm = lane & (HEAD_DIM - 1)
    aug = jnp.where(lm == 0, pieces[0],
                    jnp.where(lm == 1, pieces[1],
                              jnp.where(lm == 2, pieces[2],
                                        jnp.where(lm < 6, 1.0, 0.0)))).astype(BF16)
    qas = [jnp.where(lo_half, q, aug), jnp.where(lo_half, aug, q)]

    def logits(hh, j):
        return jnp.dot(qas[hh], kaug_ref[hh, j], preferred_element_type=F32)

    def vblock(j):
        return v_ref[pl.ds(pl.multiple_of(j * tk, tk), tk), :]

    def lane_fold(x):
        out = x[:, 0:LANES]
        for c in range(1, x.shape[1] // LANES):
            out = out + x[:, c * LANES:(c + 1) * LANES]
        return out

    def online_diag(hh):
        s = jnp.where(causal, logits(hh, i), NEG_INF)
        m0 = jnp.max(s, axis=1, keepdims=True)
        p0 = jnp.exp2(s - m0)
        m_ref[hh] = jnp.broadcast_to(m0, (tq, LANES))
        l_ref[hh] = lane_fold(p0)
        acc_ref[hh] = jnp.dot(p0.astype(BF16), vblock(i), preferred_element_type=F32)

    def online_step(hh, j):
        s = logits(hh, j)
        m_prev = m_ref[hh]
        m_new = jnp.maximum(m_prev, jnp.max(s, axis=1, keepdims=True))
        alpha = jnp.exp2(m_prev - m_new)
        pj = jnp.exp2(s - m_new[:, 0:1])
        l_ref[hh] = alpha * l_ref[hh] + lane_fold(pj)
        acc_ref[hh] = alpha * acc_ref[hh] + jnp.dot(
            pj.astype(BF16), vblock(j), preferred_element_type=F32)
        m_ref[hh] = m_new

    half = tq // 2

    def fixed_diag(hh):
        vb = vblock(i)
        s_top = jnp.dot(qas[hh][0:half], kaug_ref[hh, i, :, 0:half],
                        preferred_element_type=F32)
        p_top = jnp.exp2(jnp.where(causal[0:half, 0:half], s_top, NEG_INF))
        s_bot = jnp.dot(qas[hh][half:], kaug_ref[hh, i], preferred_element_type=F32)
        p_bot = jnp.exp2(jnp.where(causal[half:, :], s_bot, NEG_INF))
        l_ref[hh, 0:half] = lane_fold(p_top)
        l_ref[hh, half:] = lane_fold(p_bot)
        acc_ref[hh, 0:half] = jnp.dot(p_top.astype(BF16), vb[0:half],
                                      preferred_element_type=F32)
        acc_ref[hh, half:] = jnp.dot(p_bot.astype(BF16), vb, preferred_element_type=F32)

    def fixed_step(hh, j):
        pj = jnp.exp2(logits(hh, j))
        l_ref[hh] += lane_fold(pj)
        acc_ref[hh] += jnp.dot(pj.astype(BF16), vblock(j), preferred_element_type=F32)

    def fixed_multi(hh, j, n):
        first = j - (n - 1)
        pj = jnp.exp2(jnp.concatenate([logits(hh, first + b) for b in range(n)], axis=1))
        l_ref[hh] += lane_fold(pj)
        vn = v_ref[pl.ds(pl.multiple_of(first * tk, tk), n * tk), :]
        acc_ref[hh] += jnp.dot(pj.astype(BF16), vn, preferred_element_type=F32)

    for hh in range(2):
        code = nkv_ref[(2 * p + hh) * nq + i]
        n_off = code >> 1
        fixed_ok = (code & 1) == 1

        @pl.when(fixed_ok)
        def _(hh=hh, n_off=n_off):
            fixed_diag(hh)

            def body(u, carry):
                fixed_multi(hh, i - 1 - 4 * u, 4)
                return carry

            quads = n_off >> 2
            lax.fori_loop(0, quads, body, 0)
            done = 4 * quads

            @pl.when((n_off & 2) == 2)
            def _():
                fixed_multi(hh, i - 1 - done, 2)

            @pl.when((n_off & 1) == 1)
            def _():
                fixed_step(hh, i - n_off)

        @pl.when(jnp.logical_not(fixed_ok))
        def _(hh=hh, n_off=n_off):
            online_diag(hh)

            def body(jj, carry):
                online_step(hh, i - 1 - jj)
                return carry

            lax.fori_loop(0, n_off, body, 0)

    l0 = jnp.sum(l_ref[0], axis=1, keepdims=True)
    l1 = jnp.sum(l_ref[1], axis=1, keepdims=True)
    o_ref[...] = jnp.where(lo_half, acc_ref[0] / l0, acc_ref[1] / l1)


def _kv_counts(qn_tiles, kn_tiles, cmax_chunks, cmin_chunks):
    nq = qn_tiles.shape[0]
    per_q = cmax_chunks.shape[0] // nq
    qn = jnp.sqrt(qn_tiles[:, 0, :FOX_HEADS])
    kn = jnp.sqrt(jnp.max(kn_tiles[:, :FOX_HEADS, 0], axis=0))
    gate = slice(FOX_HEADS, 2 * FOX_HEADS)
    cmax = jnp.max(cmax_chunks[:, 0, gate].reshape(nq, per_q, FOX_HEADS), axis=1) * LOG2E
    cmin = jnp.min(cmin_chunks[:, 0, gate].reshape(nq, per_q, FOX_HEADS), axis=1) * LOG2E
    pmin = lax.cummin(cmin, axis=0)
    qk = NORM_MARGIN * qn * kn[None, :] + 1.0
    bound = (2.0 * qk + cmax)[:, None, :] - pmin[None, :, :]
    ii = lax.broadcasted_iota(jnp.int32, bound.shape, 0)
    jj = lax.broadcasted_iota(jnp.int32, bound.shape, 1)
    skip = (bound < SKIP_LOG2) & (jj < ii)
    jstar = jnp.max(jnp.where(skip, jj, -1), axis=1)
    n_off = jnp.arange(nq, dtype=jnp.int32)[:, None] - 1 - jstar
    pmin_prev = jnp.concatenate([jnp.full((1, FOX_HEADS), jnp.inf, F32), pmin[:-1]], axis=0)
    fixed_ok = (qk + jnp.maximum(cmax - pmin_prev, 0.0)) < FIXED_MAX_LOG2
    code = 2 * n_off + fixed_ok.astype(jnp.int32)
    return code.T.reshape(-1).astype(jnp.int32)


def _fox(nkv, q, kt3, v, cum, cumt):
    s = q.shape[0]
    tq, tk = ATT_TQ, ATT_TK
    nb = s // tk
    grid_spec = pltpu.PrefetchScalarGridSpec(
        num_scalar_prefetch=1,
        grid=(FOX_HEADS // 2, s // tq),
        in_specs=[
            pl.BlockSpec((tq, LANES), lambda p, i, n: (i, p)),
            pl.BlockSpec((nb, LANES, tk), lambda p, i, n: (0, p, 0)),
            pl.BlockSpec((s, LANES), lambda p, i, n: (0, p)),
            pl.BlockSpec((tq, LANES), lambda p, i, n: (i, 0)),
            pl.BlockSpec((16, s), lambda p, i, n: (0, 0)),
        ],
        out_specs=pl.BlockSpec((tq, LANES), lambda p, i, n: (i, p)),
        scratch_shapes=[
            pltpu.VMEM((2, nb, LANES, tk), BF16),
            pltpu.VMEM((2, tq, LANES), F32),
            pltpu.VMEM((2, tq, LANES), F32),
            pltpu.VMEM((2, tq, LANES), F32),
        ],
    )
    return pl.pallas_call(
        _fox_kernel,
        grid_spec=grid_spec,
        out_shape=jax.ShapeDtypeStruct((s, FOX_WIDTH), F32),
        compiler_params=pltpu.CompilerParams(
            dimension_semantics=("arbitrary", "arbitrary"),
            vmem_limit_bytes=VMEM_LIMIT),
        name="fox_attn",
    )(nkv, q, kt3, v, cum, cumt)


def _pad_lanes(vec_dt, vec_f):
    out = jnp.zeros((1, LANES), F32)
    out = out.at[0, 0:SSM_HEADS].set(vec_dt.astype(F32))
    if vec_f is not None:
        out = out.at[0, SSM_HEADS:SSM_HEADS + FOX_HEADS].set(vec_f.astype(F32))
    return out


def _layer(x, ffn1_norm, ffn1_w_in, ffn1_w_out, mix_norm, w_in, conv_w, conv_b, dt_bias,
           a_log, d_skip, ssm_norm, f_bias, fox_norm, w_out, ffn2_norm, ffn2_w_in,
           ffn2_w_out, final_g, *, final):
    ones = jnp.ones((1, D_MODEL), F32)
    row = lambda a: a.reshape(1, -1).astype(F32)

    x1 = _ffn(x, row(ffn1_norm), ffn1_w_in.astype(BF16), ffn1_w_out.astype(BF16), ones,
              final=False)

    o0 = SSM_WIDTH
    o1 = o0 + SSM_XBC
    o2 = o1 + SSM_HEADS
    o3 = o2 + 3 * FOX_WIDTH
    wz = w_in[:, :o0].astype(BF16)
    wxbc = w_in[:, o0:o1].astype(BF16)
    w_dt = w_in[:, o1:o2]
    wq = w_in[:, o2:o2 + FOX_WIDTH].astype(BF16)
    wkt = w_in[:, o2 + FOX_WIDTH:o2 + 2 * FOX_WIDTH].astype(BF16)
    wv = w_in[:, o2 + 2 * FOX_WIDTH:o3].astype(BF16)
    w_f = w_in[:, o3:]
    ws = jnp.concatenate(
        [w_dt, w_f, jnp.zeros((D_MODEL, LANES - SSM_HEADS - FOX_HEADS), w_in.dtype)],
        axis=1).astype(BF16)
    zs, xs, bc, q, kt3, v, small, qn2, kn2 = _inproj(
        x1, row(mix_norm), wz, wxbc, wq, wkt, wv, ws, conv_w.astype(F32), row(conv_b))

    bias_pad = _pad_lanes(dt_bias, f_bias)
    a_pad = _pad_lanes(-jnp.exp(a_log.astype(F32)), None)
    dskip_e = jnp.repeat(d_skip.astype(F32), HEAD_DIM).reshape(1, SSM_WIDTH)
    tri = jnp.asarray(np.tril(np.ones((SSD_T, SSD_T), np.float32)), dtype=BF16)
    expand_np = np.zeros((LANES, SSM_WIDTH), np.float32)
    for hd in range(SSM_HEADS):
        expand_np[hd, hd * HEAD_DIM:(hd + 1) * HEAD_DIM] = 1.0
    y_ssd, cum, cumt, cmax_c, cmin_c = _ssd(zs, xs, bc, small, bias_pad, a_pad,
                            dskip_e, row(ssm_norm), tri, jnp.asarray(expand_np, dtype=BF16))

    o_fox = _fox(_kv_counts(qn2, kn2, cmax_c, cmin_c), q, kt3, v, cum, cumt)

    wo = w_out.astype(BF16)
    mix = (y_ssd, o_fox, row(fox_norm), wo[:SSM_WIDTH], wo[SSM_WIDTH:])
    return _ffn(x1, row(ffn2_norm), ffn2_w_in.astype(BF16), ffn2_w_out.astype(BF16),
                row(final_g), final=final, mix=mix)


def kernel(x, ffn1_norm, ffn1_w_in, ffn1_w_out, mix_norm, w_in, conv_w, conv_b, dt_bias, a_log,
           d_skip, ssm_norm, f_bias, fox_norm, w_out, ffn2_norm, ffn2_w_in, ffn2_w_out, final_norm):
    b, s, d = x.shape
    depth = ffn1_norm.shape[0]
    outs = []
    for bi in range(b):
        xb = x[bi]
        for l in range(depth):
            xb = _layer(xb, ffn1_norm[l], ffn1_w_in[l], ffn1_w_out[l], mix_norm[l], w_in[l],
                        conv_w[l], conv_b[l], dt_bias[l], a_log[l], d_skip[l], ssm_norm[l],
                        f_bias[l], fox_norm[l], w_out[l], ffn2_norm[l], ffn2_w_in[l],
                        ffn2_w_out[l], final_norm, final=(l == depth - 1))
        outs.append(xb)
    return jnp.stack(outs, axis=0)
```

```python
import functools

import jax
import jax.numpy as jnp
import numpy as np
from jax import lax
from jax.experimental import pallas as pl
from jax.experimental.pallas import tpu as pltpu

F32 = jnp.float32
BF16 = jnp.bfloat16
HIGHEST = lax.Precision.HIGHEST

D_MODEL = 1024
SEQ = 16384
HEAD_DIM = 64
SSM_WIDTH = 512
SSM_HEADS = 8
SSM_GROUPS = 2
SSM_STATE = 128
CONV_WIDTH = 4
SSM_XBC = SSM_WIDTH + 2 * SSM_GROUPS * SSM_STATE
FOX_WIDTH = 512
FOX_HEADS = 8
D_FF = 2816
EPS = 1e-6

LANES = 128
VMEM_LIMIT = 56 * 1024 * 1024

FFN_TM = 1024
FFN_TF = 256
PROJ_TM = 512
SSD_T = 256
SSD_STEP = 1024
ATT_TQ = 512
ATT_TK = 512
NEG_INF = float("-inf")
LOG2E = 1.4426950408889634
SKIP_LOG2 = -150.0
NORM_MARGIN = 1.05
FIXED_MAX_LOG2 = 64.0


def _rms(x, g):
    ms = jnp.mean(x * x, axis=-1, keepdims=True)
    return x * lax.rsqrt(ms + EPS) * g


def _silu(x):
    return x * (1.0 / (1.0 + jnp.exp(-x)))


def _softplus(x):
    return jnp.maximum(x, 0.0) + jnp.log1p(jnp.exp(-jnp.abs(x)))


def _split3(x):
    hi = x.astype(BF16).astype(F32)
    r = x - hi
    mid = r.astype(BF16).astype(F32)
    lo = (r - mid).astype(BF16).astype(F32)
    return hi, mid, lo


def _dot_f32_rhs(a_bf16, x):
    hi, mid, lo = _split3(x)
    d = lambda piece: jnp.dot(a_bf16, piece.astype(BF16), preferred_element_type=F32)
    return (d(lo) + d(mid)) + d(hi)


def _dot_f32_lhs(x, b_bf16):
    hi, mid, lo = _split3(x)
    d = lambda piece: jnp.dot(piece.astype(BF16), b_bf16, preferred_element_type=F32)
    return (d(lo) + d(mid)) + d(hi)


def _ffn_kernel(*refs, mix, final):
    if mix:
        (x_ref, ys_ref, of_ref, mg_ref, wm1_ref, wm2_ref,
         g_ref, wg_ref, wu_ref, wo_ref, fg_ref, o_ref) = refs
        yf = _rms(of_ref[...], mg_ref[...]).astype(BF16)
        x = (x_ref[...]
             + jnp.dot(ys_ref[...], wm1_ref[...], preferred_element_type=F32)
             + jnp.dot(yf, wm2_ref[...], preferred_element_type=F32))
    else:
        x_ref, g_ref, wg_ref, wu_ref, wo_ref, fg_ref, o_ref = refs
        x = x_ref[...]
    h = _rms(x, g_ref[...]).astype(BF16)
    o_ref[...] = 2.0 * x
    for f in range(D_FF // FFN_TF):
        cols = slice(f * FFN_TF, (f + 1) * FFN_TF)
        gate = jnp.dot(h, wg_ref[:, cols], preferred_element_type=F32)
        up = jnp.dot(h, wu_ref[:, cols], preferred_element_type=F32)
        act = (_silu(gate) * up).astype(BF16)
        o_ref[...] += jnp.dot(act, wo_ref[cols, :], preferred_element_type=F32)
    r = 0.5 * o_ref[...]
    if final:
        r = _rms(r, fg_ref[...])
    o_ref[...] = r


def _ffn(x, g, w_in, w_out, fg, *, final, mix=None):
    s = x.shape[0]
    tm = FFN_TM
    row = lambda i: (i, 0)
    const = lambda i: (0, 0)
    resident = functools.partial(pl.BlockSpec, pipeline_mode=pl.Buffered(1))
    x_spec = pl.BlockSpec((tm, D_MODEL), row)
    vec_spec = pl.BlockSpec((1, D_MODEL), const)
    ffn_specs = [
        vec_spec,
        resident((D_MODEL, D_FF), lambda i: (0, 0)),
        resident((D_MODEL, D_FF), lambda i: (0, 1)),
        resident((D_FF, D_MODEL), const),
        vec_spec,
    ]
    ffn_args = (g, w_in, w_in, w_out, fg)
    if mix is None:
        in_specs = [x_spec] + ffn_specs
        args = (x,) + ffn_args
    else:
        ys, of, mg, wm1, wm2 = mix
        in_specs = [
            x_spec,
            pl.BlockSpec((tm, SSM_WIDTH), row),
            pl.BlockSpec((tm, FOX_WIDTH), row),
            pl.BlockSpec((1, FOX_WIDTH), const),
            resident((SSM_WIDTH, D_MODEL), const),
            resident((FOX_WIDTH, D_MODEL), const),
        ] + ffn_specs
        args = (x, ys, of, mg, wm1, wm2) + ffn_args
    return pl.pallas_call(
        functools.partial(_ffn_kernel, mix=mix is not None, final=final),
        grid=(s // tm,),
        in_specs=in_specs,
        out_specs=pl.BlockSpec((tm, D_MODEL), row),
        out_shape=jax.ShapeDtypeStruct((s, D_MODEL), F32),
        compiler_params=pltpu.CompilerParams(
            dimension_semantics=("parallel",),
            vmem_limit_bytes=VMEM_LIMIT),
        name="ffn_final" if final else "ffn",
    )(*args)


def _inproj_kernel(x_ref, g_ref, wz_ref, wxbc_ref, wq_ref, wkt_ref, wv_ref, ws_ref,
                   ind_ref, indt_ref, cw_ref, cb_ref,
                   zs_ref, xs_ref, bc_ref, q_ref, kt_ref, v_ref, small_ref, qn_ref, kn_ref,
                   halo_ref):
    tm = PROJ_TM
    i = pl.program_id(0)

    @pl.when(i == 0)
    def _():
        halo_ref[...] = jnp.zeros_like(halo_ref)

    h = _rms(x_ref[...], g_ref[...]).astype(BF16)
    xb = jnp.dot(h, wxbc_ref[...], preferred_element_type=F32)
    zs_ref[...] = _silu(jnp.dot(h, wz_ref[...], preferred_element_type=F32))
    q = jnp.dot(h, wq_ref[...], preferred_element_type=F32)
    qb = (q * (HEAD_DIM ** -0.5 * LOG2E)).astype(BF16)
    q_ref[...] = qb
    kt = lax.dot_general(wkt_ref[...], h, (((0,), (1,)), ((), ())),
                         preferred_element_type=F32)
    ktb = kt.astype(BF16)
    kt_ref[0] = ktb
    v_ref[...] = jnp.dot(h, wv_ref[...], preferred_element_type=F32).astype(BF16)
    small_ref[...] = jnp.dot(h, ws_ref[...], preferred_element_type=F32)
    qf = qb.astype(F32)
    qn = jnp.dot((qf * qf).astype(BF16), ind_ref[...], preferred_element_type=F32)
    qn_ref[0] = jnp.broadcast_to(jnp.max(qn, axis=0, keepdims=True), (8, LANES))
    kf = ktb.astype(F32)
    kn = jnp.dot(indt_ref[...], (kf * kf).astype(BF16), preferred_element_type=F32)
    kn_ref[0] = jnp.broadcast_to(jnp.max(kn, axis=1, keepdims=True), (16, LANES))
    prev = halo_ref[...]
    row8 = lax.broadcasted_iota(jnp.int32, (8, SSM_XBC), 0)
    conv = cb_ref[...] + xb * cw_ref[CONV_WIDTH - 1:CONV_WIDTH, :]
    for s in range(1, CONV_WIDTH):
        rolled = pltpu.roll(xb, s, axis=0)
        head = jnp.where(row8 < s, pltpu.roll(prev, s, axis=0), rolled[0:8])
        shifted = jnp.concatenate([head, rolled[8:]], axis=0)
        k = CONV_WIDTH - 1 - s
        conv = conv + shifted * cw_ref[k:k + 1, :]
    halo_ref[...] = xb[tm - 8:tm]
    u = _silu(conv)
    xs_ref[...] = u[:, :SSM_WIDTH]
    bc_ref[...] = u[:, SSM_WIDTH:].astype(BF16)


def _inproj(x, g, wz, wxbc, wq, wkt, wv, ws, cw, cb):
    s = x.shape[0]
    tm = PROJ_TM
    nb = s // tm
    const = lambda i: (0, 0)
    row = lambda i: (i, 0)
    ind_np = np.zeros((FOX_WIDTH, LANES), np.float32)
    for hd in range(FOX_HEADS):
        ind_np[hd * HEAD_DIM:(hd + 1) * HEAD_DIM, hd] = 1.0
    ind = jnp.asarray(ind_np, dtype=BF16)
    indt = jnp.asarray(ind_np.T[:16], dtype=BF16)
    return pl.pallas_call(
        _inproj_kernel,
        grid=(nb,),
        in_specs=[
            pl.BlockSpec((tm, D_MODEL), row),
            pl.BlockSpec((1, D_MODEL), const),
            pl.BlockSpec((D_MODEL, SSM_WIDTH), const),
            pl.BlockSpec((D_MODEL, SSM_XBC), const),
            pl.BlockSpec((D_MODEL, FOX_WIDTH), const),
            pl.BlockSpec((D_MODEL, FOX_WIDTH), const),
            pl.BlockSpec((D_MODEL, FOX_WIDTH), const),
            pl.BlockSpec((D_MODEL, LANES), const),
            pl.BlockSpec((FOX_WIDTH, LANES), const),
            pl.BlockSpec((16, FOX_WIDTH), const),
            pl.BlockSpec((CONV_WIDTH, SSM_XBC), const),
            pl.BlockSpec((1, SSM_XBC), const),
        ],
        out_specs=[
            pl.BlockSpec((tm, SSM_WIDTH), row),
            pl.BlockSpec((tm, SSM_WIDTH), row),
            pl.BlockSpec((tm, SSM_XBC - SSM_WIDTH), row),
            pl.BlockSpec((tm, FOX_WIDTH), row),
            pl.BlockSpec((1, FOX_WIDTH, tm), lambda i: (i, 0, 0)),
            pl.BlockSpec((tm, FOX_WIDTH), row),
            pl.BlockSpec((tm, LANES), row),
            pl.BlockSpec((1, 8, LANES), lambda i: (i, 0, 0)),
            pl.BlockSpec((1, 16, LANES), lambda i: (i, 0, 0)),
        ],
        out_shape=[
            jax.ShapeDtypeStruct((s, SSM_WIDTH), F32),
            jax.ShapeDtypeStruct((s, SSM_WIDTH), F32),
            jax.ShapeDtypeStruct((s, SSM_XBC - SSM_WIDTH), BF16),
            jax.ShapeDtypeStruct((s, FOX_WIDTH), BF16),
            jax.ShapeDtypeStruct((nb, FOX_WIDTH, tm), BF16),
            jax.ShapeDtypeStruct((s, FOX_WIDTH), BF16),
            jax.ShapeDtypeStruct((s, LANES), F32),
            jax.ShapeDtypeStruct((nb, 8, LANES), F32),
            jax.ShapeDtypeStruct((nb, 16, LANES), F32),
        ],
        scratch_shapes=[pltpu.VMEM((8, SSM_XBC), F32)],
        compiler_params=pltpu.CompilerParams(
            dimension_semantics=("arbitrary",),
            vmem_limit_bytes=VMEM_LIMIT),
        name="in_proj",
    )(x, g, wz, wxbc, wq, wkt, wv, ws, ind, indt, cw, cb)


def _ssd_kernel(zs_ref, xs_ref, bc_ref, small_ref, bias_ref, apad_ref,
                dskip_ref, ng_ref, tri_ref, expand_ref, place_ref,
                y_ref, qaug_ref, cumt_ref, cmax_ref, cmin_ref,
                state_ref, carry_ref):
    t_rows = SSD_T
    i = pl.program_id(0)

    @pl.when(i == 0)
    def _():
        state_ref[...] = jnp.zeros_like(state_ref)
        carry_ref[...] = jnp.zeros_like(carry_ref)

    for c in range(SSD_STEP // SSD_T):
        _ssd_chunk(slice(c * t_rows, (c + 1) * t_rows), c,
                   zs_ref, xs_ref, bc_ref, small_ref, bias_ref, apad_ref,
                   dskip_ref, ng_ref, tri_ref, expand_ref, place_ref,
                   y_ref, qaug_ref, cumt_ref, cmax_ref, cmin_ref, state_ref, carry_ref)


def _ssd_chunk(rows, c, zs_ref, xs_ref, bc_ref, small_ref, bias_ref, apad_ref,
               dskip_ref, ng_ref, tri_ref, expand_ref, place_ref,
               y_ref, qaug_ref, cumt_ref, cmax_ref, cmin_ref, state_ref, carry_ref):
    t_rows = SSD_T
    xs = xs_ref[rows, :]
    bm = bc_ref[rows, 0:SSM_GROUPS * SSM_STATE]
    cm = bc_ref[rows, SSM_GROUPS * SSM_STATE:]

    t = small_ref[rows, :] + bias_ref[...]
    lane = lax.broadcasted_iota(jnp.int32, (t_rows, LANES), 1)
    is_dt = lane < SSM_HEADS
    sp = _softplus(jnp.where(is_dt, t, -t))
    v = jnp.where(is_dt, sp * apad_ref[...], -sp)
    cs = _dot_f32_rhs(tri_ref[...], v)
    cs = cs + carry_ref[...]
    last = cs[t_rows - 1:t_rows, :]
    carry_ref[...] = jnp.where(lane[0:1, :] < SSM_HEADS, 0.0, last)
    cst = cs.T
    cumt_ref[:, rows] = cst[0:16, :]
    lane4 = lax.broadcasted_iota(jnp.int32, (t_rows, FOX_WIDTH), 1) & (HEAD_DIM - 1)
    placed = [jnp.dot(piece.astype(BF16), place_ref[...], preferred_element_type=F32)
              for piece in _split3(cs * LOG2E)]
    qaug_ref[rows, :] = jnp.where(
        lane4 == 0, placed[0],
        jnp.where(lane4 == 1, placed[1],
                  jnp.where(lane4 == 2, placed[2],
                            jnp.where(lane4 < 6, 1.0, 0.0)))).astype(BF16)
    cmax_ref[c] = jnp.broadcast_to(jnp.max(cs, axis=0, keepdims=True), (8, LANES))
    cmin_ref[c] = jnp.broadcast_to(jnp.min(cs, axis=0, keepdims=True), (8, LANES))

    stacked = jnp.concatenate([sp, cs], axis=0)
    exp2 = _dot_f32_lhs(stacked, expand_ref[...])
    dt_e = exp2[0:t_rows]
    cs_e = exp2[t_rows:2 * t_rows]
    ea_e = jnp.exp(cs_e)
    de_e = jnp.exp(cs_e[t_rows - 1:t_rows, :] - cs_e)
    cd_e = ea_e[t_rows - 1:t_rows, :]

    xdt = xs * dt_e
    xdt_b = xdt.astype(BF16)
    wst = (xdt * de_e).astype(BF16)

    r_i = lax.broadcasted_iota(jnp.int32, (t_rows, t_rows), 0)
    c_i = lax.broadcasted_iota(jnp.int32, (t_rows, t_rows), 1)
    causal = r_i >= c_i
    lane_pair = lax.broadcasted_iota(jnp.int32, (t_rows, LANES), 1)
    lo = lane_pair < HEAD_DIM

    gw = SSM_WIDTH // SSM_GROUPS
    hpg = SSM_HEADS // SSM_GROUPS
    y_parts = []
    for g in range(SSM_GROUPS):
        cg = cm[:, g * SSM_STATE:(g + 1) * SSM_STATE]
        bg = bm[:, g * SSM_STATE:(g + 1) * SSM_STATE]
        gmat = lax.dot_general(cg, bg, (((1,), (1,)), ((), ())),
                               preferred_element_type=F32)
        for pr in range(hpg // 2):
            c0 = g * gw + pr * LANES
            xpair = xdt_b[:, c0:c0 + LANES]
            acc = None
            for hh in range(2):
                h = g * hpg + pr * 2 + hh
                seg = cs[:, h:h + 1] - cst[h:h + 1, :]
                dec = jnp.exp(jnp.where(causal, seg, NEG_INF))
                m = (gmat * dec).astype(BF16)
                xm = jnp.where(lo if hh == 0 else jnp.logical_not(lo), xpair,
                               jnp.zeros_like(xpair))
                part = jnp.dot(m, xm, preferred_element_type=F32)
                acc = part if acc is None else acc + part
            y_parts.append(acc)
        s_prev = state_ref[g]
        y_off = jnp.dot(cg, s_prev.astype(BF16), preferred_element_type=F32)
        y_parts[-2] = y_parts[-2] + y_off[:, :LANES] * ea_e[:, g * gw:g * gw + LANES]
        y_parts[-1] = y_parts[-1] + y_off[:, LANES:] * ea_e[:, g * gw + LANES:(g + 1) * gw]
        bgt = bg.astype(F32).T.astype(BF16)
        upd = jnp.dot(bgt, wst[:, g * gw:(g + 1) * gw], preferred_element_type=F32)
        state_ref[g] = s_prev * cd_e[:, g * gw:(g + 1) * gw] + upd

    y = jnp.concatenate(y_parts, axis=1) + dskip_ref[...] * xs
    y = y * zs_ref[rows, :]
    y_ref[rows, :] = _rms(y, ng_ref[...]).astype(BF16)


def _ssd(zs, xs, bc, small, bias_pad, a_pad, dskip_e, ng, tri, expand):
    s = zs.shape[0]
    t = SSD_T
    st = SSD_STEP
    nsub = st // t
    const = lambda i: (0, 0)
    row = lambda i: (i, 0)
    place_np = np.zeros((LANES, FOX_WIDTH), np.float32)
    for hd in range(FOX_HEADS):
        pair, odd = divmod(hd, 2)
        c0 = pair * LANES + (0 if odd else HEAD_DIM)
        place_np[FOX_HEADS + hd, c0:c0 + HEAD_DIM] = 1.0
    place = jnp.asarray(place_np, dtype=BF16)
    return pl.pallas_call(
        _ssd_kernel,
        grid=(s // st,),
        in_specs=[
            pl.BlockSpec((st, SSM_WIDTH), row),
            pl.BlockSpec((st, SSM_WIDTH), row),
            pl.BlockSpec((st, SSM_XBC - SSM_WIDTH), row),
            pl.BlockSpec((st, LANES), row),
            pl.BlockSpec((1, LANES), const),
            pl.BlockSpec((1, LANES), const),
            pl.BlockSpec((1, SSM_WIDTH), const),
            pl.BlockSpec((1, SSM_WIDTH), const),
            pl.BlockSpec((t, t), const),
            pl.BlockSpec((LANES, SSM_WIDTH), const),
            pl.BlockSpec((LANES, FOX_WIDTH), const),
        ],
        out_specs=[
            pl.BlockSpec((st, SSM_WIDTH), row),
            pl.BlockSpec((st, FOX_WIDTH), row),
            pl.BlockSpec((16, st), lambda i: (0, i)),
            pl.BlockSpec((nsub, 8, LANES), lambda i: (i, 0, 0)),
            pl.BlockSpec((nsub, 8, LANES), lambda i: (i, 0, 0)),
        ],
        out_shape=[
            jax.ShapeDtypeStruct((s, SSM_WIDTH), BF16),
            jax.ShapeDtypeStruct((s, FOX_WIDTH), BF16),
            jax.ShapeDtypeStruct((16, s), F32),
            jax.ShapeDtypeStruct((s // t, 8, LANES), F32),
            jax.ShapeDtypeStruct((s // t, 8, LANES), F32),
        ],
        scratch_shapes=[
            pltpu.VMEM((SSM_GROUPS, SSM_STATE, SSM_WIDTH // SSM_GROUPS), F32),
            pltpu.VMEM((1, LANES), F32),
        ],
        compiler_params=pltpu.CompilerParams(
            dimension_semantics=("arbitrary",),
            vmem_limit_bytes=VMEM_LIMIT),
        name="ssd",
    )(zs, xs, bc, small, bias_pad, a_pad, dskip_e, ng, tri, expand, place)


def _fox_kernel(nkv_ref, q_ref, kt_ref, v_ref, qaug_ref, cumt_ref, o_ref,
                kaug_ref, m_ref, l_ref, acc_ref):
    tq, tk = ATT_TQ, ATT_TK
    nb = kt_ref.shape[0]
    nq = pl.num_programs(1)
    p = pl.program_id(0)
    i = pl.program_id(1)

    @pl.when(i == 0)
    def _():
        row16 = lax.broadcasted_iota(jnp.int32, (16, tk), 0)
        zeros48 = jnp.zeros((48, tk), BF16)

        def build(j, carry):
            off = pl.multiple_of(j * tk, tk)
            kt = kt_ref[j]
            for hh in range(2):
                ck = cumt_ref[pl.ds(FOX_HEADS + 2 * p + hh, 1), pl.ds(off, tk)] * LOG2E
                hi, mid, lo = _split3(ck)
                bias = jnp.where(row16 < 3, 1.0,
                                 jnp.where(row16 == 3, -hi,
                                           jnp.where(row16 == 4, -mid,
                                                     jnp.where(row16 == 5, -lo, 0.0))))
                bias = bias.astype(BF16)
                if hh == 0:
                    kaug_ref[0, j, 0:64, :] = kt[0:64]
                    kaug_ref[0, j, 64:80, :] = bias
                    kaug_ref[0, j, 80:128, :] = zeros48
                else:
                    kaug_ref[1, j, 0:16, :] = bias
                    kaug_ref[1, j, 16:64, :] = zeros48
                    kaug_ref[1, j, 64:128, :] = kt[64:128]
            return carry

        lax.fori_loop(0, nb, build, 0)

    q = q_ref[...]
    lane = lax.broadcasted_iota(jnp.int32, (tq, LANES), 1)
    lo_half = lane < HEAD_DIM
    r_i = lax.broadcasted_iota(jnp.int32, (tq, tk), 0)
    c_i = lax.broadcasted_iota(jnp.int32, (tq, tk), 1)
    causal = r_i >= c_i

    aug = qaug_ref[...]
    qas = [jnp.where(lo_half, q, aug), jnp.where(lo_half, aug, q)]

    def logits(hh, j):
        return jnp.dot(qas[hh], kaug_ref[hh, j], preferred_element_type=F32)

    def vblock(j):
        return v_ref[pl.ds(pl.multiple_of(j * tk, tk), tk), :]

    def lane_fold(x):
        out = x[:, 0:LANES]
        for c in range(1, x.shape[1] // LANES):
            out = out + x[:, c * LANES:(c + 1) * LANES]
        return out

    def online_diag(hh):
        s = jnp.where(causal, logits(hh, i), NEG_INF)
        m0 = jnp.max(s, axis=1, keepdims=True)
        p0 = jnp.exp2(s - m0)
        m_ref[hh] = jnp.broadcast_to(m0, (tq, LANES))
        l_ref[hh] = lane_fold(p0)
        acc_ref[hh] = jnp.dot(p0.astype(BF16), vblock(i), preferred_element_type=F32)

    def online_step(hh, j):
        s = logits(hh, j)
        m_prev = m_ref[hh]
        m_new = jnp.maximum(m_prev, jnp.max(s, axis=1, keepdims=True))
        alpha = jnp.exp2(m_prev - m_new)
        pj = jnp.exp2(s - m_new[:, 0:1])
        l_ref[hh] = alpha * l_ref[hh] + lane_fold(pj)
        acc_ref[hh] = alpha * acc_ref[hh] + jnp.dot(
            pj.astype(BF16), vblock(j), preferred_element_type=F32)
        m_ref[hh] = m_new

    half = tq // 2

    def fixed_diag(hh):
        vb = vblock(i)
        s_top = jnp.dot(qas[hh][0:half], kaug_ref[hh, i, :, 0:half],
                        preferred_element_type=F32)
        p_top = jnp.exp2(jnp.where(causal[0:half, 0:half], s_top, NEG_INF))
        s_bot = jnp.dot(qas[hh][half:], kaug_ref[hh, i], preferred_element_type=F32)
        p_bot = jnp.exp2(jnp.where(causal[half:, :], s_bot, NEG_INF))
        l_ref[hh, 0:half] = lane_fold(p_top)
        l_ref[hh, half:] = lane_fold(p_bot)
        acc_ref[hh, 0:half] = jnp.dot(p_top.astype(BF16), vb[0:half],
                                      preferred_element_type=F32)
        acc_ref[hh, half:] = jnp.dot(p_bot.astype(BF16), vb, preferred_element_type=F32)

    def fixed_step(hh, j):
        pj = jnp.exp2(logits(hh, j))
        l_ref[hh] += lane_fold(pj)
        acc_ref[hh] += jnp.dot(pj.astype(BF16), vblock(j), preferred_element_type=F32)

    def fixed_multi(hh, j, n):
        first = j - (n - 1)
        pj = jnp.exp2(jnp.concatenate([logits(hh, first + b) for b in range(n)], axis=1))
        l_ref[hh] += lane_fold(pj)
        vn = v_ref[pl.ds(pl.multiple_of(first * tk, tk), n * tk), :]
        acc_ref[hh] += jnp.dot(pj.astype(BF16), vn, preferred_element_type=F32)

    for hh in range(2):
        code = nkv_ref[(2 * p + hh) * nq + i]
        n_off = code >> 1
        fixed_ok = (code & 1) == 1

        @pl.when(fixed_ok)
        def _(hh=hh, n_off=n_off):
            fixed_diag(hh)

            def body(u, carry):
                fixed_multi(hh, i - 1 - 4 * u, 4)
                return carry

            quads = n_off >> 2
            lax.fori_loop(0, quads, body, 0)
            done = 4 * quads

            @pl.when((n_off & 2) == 2)
            def _():
                fixed_multi(hh, i - 1 - done, 2)

            @pl.when((n_off & 1) == 1)
            def _():
                fixed_step(hh, i - n_off)

        @pl.when(jnp.logical_not(fixed_ok))
        def _(hh=hh, n_off=n_off):
            online_diag(hh)

            def body(jj, carry):
                online_step(hh, i - 1 - jj)
                return carry

            lax.fori_loop(0, n_off, body, 0)

    l0 = jnp.sum(l_ref[0], axis=1, keepdims=True)
    l1 = jnp.sum(l_ref[1], axis=1, keepdims=True)
    o_ref[...] = jnp.where(lo_half, acc_ref[0] / l0, acc_ref[1] / l1)


def _kv_counts(qn_tiles, kn_tiles, cmax_chunks, cmin_chunks):
    nq = qn_tiles.shape[0]
    per_q = cmax_chunks.shape[0] // nq
    qn = jnp.sqrt(qn_tiles[:, 0, :FOX_HEADS])
    kn = jnp.sqrt(jnp.max(kn_tiles[:, :FOX_HEADS, 0], axis=0))
    gate = slice(FOX_HEADS, 2 * FOX_HEADS)
    cmax = jnp.max(cmax_chunks[:, 0, gate].reshape(nq, per_q, FOX_HEADS), axis=1) * LOG2E
    cmin = jnp.min(cmin_chunks[:, 0, gate].reshape(nq, per_q, FOX_HEADS), axis=1) * LOG2E
    pmin = lax.cummin(cmin, axis=0)
    qk = NORM_MARGIN * qn * kn[None, :] + 1.0
    bound = (2.0 * qk + cmax)[:, None, :] - pmin[None, :, :]
    ii = lax.broadcasted_iota(jnp.int32, bound.shape, 0)
    jj = lax.broadcasted_iota(jnp.int32, bound.shape, 1)
    skip = (bound < SKIP_LOG2) & (jj < ii)
    jstar = jnp.max(jnp.where(skip, jj, -1), axis=1)
    n_off = jnp.arange(nq, dtype=jnp.int32)[:, None] - 1 - jstar
    pmin_prev = jnp.concatenate([jnp.full((1, FOX_HEADS), jnp.inf, F32), pmin[:-1]], axis=0)
    fixed_ok = (qk + jnp.maximum(cmax - pmin_prev, 0.0)) < FIXED_MAX_LOG2
    code = 2 * n_off + fixed_ok.astype(jnp.int32)
    return code.T.reshape(-1).astype(jnp.int32)


def _fox(nkv, q, kt3, v, qaug, cumt):
    s = q.shape[0]
    tq, tk = ATT_TQ, ATT_TK
    nb = s // tk
    grid_spec = pltpu.PrefetchScalarGridSpec(
        num_scalar_prefetch=1,
        grid=(FOX_HEADS // 2, s // tq),
        in_specs=[
            pl.BlockSpec((tq, LANES), lambda p, i, n: (i, p)),
            pl.BlockSpec((nb, LANES, tk), lambda p, i, n: (0, p, 0)),
            pl.BlockSpec((s, LANES), lambda p, i, n: (0, p)),
            pl.BlockSpec((tq, LANES), lambda p, i, n: (i, p)),
            pl.BlockSpec((16, s), lambda p, i, n: (0, 0)),
        ],
        out_specs=pl.BlockSpec((tq, LANES), lambda p, i, n: (i, p)),
        scratch_shapes=[
            pltpu.VMEM((2, nb, LANES, tk), BF16),
            pltpu.VMEM((2, tq, LANES), F32),
            pltpu.VMEM((2, tq, LANES), F32),
            pltpu.VMEM((2, tq, LANES), F32),
        ],
    )
    return pl.pallas_call(
        _fox_kernel,
        grid_spec=grid_spec,
        out_shape=jax.ShapeDtypeStruct((s, FOX_WIDTH), F32),
        compiler_params=pltpu.CompilerParams(
            dimension_semantics=("arbitrary", "arbitrary"),
            vmem_limit_bytes=VMEM_LIMIT),
        name="fox_attn",
    )(nkv, q, kt3, v, qaug, cumt)


def _pad_lanes(vec_dt, vec_f):
    out = jnp.zeros((1, LANES), F32)
    out = out.at[0, 0:SSM_HEADS].set(vec_dt.astype(F32))
    if vec_f is not None:
        out = out.at[0, SSM_HEADS:SSM_HEADS + FOX_HEADS].set(vec_f.astype(F32))
    return out


def _layer(x, ffn1_norm, ffn1_w_in, ffn1_w_out, mix_norm, w_in, conv_w, conv_b, dt_bias,
           a_log, d_skip, ssm_norm, f_bias, fox_norm, w_out, ffn2_norm, ffn2_w_in,
           ffn2_w_out, final_g, *, final):
    ones = jnp.ones((1, D_MODEL), F32)
    row = lambda a: a.reshape(1, -1).astype(F32)

    x1 = _ffn(x, row(ffn1_norm), ffn1_w_in.astype(BF16), ffn1_w_out.astype(BF16), ones,
              final=False)

    o0 = SSM_WIDTH
    o1 = o0 + SSM_XBC
    o2 = o1 + SSM_HEADS
    o3 = o2 + 3 * FOX_WIDTH
    wz = w_in[:, :o0].astype(BF16)
    wxbc = w_in[:, o0:o1].astype(BF16)
    w_dt = w_in[:, o1:o2]
    wq = w_in[:, o2:o2 + FOX_WIDTH].astype(BF16)
    wkt = w_in[:, o2 + FOX_WIDTH:o2 + 2 * FOX_WIDTH].astype(BF16)
    wv = w_in[:, o2 + 2 * FOX_WIDTH:o3].astype(BF16)
    w_f = w_in[:, o3:]
    ws = jnp.concatenate(
        [w_dt, w_f, jnp.zeros((D_MODEL, LANES - SSM_HEADS - FOX_HEADS), w_in.dtype)],
        axis=1).astype(BF16)
    zs, xs, bc, q, kt3, v, small, qn2, kn2 = _inproj(
        x1, row(mix_norm), wz, wxbc, wq, wkt, wv, ws, conv_w.astype(F32), row(conv_b))

    bias_pad = _pad_lanes(dt_bias, f_bias)
    a_pad = _pad_lanes(-jnp.exp(a_log.astype(F32)), None)
    dskip_e = jnp.repeat(d_skip.astype(F32), HEAD_DIM).reshape(1, SSM_WIDTH)
    tri = jnp.asarray(np.tril(np.ones((SSD_T, SSD_T), np.float32)), dtype=BF16)
    expand_np = np.zeros((LANES, SSM_WIDTH), np.float32)
    for hd in range(SSM_HEADS):
        expand_np[hd, hd * HEAD_DIM:(hd + 1) * HEAD_DIM] = 1.0
    y_ssd, qaug, cumt, cmax_c, cmin_c = _ssd(zs, xs, bc, small, bias_pad, a_pad,
                            dskip_e, row(ssm_norm), tri, jnp.asarray(expand_np, dtype=BF16))

    o_fox = _fox(_kv_counts(qn2, kn2, cmax_c, cmin_c), q, kt3, v, qaug, cumt)

    wo = w_out.astype(BF16)
    mix = (y_ssd, o_fox, row(fox_norm), wo[:SSM_WIDTH], wo[SSM_WIDTH:])
    return _ffn(x1, row(ffn2_norm), ffn2_w_in.astype(BF16), ffn2_w_out.astype(BF16),
                row(final_g), final=final, mix=mix)


def kernel(x, ffn1_norm, ffn1_w_in, ffn1_w_out, mix_norm, w_in, conv_w, conv_b, dt_bias, a_log,
           d_skip, ssm_norm, f_bias, fox_norm, w_out, ffn2_norm, ffn2_w_in, ffn2_w_out, final_norm):
    b, s, d = x.shape
    depth = ffn1_norm.shape[0]
    outs = []
    for bi in range(b):
        xb = x[bi]
        for l in range(depth):
            xb = _layer(xb, ffn1_norm[l], ffn1_w_in[l], ffn1_w_out[l], mix_norm[l], w_in[l],
                        conv_w[l], conv_b[l], dt_bias[l], a_log[l], d_skip[l], ssm_norm[l],
                        f_bias[l], fox_norm[l], w_out[l], ffn2_norm[l], ffn2_w_in[l],
                        ffn2_w_out[l], final_norm, final=(l == depth - 1))
        outs.append(xb)
    return jnp.stack(outs, axis=0)
```

```python
import functools

import jax
import jax.numpy as jnp
import numpy as np
from jax import lax
from jax.experimental import pallas as pl
from jax.experimental.pallas import tpu as pltpu

F32 = jnp.float32
BF16 = jnp.bfloat16
HIGHEST = lax.Precision.HIGHEST

D_MODEL = 1024
SEQ = 16384
HEAD_DIM = 64
SSM_WIDTH = 512
SSM_HEADS = 8
SSM_GROUPS = 2
SSM_STATE = 128
CONV_WIDTH = 4
SSM_XBC = SSM_WIDTH + 2 * SSM_GROUPS * SSM_STATE
FOX_WIDTH = 512
FOX_HEADS = 8
D_FF = 2816
EPS = 1e-6

LANES = 128
VMEM_LIMIT = 56 * 1024 * 1024

FFN_TM = 1024
FFN_TF = 256
PROJ_TM = 512
SSD_T = 256
SSD_STEP = 1024
ATT_TQ = 512
ATT_TK = 512
NEG_INF = float("-inf")
LOG2E = 1.4426950408889634
SKIP_LOG2 = -150.0
NORM_MARGIN = 1.05
FIXED_MAX_LOG2 = 64.0


def _rms(x, g):
    ms = jnp.mean(x * x, axis=-1, keepdims=True)
    return x * lax.rsqrt(ms + EPS) * g


def _silu(x):
    return x * (1.0 / (1.0 + jnp.exp(-x)))


def _softplus(x):
    return jnp.maximum(x, 0.0) + jnp.log1p(jnp.exp(-jnp.abs(x)))


def _split3(x):
    hi = x.astype(BF16).astype(F32)
    r = x - hi
    mid = r.astype(BF16).astype(F32)
    lo = (r - mid).astype(BF16).astype(F32)
    return hi, mid, lo


def _dot_f32_rhs(a_bf16, x):
    hi, mid, lo = _split3(x)
    d = lambda piece: jnp.dot(a_bf16, piece.astype(BF16), preferred_element_type=F32)
    return (d(lo) + d(mid)) + d(hi)


def _dot_f32_lhs(x, b_bf16):
    hi, mid, lo = _split3(x)
    d = lambda piece: jnp.dot(piece.astype(BF16), b_bf16, preferred_element_type=F32)
    return (d(lo) + d(mid)) + d(hi)


def _ffn_kernel(*refs, mix, final):
    if mix:
        (x_ref, ys_ref, of_ref, mg_ref, wm1_ref, wm2_ref,
         g_ref, wg_ref, wu_ref, wo_ref, fg_ref, o_ref) = refs
        yf = _rms(of_ref[...], mg_ref[...]).astype(BF16)
        x = (x_ref[...]
             + jnp.dot(ys_ref[...], wm1_ref[...], preferred_element_type=F32)
             + jnp.dot(yf, wm2_ref[...], preferred_element_type=F32))
    else:
        x_ref, g_ref, wg_ref, wu_ref, wo_ref, fg_ref, o_ref = refs
        x = x_ref[...]
    h = _rms(x, g_ref[...]).astype(BF16)
    o_ref[...] = 2.0 * x
    for f in range(D_FF // FFN_TF):
        cols = slice(f * FFN_TF, (f + 1) * FFN_TF)
        gate = jnp.dot(h, wg_ref[:, cols], preferred_element_type=F32)
        up = jnp.dot(h, wu_ref[:, cols], preferred_element_type=F32)
        act = (_silu(gate) * up).astype(BF16)
        o_ref[...] += jnp.dot(act, wo_ref[cols, :], preferred_element_type=F32)
    r = 0.5 * o_ref[...]
    if final:
        r = _rms(r, fg_ref[...])
    o_ref[...] = r


def _ffn(x, g, w_in, w_out, fg, *, final, mix=None):
    s = x.shape[0]
    tm = FFN_TM
    row = lambda i: (i, 0)
    const = lambda i: (0, 0)
    resident = functools.partial(pl.BlockSpec, pipeline_mode=pl.Buffered(1))
    x_spec = pl.BlockSpec((tm, D_MODEL), row)
    vec_spec = pl.BlockSpec((1, D_MODEL), const)
    ffn_specs = [
        vec_spec,
        resident((D_MODEL, D_FF), lambda i: (0, 0)),
        resident((D_MODEL, D_FF), lambda i: (0, 1)),
        resident((D_FF, D_MODEL), const),
        vec_spec,
    ]
    ffn_args = (g, w_in, w_in, w_out, fg)
    if mix is None:
        in_specs = [x_spec] + ffn_specs
        args = (x,) + ffn_args
    else:
        ys, of, mg, wm1, wm2 = mix
        in_specs = [
            x_spec,
            pl.BlockSpec((tm, SSM_WIDTH), row),
            pl.BlockSpec((tm, FOX_WIDTH), row),
            pl.BlockSpec((1, FOX_WIDTH), const),
            resident((SSM_WIDTH, D_MODEL), const),
            resident((FOX_WIDTH, D_MODEL), const),
        ] + ffn_specs
        args = (x, ys, of, mg, wm1, wm2) + ffn_args
    return pl.pallas_call(
        functools.partial(_ffn_kernel, mix=mix is not None, final=final),
        grid=(s // tm,),
        in_specs=in_specs,
        out_specs=pl.BlockSpec((tm, D_MODEL), row),
        out_shape=jax.ShapeDtypeStruct((s, D_MODEL), F32),
        compiler_params=pltpu.CompilerParams(
            dimension_semantics=("parallel",),
            vmem_limit_bytes=VMEM_LIMIT),
        name="ffn_final" if final else "ffn",
    )(*args)


def _inproj_kernel(x_ref, g_ref, wz_ref, wxbc_ref, wq_ref, wkt_ref, wv_ref, ws_ref,
                   ind_ref, indt_ref, cw_ref, cb_ref,
                   zs_ref, xs_ref, bc_ref, q_ref, kt_ref, v_ref, small_ref, qn_ref, kn_ref,
                   halo_ref):
    tm = PROJ_TM
    i = pl.program_id(0)

    @pl.when(i == 0)
    def _():
        halo_ref[...] = jnp.zeros_like(halo_ref)

    h = _rms(x_ref[...], g_ref[...]).astype(BF16)
    xb = jnp.dot(h, wxbc_ref[...], preferred_element_type=F32)
    zs_ref[...] = _silu(jnp.dot(h, wz_ref[...], preferred_element_type=F32))
    q = jnp.dot(h, wq_ref[...], preferred_element_type=F32)
    qb = (q * (HEAD_DIM ** -0.5 * LOG2E)).astype(BF16)
    q_ref[...] = qb
    kt = lax.dot_general(wkt_ref[...], h, (((0,), (1,)), ((), ())),
                         preferred_element_type=F32)
    ktb = kt.astype(BF16)
    kt_ref[0] = ktb
    v_ref[...] = jnp.dot(h, wv_ref[...], preferred_element_type=F32).astype(BF16)
    small_ref[...] = jnp.dot(h, ws_ref[...], preferred_element_type=F32)
    qf = qb.astype(F32)
    qn = jnp.dot((qf * qf).astype(BF16), ind_ref[...], preferred_element_type=F32)
    qn_ref[0] = jnp.broadcast_to(jnp.max(qn, axis=0, keepdims=True), (8, LANES))
    kf = ktb.astype(F32)
    kn = jnp.dot(indt_ref[...], (kf * kf).astype(BF16), preferred_element_type=F32)
    kn_ref[0] = jnp.broadcast_to(jnp.max(kn, axis=1, keepdims=True), (16, LANES))
    prev = halo_ref[...]
    row8 = lax.broadcasted_iota(jnp.int32, (8, SSM_XBC), 0)
    conv = cb_ref[...] + xb * cw_ref[CONV_WIDTH - 1:CONV_WIDTH, :]
    for s in range(1, CONV_WIDTH):
        rolled = pltpu.roll(xb, s, axis=0)
        head = jnp.where(row8 < s, pltpu.roll(prev, s, axis=0), rolled[0:8])
        shifted = jnp.concatenate([head, rolled[8:]], axis=0)
        k = CONV_WIDTH - 1 - s
        conv = conv + shifted * cw_ref[k:k + 1, :]
    halo_ref[...] = xb[tm - 8:tm]
    u = _silu(conv)
    xs_ref[...] = u[:, :SSM_WIDTH]
    bc_ref[...] = u[:, SSM_WIDTH:].astype(BF16)


def _inproj(x, g, wz, wxbc, wq, wkt, wv, ws, cw, cb):
    s = x.shape[0]
    tm = PROJ_TM
    nb = s // tm
    const = lambda i: (0, 0)
    row = lambda i: (i, 0)
    ind_np = np.zeros((FOX_WIDTH, LANES), np.float32)
    for hd in range(FOX_HEADS):
        ind_np[hd * HEAD_DIM:(hd + 1) * HEAD_DIM, hd] = 1.0
    ind = jnp.asarray(ind_np, dtype=BF16)
    indt = jnp.asarray(ind_np.T[:16], dtype=BF16)
    return pl.pallas_call(
        _inproj_kernel,
        grid=(nb,),
        in_specs=[
            pl.BlockSpec((tm, D_MODEL), row),
            pl.BlockSpec((1, D_MODEL), const),
            pl.BlockSpec((D_MODEL, SSM_WIDTH), const),
            pl.BlockSpec((D_MODEL, SSM_XBC), const),
            pl.BlockSpec((D_MODEL, FOX_WIDTH), const),
            pl.BlockSpec((D_MODEL, FOX_WIDTH), const),
            pl.BlockSpec((D_MODEL, FOX_WIDTH), const),
            pl.BlockSpec((D_MODEL, LANES), const),
            pl.BlockSpec((FOX_WIDTH, LANES), const),
            pl.BlockSpec((16, FOX_WIDTH), const),
            pl.BlockSpec((CONV_WIDTH, SSM_XBC), const),
            pl.BlockSpec((1, SSM_XBC), const),
        ],
        out_specs=[
            pl.BlockSpec((tm, SSM_WIDTH), row),
            pl.BlockSpec((tm, SSM_WIDTH), row),
            pl.BlockSpec((tm, SSM_XBC - SSM_WIDTH), row),
            pl.BlockSpec((tm, FOX_WIDTH), row),
            pl.BlockSpec((1, FOX_WIDTH, tm), lambda i: (i, 0, 0)),
            pl.BlockSpec((tm, FOX_WIDTH), row),
            pl.BlockSpec((tm, LANES), row),
            pl.BlockSpec((1, 8, LANES), lambda i: (i, 0, 0)),
            pl.BlockSpec((1, 16, LANES), lambda i: (i, 0, 0)),
        ],
        out_shape=[
            jax.ShapeDtypeStruct((s, SSM_WIDTH), F32),
            jax.ShapeDtypeStruct((s, SSM_WIDTH), F32),
            jax.ShapeDtypeStruct((s, SSM_XBC - SSM_WIDTH), BF16),
            jax.ShapeDtypeStruct((s, FOX_WIDTH), BF16),
            jax.ShapeDtypeStruct((nb, FOX_WIDTH, tm), BF16),
            jax.ShapeDtypeStruct((s, FOX_WIDTH), BF16),
            jax.ShapeDtypeStruct((s, LANES), F32),
            jax.ShapeDtypeStruct((nb, 8, LANES), F32),
            jax.ShapeDtypeStruct((nb, 16, LANES), F32),
        ],
        scratch_shapes=[pltpu.VMEM((8, SSM_XBC), F32)],
        compiler_params=pltpu.CompilerParams(
            dimension_semantics=("arbitrary",),
            vmem_limit_bytes=VMEM_LIMIT),
        name="in_proj",
    )(x, g, wz, wxbc, wq, wkt, wv, ws, ind, indt, cw, cb)


def _ssd_kernel(zs_ref, xs_ref, bc_ref, small_ref, bias_ref, apad_ref,
                dskip_ref, ng_ref, tri_ref, expand_ref, place_ref,
                y_ref, qaug_ref, cumt_ref, cmax_ref, cmin_ref,
                state_ref, carry_ref):
    t_rows = SSD_T
    i = pl.program_id(0)

    @pl.when(i == 0)
    def _():
        state_ref[...] = jnp.zeros_like(state_ref)
        carry_ref[...] = jnp.zeros_like(carry_ref)

    for c in range(SSD_STEP // SSD_T):
        _ssd_chunk(slice(c * t_rows, (c + 1) * t_rows), c,
                   zs_ref, xs_ref, bc_ref, small_ref, bias_ref, apad_ref,
                   dskip_ref, ng_ref, tri_ref, expand_ref, place_ref,
                   y_ref, qaug_ref, cumt_ref, cmax_ref, cmin_ref, state_ref, carry_ref)


def _ssd_chunk(rows, c, zs_ref, xs_ref, bc_ref, small_ref, bias_ref, apad_ref,
               dskip_ref, ng_ref, tri_ref, expand_ref, place_ref,
               y_ref, qaug_ref, cumt_ref, cmax_ref, cmin_ref, state_ref, carry_ref):
    t_rows = SSD_T
    xs = xs_ref[rows, :]
    bm = bc_ref[rows, 0:SSM_GROUPS * SSM_STATE]
    cm = bc_ref[rows, SSM_GROUPS * SSM_STATE:]

    t = small_ref[rows, :] + bias_ref[...]
    lane = lax.broadcasted_iota(jnp.int32, (t_rows, LANES), 1)
    is_dt = lane < SSM_HEADS
    sp = _softplus(jnp.where(is_dt, t, -t))
    v = jnp.where(is_dt, sp * apad_ref[...], -sp)
    cs = _dot_f32_rhs(tri_ref[...], v)
    cs = cs + carry_ref[...]
    last = cs[t_rows - 1:t_rows, :]
    carry_ref[...] = jnp.where(lane[0:1, :] < SSM_HEADS, 0.0, last)
    cst = cs.T
    cumt_ref[:, rows] = cst[0:16, :]
    lane4 = lax.broadcasted_iota(jnp.int32, (t_rows, FOX_WIDTH), 1) & (HEAD_DIM - 1)
    placed = [jnp.dot(piece.astype(BF16), place_ref[...], preferred_element_type=F32)
              for piece in _split3(cs * LOG2E)]
    qaug_ref[rows, :] = jnp.where(
        lane4 == 0, placed[0],
        jnp.where(lane4 == 1, placed[1],
                  jnp.where(lane4 == 2, placed[2],
                            jnp.where(lane4 < 6, 1.0, 0.0)))).astype(BF16)
    cmax_ref[c] = jnp.broadcast_to(jnp.max(cs, axis=0, keepdims=True), (8, LANES))
    cmin_ref[c] = jnp.broadcast_to(jnp.min(cs, axis=0, keepdims=True), (8, LANES))

    stacked = jnp.concatenate([sp, cs], axis=0)
    exp2 = _dot_f32_lhs(stacked, expand_ref[...])
    dt_e = exp2[0:t_rows]
    cs_e = exp2[t_rows:2 * t_rows]
    ea_e = jnp.exp(cs_e)
    de_e = jnp.exp(cs_e[t_rows - 1:t_rows, :] - cs_e)
    cd_e = ea_e[t_rows - 1:t_rows, :]

    xdt = xs * dt_e
    xdt_b = xdt.astype(BF16)
    wst = (xdt * de_e).astype(BF16)

    r_i = lax.broadcasted_iota(jnp.int32, (t_rows, t_rows), 0)
    c_i = lax.broadcasted_iota(jnp.int32, (t_rows, t_rows), 1)
    causal = r_i >= c_i
    lane_pair = lax.broadcasted_iota(jnp.int32, (t_rows, LANES), 1)
    lo = lane_pair < HEAD_DIM

    gw = SSM_WIDTH // SSM_GROUPS
    hpg = SSM_HEADS // SSM_GROUPS
    y_parts = []
    for g in range(SSM_GROUPS):
        cg = cm[:, g * SSM_STATE:(g + 1) * SSM_STATE]
        bg = bm[:, g * SSM_STATE:(g + 1) * SSM_STATE]
        gmat = lax.dot_general(cg, bg, (((1,), (1,)), ((), ())),
                               preferred_element_type=F32)
        for pr in range(hpg // 2):
            c0 = g * gw + pr * LANES
            xpair = xdt_b[:, c0:c0 + LANES]
            acc = None
            for hh in range(2):
                h = g * hpg + pr * 2 + hh
                seg = cs[:, h:h + 1] - cst[h:h + 1, :]
                dec = jnp.exp(jnp.where(causal, seg, NEG_INF))
                m = (gmat * dec).astype(BF16)
                xm = jnp.where(lo if hh == 0 else jnp.logical_not(lo), xpair,
                               jnp.zeros_like(xpair))
                part = jnp.dot(m, xm, preferred_element_type=F32)
                acc = part if acc is None else acc + part
            y_parts.append(acc)
        s_prev = state_ref[g]
        y_off = jnp.dot(cg, s_prev.astype(BF16), preferred_element_type=F32)
        y_parts[-2] = y_parts[-2] + y_off[:, :LANES] * ea_e[:, g * gw:g * gw + LANES]
        y_parts[-1] = y_parts[-1] + y_off[:, LANES:] * ea_e[:, g * gw + LANES:(g + 1) * gw]
        bgt = bg.astype(F32).T.astype(BF16)
        upd = jnp.dot(bgt, wst[:, g * gw:(g + 1) * gw], preferred_element_type=F32)
        state_ref[g] = s_prev * cd_e[:, g * gw:(g + 1) * gw] + upd

    y = jnp.concatenate(y_parts, axis=1) + dskip_ref[...] * xs
    y = y * zs_ref[rows, :]
    y_ref[rows, :] = _rms(y, ng_ref[...]).astype(BF16)


def _ssd(zs, xs, bc, small, bias_pad, a_pad, dskip_e, ng, tri, expand):
    s = zs.shape[0]
    t = SSD_T
    st = SSD_STEP
    nsub = st // t
    const = lambda i: (0, 0)
    row = lambda i: (i, 0)
    place_np = np.zeros((LANES, FOX_WIDTH), np.float32)
    for hd in range(FOX_HEADS):
        pair, odd = divmod(hd, 2)
        c0 = pair * LANES + (0 if odd else HEAD_DIM)
        place_np[FOX_HEADS + hd, c0:c0 + HEAD_DIM] = 1.0
    place = jnp.asarray(place_np, dtype=BF16)
    return pl.pallas_call(
        _ssd_kernel,
        grid=(s // st,),
        in_specs=[
            pl.BlockSpec((st, SSM_WIDTH), row),
            pl.BlockSpec((st, SSM_WIDTH), row),
            pl.BlockSpec((st, SSM_XBC - SSM_WIDTH), row),
            pl.BlockSpec((st, LANES), row),
            pl.BlockSpec((1, LANES), const),
            pl.BlockSpec((1, LANES), const),
            pl.BlockSpec((1, SSM_WIDTH), const),
            pl.BlockSpec((1, SSM_WIDTH), const),
            pl.BlockSpec((t, t), const),
            pl.BlockSpec((LANES, SSM_WIDTH), const),
            pl.BlockSpec((LANES, FOX_WIDTH), const),
        ],
        out_specs=[
            pl.BlockSpec((st, SSM_WIDTH), row),
            pl.BlockSpec((st, FOX_WIDTH), row),
            pl.BlockSpec((16, st), lambda i: (0, i)),
            pl.BlockSpec((nsub, 8, LANES), lambda i: (i, 0, 0)),
            pl.BlockSpec((nsub, 8, LANES), lambda i: (i, 0, 0)),
        ],
        out_shape=[
            jax.ShapeDtypeStruct((s, SSM_WIDTH), BF16),
            jax.ShapeDtypeStruct((s, FOX_WIDTH), BF16),
            jax.ShapeDtypeStruct((16, s), F32),
            jax.ShapeDtypeStruct((s // t, 8, LANES), F32),
            jax.ShapeDtypeStruct((s // t, 8, LANES), F32),
        ],
        scratch_shapes=[
            pltpu.VMEM((SSM_GROUPS, SSM_STATE, SSM_WIDTH // SSM_GROUPS), F32),
            pltpu.VMEM((1, LANES), F32),
        ],
        compiler_params=pltpu.CompilerParams(
            dimension_semantics=("arbitrary",),
            vmem_limit_bytes=VMEM_LIMIT),
        name="ssd",
    )(zs, xs, bc, small, bias_pad, a_pad, dskip_e, ng, tri, expand, place)


def _fox_kernel(nkv_ref, q_ref, kt_ref, v_ref, qaug_ref, cumt_ref, o_ref,
                kaug_ref, m_ref, l_ref, acc_ref):
    tq, tk = ATT_TQ, ATT_TK
    nb = kt_ref.shape[0]
    nq = pl.num_programs(1)
    p = pl.program_id(0)
    i = pl.program_id(1)

    @pl.when(i == 0)
    def _():
        row16 = lax.broadcasted_iota(jnp.int32, (16, tk), 0)
        zeros48 = jnp.zeros((48, tk), BF16)

        def build(j, carry):
            off = pl.multiple_of(j * tk, tk)
            kt = kt_ref[j]
            for hh in range(2):
                ck = cumt_ref[pl.ds(FOX_HEADS + 2 * p + hh, 1), pl.ds(off, tk)] * LOG2E
                hi, mid, lo = _split3(ck)
                bias = jnp.where(row16 < 3, 1.0,
                                 jnp.where(row16 == 3, -hi,
                                           jnp.where(row16 == 4, -mid,
                                                     jnp.where(row16 == 5, -lo, 0.0))))
                bias = bias.astype(BF16)
                if hh == 0:
                    kaug_ref[0, j, 0:64, :] = kt[0:64]
                    kaug_ref[0, j, 64:80, :] = bias
                    kaug_ref[0, j, 80:128, :] = zeros48
                else:
                    kaug_ref[1, j, 0:16, :] = bias
                    kaug_ref[1, j, 16:64, :] = zeros48
                    kaug_ref[1, j, 64:128, :] = kt[64:128]
            return carry

        lax.fori_loop(0, nb, build, 0)

    q = q_ref[...]
    lane = lax.broadcasted_iota(jnp.int32, (tq, LANES), 1)
    lo_half = lane < HEAD_DIM
    r_i = lax.broadcasted_iota(jnp.int32, (tq, tk), 0)
    c_i = lax.broadcasted_iota(jnp.int32, (tq, tk), 1)
    causal = r_i >= c_i

    aug = qaug_ref[...]
    qas = [jnp.where(lo_half, q, aug), jnp.where(lo_half, aug, q)]

    def logits(hh, j):
        return jnp.dot(qas[hh], kaug_ref[hh, j], preferred_element_type=F32)

    def vblock(j):
        return v_ref[pl.ds(pl.multiple_of(j * tk, tk), tk), :]

    def lane_fold(x):
        out = x[:, 0:LANES]
        for c in range(1, x.shape[1] // LANES):
            out = out + x[:, c * LANES:(c + 1) * LANES]
        return out

    def online_diag(hh):
        s = jnp.where(causal, logits(hh, i), NEG_INF)
        m0 = jnp.max(s, axis=1, keepdims=True)
        p0 = jnp.exp2(s - m0)
        m_ref[hh] = jnp.broadcast_to(m0, (tq, LANES))
        l_ref[hh] = lane_fold(p0)
        acc_ref[hh] = jnp.dot(p0.astype(BF16), vblock(i), preferred_element_type=F32)

    def online_step(hh, j):
        s = logits(hh, j)
        m_prev = m_ref[hh]
        m_new = jnp.maximum(m_prev, jnp.max(s, axis=1, keepdims=True))
        alpha = jnp.exp2(m_prev - m_new)
        pj = jnp.exp2(s - m_new[:, 0:1])
        l_ref[hh] = alpha * l_ref[hh] + lane_fold(pj)
        acc_ref[hh] = alpha * acc_ref[hh] + jnp.dot(
            pj.astype(BF16), vblock(j), preferred_element_type=F32)
        m_ref[hh] = m_new

    half = tq // 2

    def fixed_diag(heads):
        vb = vblock(i)
        logit_pairs = [
            (jnp.dot(qas[hh][0:half], kaug_ref[hh, i, :, 0:half],
                     preferred_element_type=F32),
             jnp.dot(qas[hh][half:], kaug_ref[hh, i], preferred_element_type=F32))
            for hh in heads]
        for hh, (s_top, s_bot) in zip(heads, logit_pairs):
            p_top = jnp.exp2(jnp.where(causal[0:half, 0:half], s_top, NEG_INF))
            p_bot = jnp.exp2(jnp.where(causal[half:, :], s_bot, NEG_INF))
            l_ref[hh, 0:half] = lane_fold(p_top)
            l_ref[hh, half:] = lane_fold(p_bot)
            acc_ref[hh, 0:half] = jnp.dot(p_top.astype(BF16), vb[0:half],
                                          preferred_element_type=F32)
            acc_ref[hh, half:] = jnp.dot(p_bot.astype(BF16), vb, preferred_element_type=F32)

    def fixed_steps(jobs):
        all_logits = [jnp.concatenate([logits(hh, j - (n - 1) + b) for b in range(n)], axis=1)
                      for hh, j, n in jobs]
        for (hh, j, n), s in zip(jobs, all_logits):
            pj = jnp.exp2(s)
            l_ref[hh] += lane_fold(pj)
            vn = v_ref[pl.ds(pl.multiple_of((j - (n - 1)) * tk, tk), n * tk), :]
            acc_ref[hh] += jnp.dot(pj.astype(BF16), vn, preferred_element_type=F32)

    codes = [nkv_ref[(2 * p + hh) * nq + i] for hh in range(2)]
    n_offs = [code >> 1 for code in codes]
    both_fixed = (codes[0] & codes[1] & 1) == 1
    joint_quads = jnp.where(both_fixed, jnp.minimum(n_offs[0] >> 2, n_offs[1] >> 2), 0)
    joint_pair = both_fixed & ((n_offs[0] & n_offs[1] & 2) == 2)
    joint_single = both_fixed & ((n_offs[0] & n_offs[1] & 1) == 1)
    pair_at = [i - 1 - 4 * (n >> 2) for n in n_offs]
    single_at = [i - n for n in n_offs]

    @pl.when(both_fixed)
    def _():
        fixed_diag((0, 1))

        def body(u, carry):
            fixed_steps([(0, i - 1 - 4 * u, 4), (1, i - 1 - 4 * u, 4)])
            return carry

        lax.fori_loop(0, joint_quads, body, 0)

        @pl.when(joint_pair)
        def _():
            fixed_steps([(0, pair_at[0], 2), (1, pair_at[1], 2)])

        @pl.when(joint_single)
        def _():
            fixed_steps([(0, single_at[0], 1), (1, single_at[1], 1)])

    for hh in range(2):
        n_off = n_offs[hh]
        fixed_ok = (codes[hh] & 1) == 1

        @pl.when(fixed_ok)
        def _(hh=hh, n_off=n_off):
            @pl.when(jnp.logical_not(both_fixed))
            def _():
                fixed_diag((hh,))

            def body(u, carry):
                fixed_steps([(hh, i - 1 - 4 * u, 4)])
                return carry

            lax.fori_loop(joint_quads, n_off >> 2, body, 0)

            @pl.when(((n_off & 2) == 2) & jnp.logical_not(joint_pair))
            def _():
                fixed_steps([(hh, pair_at[hh], 2)])

            @pl.when(((n_off & 1) == 1) & jnp.logical_not(joint_single))
            def _():
                fixed_steps([(hh, single_at[hh], 1)])

        @pl.when(jnp.logical_not(fixed_ok))
        def _(hh=hh, n_off=n_off):
            online_diag(hh)

            def body(jj, carry):
                online_step(hh, i - 1 - jj)
                return carry

            lax.fori_loop(0, n_off, body, 0)

    l0 = jnp.sum(l_ref[0], axis=1, keepdims=True)
    l1 = jnp.sum(l_ref[1], axis=1, keepdims=True)
    o_ref[...] = jnp.where(lo_half, acc_ref[0] / l0, acc_ref[1] / l1)


def _kv_counts(qn_tiles, kn_tiles, cmax_chunks, cmin_chunks):
    nq = qn_tiles.shape[0]
    per_q = cmax_chunks.shape[0] // nq
    qn = jnp.sqrt(qn_tiles[:, 0, :FOX_HEADS])
    kn = jnp.sqrt(jnp.max(kn_tiles[:, :FOX_HEADS, 0], axis=0))
    gate = slice(FOX_HEADS, 2 * FOX_HEADS)
    cmax = jnp.max(cmax_chunks[:, 0, gate].reshape(nq, per_q, FOX_HEADS), axis=1) * LOG2E
    cmin = jnp.min(cmin_chunks[:, 0, gate].reshape(nq, per_q, FOX_HEADS), axis=1) * LOG2E
    pmin = lax.cummin(cmin, axis=0)
    qk = NORM_MARGIN * qn * kn[None, :] + 1.0
    bound = (2.0 * qk + cmax)[:, None, :] - pmin[None, :, :]
    ii = lax.broadcasted_iota(jnp.int32, bound.shape, 0)
    jj = lax.broadcasted_iota(jnp.int32, bound.shape, 1)
    skip = (bound < SKIP_LOG2) & (jj < ii)
    jstar = jnp.max(jnp.where(skip, jj, -1), axis=1)
    n_off = jnp.arange(nq, dtype=jnp.int32)[:, None] - 1 - jstar
    pmin_prev = jnp.concatenate([jnp.full((1, FOX_HEADS), jnp.inf, F32), pmin[:-1]], axis=0)
    fixed_ok = (qk + jnp.maximum(cmax - pmin_prev, 0.0)) < FIXED_MAX_LOG2
    code = 2 * n_off + fixed_ok.astype(jnp.int32)
    return code.T.reshape(-1).astype(jnp.int32)


def _fox(nkv, q, kt3, v, qaug, cumt):
    s = q.shape[0]
    tq, tk = ATT_TQ, ATT_TK
    nb = s // tk
    grid_spec = pltpu.PrefetchScalarGridSpec(
        num_scalar_prefetch=1,
        grid=(FOX_HEADS // 2, s // tq),
        in_specs=[
            pl.BlockSpec((tq, LANES), lambda p, i, n: (i, p)),
            pl.BlockSpec((nb, LANES, tk), lambda p, i, n: (0, p, 0)),
            pl.BlockSpec((s, LANES), lambda p, i, n: (0, p)),
            pl.BlockSpec((tq, LANES), lambda p, i, n: (i, p)),
            pl.BlockSpec((16, s), lambda p, i, n: (0, 0)),
        ],
        out_specs=pl.BlockSpec((tq, LANES), lambda p, i, n: (i, p)),
        scratch_shapes=[
            pltpu.VMEM((2, nb, LANES, tk), BF16),
            pltpu.VMEM((2, tq, LANES), F32),
            pltpu.VMEM((2, tq, LANES), F32),
            pltpu.VMEM((2, tq, LANES), F32),
        ],
    )
    return pl.pallas_call(
        _fox_kernel,
        grid_spec=grid_spec,
        out_shape=jax.ShapeDtypeStruct((s, FOX_WIDTH), F32),
        compiler_params=pltpu.CompilerParams(
            dimension_semantics=("arbitrary", "arbitrary"),
            vmem_limit_bytes=VMEM_LIMIT),
        name="fox_attn",
    )(nkv, q, kt3, v, qaug, cumt)


def _pad_lanes(vec_dt, vec_f):
    out = jnp.zeros((1, LANES), F32)
    out = out.at[0, 0:SSM_HEADS].set(vec_dt.astype(F32))
    if vec_f is not None:
        out = out.at[0, SSM_HEADS:SSM_HEADS + FOX_HEADS].set(vec_f.astype(F32))
    return out


def _layer(x, ffn1_norm, ffn1_w_in, ffn1_w_out, mix_norm, w_in, conv_w, conv_b, dt_bias,
           a_log, d_skip, ssm_norm, f_bias, fox_norm, w_out, ffn2_norm, ffn2_w_in,
           ffn2_w_out, final_g, *, final):
    ones = jnp.ones((1, D_MODEL), F32)
    row = lambda a: a.reshape(1, -1).astype(F32)

    x1 = _ffn(x, row(ffn1_norm), ffn1_w_in.astype(BF16), ffn1_w_out.astype(BF16), ones,
              final=False)

    o0 = SSM_WIDTH
    o1 = o0 + SSM_XBC
    o2 = o1 + SSM_HEADS
    o3 = o2 + 3 * FOX_WIDTH
    wz = w_in[:, :o0].astype(BF16)
    wxbc = w_in[:, o0:o1].astype(BF16)
    w_dt = w_in[:, o1:o2]
    wq = w_in[:, o2:o2 + FOX_WIDTH].astype(BF16)
    wkt = w_in[:, o2 + FOX_WIDTH:o2 + 2 * FOX_WIDTH].astype(BF16)
    wv = w_in[:, o2 + 2 * FOX_WIDTH:o3].astype(BF16)
    w_f = w_in[:, o3:]
    ws = jnp.concatenate(
        [w_dt, w_f, jnp.zeros((D_MODEL, LANES - SSM_HEADS - FOX_HEADS), w_in.dtype)],
        axis=1).astype(BF16)
    zs, xs, bc, q, kt3, v, small, qn2, kn2 = _inproj(
        x1, row(mix_norm), wz, wxbc, wq, wkt, wv, ws, conv_w.astype(F32), row(conv_b))

    bias_pad = _pad_lanes(dt_bias, f_bias)
    a_pad = _pad_lanes(-jnp.exp(a_log.astype(F32)), None)
    dskip_e = jnp.repeat(d_skip.astype(F32), HEAD_DIM).reshape(1, SSM_WIDTH)
    tri = jnp.asarray(np.tril(np.ones((SSD_T, SSD_T), np.float32)), dtype=BF16)
    expand_np = np.zeros((LANES, SSM_WIDTH), np.float32)
    for hd in range(SSM_HEADS):
        expand_np[hd, hd * HEAD_DIM:(hd + 1) * HEAD_DIM] = 1.0
    y_ssd, qaug, cumt, cmax_c, cmin_c = _ssd(zs, xs, bc, small, bias_pad, a_pad,
                            dskip_e, row(ssm_norm), tri, jnp.asarray(expand_np, dtype=BF16))

    o_fox = _fox(_kv_counts(qn2, kn2, cmax_c, cmin_c), q, kt3, v, qaug, cumt)

    wo = w_out.astype(BF16)
    mix = (y_ssd, o_fox, row(fox_norm), wo[:SSM_WIDTH], wo[SSM_WIDTH:])
    return _ffn(x1, row(ffn2_norm), ffn2_w_in.astype(BF16), ffn2_w_out.astype(BF16),
                row(final_g), final=final, mix=mix)


def kernel(x, ffn1_norm, ffn1_w_in, ffn1_w_out, mix_norm, w_in, conv_w, conv_b, dt_bias, a_log,
           d_skip, ssm_norm, f_bias, fox_norm, w_out, ffn2_norm, ffn2_w_in, ffn2_w_out, final_norm):
    b, s, d = x.shape
    depth = ffn1_norm.shape[0]
    outs = []
    for bi in range(b):
        xb = x[bi]
        for l in range(depth):
            xb = _layer(xb, ffn1_norm[l], ffn1_w_in[l], ffn1_w_out[l], mix_norm[l], w_in[l],
                        conv_w[l], conv_b[l], dt_bias[l], a_log[l], d_skip[l], ssm_norm[l],
                        f_bias[l], fox_norm[l], w_out[l], ffn2_norm[l], ffn2_w_in[l],
                        ffn2_w_out[l], final_norm, final=(l == depth - 1))
        outs.append(xb)
    return jnp.stack(outs, axis=0)
```

```python
import functools

import jax
import jax.numpy as jnp
import numpy as np
from jax import lax
from jax.experimental import pallas as pl
from jax.experimental.pallas import tpu as pltpu

F32 = jnp.float32
BF16 = jnp.bfloat16
HIGHEST = lax.Precision.HIGHEST

D_MODEL = 1024
SEQ = 16384
HEAD_DIM = 64
SSM_WIDTH = 512
SSM_HEADS = 8
SSM_GROUPS = 2
SSM_STATE = 128
CONV_WIDTH = 4
SSM_XBC = SSM_WIDTH + 2 * SSM_GROUPS * SSM_STATE
FOX_WIDTH = 512
FOX_HEADS = 8
D_FF = 2816
EPS = 1e-6

LANES = 128
VMEM_LIMIT = 56 * 1024 * 1024

FFN_TM = 1024
FFN_TF = 256
PROJ_TM = 512
SSD_T = 256
SSD_STEP = 1024
ATT_TQ = 512
ATT_TK = 512
NEG_INF = float("-inf")
LOG2E = 1.4426950408889634
SKIP_LOG2 = -150.0
NORM_MARGIN = 1.05
FIXED_MAX_LOG2 = 64.0


def _rms(x, g):
    ms = jnp.mean(x * x, axis=-1, keepdims=True)
    return x * lax.rsqrt(ms + EPS) * g


def _silu(x):
    return x * (1.0 / (1.0 + jnp.exp(-x)))


def _softplus(x):
    return jnp.maximum(x, 0.0) + jnp.log1p(jnp.exp(-jnp.abs(x)))


def _split3(x):
    hi = x.astype(BF16).astype(F32)
    r = x - hi
    mid = r.astype(BF16).astype(F32)
    lo = (r - mid).astype(BF16).astype(F32)
    return hi, mid, lo


def _dot_f32_rhs(a_bf16, x):
    hi, mid, lo = _split3(x)
    d = lambda piece: jnp.dot(a_bf16, piece.astype(BF16), preferred_element_type=F32)
    return (d(lo) + d(mid)) + d(hi)


def _dot_f32_lhs(x, b_bf16):
    hi, mid, lo = _split3(x)
    d = lambda piece: jnp.dot(piece.astype(BF16), b_bf16, preferred_element_type=F32)
    return (d(lo) + d(mid)) + d(hi)


def _ffn_kernel(*refs, mix, final):
    if mix:
        (x_ref, ys_ref, of_ref, mg_ref, wm1_ref, wm2_ref,
         g_ref, wg_ref, wu_ref, wo_ref, fg_ref, o_ref) = refs
        yf = _rms(of_ref[...], mg_ref[...]).astype(BF16)
        x = (x_ref[...]
             + jnp.dot(ys_ref[...], wm1_ref[...], preferred_element_type=F32)
             + jnp.dot(yf, wm2_ref[...], preferred_element_type=F32))
    else:
        x_ref, g_ref, wg_ref, wu_ref, wo_ref, fg_ref, o_ref = refs
        x = x_ref[...]
    h = _rms(x, g_ref[...]).astype(BF16)
    o_ref[...] = 2.0 * x
    nf = D_FF // FFN_TF

    def gate_up(f):
        cols = slice(f * FFN_TF, (f + 1) * FFN_TF)
        return (jnp.dot(h, wg_ref[:, cols], preferred_element_type=F32),
                jnp.dot(h, wu_ref[:, cols], preferred_element_type=F32))

    pending = gate_up(0)
    for f in range(nf):
        gate, up = pending
        if f + 1 < nf:
            pending = gate_up(f + 1)
        act = (_silu(gate) * up).astype(BF16)
        o_ref[...] += jnp.dot(act, wo_ref[f * FFN_TF:(f + 1) * FFN_TF, :],
                              preferred_element_type=F32)
    r = 0.5 * o_ref[...]
    if final:
        r = _rms(r, fg_ref[...])
    o_ref[...] = r


def _ffn(x, g, w_in, w_out, fg, *, final, mix=None):
    s = x.shape[0]
    tm = FFN_TM
    row = lambda i: (i, 0)
    const = lambda i: (0, 0)
    resident = functools.partial(pl.BlockSpec, pipeline_mode=pl.Buffered(1))
    x_spec = pl.BlockSpec((tm, D_MODEL), row)
    vec_spec = pl.BlockSpec((1, D_MODEL), const)
    ffn_specs = [
        vec_spec,
        resident((D_MODEL, D_FF), lambda i: (0, 0)),
        resident((D_MODEL, D_FF), lambda i: (0, 1)),
        resident((D_FF, D_MODEL), const),
        vec_spec,
    ]
    ffn_args = (g, w_in, w_in, w_out, fg)
    if mix is None:
        in_specs = [x_spec] + ffn_specs
        args = (x,) + ffn_args
    else:
        ys, of, mg, wm1, wm2 = mix
        in_specs = [
            x_spec,
            pl.BlockSpec((tm, SSM_WIDTH), row),
            pl.BlockSpec((tm, FOX_WIDTH), row),
            pl.BlockSpec((1, FOX_WIDTH), const),
            resident((SSM_WIDTH, D_MODEL), const),
            resident((FOX_WIDTH, D_MODEL), const),
        ] + ffn_specs
        args = (x, ys, of, mg, wm1, wm2) + ffn_args
    return pl.pallas_call(
        functools.partial(_ffn_kernel, mix=mix is not None, final=final),
        grid=(s // tm,),
        in_specs=in_specs,
        out_specs=pl.BlockSpec((tm, D_MODEL), row),
        out_shape=jax.ShapeDtypeStruct((s, D_MODEL), F32),
        compiler_params=pltpu.CompilerParams(
            dimension_semantics=("parallel",),
            vmem_limit_bytes=VMEM_LIMIT),
        name="ffn_final" if final else "ffn",
    )(*args)


def _inproj_kernel(x_ref, g_ref, wz_ref, wxbc_ref, wq_ref, wkt_ref, wv_ref, ws_ref,
                   ind_ref, indt_ref, cw_ref, cb_ref,
                   zs_ref, xs_ref, bc_ref, q_ref, kt_ref, v_ref, small_ref, qn_ref, kn_ref,
                   halo_ref):
    tm = PROJ_TM
    i = pl.program_id(0)

    @pl.when(i == 0)
    def _():
        halo_ref[...] = jnp.zeros_like(halo_ref)

    h = _rms(x_ref[...], g_ref[...]).astype(BF16)
    xb = jnp.dot(h, wxbc_ref[...], preferred_element_type=F32)
    zs_ref[...] = _silu(jnp.dot(h, wz_ref[...], preferred_element_type=F32))
    q = jnp.dot(h, wq_ref[...], preferred_element_type=F32)
    qb = (q * (HEAD_DIM ** -0.5 * LOG2E)).astype(BF16)
    q_ref[...] = qb
    kt = lax.dot_general(wkt_ref[...], h, (((0,), (1,)), ((), ())),
                         preferred_element_type=F32)
    ktb = kt.astype(BF16)
    kt_ref[0] = ktb
    v_ref[...] = jnp.dot(h, wv_ref[...], preferred_element_type=F32).astype(BF16)
    small_ref[...] = jnp.dot(h, ws_ref[...], preferred_element_type=F32)
    qf = qb.astype(F32)
    qn = jnp.dot((qf * qf).astype(BF16), ind_ref[...], preferred_element_type=F32)
    qn_ref[0] = jnp.broadcast_to(jnp.max(qn, axis=0, keepdims=True), (8, LANES))
    kf = ktb.astype(F32)
    kn = jnp.dot(indt_ref[...], (kf * kf).astype(BF16), preferred_element_type=F32)
    kn_ref[0] = jnp.broadcast_to(jnp.max(kn, axis=1, keepdims=True), (16, LANES))
    prev = halo_ref[...]
    row8 = lax.broadcasted_iota(jnp.int32, (8, SSM_XBC), 0)
    conv = cb_ref[...] + xb * cw_ref[CONV_WIDTH - 1:CONV_WIDTH, :]
    for s in range(1, CONV_WIDTH):
        rolled = pltpu.roll(xb, s, axis=0)
        head = jnp.where(row8 < s, pltpu.roll(prev, s, axis=0), rolled[0:8])
        shifted = jnp.concatenate([head, rolled[8:]], axis=0)
        k = CONV_WIDTH - 1 - s
        conv = conv + shifted * cw_ref[k:k + 1, :]
    halo_ref[...] = xb[tm - 8:tm]
    u = _silu(conv)
    xs_ref[...] = u[:, :SSM_WIDTH]
    bc_ref[...] = u[:, SSM_WIDTH:].astype(BF16)


def _inproj(x, g, wz, wxbc, wq, wkt, wv, ws, cw, cb):
    s = x.shape[0]
    tm = PROJ_TM
    nb = s // tm
    const = lambda i: (0, 0)
    row = lambda i: (i, 0)
    ind_np = np.zeros((FOX_WIDTH, LANES), np.float32)
    for hd in range(FOX_HEADS):
        ind_np[hd * HEAD_DIM:(hd + 1) * HEAD_DIM, hd] = 1.0
    ind = jnp.asarray(ind_np, dtype=BF16)
    indt = jnp.asarray(ind_np.T[:16], dtype=BF16)
    return pl.pallas_call(
        _inproj_kernel,
        grid=(nb,),
        in_specs=[
            pl.BlockSpec((tm, D_MODEL), row),
            pl.BlockSpec((1, D_MODEL), const),
            pl.BlockSpec((D_MODEL, SSM_WIDTH), const),
            pl.BlockSpec((D_MODEL, SSM_XBC), const),
            pl.BlockSpec((D_MODEL, FOX_WIDTH), const),
            pl.BlockSpec((D_MODEL, FOX_WIDTH), const),
            pl.BlockSpec((D_MODEL, FOX_WIDTH), const),
            pl.BlockSpec((D_MODEL, LANES), const),
            pl.BlockSpec((FOX_WIDTH, LANES), const),
            pl.BlockSpec((16, FOX_WIDTH), const),
            pl.BlockSpec((CONV_WIDTH, SSM_XBC), const),
            pl.BlockSpec((1, SSM_XBC), const),
        ],
        out_specs=[
            pl.BlockSpec((tm, SSM_WIDTH), row),
            pl.BlockSpec((tm, SSM_WIDTH), row),
            pl.BlockSpec((tm, SSM_XBC - SSM_WIDTH), row),
            pl.BlockSpec((tm, FOX_WIDTH), row),
            pl.BlockSpec((1, FOX_WIDTH, tm), lambda i: (i, 0, 0)),
            pl.BlockSpec((tm, FOX_WIDTH), row),
            pl.BlockSpec((tm, LANES), row),
            pl.BlockSpec((1, 8, LANES), lambda i: (i, 0, 0)),
            pl.BlockSpec((1, 16, LANES), lambda i: (i, 0, 0)),
        ],
        out_shape=[
            jax.ShapeDtypeStruct((s, SSM_WIDTH), F32),
            jax.ShapeDtypeStruct((s, SSM_WIDTH), F32),
            jax.ShapeDtypeStruct((s, SSM_XBC - SSM_WIDTH), BF16),
            jax.ShapeDtypeStruct((s, FOX_WIDTH), BF16),
            jax.ShapeDtypeStruct((nb, FOX_WIDTH, tm), BF16),
            jax.ShapeDtypeStruct((s, FOX_WIDTH), BF16),
            jax.ShapeDtypeStruct((s, LANES), F32),
            jax.ShapeDtypeStruct((nb, 8, LANES), F32),
            jax.ShapeDtypeStruct((nb, 16, LANES), F32),
        ],
        scratch_shapes=[pltpu.VMEM((8, SSM_XBC), F32)],
        compiler_params=pltpu.CompilerParams(
            dimension_semantics=("arbitrary",),
            vmem_limit_bytes=VMEM_LIMIT),
        name="in_proj",
    )(x, g, wz, wxbc, wq, wkt, wv, ws, ind, indt, cw, cb)


def _ssd_kernel(zs_ref, xs_ref, bc_ref, small_ref, bias_ref, apad_ref,
                dskip_ref, ng_ref, tri_ref, expand_ref, place_ref,
                y_ref, qaug_ref, cumt_ref, cmax_ref, cmin_ref,
                state_ref, carry_ref):
    t_rows = SSD_T
    i = pl.program_id(0)

    @pl.when(i == 0)
    def _():
        state_ref[...] = jnp.zeros_like(state_ref)
        carry_ref[...] = jnp.zeros_like(carry_ref)

    chunks = range(SSD_STEP // t_rows)
    rows = [slice(c * t_rows, (c + 1) * t_rows) for c in chunks]
    gw = SSM_WIDTH // SSM_GROUPS
    hpg = SSM_HEADS // SSM_GROUPS
    groups = range(SSM_GROUPS)
    lane = lax.broadcasted_iota(jnp.int32, (t_rows, LANES), 1)
    is_dt = lane < SSM_HEADS
    lo = lane < HEAD_DIM

    sp, v = [], []
    for c in chunks:
        t = small_ref[rows[c], :] + bias_ref[...]
        sp.append(_softplus(jnp.where(is_dt, t, -t)))
        v.append(jnp.where(is_dt, sp[c] * apad_ref[...], -sp[c]))
    local = [_dot_f32_rhs(tri_ref[...], v[c]) for c in chunks]
    carry = carry_ref[...]
    cs, cst = [], []
    for c in chunks:
        cs.append(local[c] + carry)
        carry = jnp.where(lane[0:1, :] < SSM_HEADS, 0.0, cs[c][t_rows - 1:t_rows, :])
        cst.append(cs[c].T)
        cumt_ref[:, rows[c]] = cst[c][0:16, :]
        cmax_ref[c] = jnp.broadcast_to(jnp.max(cs[c], axis=0, keepdims=True), (8, LANES))
        cmin_ref[c] = jnp.broadcast_to(jnp.min(cs[c], axis=0, keepdims=True), (8, LANES))
    carry_ref[...] = carry

    placed = [[jnp.dot(piece.astype(BF16), place_ref[...], preferred_element_type=F32)
               for piece in _split3(cs[c] * LOG2E)] for c in chunks]
    expanded = [_dot_f32_lhs(jnp.concatenate([sp[c], cs[c]], axis=0), expand_ref[...])
                for c in chunks]
    lane4 = lax.broadcasted_iota(jnp.int32, (t_rows, FOX_WIDTH), 1) & (HEAD_DIM - 1)
    xs, ea_e, cd_e, xdt_b, wst = [], [], [], [], []
    for c in chunks:
        qaug_ref[rows[c], :] = jnp.where(
            lane4 == 0, placed[c][0],
            jnp.where(lane4 == 1, placed[c][1],
                      jnp.where(lane4 == 2, placed[c][2],
                                jnp.where(lane4 < 6, 1.0, 0.0)))).astype(BF16)
        dt_e = expanded[c][0:t_rows]
        cs_e = expanded[c][t_rows:2 * t_rows]
        ea_e.append(jnp.exp(cs_e))
        de_e = jnp.exp(cs_e[t_rows - 1:t_rows, :] - cs_e)
        cd_e.append(ea_e[c][t_rows - 1:t_rows, :])
        xs.append(xs_ref[rows[c], :])
        xdt = xs[c] * dt_e
        xdt_b.append(xdt.astype(BF16))
        wst.append((xdt * de_e).astype(BF16))

    def cgrp(c, g):
        return bc_ref[rows[c], SSM_GROUPS * SSM_STATE + g * SSM_STATE:
                      SSM_GROUPS * SSM_STATE + (g + 1) * SSM_STATE]

    def bgrp(c, g):
        return bc_ref[rows[c], g * SSM_STATE:(g + 1) * SSM_STATE]

    gmat = [[lax.dot_general(cgrp(c, g), bgrp(c, g), (((1,), (1,)), ((), ())),
                             preferred_element_type=F32) for g in groups]
            for c in chunks]
    r_i = lax.broadcasted_iota(jnp.int32, (t_rows, t_rows), 0)
    c_i = lax.broadcasted_iota(jnp.int32, (t_rows, t_rows), 1)
    causal = r_i >= c_i
    y_parts = []
    for c in chunks:
        parts = []
        for g in groups:
            for pr in range(hpg // 2):
                c0 = g * gw + pr * LANES
                xpair = xdt_b[c][:, c0:c0 + LANES]
                acc = None
                for hh in range(2):
                    h = g * hpg + pr * 2 + hh
                    seg = cs[c][:, h:h + 1] - cst[c][h:h + 1, :]
                    dec = jnp.exp(jnp.where(causal, seg, NEG_INF))
                    m = (gmat[c][g] * dec).astype(BF16)
                    xm = jnp.where(lo if hh == 0 else jnp.logical_not(lo), xpair,
                                   jnp.zeros_like(xpair))
                    part = jnp.dot(m, xm, preferred_element_type=F32)
                    acc = part if acc is None else acc + part
                parts.append(acc)
        y_parts.append(parts)

    upd = [[jnp.dot(bgrp(c, g).astype(F32).T.astype(BF16), wst[c][:, g * gw:(g + 1) * gw],
                    preferred_element_type=F32) for g in groups]
           for c in chunks]

    state = [state_ref[g] for g in groups]
    for c in chunks:
        for g in groups:
            y_off = jnp.dot(cgrp(c, g), state[g].astype(BF16), preferred_element_type=F32)
            for pr in range(hpg // 2):
                cols = slice(g * gw + pr * LANES, g * gw + (pr + 1) * LANES)
                y_parts[c][g * (hpg // 2) + pr] += (
                    y_off[:, pr * LANES:(pr + 1) * LANES] * ea_e[c][:, cols])
            state[g] = state[g] * cd_e[c][:, g * gw:(g + 1) * gw] + upd[c][g]
    for g in groups:
        state_ref[g] = state[g]

    for c in chunks:
        y = jnp.concatenate(y_parts[c], axis=1) + dskip_ref[...] * xs[c]
        y = y * zs_ref[rows[c], :]
        y_ref[rows[c], :] = _rms(y, ng_ref[...]).astype(BF16)


def _ssd(zs, xs, bc, small, bias_pad, a_pad, dskip_e, ng, tri, expand):
    s = zs.shape[0]
    t = SSD_T
    st = SSD_STEP
    nsub = st // t
    const = lambda i: (0, 0)
    row = lambda i: (i, 0)
    place_np = np.zeros((LANES, FOX_WIDTH), np.float32)
    for hd in range(FOX_HEADS):
        pair, odd = divmod(hd, 2)
        c0 = pair * LANES + (0 if odd else HEAD_DIM)
        place_np[FOX_HEADS + hd, c0:c0 + HEAD_DIM] = 1.0
    place = jnp.asarray(place_np, dtype=BF16)
    return pl.pallas_call(
        _ssd_kernel,
        grid=(s // st,),
        in_specs=[
            pl.BlockSpec((st, SSM_WIDTH), row),
            pl.BlockSpec((st, SSM_WIDTH), row),
            pl.BlockSpec((st, SSM_XBC - SSM_WIDTH), row),
            pl.BlockSpec((st, LANES), row),
            pl.BlockSpec((1, LANES), const),
            pl.BlockSpec((1, LANES), const),
            pl.BlockSpec((1, SSM_WIDTH), const),
            pl.BlockSpec((1, SSM_WIDTH), const),
            pl.BlockSpec((t, t), const),
            pl.BlockSpec((LANES, SSM_WIDTH), const),
            pl.BlockSpec((LANES, FOX_WIDTH), const),
        ],
        out_specs=[
            pl.BlockSpec((st, SSM_WIDTH), row),
            pl.BlockSpec((st, FOX_WIDTH), row),
            pl.BlockSpec((16, st), lambda i: (0, i)),
            pl.BlockSpec((nsub, 8, LANES), lambda i: (i, 0, 0)),
            pl.BlockSpec((nsub, 8, LANES), lambda i: (i, 0, 0)),
        ],
        out_shape=[
            jax.ShapeDtypeStruct((s, SSM_WIDTH), BF16),
            jax.ShapeDtypeStruct((s, FOX_WIDTH), BF16),
            jax.ShapeDtypeStruct((16, s), F32),
            jax.ShapeDtypeStruct((s // t, 8, LANES), F32),
            jax.ShapeDtypeStruct((s // t, 8, LANES), F32),
        ],
        scratch_shapes=[
            pltpu.VMEM((SSM_GROUPS, SSM_STATE, SSM_WIDTH // SSM_GROUPS), F32),
            pltpu.VMEM((1, LANES), F32),
        ],
        compiler_params=pltpu.CompilerParams(
            dimension_semantics=("arbitrary",),
            vmem_limit_bytes=VMEM_LIMIT),
        name="ssd",
    )(zs, xs, bc, small, bias_pad, a_pad, dskip_e, ng, tri, expand, place)


def _fox_kernel(nkv_ref, q_ref, kt_ref, v_ref, qaug_ref, cumt_ref, o_ref,
                kaug_ref, m_ref, l_ref, acc_ref):
    tq, tk = ATT_TQ, ATT_TK
    nb = kt_ref.shape[0]
    nq = pl.num_programs(1)
    p = pl.program_id(0)
    i = pl.program_id(1)

    @pl.when(i == 0)
    def _():
        row16 = lax.broadcasted_iota(jnp.int32, (16, tk), 0)
        zeros48 = jnp.zeros((48, tk), BF16)

        def build(j, carry):
            off = pl.multiple_of(j * tk, tk)
            kt = kt_ref[j]
            for hh in range(2):
                ck = cumt_ref[pl.ds(FOX_HEADS + 2 * p + hh, 1), pl.ds(off, tk)] * LOG2E
                hi, mid, lo = _split3(ck)
                bias = jnp.where(row16 < 3, 1.0,
                                 jnp.where(row16 == 3, -hi,
                                           jnp.where(row16 == 4, -mid,
                                                     jnp.where(row16 == 5, -lo, 0.0))))
                bias = bias.astype(BF16)
                if hh == 0:
                    kaug_ref[0, j, 0:64, :] = kt[0:64]
                    kaug_ref[0, j, 64:80, :] = bias
                    kaug_ref[0, j, 80:128, :] = zeros48
                else:
                    kaug_ref[1, j, 0:16, :] = bias
                    kaug_ref[1, j, 16:64, :] = zeros48
                    kaug_ref[1, j, 64:128, :] = kt[64:128]
            return carry

        lax.fori_loop(0, nb, build, 0)

    q = q_ref[...]
    lane = lax.broadcasted_iota(jnp.int32, (tq, LANES), 1)
    lo_half = lane < HEAD_DIM
    r_i = lax.broadcasted_iota(jnp.int32, (tq, tk), 0)
    c_i = lax.broadcasted_iota(jnp.int32, (tq, tk), 1)
    causal = r_i >= c_i

    aug = qaug_ref[...]
    qas = [jnp.where(lo_half, q, aug), jnp.where(lo_half, aug, q)]

    def logits(hh, j):
        return jnp.dot(qas[hh], kaug_ref[hh, j], preferred_element_type=F32)

    def vblock(j):
        return v_ref[pl.ds(pl.multiple_of(j * tk, tk), tk), :]

    def lane_fold(x):
        out = x[:, 0:LANES]
        for c in range(1, x.shape[1] // LANES):
            out = out + x[:, c * LANES:(c + 1) * LANES]
        return out

    def online_diag(hh):
        s = jnp.where(causal, logits(hh, i), NEG_INF)
        m0 = jnp.max(s, axis=1, keepdims=True)
        p0 = jnp.exp2(s - m0)
        m_ref[hh] = jnp.broadcast_to(m0, (tq, LANES))
        l_ref[hh] = lane_fold(p0)
        acc_ref[hh] = jnp.dot(p0.astype(BF16), vblock(i), preferred_element_type=F32)

    def online_step(hh, j):
        s = logits(hh, j)
        m_prev = m_ref[hh]
        m_new = jnp.maximum(m_prev, jnp.max(s, axis=1, keepdims=True))
        alpha = jnp.exp2(m_prev - m_new)
        pj = jnp.exp2(s - m_new[:, 0:1])
        l_ref[hh] = alpha * l_ref[hh] + lane_fold(pj)
        acc_ref[hh] = alpha * acc_ref[hh] + jnp.dot(
            pj.astype(BF16), vblock(j), preferred_element_type=F32)
        m_ref[hh] = m_new

    half = tq // 2

    def fixed_diag(heads):
        vb = vblock(i)
        logit_pairs = [
            (jnp.dot(qas[hh][0:half], kaug_ref[hh, i, :, 0:half],
                     preferred_element_type=F32),
             jnp.dot(qas[hh][half:], kaug_ref[hh, i], preferred_element_type=F32))
            for hh in heads]
        for hh, (s_top, s_bot) in zip(heads, logit_pairs):
            p_top = jnp.exp2(jnp.where(causal[0:half, 0:half], s_top, NEG_INF))
            p_bot = jnp.exp2(jnp.where(causal[half:, :], s_bot, NEG_INF))
            l_ref[hh, 0:half] = lane_fold(p_top)
            l_ref[hh, half:] = lane_fold(p_bot)
            acc_ref[hh, 0:half] = jnp.dot(p_top.astype(BF16), vb[0:half],
                                          preferred_element_type=F32)
            acc_ref[hh, half:] = jnp.dot(p_bot.astype(BF16), vb, preferred_element_type=F32)

    def fixed_steps(jobs):
        all_logits = [jnp.concatenate([logits(hh, j - (n - 1) + b) for b in range(n)], axis=1)
                      for hh, j, n in jobs]
        for (hh, j, n), s in zip(jobs, all_logits):
            pj = jnp.exp2(s)
            l_ref[hh] += lane_fold(pj)
            vn = v_ref[pl.ds(pl.multiple_of((j - (n - 1)) * tk, tk), n * tk), :]
            acc_ref[hh] += jnp.dot(pj.astype(BF16), vn, preferred_element_type=F32)

    codes = [nkv_ref[(2 * p + hh) * nq + i] for hh in range(2)]
    n_offs = [code >> 1 for code in codes]
    both_fixed = (codes[0] & codes[1] & 1) == 1
    joint_quads = jnp.where(both_fixed, jnp.minimum(n_offs[0] >> 2, n_offs[1] >> 2), 0)
    joint_pair = both_fixed & ((n_offs[0] & n_offs[1] & 2) == 2)
    joint_single = both_fixed & ((n_offs[0] & n_offs[1] & 1) == 1)
    pair_at = [i - 1 - 4 * (n >> 2) for n in n_offs]
    single_at = [i - n for n in n_offs]

    @pl.when(both_fixed)
    def _():
        fixed_diag((0, 1))

        def body(u, carry):
            fixed_steps([(0, i - 1 - 4 * u, 4), (1, i - 1 - 4 * u, 4)])
            return carry

        lax.fori_loop(0, joint_quads, body, 0)

        @pl.when(joint_pair)
        def _():
            fixed_steps([(0, pair_at[0], 2), (1, pair_at[1], 2)])

        @pl.when(joint_single)
        def _():
            fixed_steps([(0, single_at[0], 1), (1, single_at[1], 1)])

    for hh in range(2):
        n_off = n_offs[hh]
        fixed_ok = (codes[hh] & 1) == 1

        @pl.when(fixed_ok)
        def _(hh=hh, n_off=n_off):
            @pl.when(jnp.logical_not(both_fixed))
            def _():
                fixed_diag((hh,))

            def body(u, carry):
                fixed_steps([(hh, i - 1 - 4 * u, 4)])
                return carry

            lax.fori_loop(joint_quads, n_off >> 2, body, 0)

            @pl.when(((n_off & 2) == 2) & jnp.logical_not(joint_pair))
            def _():
                fixed_steps([(hh, pair_at[hh], 2)])

            @pl.when(((n_off & 1) == 1) & jnp.logical_not(joint_single))
            def _():
                fixed_steps([(hh, single_at[hh], 1)])

        @pl.when(jnp.logical_not(fixed_ok))
        def _(hh=hh, n_off=n_off):
            online_diag(hh)

            def body(jj, carry):
                online_step(hh, i - 1 - jj)
                return carry

            lax.fori_loop(0, n_off, body, 0)

    l0 = jnp.sum(l_ref[0], axis=1, keepdims=True)
    l1 = jnp.sum(l_ref[1], axis=1, keepdims=True)
    o_ref[...] = jnp.where(lo_half, acc_ref[0] / l0, acc_ref[1] / l1)


def _kv_counts(qn_tiles, kn_tiles, cmax_chunks, cmin_chunks):
    nq = qn_tiles.shape[0]
    per_q = cmax_chunks.shape[0] // nq
    qn = jnp.sqrt(qn_tiles[:, 0, :FOX_HEADS])
    kn = jnp.sqrt(jnp.max(kn_tiles[:, :FOX_HEADS, 0], axis=0))
    gate = slice(FOX_HEADS, 2 * FOX_HEADS)
    cmax = jnp.max(cmax_chunks[:, 0, gate].reshape(nq, per_q, FOX_HEADS), axis=1) * LOG2E
    cmin = jnp.min(cmin_chunks[:, 0, gate].reshape(nq, per_q, FOX_HEADS), axis=1) * LOG2E
    pmin = lax.cummin(cmin, axis=0)
    qk = NORM_MARGIN * qn * kn[None, :] + 1.0
    bound = (2.0 * qk + cmax)[:, None, :] - pmin[None, :, :]
    ii = lax.broadcasted_iota(jnp.int32, bound.shape, 0)
    jj = lax.broadcasted_iota(jnp.int32, bound.shape, 1)
    skip = (bound < SKIP_LOG2) & (jj < ii)
    jstar = jnp.max(jnp.where(skip, jj, -1), axis=1)
    n_off = jnp.arange(nq, dtype=jnp.int32)[:, None] - 1 - jstar
    pmin_prev = jnp.concatenate([jnp.full((1, FOX_HEADS), jnp.inf, F32), pmin[:-1]], axis=0)
    fixed_ok = (qk + jnp.maximum(cmax - pmin_prev, 0.0)) < FIXED_MAX_LOG2
    code = 2 * n_off + fixed_ok.astype(jnp.int32)
    return code.T.reshape(-1).astype(jnp.int32)


def _fox(nkv, q, kt3, v, qaug, cumt):
    s = q.shape[0]
    tq, tk = ATT_TQ, ATT_TK
    nb = s // tk
    grid_spec = pltpu.PrefetchScalarGridSpec(
        num_scalar_prefetch=1,
        grid=(FOX_HEADS // 2, s // tq),
        in_specs=[
            pl.BlockSpec((tq, LANES), lambda p, i, n: (i, p)),
            pl.BlockSpec((nb, LANES, tk), lambda p, i, n: (0, p, 0)),
            pl.BlockSpec((s, LANES), lambda p, i, n: (0, p)),
            pl.BlockSpec((tq, LANES), lambda p, i, n: (i, p)),
            pl.BlockSpec((16, s), lambda p, i, n: (0, 0)),
        ],
        out_specs=pl.BlockSpec((tq, LANES), lambda p, i, n: (i, p)),
        scratch_shapes=[
            pltpu.VMEM((2, nb, LANES, tk), BF16),
            pltpu.VMEM((2, tq, LANES), F32),
            pltpu.VMEM((2, tq, LANES), F32),
            pltpu.VMEM((2, tq, LANES), F32),
        ],
    )
    return pl.pallas_call(
        _fox_kernel,
        grid_spec=grid_spec,
        out_shape=jax.ShapeDtypeStruct((s, FOX_WIDTH), F32),
        compiler_params=pltpu.CompilerParams(
            dimension_semantics=("arbitrary", "arbitrary"),
            vmem_limit_bytes=VMEM_LIMIT),
        name="fox_attn",
    )(nkv, q, kt3, v, qaug, cumt)


def _pad_lanes(vec_dt, vec_f):
    out = jnp.zeros((1, LANES), F32)
    out = out.at[0, 0:SSM_HEADS].set(vec_dt.astype(F32))
    if vec_f is not None:
        out = out.at[0, SSM_HEADS:SSM_HEADS + FOX_HEADS].set(vec_f.astype(F32))
    return out


def _layer(x, ffn1_norm, ffn1_w_in, ffn1_w_out, mix_norm, w_in, conv_w, conv_b, dt_bias,
           a_log, d_skip, ssm_norm, f_bias, fox_norm, w_out, ffn2_norm, ffn2_w_in,
           ffn2_w_out, final_g, *, final):
    ones = jnp.ones((1, D_MODEL), F32)
    row = lambda a: a.reshape(1, -1).astype(F32)

    x1 = _ffn(x, row(ffn1_norm), ffn1_w_in.astype(BF16), ffn1_w_out.astype(BF16), ones,
              final=False)

    o0 = SSM_WIDTH
    o1 = o0 + SSM_XBC
    o2 = o1 + SSM_HEADS
    o3 = o2 + 3 * FOX_WIDTH
    wz = w_in[:, :o0].astype(BF16)
    wxbc = w_in[:, o0:o1].astype(BF16)
    w_dt = w_in[:, o1:o2]
    wq = w_in[:, o2:o2 + FOX_WIDTH].astype(BF16)
    wkt = w_in[:, o2 + FOX_WIDTH:o2 + 2 * FOX_WIDTH].astype(BF16)
    wv = w_in[:, o2 + 2 * FOX_WIDTH:o3].astype(BF16)
    w_f = w_in[:, o3:]
    ws = jnp.concatenate(
        [w_dt, w_f, jnp.zeros((D_MODEL, LANES - SSM_HEADS - FOX_HEADS), w_in.dtype)],
        axis=1).astype(BF16)
    zs, xs, bc, q, kt3, v, small, qn2, kn2 = _inproj(
        x1, row(mix_norm), wz, wxbc, wq, wkt, wv, ws, conv_w.astype(F32), row(conv_b))

    bias_pad = _pad_lanes(dt_bias, f_bias)
    a_pad = _pad_lanes(-jnp.exp(a_log.astype(F32)), None)
    dskip_e = jnp.repeat(d_skip.astype(F32), HEAD_DIM).reshape(1, SSM_WIDTH)
    tri = jnp.asarray(np.tril(np.ones((SSD_T, SSD_T), np.float32)), dtype=BF16)
    expand_np = np.zeros((LANES, SSM_WIDTH), np.float32)
    for hd in range(SSM_HEADS):
        expand_np[hd, hd * HEAD_DIM:(hd + 1) * HEAD_DIM] = 1.0
    y_ssd, qaug, cumt, cmax_c, cmin_c = _ssd(zs, xs, bc, small, bias_pad, a_pad,
                            dskip_e, row(ssm_norm), tri, jnp.asarray(expand_np, dtype=BF16))

    o_fox = _fox(_kv_counts(qn2, kn2, cmax_c, cmin_c), q, kt3, v, qaug, cumt)

    wo = w_out.astype(BF16)
    mix = (y_ssd, o_fox, row(fox_norm), wo[:SSM_WIDTH], wo[SSM_WIDTH:])
    return _ffn(x1, row(ffn2_norm), ffn2_w_in.astype(BF16), ffn2_w_out.astype(BF16),
                row(final_g), final=final, mix=mix)


def kernel(x, ffn1_norm, ffn1_w_in, ffn1_w_out, mix_norm, w_in, conv_w, conv_b, dt_bias, a_log,
           d_skip, ssm_norm, f_bias, fox_norm, w_out, ffn2_norm, ffn2_w_in, ffn2_w_out, final_norm):
    b, s, d = x.shape
    depth = ffn1_norm.shape[0]
    outs = []
    for bi in range(b):
        xb = x[bi]
        for l in range(depth):
            xb = _layer(xb, ffn1_norm[l], ffn1_w_in[l], ffn1_w_out[l], mix_norm[l], w_in[l],
                        conv_w[l], conv_b[l], dt_bias[l], a_log[l], d_skip[l], ssm_norm[l],
                        f_bias[l], fox_norm[l], w_out[l], ffn2_norm[l], ffn2_w_in[l],
                        ffn2_w_out[l], final_norm, final=(l == depth - 1))
        outs.append(xb)
    return jnp.stack(outs, axis=0)
```

```python
import functools

import jax
import jax.numpy as jnp
import numpy as np
from jax import lax
from jax.experimental import pallas as pl
from jax.experimental.pallas import tpu as pltpu

F32 = jnp.float32
BF16 = jnp.bfloat16

D_MODEL = 1024
HEAD_DIM = 64
SSM_WIDTH = 512
SSM_HEADS = 8
SSM_GROUPS = 2
SSM_STATE = 128
CONV_WIDTH = 4
SSM_XBC = SSM_WIDTH + 2 * SSM_GROUPS * SSM_STATE
FOX_WIDTH = 512
FOX_HEADS = 8
D_FF = 2816
EPS = 1e-6

LANES = 128
VMEM_LIMIT = 56 * 1024 * 1024

FFN_TM = 1024
FFN_TF = 256
PROJ_TM = 512
SSD_T = 256
SSD_STEP = 1024
ATT_TQ = 512
ATT_TK = 512
NEG_INF = float("-inf")
LOG2E = 1.4426950408889634
SKIP_LOG2 = -150.0
NORM_MARGIN = 1.05
FIXED_MAX_LOG2 = 64.0


def _rms(x, g):
    ms = jnp.mean(x * x, axis=-1, keepdims=True)
    return x * lax.rsqrt(ms + EPS) * g


def _silu(x):
    return x * (1.0 / (1.0 + jnp.exp(-x)))


def _softplus(x):
    return jnp.maximum(x, 0.0) + jnp.log1p(jnp.exp(-jnp.abs(x)))


def _split3(x):
    hi = x.astype(BF16).astype(F32)
    r = x - hi
    mid = r.astype(BF16).astype(F32)
    lo = (r - mid).astype(BF16).astype(F32)
    return hi, mid, lo


def _dot_f32_rhs(a_bf16, x):
    hi, mid, lo = _split3(x)
    d = lambda piece: jnp.dot(a_bf16, piece.astype(BF16), preferred_element_type=F32)
    return (d(lo) + d(mid)) + d(hi)


def _dot_f32_lhs(x, b_bf16):
    hi, mid, lo = _split3(x)
    d = lambda piece: jnp.dot(piece.astype(BF16), b_bf16, preferred_element_type=F32)
    return (d(lo) + d(mid)) + d(hi)


def _ffn_kernel(*refs, mix, final, n_cast):
    if n_cast:
        n_in = len(refs) - 1 - 2 * n_cast
        for src_ref, dst_ref in zip(refs[n_in:n_in + n_cast], refs[n_in + n_cast + 1:]):
            dst_ref[...] = src_ref[...].astype(BF16)
        refs = refs[:n_in] + (refs[n_in + n_cast],)
    if mix:
        (x_ref, ys_ref, of_ref, mg_ref, wm1_ref, wm2_ref,
         g_ref, wg_ref, wu_ref, wo_ref, fg_ref, o_ref) = refs
        yf = _rms(of_ref[...], mg_ref[...]).astype(BF16)
        x = (x_ref[...]
             + jnp.dot(ys_ref[...], wm1_ref[...], preferred_element_type=F32)
             + jnp.dot(yf, wm2_ref[...], preferred_element_type=F32))
    else:
        x_ref, g_ref, wg_ref, wu_ref, wo_ref, fg_ref, o_ref = refs
        x = x_ref[...]
    h = _rms(x, g_ref[...]).astype(BF16)
    o_ref[...] = 2.0 * x
    nf = D_FF // FFN_TF

    def gate_up(f):
        cols = slice(f * FFN_TF, (f + 1) * FFN_TF)
        return (jnp.dot(h, wg_ref[:, cols], preferred_element_type=F32),
                jnp.dot(h, wu_ref[:, cols], preferred_element_type=F32))

    pending = gate_up(0)
    for f in range(nf):
        gate, up = pending
        if f + 1 < nf:
            pending = gate_up(f + 1)
        act = (_silu(gate) * up).astype(BF16)
        o_ref[...] += jnp.dot(act, wo_ref[f * FFN_TF:(f + 1) * FFN_TF, :],
                              preferred_element_type=F32)
    r = 0.5 * o_ref[...]
    if final:
        r = _rms(r, fg_ref[...])
    o_ref[...] = r


def _ffn(x, g, w_in, w_out, fg, *, final, mix=None, casts=()):
    s = x.shape[0]
    tm = FFN_TM
    steps = s // tm
    row = lambda i: (i, 0)
    const = lambda i: (0, 0)
    resident = functools.partial(pl.BlockSpec, pipeline_mode=pl.Buffered(1))
    x_spec = pl.BlockSpec((tm, D_MODEL), row)
    vec_spec = pl.BlockSpec((1, D_MODEL), const)
    ffn_specs = [
        vec_spec,
        resident((D_MODEL, D_FF), lambda i: (0, 0)),
        resident((D_MODEL, D_FF), lambda i: (0, 1)),
        resident((D_FF, D_MODEL), const),
        vec_spec,
    ]
    ffn_args = (g, w_in, w_in, w_out, fg)
    if mix is None:
        in_specs = [x_spec] + ffn_specs
        args = (x,) + ffn_args
    else:
        ys, of, mg, wm1, wm2 = mix
        in_specs = [
            x_spec,
            pl.BlockSpec((tm, SSM_WIDTH), row),
            pl.BlockSpec((tm, FOX_WIDTH), row),
            pl.BlockSpec((1, FOX_WIDTH), const),
            resident((SSM_WIDTH, D_MODEL), const),
            resident((FOX_WIDTH, D_MODEL), const),
        ] + ffn_specs
        args = (x, ys, of, mg, wm1, wm2) + ffn_args
    out_specs = [pl.BlockSpec((tm, D_MODEL), row)]
    out_shape = [jax.ShapeDtypeStruct((s, D_MODEL), F32)]
    for w in casts:
        slab = (w.shape[0] // steps, w.shape[1])
        assert slab[0] * steps == w.shape[0] and slab[0] % 16 == 0, w.shape
        in_specs.append(pl.BlockSpec(slab, row))
        out_specs.append(pl.BlockSpec(slab, row))
        out_shape.append(jax.ShapeDtypeStruct(w.shape, BF16))
    return pl.pallas_call(
        functools.partial(_ffn_kernel, mix=mix is not None, final=final, n_cast=len(casts)),
        grid=(steps,),
        in_specs=in_specs,
        out_specs=out_specs,
        out_shape=out_shape,
        compiler_params=pltpu.CompilerParams(
            dimension_semantics=("parallel",),
            vmem_limit_bytes=VMEM_LIMIT),
        name="ffn_final" if final else "ffn",
    )(*args, *casts)


def _inproj_kernel(x_ref, g_ref, wz_ref, wxbc_ref, wq_ref, wkt_ref, wv_ref, ws_ref,
                   ind_ref, indt_ref, cw_ref, cb_ref,
                   zs_ref, xs_ref, bc_ref, q_ref, kt_ref, v_ref, small_ref, qn_ref, kn_ref,
                   halo_ref):
    tm = PROJ_TM
    i = pl.program_id(0)

    @pl.when(i == 0)
    def _():
        halo_ref[...] = jnp.zeros_like(halo_ref)

    h = _rms(x_ref[...], g_ref[...]).astype(BF16)
    xb = jnp.dot(h, wxbc_ref[...], preferred_element_type=F32)
    zs_ref[...] = _silu(jnp.dot(h, wz_ref[...], preferred_element_type=F32))
    q = jnp.dot(h, wq_ref[...], preferred_element_type=F32)
    qb = (q * (HEAD_DIM ** -0.5 * LOG2E)).astype(BF16)
    q_ref[...] = qb
    kt = lax.dot_general(wkt_ref[...], h, (((0,), (1,)), ((), ())),
                         preferred_element_type=F32)
    ktb = kt.astype(BF16)
    kt_ref[0] = ktb
    v_ref[...] = jnp.dot(h, wv_ref[...], preferred_element_type=F32).astype(BF16)
    small_ref[...] = jnp.dot(h, ws_ref[...], preferred_element_type=F32)
    qf = qb.astype(F32)
    qn = jnp.dot((qf * qf).astype(BF16), ind_ref[...], preferred_element_type=F32)
    qn_ref[0] = jnp.broadcast_to(jnp.max(qn, axis=0, keepdims=True), (8, LANES))
    kf = ktb.astype(F32)
    kn = jnp.dot(indt_ref[...], (kf * kf).astype(BF16), preferred_element_type=F32)
    kn_ref[0] = jnp.broadcast_to(jnp.max(kn, axis=1, keepdims=True), (16, LANES))
    prev = halo_ref[...]
    row8 = lax.broadcasted_iota(jnp.int32, (8, SSM_XBC), 0)
    conv = cb_ref[...] + xb * cw_ref[CONV_WIDTH - 1:CONV_WIDTH, :]
    for s in range(1, CONV_WIDTH):
        rolled = pltpu.roll(xb, s, axis=0)
        head = jnp.where(row8 < s, pltpu.roll(prev, s, axis=0), rolled[0:8])
        shifted = jnp.concatenate([head, rolled[8:]], axis=0)
        k = CONV_WIDTH - 1 - s
        conv = conv + shifted * cw_ref[k:k + 1, :]
    halo_ref[...] = xb[tm - 8:tm]
    u = _silu(conv)
    xs_ref[...] = u[:, :SSM_WIDTH]
    bc_ref[...] = u[:, SSM_WIDTH:].astype(BF16)


def _inproj(x, g, wz, wxbc, wq, wkt, wv, ws, cw, cb):
    s = x.shape[0]
    tm = PROJ_TM
    nb = s // tm
    const = lambda i: (0, 0)
    row = lambda i: (i, 0)
    ind_np = np.zeros((FOX_WIDTH, LANES), np.float32)
    for hd in range(FOX_HEADS):
        ind_np[hd * HEAD_DIM:(hd + 1) * HEAD_DIM, hd] = 1.0
    ind = jnp.asarray(ind_np, dtype=BF16)
    indt = jnp.asarray(ind_np.T[:16], dtype=BF16)
    return pl.pallas_call(
        _inproj_kernel,
        grid=(nb,),
        in_specs=[
            pl.BlockSpec((tm, D_MODEL), row),
            pl.BlockSpec((1, D_MODEL), const),
            pl.BlockSpec((D_MODEL, SSM_WIDTH), const),
            pl.BlockSpec((D_MODEL, SSM_XBC), const),
            pl.BlockSpec((D_MODEL, FOX_WIDTH), const),
            pl.BlockSpec((D_MODEL, FOX_WIDTH), const),
            pl.BlockSpec((D_MODEL, FOX_WIDTH), const),
            pl.BlockSpec((D_MODEL, LANES), const),
            pl.BlockSpec((FOX_WIDTH, LANES), const),
            pl.BlockSpec((16, FOX_WIDTH), const),
            pl.BlockSpec((CONV_WIDTH, SSM_XBC), const),
            pl.BlockSpec((1, SSM_XBC), const),
        ],
        out_specs=[
            pl.BlockSpec((tm, SSM_WIDTH), row),
            pl.BlockSpec((tm, SSM_WIDTH), row),
            pl.BlockSpec((tm, SSM_XBC - SSM_WIDTH), row),
            pl.BlockSpec((tm, FOX_WIDTH), row),
            pl.BlockSpec((1, FOX_WIDTH, tm), lambda i: (i, 0, 0)),
            pl.BlockSpec((tm, FOX_WIDTH), row),
            pl.BlockSpec((tm, LANES), row),
            pl.BlockSpec((1, 8, LANES), lambda i: (i, 0, 0)),
            pl.BlockSpec((1, 16, LANES), lambda i: (i, 0, 0)),
        ],
        out_shape=[
            jax.ShapeDtypeStruct((s, SSM_WIDTH), F32),
            jax.ShapeDtypeStruct((s, SSM_WIDTH), F32),
            jax.ShapeDtypeStruct((s, SSM_XBC - SSM_WIDTH), BF16),
            jax.ShapeDtypeStruct((s, FOX_WIDTH), BF16),
            jax.ShapeDtypeStruct((nb, FOX_WIDTH, tm), BF16),
            jax.ShapeDtypeStruct((s, FOX_WIDTH), BF16),
            jax.ShapeDtypeStruct((s, LANES), F32),
            jax.ShapeDtypeStruct((nb, 8, LANES), F32),
            jax.ShapeDtypeStruct((nb, 16, LANES), F32),
        ],
        scratch_shapes=[pltpu.VMEM((8, SSM_XBC), F32)],
        compiler_params=pltpu.CompilerParams(
            dimension_semantics=("arbitrary",),
            vmem_limit_bytes=VMEM_LIMIT),
        name="in_proj",
    )(x, g, wz, wxbc, wq, wkt, wv, ws, ind, indt, cw, cb)


def _ssd_kernel(zs_ref, xs_ref, bc_ref, small_ref, bias_ref, apad_ref,
                dskip_ref, ng_ref, tri_ref, expand_ref, place_ref,
                y_ref, qaug_ref, cumt_ref, cmax_ref, cmin_ref,
                state_ref, carry_ref):
    t_rows = SSD_T
    i = pl.program_id(0)

    @pl.when(i == 0)
    def _():
        state_ref[...] = jnp.zeros_like(state_ref)
        carry_ref[...] = jnp.zeros_like(carry_ref)

    chunks = range(SSD_STEP // t_rows)
    rows = [slice(c * t_rows, (c + 1) * t_rows) for c in chunks]
    gw = SSM_WIDTH // SSM_GROUPS
    hpg = SSM_HEADS // SSM_GROUPS
    groups = range(SSM_GROUPS)
    lane = lax.broadcasted_iota(jnp.int32, (t_rows, LANES), 1)
    is_dt = lane < SSM_HEADS
    lo = lane < HEAD_DIM

    sp, v = [], []
    for c in chunks:
        t = small_ref[rows[c], :] + bias_ref[...]
        sp.append(_softplus(jnp.where(is_dt, t, -t)))
        v.append(jnp.where(is_dt, sp[c] * apad_ref[...], -sp[c]))
    local = [_dot_f32_rhs(tri_ref[...], v[c]) for c in chunks]
    carry = carry_ref[...]
    cs, cst = [], []
    for c in chunks:
        cs.append(local[c] + carry)
        carry = jnp.where(lane[0:1, :] < SSM_HEADS, 0.0, cs[c][t_rows - 1:t_rows, :])
        cst.append(cs[c].T)
        cumt_ref[:, rows[c]] = cst[c][0:16, :]
        cmax_ref[c] = jnp.broadcast_to(jnp.max(cs[c], axis=0, keepdims=True), (8, LANES))
        cmin_ref[c] = jnp.broadcast_to(jnp.min(cs[c], axis=0, keepdims=True), (8, LANES))
    carry_ref[...] = carry

    placed = [[jnp.dot(piece.astype(BF16), place_ref[...], preferred_element_type=F32)
               for piece in _split3(cs[c] * LOG2E)] for c in chunks]
    expanded = [_dot_f32_lhs(jnp.concatenate([sp[c], cs[c]], axis=0), expand_ref[...])
                for c in chunks]
    lane4 = lax.broadcasted_iota(jnp.int32, (t_rows, FOX_WIDTH), 1) & (HEAD_DIM - 1)
    xs, ea_e, cd_e, xdt_b, wst = [], [], [], [], []
    for c in chunks:
        qaug_ref[rows[c], :] = jnp.where(
            lane4 == 0, placed[c][0],
            jnp.where(lane4 == 1, placed[c][1],
                      jnp.where(lane4 == 2, placed[c][2],
                                jnp.where(lane4 < 6, 1.0, 0.0)))).astype(BF16)
        dt_e = expanded[c][0:t_rows]
        cs_e = expanded[c][t_rows:2 * t_rows]
        ea_e.append(jnp.exp(cs_e))
        de_e = jnp.exp(cs_e[t_rows - 1:t_rows, :] - cs_e)
        cd_e.append(ea_e[c][t_rows - 1:t_rows, :])
        xs.append(xs_ref[rows[c], :])
        xdt = xs[c] * dt_e
        xdt_b.append(xdt.astype(BF16))
        wst.append((xdt * de_e).astype(BF16))

    def cgrp(c, g):
        return bc_ref[rows[c], SSM_GROUPS * SSM_STATE + g * SSM_STATE:
                      SSM_GROUPS * SSM_STATE + (g + 1) * SSM_STATE]

    def bgrp(c, g):
        return bc_ref[rows[c], g * SSM_STATE:(g + 1) * SSM_STATE]

    gmat = [[lax.dot_general(cgrp(c, g), bgrp(c, g), (((1,), (1,)), ((), ())),
                             preferred_element_type=F32) for g in groups]
            for c in chunks]
    r_i = lax.broadcasted_iota(jnp.int32, (t_rows, t_rows), 0)
    c_i = lax.broadcasted_iota(jnp.int32, (t_rows, t_rows), 1)
    causal = r_i >= c_i
    y_parts = []
    for c in chunks:
        parts = []
        for g in groups:
            for pr in range(hpg // 2):
                c0 = g * gw + pr * LANES
                xpair = xdt_b[c][:, c0:c0 + LANES]
                acc = None
                for hh in range(2):
                    h = g * hpg + pr * 2 + hh
                    seg = cs[c][:, h:h + 1] - cst[c][h:h + 1, :]
                    dec = jnp.exp(jnp.where(causal, seg, NEG_INF))
                    m = (gmat[c][g] * dec).astype(BF16)
                    xm = jnp.where(lo if hh == 0 else jnp.logical_not(lo), xpair,
                                   jnp.zeros_like(xpair))
                    part = jnp.dot(m, xm, preferred_element_type=F32)
                    acc = part if acc is None else acc + part
                parts.append(acc)
        y_parts.append(parts)

    upd = [[jnp.dot(bgrp(c, g).astype(F32).T.astype(BF16), wst[c][:, g * gw:(g + 1) * gw],
                    preferred_element_type=F32) for g in groups]
           for c in chunks]

    state = [state_ref[g] for g in groups]
    for c in chunks:
        for g in groups:
            y_off = jnp.dot(cgrp(c, g), state[g].astype(BF16), preferred_element_type=F32)
            for pr in range(hpg // 2):
                cols = slice(g * gw + pr * LANES, g * gw + (pr + 1) * LANES)
                y_parts[c][g * (hpg // 2) + pr] += (
                    y_off[:, pr * LANES:(pr + 1) * LANES] * ea_e[c][:, cols])
            state[g] = state[g] * cd_e[c][:, g * gw:(g + 1) * gw] + upd[c][g]
    for g in groups:
        state_ref[g] = state[g]

    for c in chunks:
        y = jnp.concatenate(y_parts[c], axis=1) + dskip_ref[...] * xs[c]
        y = y * zs_ref[rows[c], :]
        y_ref[rows[c], :] = _rms(y, ng_ref[...]).astype(BF16)


def _ssd(zs, xs, bc, small, bias_pad, a_pad, dskip_e, ng, tri, expand):
    s = zs.shape[0]
    t = SSD_T
    st = SSD_STEP
    nsub = st // t
    const = lambda i: (0, 0)
    row = lambda i: (i, 0)
    place_np = np.zeros((LANES, FOX_WIDTH), np.float32)
    for hd in range(FOX_HEADS):
        pair, odd = divmod(hd, 2)
        c0 = pair * LANES + (0 if odd else HEAD_DIM)
        place_np[FOX_HEADS + hd, c0:c0 + HEAD_DIM] = 1.0
    place = jnp.asarray(place_np, dtype=BF16)
    return pl.pallas_call(
        _ssd_kernel,
        grid=(s // st,),
        in_specs=[
            pl.BlockSpec((st, SSM_WIDTH), row),
            pl.BlockSpec((st, SSM_WIDTH), row),
            pl.BlockSpec((st, SSM_XBC - SSM_WIDTH), row),
            pl.BlockSpec((st, LANES), row),
            pl.BlockSpec((1, LANES), const),
            pl.BlockSpec((1, LANES), const),
            pl.BlockSpec((1, SSM_WIDTH), const),
            pl.BlockSpec((1, SSM_WIDTH), const),
            pl.BlockSpec((t, t), const),
            pl.BlockSpec((LANES, SSM_WIDTH), const),
            pl.BlockSpec((LANES, FOX_WIDTH), const),
        ],
        out_specs=[
            pl.BlockSpec((st, SSM_WIDTH), row),
            pl.BlockSpec((st, FOX_WIDTH), row),
            pl.BlockSpec((16, st), lambda i: (0, i)),
            pl.BlockSpec((nsub, 8, LANES), lambda i: (i, 0, 0)),
            pl.BlockSpec((nsub, 8, LANES), lambda i: (i, 0, 0)),
        ],
        out_shape=[
            jax.ShapeDtypeStruct((s, SSM_WIDTH), BF16),
            jax.ShapeDtypeStruct((s, FOX_WIDTH), BF16),
            jax.ShapeDtypeStruct((16, s), F32),
            jax.ShapeDtypeStruct((s // t, 8, LANES), F32),
            jax.ShapeDtypeStruct((s // t, 8, LANES), F32),
        ],
        scratch_shapes=[
            pltpu.VMEM((SSM_GROUPS, SSM_STATE, SSM_WIDTH // SSM_GROUPS), F32),
            pltpu.VMEM((1, LANES), F32),
        ],
        compiler_params=pltpu.CompilerParams(
            dimension_semantics=("arbitrary",),
            vmem_limit_bytes=VMEM_LIMIT),
        name="ssd",
    )(zs, xs, bc, small, bias_pad, a_pad, dskip_e, ng, tri, expand, place)


def _fox_kernel(nkv_ref, q_ref, kt_ref, v_ref, qaug_ref, cumt_ref, o_ref,
                kaug_ref, m_ref, l_ref, acc_ref):
    tq, tk = ATT_TQ, ATT_TK
    nb = kt_ref.shape[0]
    nq = pl.num_programs(1)
    p = pl.program_id(0)
    i = pl.program_id(1)

    @pl.when(i == 0)
    def _():
        row16 = lax.broadcasted_iota(jnp.int32, (16, tk), 0)
        zeros48 = jnp.zeros((48, tk), BF16)

        def build(j, carry):
            off = pl.multiple_of(j * tk, tk)
            kt = kt_ref[j]
            for hh in range(2):
                ck = cumt_ref[pl.ds(FOX_HEADS + 2 * p + hh, 1), pl.ds(off, tk)] * LOG2E
                hi, mid, lo = _split3(ck)
                bias = jnp.where(row16 < 3, 1.0,
                                 jnp.where(row16 == 3, -hi,
                                           jnp.where(row16 == 4, -mid,
                                                     jnp.where(row16 == 5, -lo, 0.0))))
                bias = bias.astype(BF16)
                if hh == 0:
                    kaug_ref[0, j, 0:64, :] = kt[0:64]
                    kaug_ref[0, j, 64:80, :] = bias
                    kaug_ref[0, j, 80:128, :] = zeros48
                else:
                    kaug_ref[1, j, 0:16, :] = bias
                    kaug_ref[1, j, 16:64, :] = zeros48
                    kaug_ref[1, j, 64:128, :] = kt[64:128]
            return carry

        lax.fori_loop(0, nb, build, 0)

    q = q_ref[...]
    lane = lax.broadcasted_iota(jnp.int32, (tq, LANES), 1)
    lo_half = lane < HEAD_DIM
    r_i = lax.broadcasted_iota(jnp.int32, (tq, tk), 0)
    c_i = lax.broadcasted_iota(jnp.int32, (tq, tk), 1)
    causal = r_i >= c_i

    aug = qaug_ref[...]
    qas = [jnp.where(lo_half, q, aug), jnp.where(lo_half, aug, q)]

    def logits(hh, j):
        return jnp.dot(qas[hh], kaug_ref[hh, j], preferred_element_type=F32)

    def vblock(j):
        return v_ref[pl.ds(pl.multiple_of(j * tk, tk), tk), :]

    def lane_fold(x):
        out = x[:, 0:LANES]
        for c in range(1, x.shape[1] // LANES):
            out = out + x[:, c * LANES:(c + 1) * LANES]
        return out

    def online_diag(hh):
        s = jnp.where(causal, logits(hh, i), NEG_INF)
        m0 = jnp.max(s, axis=1, keepdims=True)
        p0 = jnp.exp2(s - m0)
        m_ref[hh] = jnp.broadcast_to(m0, (tq, LANES))
        l_ref[hh] = lane_fold(p0)
        acc_ref[hh] = jnp.dot(p0.astype(BF16), vblock(i), preferred_element_type=F32)

    def online_step(hh, j):
        s = logits(hh, j)
        m_prev = m_ref[hh]
        m_new = jnp.maximum(m_prev, jnp.max(s, axis=1, keepdims=True))
        alpha = jnp.exp2(m_prev - m_new)
        pj = jnp.exp2(s - m_new[:, 0:1])
        l_ref[hh] = alpha * l_ref[hh] + lane_fold(pj)
        acc_ref[hh] = alpha * acc_ref[hh] + jnp.dot(
            pj.astype(BF16), vblock(j), preferred_element_type=F32)
        m_ref[hh] = m_new

    half = tq // 2

    def fixed_diag(heads):
        vb = vblock(i)
        logit_pairs = [
            (jnp.dot(qas[hh][0:half], kaug_ref[hh, i, :, 0:half],
                     preferred_element_type=F32),
             jnp.dot(qas[hh][half:], kaug_ref[hh, i], preferred_element_type=F32))
            for hh in heads]
        for hh, (s_top, s_bot) in zip(heads, logit_pairs):
            p_top = jnp.exp2(jnp.where(causal[0:half, 0:half], s_top, NEG_INF))
            p_bot = jnp.exp2(jnp.where(causal[half:, :], s_bot, NEG_INF))
            l_ref[hh, 0:half] = lane_fold(p_top)
            l_ref[hh, half:] = lane_fold(p_bot)
            acc_ref[hh, 0:half] = jnp.dot(p_top.astype(BF16), vb[0:half],
                                          preferred_element_type=F32)
            acc_ref[hh, half:] = jnp.dot(p_bot.astype(BF16), vb, preferred_element_type=F32)

    def fixed_steps(jobs):
        all_logits = [jnp.concatenate([logits(hh, j - (n - 1) + b) for b in range(n)], axis=1)
                      for hh, j, n in jobs]
        for (hh, j, n), s in zip(jobs, all_logits):
            pj = jnp.exp2(s)
            l_ref[hh] += lane_fold(pj)
            vn = v_ref[pl.ds(pl.multiple_of((j - (n - 1)) * tk, tk), n * tk), :]
            acc_ref[hh] += jnp.dot(pj.astype(BF16), vn, preferred_element_type=F32)

    codes = [nkv_ref[(2 * p + hh) * nq + i] for hh in range(2)]
    n_offs = [code >> 1 for code in codes]
    both_fixed = (codes[0] & codes[1] & 1) == 1
    joint_quads = jnp.where(both_fixed, jnp.minimum(n_offs[0] >> 2, n_offs[1] >> 2), 0)
    joint_pair = both_fixed & ((n_offs[0] & n_offs[1] & 2) == 2)
    joint_single = both_fixed & ((n_offs[0] & n_offs[1] & 1) == 1)
    pair_at = [i - 1 - 4 * (n >> 2) for n in n_offs]
    single_at = [i - n for n in n_offs]

    @pl.when(both_fixed)
    def _():
        fixed_diag((0, 1))

        def body(u, carry):
            fixed_steps([(0, i - 1 - 4 * u, 4), (1, i - 1 - 4 * u, 4)])
            return carry

        lax.fori_loop(0, joint_quads, body, 0)

        @pl.when(joint_pair)
        def _():
            fixed_steps([(0, pair_at[0], 2), (1, pair_at[1], 2)])

        @pl.when(joint_single)
        def _():
            fixed_steps([(0, single_at[0], 1), (1, single_at[1], 1)])

    for hh in range(2):
        n_off = n_offs[hh]
        fixed_ok = (codes[hh] & 1) == 1

        @pl.when(fixed_ok)
        def _(hh=hh, n_off=n_off):
            @pl.when(jnp.logical_not(both_fixed))
            def _():
                fixed_diag((hh,))

            def body(u, carry):
                fixed_steps([(hh, i - 1 - 4 * u, 4)])
                return carry

            lax.fori_loop(joint_quads, n_off >> 2, body, 0)

            @pl.when(((n_off & 2) == 2) & jnp.logical_not(joint_pair))
            def _():
                fixed_steps([(hh, pair_at[hh], 2)])

            @pl.when(((n_off & 1) == 1) & jnp.logical_not(joint_single))
            def _():
                fixed_steps([(hh, single_at[hh], 1)])

        @pl.when(jnp.logical_not(fixed_ok))
        def _(hh=hh, n_off=n_off):
            online_diag(hh)

            def body(jj, carry):
                online_step(hh, i - 1 - jj)
                return carry

            lax.fori_loop(0, n_off, body, 0)

    l0 = jnp.sum(l_ref[0], axis=1, keepdims=True)
    l1 = jnp.sum(l_ref[1], axis=1, keepdims=True)
    o_ref[...] = jnp.where(lo_half, acc_ref[0] / l0, acc_ref[1] / l1)


def _kv_counts(qn_tiles, kn_tiles, cmax_chunks, cmin_chunks):
    nq = qn_tiles.shape[0]
    per_q = cmax_chunks.shape[0] // nq
    qn = jnp.sqrt(qn_tiles[:, 0, :FOX_HEADS])
    kn = jnp.sqrt(jnp.max(kn_tiles[:, :FOX_HEADS, 0], axis=0))
    gate = slice(FOX_HEADS, 2 * FOX_HEADS)
    cmax = jnp.max(cmax_chunks[:, 0, gate].reshape(nq, per_q, FOX_HEADS), axis=1) * LOG2E
    cmin = jnp.min(cmin_chunks[:, 0, gate].reshape(nq, per_q, FOX_HEADS), axis=1) * LOG2E
    pmin = lax.cummin(cmin, axis=0)
    qk = NORM_MARGIN * qn * kn[None, :] + 1.0
    bound = (2.0 * qk + cmax)[:, None, :] - pmin[None, :, :]
    ii = lax.broadcasted_iota(jnp.int32, bound.shape, 0)
    jj = lax.broadcasted_iota(jnp.int32, bound.shape, 1)
    skip = (bound < SKIP_LOG2) & (jj < ii)
    jstar = jnp.max(jnp.where(skip, jj, -1), axis=1)
    n_off = jnp.arange(nq, dtype=jnp.int32)[:, None] - 1 - jstar
    pmin_prev = jnp.concatenate([jnp.full((1, FOX_HEADS), jnp.inf, F32), pmin[:-1]], axis=0)
    fixed_ok = (qk + jnp.maximum(cmax - pmin_prev, 0.0)) < FIXED_MAX_LOG2
    code = 2 * n_off + fixed_ok.astype(jnp.int32)
    return code.T.reshape(-1).astype(jnp.int32)


def _fox(nkv, q, kt3, v, qaug, cumt):
    s = q.shape[0]
    tq, tk = ATT_TQ, ATT_TK
    nb = s // tk
    grid_spec = pltpu.PrefetchScalarGridSpec(
        num_scalar_prefetch=1,
        grid=(FOX_HEADS // 2, s // tq),
        in_specs=[
            pl.BlockSpec((tq, LANES), lambda p, i, n: (i, p)),
            pl.BlockSpec((nb, LANES, tk), lambda p, i, n: (0, p, 0)),
            pl.BlockSpec((s, LANES), lambda p, i, n: (0, p)),
            pl.BlockSpec((tq, LANES), lambda p, i, n: (i, p)),
            pl.BlockSpec((16, s), lambda p, i, n: (0, 0)),
        ],
        out_specs=pl.BlockSpec((tq, LANES), lambda p, i, n: (i, p)),
        scratch_shapes=[
            pltpu.VMEM((2, nb, LANES, tk), BF16),
            pltpu.VMEM((2, tq, LANES), F32),
            pltpu.VMEM((2, tq, LANES), F32),
            pltpu.VMEM((2, tq, LANES), F32),
        ],
    )
    return pl.pallas_call(
        _fox_kernel,
        grid_spec=grid_spec,
        out_shape=jax.ShapeDtypeStruct((s, FOX_WIDTH), F32),
        compiler_params=pltpu.CompilerParams(
            dimension_semantics=("arbitrary", "arbitrary"),
            vmem_limit_bytes=VMEM_LIMIT),
        name="fox_attn",
    )(nkv, q, kt3, v, qaug, cumt)


def _pad_lanes(vec_dt, vec_f):
    out = jnp.zeros((1, LANES), F32)
    out = out.at[0, 0:SSM_HEADS].set(vec_dt.astype(F32))
    if vec_f is not None:
        out = out.at[0, SSM_HEADS:SSM_HEADS + FOX_HEADS].set(vec_f.astype(F32))
    return out


def _layer(x, ffn1_norm, ffn1_w_in, ffn1_w_out, mix_norm, w_in, conv_w, conv_b, dt_bias,
           a_log, d_skip, ssm_norm, f_bias, fox_norm, w_out, ffn2_norm, ffn2_w_in,
           ffn2_w_out, final_g, *, final):
    ones = jnp.ones((1, D_MODEL), F32)
    row = lambda a: a.reshape(1, -1).astype(F32)

    x1, ffn2_wi, ffn2_wo, w_in, wo = _ffn(
        x, row(ffn1_norm), ffn1_w_in.astype(BF16), ffn1_w_out.astype(BF16), ones,
        final=False, casts=(ffn2_w_in, ffn2_w_out, w_in, w_out))

    o0 = SSM_WIDTH
    o1 = o0 + SSM_XBC
    o2 = o1 + SSM_HEADS
    o3 = o2 + 3 * FOX_WIDTH
    wz = w_in[:, :o0]
    wxbc = w_in[:, o0:o1]
    w_dt = w_in[:, o1:o2]
    wq = w_in[:, o2:o2 + FOX_WIDTH]
    wkt = w_in[:, o2 + FOX_WIDTH:o2 + 2 * FOX_WIDTH]
    wv = w_in[:, o2 + 2 * FOX_WIDTH:o3]
    w_f = w_in[:, o3:]
    ws = jnp.concatenate(
        [w_dt, w_f, jnp.zeros((D_MODEL, LANES - SSM_HEADS - FOX_HEADS), w_in.dtype)], axis=1)
    zs, xs, bc, q, kt3, v, small, qn2, kn2 = _inproj(
        x1, row(mix_norm), wz, wxbc, wq, wkt, wv, ws, conv_w.astype(F32), row(conv_b))

    bias_pad = _pad_lanes(dt_bias, f_bias)
    a_pad = _pad_lanes(-jnp.exp(a_log.astype(F32)), None)
    dskip_e = jnp.repeat(d_skip.astype(F32), HEAD_DIM).reshape(1, SSM_WIDTH)
    tri = jnp.asarray(np.tril(np.ones((SSD_T, SSD_T), np.float32)), dtype=BF16)
    expand_np = np.zeros((LANES, SSM_WIDTH), np.float32)
    for hd in range(SSM_HEADS):
        expand_np[hd, hd * HEAD_DIM:(hd + 1) * HEAD_DIM] = 1.0
    y_ssd, qaug, cumt, cmax_c, cmin_c = _ssd(zs, xs, bc, small, bias_pad, a_pad,
                            dskip_e, row(ssm_norm), tri, jnp.asarray(expand_np, dtype=BF16))

    o_fox = _fox(_kv_counts(qn2, kn2, cmax_c, cmin_c), q, kt3, v, qaug, cumt)

    mix = (y_ssd, o_fox, row(fox_norm), wo[:SSM_WIDTH], wo[SSM_WIDTH:])
    out, = _ffn(x1, row(ffn2_norm), ffn2_wi, ffn2_wo, row(final_g), final=final, mix=mix)
    return out


def kernel(x, ffn1_norm, ffn1_w_in, ffn1_w_out, mix_norm, w_in, conv_w, conv_b, dt_bias, a_log,
           d_skip, ssm_norm, f_bias, fox_norm, w_out, ffn2_norm, ffn2_w_in, ffn2_w_out, final_norm):
    b, s, d = x.shape
    depth = ffn1_norm.shape[0]
    outs = []
    for bi in range(b):
        xb = x[bi]
        for l in range(depth):
            xb = _layer(xb, ffn1_norm[l], ffn1_w_in[l], ffn1_w_out[l], mix_norm[l], w_in[l],
                        conv_w[l], conv_b[l], dt_bias[l], a_log[l], d_skip[l], ssm_norm[l],
                        f_bias[l], fox_norm[l], w_out[l], ffn2_norm[l], ffn2_w_in[l],
                        ffn2_w_out[l], final_norm, final=(l == depth - 1))
        outs.append(xb)
    return jnp.stack(outs, axis=0)
```

```python
import functools

import jax
import jax.numpy as jnp
import numpy as np
from jax import lax
from jax.experimental import pallas as pl
from jax.experimental.pallas import tpu as pltpu

F32 = jnp.float32
BF16 = jnp.bfloat16

D_MODEL = 1024
HEAD_DIM = 64
SSM_WIDTH = 512
SSM_HEADS = 8
SSM_GROUPS = 2
SSM_STATE = 128
CONV_WIDTH = 4
SSM_XBC = SSM_WIDTH + 2 * SSM_GROUPS * SSM_STATE
FOX_WIDTH = 512
FOX_HEADS = 8
D_FF = 2816
EPS = 1e-6

LANES = 128
VMEM_LIMIT = 56 * 1024 * 1024

FFN_TM = 1024
FFN_TF = 256
PROJ_TM = 512
SSD_T = 256
SSD_STEP = 1024
ATT_TQ = 512
ATT_TK = 512
NEG_INF = float("-inf")
LOG2E = 1.4426950408889634
SKIP_LOG2 = -150.0
NORM_MARGIN = 1.05
FIXED_MAX_LOG2 = 64.0


def _rms(x, g):
    ms = jnp.mean(x * x, axis=-1, keepdims=True)
    return x * lax.rsqrt(ms + EPS) * g


def _silu(x):
    return x * (1.0 / (1.0 + jnp.exp(-x)))


def _softplus(x):
    return jnp.maximum(x, 0.0) + jnp.log1p(jnp.exp(-jnp.abs(x)))


def _split3(x):
    hi = x.astype(BF16).astype(F32)
    r = x - hi
    mid = r.astype(BF16).astype(F32)
    lo = (r - mid).astype(BF16).astype(F32)
    return hi, mid, lo


def _dot_f32_rhs(a_bf16, x):
    hi, mid, lo = _split3(x)
    d = lambda piece: jnp.dot(a_bf16, piece.astype(BF16), preferred_element_type=F32)
    return (d(lo) + d(mid)) + d(hi)


def _dot_f32_lhs(x, b_bf16):
    hi, mid, lo = _split3(x)
    d = lambda piece: jnp.dot(piece.astype(BF16), b_bf16, preferred_element_type=F32)
    return (d(lo) + d(mid)) + d(hi)


def _ffn_kernel(*refs, mix, final, cast_cols):
    if cast_cols:
        n_src = len(cast_cols)
        n_dst = sum(len(ranges) for ranges in cast_cols)
        n_in = len(refs) - 1 - n_src - n_dst
        dst_refs = iter(refs[n_in + n_src + 1:])
        for src_ref, ranges in zip(refs[n_in:n_in + n_src], cast_cols):
            for c0, c1 in ranges:
                next(dst_refs)[...] = src_ref[:, c0:c1].astype(BF16)
        refs = refs[:n_in] + (refs[n_in + n_src],)
    if mix:
        (x_ref, ys_ref, of_ref, mg_ref, wm1_ref, wm2_ref,
         g_ref, wg_ref, wu_ref, wo_ref, fg_ref, o_ref) = refs
        yf = _rms(of_ref[...], mg_ref[...]).astype(BF16)
        x = (x_ref[...]
             + jnp.dot(ys_ref[...], wm1_ref[...], preferred_element_type=F32)
             + jnp.dot(yf, wm2_ref[...], preferred_element_type=F32))
    else:
        x_ref, g_ref, wg_ref, wu_ref, wo_ref, fg_ref, o_ref = refs
        x = x_ref[...]
    h = _rms(x, g_ref[...]).astype(BF16)
    o_ref[...] = 2.0 * x
    nf = D_FF // FFN_TF

    def gate_up(f):
        cols = slice(f * FFN_TF, (f + 1) * FFN_TF)
        return (jnp.dot(h, wg_ref[:, cols], preferred_element_type=F32),
                jnp.dot(h, wu_ref[:, cols], preferred_element_type=F32))

    pending = gate_up(0)
    for f in range(nf):
        gate, up = pending
        if f + 1 < nf:
            pending = gate_up(f + 1)
        act = (_silu(gate) * up).astype(BF16)
        o_ref[...] += jnp.dot(act, wo_ref[f * FFN_TF:(f + 1) * FFN_TF, :],
                              preferred_element_type=F32)
    r = 0.5 * o_ref[...]
    if final:
        r = _rms(r, fg_ref[...])
    o_ref[...] = r


def _ffn(x, g, w_in, w_out, fg, *, final, mix=None, casts=()):
    s = x.shape[0]
    tm = FFN_TM
    steps = s // tm
    row = lambda i: (i, 0)
    const = lambda i: (0, 0)
    resident = functools.partial(pl.BlockSpec, pipeline_mode=pl.Buffered(1))
    x_spec = pl.BlockSpec((tm, D_MODEL), row)
    vec_spec = pl.BlockSpec((1, D_MODEL), const)
    ffn_specs = [
        vec_spec,
        resident((D_MODEL, D_FF), lambda i: (0, 0)),
        resident((D_MODEL, D_FF), lambda i: (0, 1)),
        resident((D_FF, D_MODEL), const),
        vec_spec,
    ]
    ffn_args = (g, w_in, w_in, w_out, fg)
    if mix is None:
        in_specs = [x_spec] + ffn_specs
        args = (x,) + ffn_args
    else:
        ys, of, mg, wm1, wm2 = mix
        in_specs = [
            x_spec,
            pl.BlockSpec((tm, SSM_WIDTH), row),
            pl.BlockSpec((tm, FOX_WIDTH), row),
            pl.BlockSpec((1, FOX_WIDTH), const),
            resident((SSM_WIDTH, D_MODEL), const),
            resident((FOX_WIDTH, D_MODEL), const),
        ] + ffn_specs
        args = (x, ys, of, mg, wm1, wm2) + ffn_args
    out_specs = [pl.BlockSpec((tm, D_MODEL), row)]
    out_shape = [jax.ShapeDtypeStruct((s, D_MODEL), F32)]
    for w, ranges in casts:
        slab_rows = w.shape[0] // steps
        assert slab_rows * steps == w.shape[0] and slab_rows % 16 == 0, w.shape
        in_specs.append(pl.BlockSpec((slab_rows, w.shape[1]), row))
        for c0, c1 in ranges:
            out_specs.append(pl.BlockSpec((slab_rows, c1 - c0), row))
            out_shape.append(jax.ShapeDtypeStruct((w.shape[0], c1 - c0), BF16))
    return pl.pallas_call(
        functools.partial(_ffn_kernel, mix=mix is not None, final=final,
                          cast_cols=tuple(tuple(ranges) for _, ranges in casts)),
        grid=(steps,),
        in_specs=in_specs,
        out_specs=out_specs,
        out_shape=out_shape,
        compiler_params=pltpu.CompilerParams(
            dimension_semantics=("parallel",),
            vmem_limit_bytes=VMEM_LIMIT),
        name="ffn_final" if final else "ffn",
    )(*args, *(w for w, _ in casts))


def _inproj_kernel(x_ref, g_ref, wz_ref, wxbc_ref, wq_ref, wkt_ref, wv_ref, ws_ref,
                   ind_ref, indt_ref, cw_ref, cb_ref,
                   zs_ref, xs_ref, bc_ref, q_ref, kt_ref, v_ref, small_ref, qn_ref, kn_ref,
                   halo_ref):
    tm = PROJ_TM
    i = pl.program_id(0)

    @pl.when(i == 0)
    def _():
        halo_ref[...] = jnp.zeros_like(halo_ref)

    h = _rms(x_ref[...], g_ref[...]).astype(BF16)
    xb = jnp.dot(h, wxbc_ref[...], preferred_element_type=F32)
    zs_ref[...] = _silu(jnp.dot(h, wz_ref[...], preferred_element_type=F32))
    q = jnp.dot(h, wq_ref[...], preferred_element_type=F32)
    qb = (q * (HEAD_DIM ** -0.5 * LOG2E)).astype(BF16)
    q_ref[...] = qb
    kt = lax.dot_general(wkt_ref[...], h, (((0,), (1,)), ((), ())),
                         preferred_element_type=F32)
    ktb = kt.astype(BF16)
    kt_ref[0] = ktb
    v_ref[...] = jnp.dot(h, wv_ref[...], preferred_element_type=F32).astype(BF16)
    small_ref[...] = jnp.dot(h, ws_ref[...], preferred_element_type=F32)
    qf = qb.astype(F32)
    qn = jnp.dot((qf * qf).astype(BF16), ind_ref[...], preferred_element_type=F32)
    qn_ref[0] = jnp.broadcast_to(jnp.max(qn, axis=0, keepdims=True), (8, LANES))
    kf = ktb.astype(F32)
    kn = jnp.dot(indt_ref[...], (kf * kf).astype(BF16), preferred_element_type=F32)
    kn_ref[0] = jnp.broadcast_to(jnp.max(kn, axis=1, keepdims=True), (16, LANES))
    prev = halo_ref[...]
    row8 = lax.broadcasted_iota(jnp.int32, (8, SSM_XBC), 0)
    conv = cb_ref[...] + xb * cw_ref[CONV_WIDTH - 1:CONV_WIDTH, :]
    for s in range(1, CONV_WIDTH):
        rolled = pltpu.roll(xb, s, axis=0)
        head = jnp.where(row8 < s, pltpu.roll(prev, s, axis=0), rolled[0:8])
        shifted = jnp.concatenate([head, rolled[8:]], axis=0)
        k = CONV_WIDTH - 1 - s
        conv = conv + shifted * cw_ref[k:k + 1, :]
    halo_ref[...] = xb[tm - 8:tm]
    u = _silu(conv)
    xs_ref[...] = u[:, :SSM_WIDTH]
    bc_ref[...] = u[:, SSM_WIDTH:].astype(BF16)


def _inproj(x, g, wz, wxbc, wq, wkt, wv, ws, cw, cb):
    s = x.shape[0]
    tm = PROJ_TM
    nb = s // tm
    const = lambda i: (0, 0)
    row = lambda i: (i, 0)
    ind_np = np.zeros((FOX_WIDTH, LANES), np.float32)
    for hd in range(FOX_HEADS):
        ind_np[hd * HEAD_DIM:(hd + 1) * HEAD_DIM, hd] = 1.0
    ind = jnp.asarray(ind_np, dtype=BF16)
    indt = jnp.asarray(ind_np.T[:16], dtype=BF16)
    return pl.pallas_call(
        _inproj_kernel,
        grid=(nb,),
        in_specs=[
            pl.BlockSpec((tm, D_MODEL), row),
            pl.BlockSpec((1, D_MODEL), const),
            pl.BlockSpec((D_MODEL, SSM_WIDTH), const),
            pl.BlockSpec((D_MODEL, SSM_XBC), const),
            pl.BlockSpec((D_MODEL, FOX_WIDTH), const),
            pl.BlockSpec((D_MODEL, FOX_WIDTH), const),
            pl.BlockSpec((D_MODEL, FOX_WIDTH), const),
            pl.BlockSpec((D_MODEL, LANES), const),
            pl.BlockSpec((FOX_WIDTH, LANES), const),
            pl.BlockSpec((16, FOX_WIDTH), const),
            pl.BlockSpec((CONV_WIDTH, SSM_XBC), const),
            pl.BlockSpec((1, SSM_XBC), const),
        ],
        out_specs=[
            pl.BlockSpec((tm, SSM_WIDTH), row),
            pl.BlockSpec((tm, SSM_WIDTH), row),
            pl.BlockSpec((tm, SSM_XBC - SSM_WIDTH), row),
            pl.BlockSpec((tm, FOX_WIDTH), row),
            pl.BlockSpec((1, FOX_WIDTH, tm), lambda i: (i, 0, 0)),
            pl.BlockSpec((tm, FOX_WIDTH), row),
            pl.BlockSpec((tm, LANES), row),
            pl.BlockSpec((1, 8, LANES), lambda i: (i, 0, 0)),
            pl.BlockSpec((1, 16, LANES), lambda i: (i, 0, 0)),
        ],
        out_shape=[
            jax.ShapeDtypeStruct((s, SSM_WIDTH), F32),
            jax.ShapeDtypeStruct((s, SSM_WIDTH), F32),
            jax.ShapeDtypeStruct((s, SSM_XBC - SSM_WIDTH), BF16),
            jax.ShapeDtypeStruct((s, FOX_WIDTH), BF16),
            jax.ShapeDtypeStruct((nb, FOX_WIDTH, tm), BF16),
            jax.ShapeDtypeStruct((s, FOX_WIDTH), BF16),
            jax.ShapeDtypeStruct((s, LANES), F32),
            jax.ShapeDtypeStruct((nb, 8, LANES), F32),
            jax.ShapeDtypeStruct((nb, 16, LANES), F32),
        ],
        scratch_shapes=[pltpu.VMEM((8, SSM_XBC), F32)],
        compiler_params=pltpu.CompilerParams(
            dimension_semantics=("arbitrary",),
            vmem_limit_bytes=VMEM_LIMIT),
        name="in_proj",
    )(x, g, wz, wxbc, wq, wkt, wv, ws, ind, indt, cw, cb)


def _ssd_kernel(zs_ref, xs_ref, bc_ref, small_ref, bias_ref, apad_ref,
                dskip_ref, ng_ref, tri_ref, expand_ref, place_ref,
                y_ref, qaug_ref, cumt_ref, cmax_ref, cmin_ref,
                state_ref, carry_ref):
    t_rows = SSD_T
    i = pl.program_id(0)

    @pl.when(i == 0)
    def _():
        state_ref[...] = jnp.zeros_like(state_ref)
        carry_ref[...] = jnp.zeros_like(carry_ref)

    chunks = range(SSD_STEP // t_rows)
    rows = [slice(c * t_rows, (c + 1) * t_rows) for c in chunks]
    gw = SSM_WIDTH // SSM_GROUPS
    hpg = SSM_HEADS // SSM_GROUPS
    groups = range(SSM_GROUPS)
    lane = lax.broadcasted_iota(jnp.int32, (t_rows, LANES), 1)
    is_dt = lane < SSM_HEADS
    lo = lane < HEAD_DIM

    sp, v = [], []
    for c in chunks:
        t = small_ref[rows[c], :] + bias_ref[...]
        sp.append(_softplus(jnp.where(is_dt, t, -t)))
        v.append(jnp.where(is_dt, sp[c] * apad_ref[...], -sp[c]))
    local = [_dot_f32_rhs(tri_ref[...], v[c]) for c in chunks]
    carry = carry_ref[...]
    cs, cst = [], []
    for c in chunks:
        cs.append(local[c] + carry)
        carry = jnp.where(lane[0:1, :] < SSM_HEADS, 0.0, cs[c][t_rows - 1:t_rows, :])
        cst.append(cs[c].T)
        cumt_ref[:, rows[c]] = cst[c][0:16, :]
        cmax_ref[c] = jnp.broadcast_to(jnp.max(cs[c], axis=0, keepdims=True), (8, LANES))
        cmin_ref[c] = jnp.broadcast_to(jnp.min(cs[c], axis=0, keepdims=True), (8, LANES))
    carry_ref[...] = carry

    placed = [[jnp.dot(piece.astype(BF16), place_ref[...], preferred_element_type=F32)
               for piece in _split3(cs[c] * LOG2E)] for c in chunks]
    expanded = [_dot_f32_lhs(jnp.concatenate([sp[c], cs[c]], axis=0), expand_ref[...])
                for c in chunks]
    lane4 = lax.broadcasted_iota(jnp.int32, (t_rows, FOX_WIDTH), 1) & (HEAD_DIM - 1)
    xs, ea_e, cd_e, xdt_b, wst = [], [], [], [], []
    for c in chunks:
        qaug_ref[rows[c], :] = jnp.where(
            lane4 == 0, placed[c][0],
            jnp.where(lane4 == 1, placed[c][1],
                      jnp.where(lane4 == 2, placed[c][2],
                                jnp.where(lane4 < 6, 1.0, 0.0)))).astype(BF16)
        dt_e = expanded[c][0:t_rows]
        cs_e = expanded[c][t_rows:2 * t_rows]
        ea_e.append(jnp.exp(cs_e))
        de_e = jnp.exp(cs_e[t_rows - 1:t_rows, :] - cs_e)
        cd_e.append(ea_e[c][t_rows - 1:t_rows, :])
        xs.append(xs_ref[rows[c], :])
        xdt = xs[c] * dt_e
        xdt_b.append(xdt.astype(BF16))
        wst.append((xdt * de_e).astype(BF16))

    def cgrp(c, g):
        return bc_ref[rows[c], SSM_GROUPS * SSM_STATE + g * SSM_STATE:
                      SSM_GROUPS * SSM_STATE + (g + 1) * SSM_STATE]

    def bgrp(c, g):
        return bc_ref[rows[c], g * SSM_STATE:(g + 1) * SSM_STATE]

    gmat = [[lax.dot_general(cgrp(c, g), bgrp(c, g), (((1,), (1,)), ((), ())),
                             preferred_element_type=F32) for g in groups]
            for c in chunks]
    r_i = lax.broadcasted_iota(jnp.int32, (t_rows, t_rows), 0)
    c_i = lax.broadcasted_iota(jnp.int32, (t_rows, t_rows), 1)
    causal = r_i >= c_i
    y_parts = []
    for c in chunks:
        parts = []
        for g in groups:
            for pr in range(hpg // 2):
                c0 = g * gw + pr * LANES
                xpair = xdt_b[c][:, c0:c0 + LANES]
                acc = None
                for hh in range(2):
                    h = g * hpg + pr * 2 + hh
                    seg = cs[c][:, h:h + 1] - cst[c][h:h + 1, :]
                    dec = jnp.exp(jnp.where(causal, seg, NEG_INF))
                    m = (gmat[c][g] * dec).astype(BF16)
                    xm = jnp.where(lo if hh == 0 else jnp.logical_not(lo), xpair,
                                   jnp.zeros_like(xpair))
                    part = jnp.dot(m, xm, preferred_element_type=F32)
                    acc = part if acc is None else acc + part
                parts.append(acc)
        y_parts.append(parts)

    upd = [[jnp.dot(bgrp(c, g).astype(F32).T.astype(BF16), wst[c][:, g * gw:(g + 1) * gw],
                    preferred_element_type=F32) for g in groups]
           for c in chunks]

    state = [state_ref[g] for g in groups]
    for c in chunks:
        for g in groups:
            y_off = jnp.dot(cgrp(c, g), state[g].astype(BF16), preferred_element_type=F32)
            for pr in range(hpg // 2):
                cols = slice(g * gw + pr * LANES, g * gw + (pr + 1) * LANES)
                y_parts[c][g * (hpg // 2) + pr] += (
                    y_off[:, pr * LANES:(pr + 1) * LANES] * ea_e[c][:, cols])
            state[g] = state[g] * cd_e[c][:, g * gw:(g + 1) * gw] + upd[c][g]
    for g in groups:
        state_ref[g] = state[g]

    for c in chunks:
        y = jnp.concatenate(y_parts[c], axis=1) + dskip_ref[...] * xs[c]
        y = y * zs_ref[rows[c], :]
        y_ref[rows[c], :] = _rms(y, ng_ref[...]).astype(BF16)


def _ssd(zs, xs, bc, small, bias_pad, a_pad, dskip_e, ng, tri, expand):
    s = zs.shape[0]
    t = SSD_T
    st = SSD_STEP
    nsub = st // t
    const = lambda i: (0, 0)
    row = lambda i: (i, 0)
    place_np = np.zeros((LANES, FOX_WIDTH), np.float32)
    for hd in range(FOX_HEADS):
        pair, odd = divmod(hd, 2)
        c0 = pair * LANES + (0 if odd else HEAD_DIM)
        place_np[FOX_HEADS + hd, c0:c0 + HEAD_DIM] = 1.0
    place = jnp.asarray(place_np, dtype=BF16)
    return pl.pallas_call(
        _ssd_kernel,
        grid=(s // st,),
        in_specs=[
            pl.BlockSpec((st, SSM_WIDTH), row),
            pl.BlockSpec((st, SSM_WIDTH), row),
            pl.BlockSpec((st, SSM_XBC - SSM_WIDTH), row),
            pl.BlockSpec((st, LANES), row),
            pl.BlockSpec((1, LANES), const),
            pl.BlockSpec((1, LANES), const),
            pl.BlockSpec((1, SSM_WIDTH), const),
            pl.BlockSpec((1, SSM_WIDTH), const),
            pl.BlockSpec((t, t), const),
            pl.BlockSpec((LANES, SSM_WIDTH), const),
            pl.BlockSpec((LANES, FOX_WIDTH), const),
        ],
        out_specs=[
            pl.BlockSpec((st, SSM_WIDTH), row),
            pl.BlockSpec((st, FOX_WIDTH), row),
            pl.BlockSpec((16, st), lambda i: (0, i)),
            pl.BlockSpec((nsub, 8, LANES), lambda i: (i, 0, 0)),
            pl.BlockSpec((nsub, 8, LANES), lambda i: (i, 0, 0)),
        ],
        out_shape=[
            jax.ShapeDtypeStruct((s, SSM_WIDTH), BF16),
            jax.ShapeDtypeStruct((s, FOX_WIDTH), BF16),
            jax.ShapeDtypeStruct((16, s), F32),
            jax.ShapeDtypeStruct((s // t, 8, LANES), F32),
            jax.ShapeDtypeStruct((s // t, 8, LANES), F32),
        ],
        scratch_shapes=[
            pltpu.VMEM((SSM_GROUPS, SSM_STATE, SSM_WIDTH // SSM_GROUPS), F32),
            pltpu.VMEM((1, LANES), F32),
        ],
        compiler_params=pltpu.CompilerParams(
            dimension_semantics=("arbitrary",),
            vmem_limit_bytes=VMEM_LIMIT),
        name="ssd",
    )(zs, xs, bc, small, bias_pad, a_pad, dskip_e, ng, tri, expand, place)


def _fox_kernel(nkv_ref, q_ref, kt_ref, v_ref, qaug_ref, cumt_ref, o_ref,
                kaug_ref, m_ref, l_ref, acc_ref):
    tq, tk = ATT_TQ, ATT_TK
    nb = kt_ref.shape[0]
    nq = pl.num_programs(1)
    p = pl.program_id(0)
    i = pl.program_id(1)

    @pl.when(i == 0)
    def _():
        row16 = lax.broadcasted_iota(jnp.int32, (16, tk), 0)
        zeros48 = jnp.zeros((48, tk), BF16)

        def build(j, carry):
            off = pl.multiple_of(j * tk, tk)
            kt = kt_ref[j]
            for hh in range(2):
                ck = cumt_ref[pl.ds(FOX_HEADS + 2 * p + hh, 1), pl.ds(off, tk)] * LOG2E
                hi, mid, lo = _split3(ck)
                bias = jnp.where(row16 < 3, 1.0,
                                 jnp.where(row16 == 3, -hi,
                                           jnp.where(row16 == 4, -mid,
                                                     jnp.where(row16 == 5, -lo, 0.0))))
                bias = bias.astype(BF16)
                if hh == 0:
                    kaug_ref[0, j, 0:64, :] = kt[0:64]
                    kaug_ref[0, j, 64:80, :] = bias
                    kaug_ref[0, j, 80:128, :] = zeros48
                else:
                    kaug_ref[1, j, 0:16, :] = bias
                    kaug_ref[1, j, 16:64, :] = zeros48
                    kaug_ref[1, j, 64:128, :] = kt[64:128]
            return carry

        lax.fori_loop(0, nb, build, 0)

    q = q_ref[...]
    lane = lax.broadcasted_iota(jnp.int32, (tq, LANES), 1)
    lo_half = lane < HEAD_DIM
    r_i = lax.broadcasted_iota(jnp.int32, (tq, tk), 0)
    c_i = lax.broadcasted_iota(jnp.int32, (tq, tk), 1)
    causal = r_i >= c_i

    aug = qaug_ref[...]
    qas = [jnp.where(lo_half, q, aug), jnp.where(lo_half, aug, q)]

    def logits(hh, j):
        return jnp.dot(qas[hh], kaug_ref[hh, j], preferred_element_type=F32)

    def vblock(j):
        return v_ref[pl.ds(pl.multiple_of(j * tk, tk), tk), :]

    def lane_fold(x):
        out = x[:, 0:LANES]
        for c in range(1, x.shape[1] // LANES):
            out = out + x[:, c * LANES:(c + 1) * LANES]
        return out

    def online_diag(hh):
        s = jnp.where(causal, logits(hh, i), NEG_INF)
        m0 = jnp.max(s, axis=1, keepdims=True)
        p0 = jnp.exp2(s - m0)
        m_ref[hh] = jnp.broadcast_to(m0, (tq, LANES))
        l_ref[hh] = lane_fold(p0)
        acc_ref[hh] = jnp.dot(p0.astype(BF16), vblock(i), preferred_element_type=F32)

    def online_step(hh, j):
        s = logits(hh, j)
        m_prev = m_ref[hh]
        m_new = jnp.maximum(m_prev, jnp.max(s, axis=1, keepdims=True))
        alpha = jnp.exp2(m_prev - m_new)
        pj = jnp.exp2(s - m_new[:, 0:1])
        l_ref[hh] = alpha * l_ref[hh] + lane_fold(pj)
        acc_ref[hh] = alpha * acc_ref[hh] + jnp.dot(
            pj.astype(BF16), vblock(j), preferred_element_type=F32)
        m_ref[hh] = m_new

    half = tq // 2

    def fixed_diag(heads):
        vb = vblock(i)
        logit_pairs = [
            (jnp.dot(qas[hh][0:half], kaug_ref[hh, i, :, 0:half],
                     preferred_element_type=F32),
             jnp.dot(qas[hh][half:], kaug_ref[hh, i], preferred_element_type=F32))
            for hh in heads]
        for hh, (s_top, s_bot) in zip(heads, logit_pairs):
            p_top = jnp.exp2(jnp.where(causal[0:half, 0:half], s_top, NEG_INF))
            p_bot = jnp.exp2(jnp.where(causal[half:, :], s_bot, NEG_INF))
            l_ref[hh, 0:half] = lane_fold(p_top)
            l_ref[hh, half:] = lane_fold(p_bot)
            acc_ref[hh, 0:half] = jnp.dot(p_top.astype(BF16), vb[0:half],
                                          preferred_element_type=F32)
            acc_ref[hh, half:] = jnp.dot(p_bot.astype(BF16), vb, preferred_element_type=F32)

    def fixed_steps(jobs):
        all_logits = [jnp.concatenate([logits(hh, j - (n - 1) + b) for b in range(n)], axis=1)
                      for hh, j, n in jobs]
        for (hh, j, n), s in zip(jobs, all_logits):
            pj = jnp.exp2(s)
            l_ref[hh] += lane_fold(pj)
            vn = v_ref[pl.ds(pl.multiple_of((j - (n - 1)) * tk, tk), n * tk), :]
            acc_ref[hh] += jnp.dot(pj.astype(BF16), vn, preferred_element_type=F32)

    codes = [nkv_ref[(2 * p + hh) * nq + i] for hh in range(2)]
    n_offs = [code >> 1 for code in codes]
    both_fixed = (codes[0] & codes[1] & 1) == 1
    joint_quads = jnp.where(both_fixed, jnp.minimum(n_offs[0] >> 2, n_offs[1] >> 2), 0)
    joint_pair = both_fixed & ((n_offs[0] & n_offs[1] & 2) == 2)
    joint_single = both_fixed & ((n_offs[0] & n_offs[1] & 1) == 1)
    pair_at = [i - 1 - 4 * (n >> 2) for n in n_offs]
    single_at = [i - n for n in n_offs]

    @pl.when(both_fixed)
    def _():
        fixed_diag((0, 1))

        def body(u, carry):
            fixed_steps([(0, i - 1 - 4 * u, 4), (1, i - 1 - 4 * u, 4)])
            return carry

        lax.fori_loop(0, joint_quads, body, 0)

        @pl.when(joint_pair)
        def _():
            fixed_steps([(0, pair_at[0], 2), (1, pair_at[1], 2)])

        @pl.when(joint_single)
        def _():
            fixed_steps([(0, single_at[0], 1), (1, single_at[1], 1)])

    for hh in range(2):
        n_off = n_offs[hh]
        fixed_ok = (codes[hh] & 1) == 1

        @pl.when(fixed_ok)
        def _(hh=hh, n_off=n_off):
            @pl.when(jnp.logical_not(both_fixed))
            def _():
                fixed_diag((hh,))

            def body(u, carry):
                fixed_steps([(hh, i - 1 - 4 * u, 4)])
                return carry

            lax.fori_loop(joint_quads, n_off >> 2, body, 0)

            @pl.when(((n_off & 2) == 2) & jnp.logical_not(joint_pair))
            def _():
                fixed_steps([(hh, pair_at[hh], 2)])

            @pl.when(((n_off & 1) == 1) & jnp.logical_not(joint_single))
            def _():
                fixed_steps([(hh, single_at[hh], 1)])

        @pl.when(jnp.logical_not(fixed_ok))
        def _(hh=hh, n_off=n_off):
            online_diag(hh)

            def body(jj, carry):
                online_step(hh, i - 1 - jj)
                return carry

            lax.fori_loop(0, n_off, body, 0)

    l0 = jnp.sum(l_ref[0], axis=1, keepdims=True)
    l1 = jnp.sum(l_ref[1], axis=1, keepdims=True)
    o_ref[...] = jnp.where(lo_half, acc_ref[0] / l0, acc_ref[1] / l1)


def _kv_counts(qn_tiles, kn_tiles, cmax_chunks, cmin_chunks):
    nq = qn_tiles.shape[0]
    per_q = cmax_chunks.shape[0] // nq
    qn = jnp.sqrt(qn_tiles[:, 0, :FOX_HEADS])
    kn = jnp.sqrt(jnp.max(kn_tiles[:, :FOX_HEADS, 0], axis=0))
    gate = slice(FOX_HEADS, 2 * FOX_HEADS)
    cmax = jnp.max(cmax_chunks[:, 0, gate].reshape(nq, per_q, FOX_HEADS), axis=1) * LOG2E
    cmin = jnp.min(cmin_chunks[:, 0, gate].reshape(nq, per_q, FOX_HEADS), axis=1) * LOG2E
    pmin = lax.cummin(cmin, axis=0)
    qk = NORM_MARGIN * qn * kn[None, :] + 1.0
    bound = (2.0 * qk + cmax)[:, None, :] - pmin[None, :, :]
    ii = lax.broadcasted_iota(jnp.int32, bound.shape, 0)
    jj = lax.broadcasted_iota(jnp.int32, bound.shape, 1)
    skip = (bound < SKIP_LOG2) & (jj < ii)
    jstar = jnp.max(jnp.where(skip, jj, -1), axis=1)
    n_off = jnp.arange(nq, dtype=jnp.int32)[:, None] - 1 - jstar
    pmin_prev = jnp.concatenate([jnp.full((1, FOX_HEADS), jnp.inf, F32), pmin[:-1]], axis=0)
    fixed_ok = (qk + jnp.maximum(cmax - pmin_prev, 0.0)) < FIXED_MAX_LOG2
    code = 2 * n_off + fixed_ok.astype(jnp.int32)
    return code.T.reshape(-1).astype(jnp.int32)


def _fox(nkv, q, kt3, v, qaug, cumt):
    s = q.shape[0]
    tq, tk = ATT_TQ, ATT_TK
    nb = s // tk
    grid_spec = pltpu.PrefetchScalarGridSpec(
        num_scalar_prefetch=1,
        grid=(FOX_HEADS // 2, s // tq),
        in_specs=[
            pl.BlockSpec((tq, LANES), lambda p, i, n: (i, p)),
            pl.BlockSpec((nb, LANES, tk), lambda p, i, n: (0, p, 0)),
            pl.BlockSpec((s, LANES), lambda p, i, n: (0, p)),
            pl.BlockSpec((tq, LANES), lambda p, i, n: (i, p)),
            pl.BlockSpec((16, s), lambda p, i, n: (0, 0)),
        ],
        out_specs=pl.BlockSpec((tq, LANES), lambda p, i, n: (i, p)),
        scratch_shapes=[
            pltpu.VMEM((2, nb, LANES, tk), BF16),
            pltpu.VMEM((2, tq, LANES), F32),
            pltpu.VMEM((2, tq, LANES), F32),
            pltpu.VMEM((2, tq, LANES), F32),
        ],
    )
    return pl.pallas_call(
        _fox_kernel,
        grid_spec=grid_spec,
        out_shape=jax.ShapeDtypeStruct((s, FOX_WIDTH), F32),
        compiler_params=pltpu.CompilerParams(
            dimension_semantics=("arbitrary", "arbitrary"),
            vmem_limit_bytes=VMEM_LIMIT),
        name="fox_attn",
    )(nkv, q, kt3, v, qaug, cumt)


def _pad_lanes(vec_dt, vec_f):
    out = jnp.zeros((1, LANES), F32)
    out = out.at[0, 0:SSM_HEADS].set(vec_dt.astype(F32))
    if vec_f is not None:
        out = out.at[0, SSM_HEADS:SSM_HEADS + FOX_HEADS].set(vec_f.astype(F32))
    return out


def _layer(x, ffn1_norm, ffn1_w_in, ffn1_w_out, mix_norm, w_in, conv_w, conv_b, dt_bias,
           a_log, d_skip, ssm_norm, f_bias, fox_norm, w_out, ffn2_norm, ffn2_w_in,
           ffn2_w_out, final_g, *, final):
    ones = jnp.ones((1, D_MODEL), F32)
    row = lambda a: a.reshape(1, -1).astype(F32)

    o0 = SSM_WIDTH
    o1 = o0 + SSM_XBC
    o2 = o1 + SSM_HEADS
    o3 = o2 + 3 * FOX_WIDTH
    proj_cols = ((0, o0), (o0, o1), (o2, o2 + FOX_WIDTH), (o2 + FOX_WIDTH, o2 + 2 * FOX_WIDTH),
                 (o2 + 2 * FOX_WIDTH, o3))
    x1, ffn2_wi, ffn2_wo, wz, wxbc, wq, wkt, wv, wo = _ffn(
        x, row(ffn1_norm), ffn1_w_in.astype(BF16), ffn1_w_out.astype(BF16), ones,
        final=False,
        casts=((ffn2_w_in, ((0, 2 * D_FF),)), (ffn2_w_out, ((0, D_MODEL),)),
               (w_in, proj_cols), (w_out, ((0, D_MODEL),))))
    ws = jnp.concatenate(
        [w_in[:, o1:o2], w_in[:, o3:],
         jnp.zeros((D_MODEL, LANES - SSM_HEADS - FOX_HEADS), w_in.dtype)], axis=1).astype(BF16)
    zs, xs, bc, q, kt3, v, small, qn2, kn2 = _inproj(
        x1, row(mix_norm), wz, wxbc, wq, wkt, wv, ws, conv_w.astype(F32), row(conv_b))

    bias_pad = _pad_lanes(dt_bias, f_bias)
    a_pad = _pad_lanes(-jnp.exp(a_log.astype(F32)), None)
    dskip_e = jnp.repeat(d_skip.astype(F32), HEAD_DIM).reshape(1, SSM_WIDTH)
    tri = jnp.asarray(np.tril(np.ones((SSD_T, SSD_T), np.float32)), dtype=BF16)
    expand_np = np.zeros((LANES, SSM_WIDTH), np.float32)
    for hd in range(SSM_HEADS):
        expand_np[hd, hd * HEAD_DIM:(hd + 1) * HEAD_DIM] = 1.0
    y_ssd, qaug, cumt, cmax_c, cmin_c = _ssd(zs, xs, bc, small, bias_pad, a_pad,
                            dskip_e, row(ssm_norm), tri, jnp.asarray(expand_np, dtype=BF16))

    o_fox = _fox(_kv_counts(qn2, kn2, cmax_c, cmin_c), q, kt3, v, qaug, cumt)

    mix = (y_ssd, o_fox, row(fox_norm), wo[:SSM_WIDTH], wo[SSM_WIDTH:])
    out, = _ffn(x1, row(ffn2_norm), ffn2_wi, ffn2_wo, row(final_g), final=final, mix=mix)
    return out


def kernel(x, ffn1_norm, ffn1_w_in, ffn1_w_out, mix_norm, w_in, conv_w, conv_b, dt_bias, a_log,
           d_skip, ssm_norm, f_bias, fox_norm, w_out, ffn2_norm, ffn2_w_in, ffn2_w_out, final_norm):
    b, s, d = x.shape
    depth = ffn1_norm.shape[0]
    outs = []
    for bi in range(b):
        xb = x[bi]
        for l in range(depth):
            xb = _layer(xb, ffn1_norm[l], ffn1_w_in[l], ffn1_w_out[l], mix_norm[l], w_in[l],
                        conv_w[l], conv_b[l], dt_bias[l], a_log[l], d_skip[l], ssm_norm[l],
                        f_bias[l], fox_norm[l], w_out[l], ffn2_norm[l], ffn2_w_in[l],
                        ffn2_w_out[l], final_norm, final=(l == depth - 1))
        outs.append(xb)
    return jnp.stack(outs, axis=0)
```

```python
import functools

import jax
import jax.numpy as jnp
import numpy as np
from jax import lax
from jax.experimental import pallas as pl
from jax.experimental.pallas import tpu as pltpu

F32 = jnp.float32
BF16 = jnp.bfloat16

D_MODEL = 1024
HEAD_DIM = 64
SSM_WIDTH = 512
SSM_HEADS = 8
SSM_GROUPS = 2
SSM_STATE = 128
CONV_WIDTH = 4
SSM_XBC = SSM_WIDTH + 2 * SSM_GROUPS * SSM_STATE
FOX_WIDTH = 512
FOX_HEADS = 8
D_FF = 2816
EPS = 1e-6

LANES = 128
VMEM_LIMIT = 56 * 1024 * 1024

FFN_TM = 1024
FFN_TF = 256
PROJ_TM = 512
SSD_T = 256
SSD_STEP = 1024
ATT_TQ = 512
ATT_TK = 512
NEG_INF = float("-inf")
LOG2E = 1.4426950408889634
SKIP_LOG2 = -150.0
NORM_MARGIN = 1.05
FIXED_MAX_LOG2 = 64.0


def _rms(x, g):
    ms = jnp.mean(x * x, axis=-1, keepdims=True)
    return x * lax.rsqrt(ms + EPS) * g


def _silu(x):
    return x * (1.0 / (1.0 + jnp.exp(-x)))


def _softplus(x):
    return jnp.maximum(x, 0.0) + jnp.log1p(jnp.exp(-jnp.abs(x)))


def _split3(x):
    hi = x.astype(BF16).astype(F32)
    r = x - hi
    mid = r.astype(BF16).astype(F32)
    lo = (r - mid).astype(BF16).astype(F32)
    return hi, mid, lo


def _dot_f32_rhs(a_bf16, x):
    hi, mid, lo = _split3(x)
    d = lambda piece: jnp.dot(a_bf16, piece.astype(BF16), preferred_element_type=F32)
    return (d(lo) + d(mid)) + d(hi)


def _dot_f32_lhs(x, b_bf16):
    hi, mid, lo = _split3(x)
    d = lambda piece: jnp.dot(piece.astype(BF16), b_bf16, preferred_element_type=F32)
    return (d(lo) + d(mid)) + d(hi)


def _ffn_kernel(*refs, mix, final, cast_cols):
    if cast_cols:
        n_src = len(cast_cols)
        n_dst = sum(len(ranges) for ranges in cast_cols)
        n_in = len(refs) - 1 - n_src - n_dst
        dst_refs = iter(refs[n_in + n_src + 1:])
        for src_ref, ranges in zip(refs[n_in:n_in + n_src], cast_cols):
            for c0, c1 in ranges:
                next(dst_refs)[...] = src_ref[:, c0:c1].astype(BF16)
        refs = refs[:n_in] + (refs[n_in + n_src],)
    if mix:
        (x_ref, ys_ref, of_ref, mg_ref, wm1_ref, wm2_ref,
         g_ref, wg_ref, wu_ref, wo_ref, fg_ref, o_ref) = refs
        yf = _rms(of_ref[...], mg_ref[...]).astype(BF16)
        x = (x_ref[...]
             + jnp.dot(ys_ref[...], wm1_ref[...], preferred_element_type=F32)
             + jnp.dot(yf, wm2_ref[...], preferred_element_type=F32))
    else:
        x_ref, g_ref, wg_ref, wu_ref, wo_ref, fg_ref, o_ref = refs
        x = x_ref[...]
    h = _rms(x, g_ref[...]).astype(BF16)
    o_ref[...] = 2.0 * x
    nf = D_FF // FFN_TF

    def gate_up(f):
        cols = slice(f * FFN_TF, (f + 1) * FFN_TF)
        return (jnp.dot(h, wg_ref[:, cols], preferred_element_type=F32),
                jnp.dot(h, wu_ref[:, cols], preferred_element_type=F32))

    pending = gate_up(0)
    for f in range(nf):
        gate, up = pending
        if f + 1 < nf:
            pending = gate_up(f + 1)
        act = (_silu(gate) * up).astype(BF16)
        o_ref[...] += jnp.dot(act, wo_ref[f * FFN_TF:(f + 1) * FFN_TF, :],
                              preferred_element_type=F32)
    r = 0.5 * o_ref[...]
    if final:
        r = _rms(r, fg_ref[...])
    o_ref[...] = r


def _ffn(x, g, w_in, w_out, fg, *, final, mix=None, casts=()):
    s = x.shape[0]
    tm = FFN_TM
    steps = s // tm
    row = lambda i: (i, 0)
    const = lambda i: (0, 0)
    resident = functools.partial(pl.BlockSpec, pipeline_mode=pl.Buffered(1))
    x_spec = pl.BlockSpec((tm, D_MODEL), row)
    vec_spec = pl.BlockSpec((1, D_MODEL), const)
    ffn_specs = [
        vec_spec,
        resident((D_MODEL, D_FF), lambda i: (0, 0)),
        resident((D_MODEL, D_FF), lambda i: (0, 1)),
        resident((D_FF, D_MODEL), const),
        vec_spec,
    ]
    ffn_args = (g, w_in, w_in, w_out, fg)
    if mix is None:
        in_specs = [x_spec] + ffn_specs
        args = (x,) + ffn_args
    else:
        ys, of, mg, wm1, wm2 = mix
        in_specs = [
            x_spec,
            pl.BlockSpec((tm, SSM_WIDTH), row),
            pl.BlockSpec((tm, FOX_WIDTH), row),
            pl.BlockSpec((1, FOX_WIDTH), const),
            resident((SSM_WIDTH, D_MODEL), const),
            resident((FOX_WIDTH, D_MODEL), const),
        ] + ffn_specs
        args = (x, ys, of, mg, wm1, wm2) + ffn_args
    out_specs = [pl.BlockSpec((tm, D_MODEL), row)]
    out_shape = [jax.ShapeDtypeStruct((s, D_MODEL), F32)]
    for w, ranges in casts:
        slab_rows = w.shape[0] // steps
        assert slab_rows * steps == w.shape[0] and slab_rows % 16 == 0, w.shape
        in_specs.append(pl.BlockSpec((slab_rows, w.shape[1]), row))
        for c0, c1 in ranges:
            out_specs.append(pl.BlockSpec((slab_rows, c1 - c0), row))
            out_shape.append(jax.ShapeDtypeStruct((w.shape[0], c1 - c0), BF16))
    return pl.pallas_call(
        functools.partial(_ffn_kernel, mix=mix is not None, final=final,
                          cast_cols=tuple(tuple(ranges) for _, ranges in casts)),
        grid=(steps,),
        in_specs=in_specs,
        out_specs=out_specs,
        out_shape=out_shape,
        compiler_params=pltpu.CompilerParams(
            dimension_semantics=("parallel",),
            vmem_limit_bytes=VMEM_LIMIT),
        name="ffn_final" if final else "ffn",
    )(*args, *(w for w, _ in casts))


def _inproj_kernel(x_ref, g_ref, wz_ref, wxbc_ref, wq_ref, wkt_ref, wv_ref, ws_ref,
                   ind_ref, indt_ref, cw_ref, cb_ref,
                   zs_ref, xs_ref, bc_ref, q_ref, kt_ref, v_ref, small_ref, qn_ref, kn_ref,
                   halo_ref):
    tm = PROJ_TM
    i = pl.program_id(0)

    @pl.when(i == 0)
    def _():
        halo_ref[...] = jnp.zeros_like(halo_ref)

    h = _rms(x_ref[...], g_ref[...]).astype(BF16)
    xb = jnp.dot(h, wxbc_ref[...], preferred_element_type=F32)
    zs_ref[...] = _silu(jnp.dot(h, wz_ref[...], preferred_element_type=F32))
    q = jnp.dot(h, wq_ref[...], preferred_element_type=F32)
    qb = (q * (HEAD_DIM ** -0.5 * LOG2E)).astype(BF16)
    q_ref[...] = qb
    kt = lax.dot_general(wkt_ref[...], h, (((0,), (1,)), ((), ())),
                         preferred_element_type=F32)
    ktb = kt.astype(BF16)
    kt_ref[0] = ktb
    v_ref[...] = jnp.dot(h, wv_ref[...], preferred_element_type=F32).astype(BF16)
    small_ref[...] = jnp.dot(h, ws_ref[...].astype(BF16), preferred_element_type=F32)
    qf = qb.astype(F32)
    qn = jnp.dot((qf * qf).astype(BF16), ind_ref[...], preferred_element_type=F32)
    qn_ref[0] = jnp.broadcast_to(jnp.max(qn, axis=0, keepdims=True), (8, LANES))
    kf = ktb.astype(F32)
    kn = jnp.dot(indt_ref[...], (kf * kf).astype(BF16), preferred_element_type=F32)
    kn_ref[0] = jnp.broadcast_to(jnp.max(kn, axis=1, keepdims=True), (16, LANES))
    prev = halo_ref[...]
    row8 = lax.broadcasted_iota(jnp.int32, (8, SSM_XBC), 0)
    conv = cb_ref[...] + xb * cw_ref[CONV_WIDTH - 1:CONV_WIDTH, :]
    for s in range(1, CONV_WIDTH):
        rolled = pltpu.roll(xb, s, axis=0)
        head = jnp.where(row8 < s, pltpu.roll(prev, s, axis=0), rolled[0:8])
        shifted = jnp.concatenate([head, rolled[8:]], axis=0)
        k = CONV_WIDTH - 1 - s
        conv = conv + shifted * cw_ref[k:k + 1, :]
    halo_ref[...] = xb[tm - 8:tm]
    u = _silu(conv)
    xs_ref[...] = u[:, :SSM_WIDTH]
    bc_ref[...] = u[:, SSM_WIDTH:].astype(BF16)


def _inproj(x, g, wz, wxbc, wq, wkt, wv, ws, cw, cb):
    s = x.shape[0]
    tm = PROJ_TM
    nb = s // tm
    const = lambda i: (0, 0)
    row = lambda i: (i, 0)
    ind_np = np.zeros((FOX_WIDTH, LANES), np.float32)
    for hd in range(FOX_HEADS):
        ind_np[hd * HEAD_DIM:(hd + 1) * HEAD_DIM, hd] = 1.0
    ind = jnp.asarray(ind_np, dtype=BF16)
    indt = jnp.asarray(ind_np.T[:16], dtype=BF16)
    return pl.pallas_call(
        _inproj_kernel,
        grid=(nb,),
        in_specs=[
            pl.BlockSpec((tm, D_MODEL), row),
            pl.BlockSpec((1, D_MODEL), const),
            pl.BlockSpec((D_MODEL, SSM_WIDTH), const),
            pl.BlockSpec((D_MODEL, SSM_XBC), const),
            pl.BlockSpec((D_MODEL, FOX_WIDTH), const),
            pl.BlockSpec((D_MODEL, FOX_WIDTH), const),
            pl.BlockSpec((D_MODEL, FOX_WIDTH), const),
            pl.BlockSpec((D_MODEL, LANES), const),
            pl.BlockSpec((FOX_WIDTH, LANES), const),
            pl.BlockSpec((16, FOX_WIDTH), const),
            pl.BlockSpec((CONV_WIDTH, SSM_XBC), const),
            pl.BlockSpec((1, SSM_XBC), const),
        ],
        out_specs=[
            pl.BlockSpec((tm, SSM_WIDTH), row),
            pl.BlockSpec((tm, SSM_WIDTH), row),
            pl.BlockSpec((tm, SSM_XBC - SSM_WIDTH), row),
            pl.BlockSpec((tm, FOX_WIDTH), row),
            pl.BlockSpec((1, FOX_WIDTH, tm), lambda i: (i, 0, 0)),
            pl.BlockSpec((tm, FOX_WIDTH), row),
            pl.BlockSpec((tm, LANES), row),
            pl.BlockSpec((1, 8, LANES), lambda i: (i, 0, 0)),
            pl.BlockSpec((1, 16, LANES), lambda i: (i, 0, 0)),
        ],
        out_shape=[
            jax.ShapeDtypeStruct((s, SSM_WIDTH), F32),
            jax.ShapeDtypeStruct((s, SSM_WIDTH), F32),
            jax.ShapeDtypeStruct((s, SSM_XBC - SSM_WIDTH), BF16),
            jax.ShapeDtypeStruct((s, FOX_WIDTH), BF16),
            jax.ShapeDtypeStruct((nb, FOX_WIDTH, tm), BF16),
            jax.ShapeDtypeStruct((s, FOX_WIDTH), BF16),
            jax.ShapeDtypeStruct((s, LANES), F32),
            jax.ShapeDtypeStruct((nb, 8, LANES), F32),
            jax.ShapeDtypeStruct((nb, 16, LANES), F32),
        ],
        scratch_shapes=[pltpu.VMEM((8, SSM_XBC), F32)],
        compiler_params=pltpu.CompilerParams(
            dimension_semantics=("arbitrary",),
            vmem_limit_bytes=VMEM_LIMIT),
        name="in_proj",
    )(x, g, wz, wxbc, wq, wkt, wv, ws, ind, indt, cw, cb)


def _ssd_kernel(zs_ref, xs_ref, bc_ref, small_ref, bias_ref, apad_ref,
                dskip_ref, ng_ref, tri_ref, expand_ref, place_ref,
                y_ref, qaug_ref, cumt_ref, cmax_ref, cmin_ref,
                state_ref, carry_ref):
    t_rows = SSD_T
    i = pl.program_id(0)

    @pl.when(i == 0)
    def _():
        state_ref[...] = jnp.zeros_like(state_ref)
        carry_ref[...] = jnp.zeros_like(carry_ref)

    chunks = range(SSD_STEP // t_rows)
    rows = [slice(c * t_rows, (c + 1) * t_rows) for c in chunks]
    gw = SSM_WIDTH // SSM_GROUPS
    hpg = SSM_HEADS // SSM_GROUPS
    groups = range(SSM_GROUPS)
    lane = lax.broadcasted_iota(jnp.int32, (t_rows, LANES), 1)
    is_dt = lane < SSM_HEADS
    lo = lane < HEAD_DIM

    sp, v = [], []
    for c in chunks:
        t = small_ref[rows[c], :] + bias_ref[...]
        sp.append(_softplus(jnp.where(is_dt, t, -t)))
        v.append(jnp.where(is_dt, sp[c] * apad_ref[...], -sp[c]))
    local = [_dot_f32_rhs(tri_ref[...], v[c]) for c in chunks]
    carry = carry_ref[...]
    cs, cst = [], []
    for c in chunks:
        cs.append(local[c] + carry)
        carry = jnp.where(lane[0:1, :] < SSM_HEADS, 0.0, cs[c][t_rows - 1:t_rows, :])
        cst.append(cs[c].T)
        cumt_ref[:, rows[c]] = cst[c][0:16, :]
        cmax_ref[c] = jnp.broadcast_to(jnp.max(cs[c], axis=0, keepdims=True), (8, LANES))
        cmin_ref[c] = jnp.broadcast_to(jnp.min(cs[c], axis=0, keepdims=True), (8, LANES))
    carry_ref[...] = carry

    placed = [[jnp.dot(piece.astype(BF16), place_ref[...], preferred_element_type=F32)
               for piece in _split3(cs[c] * LOG2E)] for c in chunks]
    expanded = [_dot_f32_lhs(jnp.concatenate([sp[c], cs[c]], axis=0), expand_ref[...])
                for c in chunks]
    lane4 = lax.broadcasted_iota(jnp.int32, (t_rows, FOX_WIDTH), 1) & (HEAD_DIM - 1)
    xs, ea_e, cd_e, xdt_b, wst = [], [], [], [], []
    for c in chunks:
        qaug_ref[rows[c], :] = jnp.where(
            lane4 == 0, placed[c][0],
            jnp.where(lane4 == 1, placed[c][1],
                      jnp.where(lane4 == 2, placed[c][2],
                                jnp.where(lane4 < 6, 1.0, 0.0)))).astype(BF16)
        dt_e = expanded[c][0:t_rows]
        cs_e = expanded[c][t_rows:2 * t_rows]
        ea_e.append(jnp.exp(cs_e))
        de_e = jnp.exp(cs_e[t_rows - 1:t_rows, :] - cs_e)
        cd_e.append(ea_e[c][t_rows - 1:t_rows, :])
        xs.append(xs_ref[rows[c], :])
        xdt = xs[c] * dt_e
        xdt_b.append(xdt.astype(BF16))
        wst.append((xdt * de_e).astype(BF16))

    def cgrp(c, g):
        return bc_ref[rows[c], SSM_GROUPS * SSM_STATE + g * SSM_STATE:
                      SSM_GROUPS * SSM_STATE + (g + 1) * SSM_STATE]

    def bgrp(c, g):
        return bc_ref[rows[c], g * SSM_STATE:(g + 1) * SSM_STATE]

    gmat = [[lax.dot_general(cgrp(c, g), bgrp(c, g), (((1,), (1,)), ((), ())),
                             preferred_element_type=F32) for g in groups]
            for c in chunks]
    r_i = lax.broadcasted_iota(jnp.int32, (t_rows, t_rows), 0)
    c_i = lax.broadcasted_iota(jnp.int32, (t_rows, t_rows), 1)
    causal = r_i >= c_i
    y_parts = []
    for c in chunks:
        parts = []
        for g in groups:
            for pr in range(hpg // 2):
                c0 = g * gw + pr * LANES
                xpair = xdt_b[c][:, c0:c0 + LANES]
                acc = None
                for hh in range(2):
                    h = g * hpg + pr * 2 + hh
                    seg = cs[c][:, h:h + 1] - cst[c][h:h + 1, :]
                    dec = jnp.exp(jnp.where(causal, seg, NEG_INF))
                    m = (gmat[c][g] * dec).astype(BF16)
                    xm = jnp.where(lo if hh == 0 else jnp.logical_not(lo), xpair,
                                   jnp.zeros_like(xpair))
                    part = jnp.dot(m, xm, preferred_element_type=F32)
                    acc = part if acc is None else acc + part
                parts.append(acc)
        y_parts.append(parts)

    upd = [[jnp.dot(bgrp(c, g).astype(F32).T.astype(BF16), wst[c][:, g * gw:(g + 1) * gw],
                    preferred_element_type=F32) for g in groups]
           for c in chunks]

    state = [state_ref[g] for g in groups]
    for c in chunks:
        for g in groups:
            y_off = jnp.dot(cgrp(c, g), state[g].astype(BF16), preferred_element_type=F32)
            for pr in range(hpg // 2):
                cols = slice(g * gw + pr * LANES, g * gw + (pr + 1) * LANES)
                y_parts[c][g * (hpg // 2) + pr] += (
                    y_off[:, pr * LANES:(pr + 1) * LANES] * ea_e[c][:, cols])
            state[g] = state[g] * cd_e[c][:, g * gw:(g + 1) * gw] + upd[c][g]
    for g in groups:
        state_ref[g] = state[g]

    for c in chunks:
        y = jnp.concatenate(y_parts[c], axis=1) + dskip_ref[...] * xs[c]
        y = y * zs_ref[rows[c], :]
        y_ref[rows[c], :] = _rms(y, ng_ref[...]).astype(BF16)


def _ssd(zs, xs, bc, small, bias_pad, a_pad, dskip_e, ng, tri, expand):
    s = zs.shape[0]
    t = SSD_T
    st = SSD_STEP
    nsub = st // t
    const = lambda i: (0, 0)
    row = lambda i: (i, 0)
    place_np = np.zeros((LANES, FOX_WIDTH), np.float32)
    for hd in range(FOX_HEADS):
        pair, odd = divmod(hd, 2)
        c0 = pair * LANES + (0 if odd else HEAD_DIM)
        place_np[FOX_HEADS + hd, c0:c0 + HEAD_DIM] = 1.0
    place = jnp.asarray(place_np, dtype=BF16)
    return pl.pallas_call(
        _ssd_kernel,
        grid=(s // st,),
        in_specs=[
            pl.BlockSpec((st, SSM_WIDTH), row),
            pl.BlockSpec((st, SSM_WIDTH), row),
            pl.BlockSpec((st, SSM_XBC - SSM_WIDTH), row),
            pl.BlockSpec((st, LANES), row),
            pl.BlockSpec((1, LANES), const),
            pl.BlockSpec((1, LANES), const),
            pl.BlockSpec((1, SSM_WIDTH), const),
            pl.BlockSpec((1, SSM_WIDTH), const),
            pl.BlockSpec((t, t), const),
            pl.BlockSpec((LANES, SSM_WIDTH), const),
            pl.BlockSpec((LANES, FOX_WIDTH), const),
        ],
        out_specs=[
            pl.BlockSpec((st, SSM_WIDTH), row),
            pl.BlockSpec((st, FOX_WIDTH), row),
            pl.BlockSpec((16, st), lambda i: (0, i)),
            pl.BlockSpec((nsub, 8, LANES), lambda i: (i, 0, 0)),
            pl.BlockSpec((nsub, 8, LANES), lambda i: (i, 0, 0)),
        ],
        out_shape=[
            jax.ShapeDtypeStruct((s, SSM_WIDTH), BF16),
            jax.ShapeDtypeStruct((s, FOX_WIDTH), BF16),
            jax.ShapeDtypeStruct((16, s), F32),
            jax.ShapeDtypeStruct((s // t, 8, LANES), F32),
            jax.ShapeDtypeStruct((s // t, 8, LANES), F32),
        ],
        scratch_shapes=[
            pltpu.VMEM((SSM_GROUPS, SSM_STATE, SSM_WIDTH // SSM_GROUPS), F32),
            pltpu.VMEM((1, LANES), F32),
        ],
        compiler_params=pltpu.CompilerParams(
            dimension_semantics=("arbitrary",),
            vmem_limit_bytes=VMEM_LIMIT),
        name="ssd",
    )(zs, xs, bc, small, bias_pad, a_pad, dskip_e, ng, tri, expand, place)


def _fox_kernel(nkv_ref, q_ref, kt_ref, v_ref, qaug_ref, cumt_ref, o_ref,
                kaug_ref, m_ref, l_ref, acc_ref):
    tq, tk = ATT_TQ, ATT_TK
    nb = kt_ref.shape[0]
    nq = pl.num_programs(1)
    p = pl.program_id(0)
    i = pl.program_id(1)

    @pl.when(i == 0)
    def _():
        row16 = lax.broadcasted_iota(jnp.int32, (16, tk), 0)
        zeros48 = jnp.zeros((48, tk), BF16)

        def build(j, carry):
            off = pl.multiple_of(j * tk, tk)
            kt = kt_ref[j]
            for hh in range(2):
                ck = cumt_ref[pl.ds(FOX_HEADS + 2 * p + hh, 1), pl.ds(off, tk)] * LOG2E
                hi, mid, lo = _split3(ck)
                bias = jnp.where(row16 < 3, 1.0,
                                 jnp.where(row16 == 3, -hi,
                                           jnp.where(row16 == 4, -mid,
                                                     jnp.where(row16 == 5, -lo, 0.0))))
                bias = bias.astype(BF16)
                if hh == 0:
                    kaug_ref[0, j, 0:64, :] = kt[0:64]
                    kaug_ref[0, j, 64:80, :] = bias
                    kaug_ref[0, j, 80:128, :] = zeros48
                else:
                    kaug_ref[1, j, 0:16, :] = bias
                    kaug_ref[1, j, 16:64, :] = zeros48
                    kaug_ref[1, j, 64:128, :] = kt[64:128]
            return carry

        lax.fori_loop(0, nb, build, 0)

    q = q_ref[...]
    lane = lax.broadcasted_iota(jnp.int32, (tq, LANES), 1)
    lo_half = lane < HEAD_DIM
    r_i = lax.broadcasted_iota(jnp.int32, (tq, tk), 0)
    c_i = lax.broadcasted_iota(jnp.int32, (tq, tk), 1)
    causal = r_i >= c_i

    aug = qaug_ref[...]
    qas = [jnp.where(lo_half, q, aug), jnp.where(lo_half, aug, q)]

    def logits(hh, j):
        return jnp.dot(qas[hh], kaug_ref[hh, j], preferred_element_type=F32)

    def vblock(j):
        return v_ref[pl.ds(pl.multiple_of(j * tk, tk), tk), :]

    def lane_fold(x):
        out = x[:, 0:LANES]
        for c in range(1, x.shape[1] // LANES):
            out = out + x[:, c * LANES:(c + 1) * LANES]
        return out

    def online_diag(hh):
        s = jnp.where(causal, logits(hh, i), NEG_INF)
        m0 = jnp.max(s, axis=1, keepdims=True)
        p0 = jnp.exp2(s - m0)
        m_ref[hh] = jnp.broadcast_to(m0, (tq, LANES))
        l_ref[hh] = lane_fold(p0)
        acc_ref[hh] = jnp.dot(p0.astype(BF16), vblock(i), preferred_element_type=F32)

    def online_step(hh, j):
        s = logits(hh, j)
        m_prev = m_ref[hh]
        m_new = jnp.maximum(m_prev, jnp.max(s, axis=1, keepdims=True))
        alpha = jnp.exp2(m_prev - m_new)
        pj = jnp.exp2(s - m_new[:, 0:1])
        l_ref[hh] = alpha * l_ref[hh] + lane_fold(pj)
        acc_ref[hh] = alpha * acc_ref[hh] + jnp.dot(
            pj.astype(BF16), vblock(j), preferred_element_type=F32)
        m_ref[hh] = m_new

    half = tq // 2

    def fixed_diag(heads):
        vb = vblock(i)
        logit_pairs = [
            (jnp.dot(qas[hh][0:half], kaug_ref[hh, i, :, 0:half],
                     preferred_element_type=F32),
             jnp.dot(qas[hh][half:], kaug_ref[hh, i], preferred_element_type=F32))
            for hh in heads]
        for hh, (s_top, s_bot) in zip(heads, logit_pairs):
            p_top = jnp.exp2(jnp.where(causal[0:half, 0:half], s_top, NEG_INF))
            p_bot = jnp.exp2(jnp.where(causal[half:, :], s_bot, NEG_INF))
            l_ref[hh, 0:half] = lane_fold(p_top)
            l_ref[hh, half:] = lane_fold(p_bot)
            acc_ref[hh, 0:half] = jnp.dot(p_top.astype(BF16), vb[0:half],
                                          preferred_element_type=F32)
            acc_ref[hh, half:] = jnp.dot(p_bot.astype(BF16), vb, preferred_element_type=F32)

    def fixed_steps(jobs):
        all_logits = [jnp.concatenate([logits(hh, j - (n - 1) + b) for b in range(n)], axis=1)
                      for hh, j, n in jobs]
        for (hh, j, n), s in zip(jobs, all_logits):
            pj = jnp.exp2(s)
            l_ref[hh] += lane_fold(pj)
            vn = v_ref[pl.ds(pl.multiple_of((j - (n - 1)) * tk, tk), n * tk), :]
            acc_ref[hh] += jnp.dot(pj.astype(BF16), vn, preferred_element_type=F32)

    codes = [nkv_ref[(2 * p + hh) * nq + i] for hh in range(2)]
    n_offs = [code >> 1 for code in codes]
    both_fixed = (codes[0] & codes[1] & 1) == 1
    joint_quads = jnp.where(both_fixed, jnp.minimum(n_offs[0] >> 2, n_offs[1] >> 2), 0)
    joint_pair = both_fixed & ((n_offs[0] & n_offs[1] & 2) == 2)
    joint_single = both_fixed & ((n_offs[0] & n_offs[1] & 1) == 1)
    pair_at = [i - 1 - 4 * (n >> 2) for n in n_offs]
    single_at = [i - n for n in n_offs]

    @pl.when(both_fixed)
    def _():
        fixed_diag((0, 1))

        def body(u, carry):
            fixed_steps([(0, i - 1 - 4 * u, 4), (1, i - 1 - 4 * u, 4)])
            return carry

        lax.fori_loop(0, joint_quads, body, 0)

        @pl.when(joint_pair)
        def _():
            fixed_steps([(0, pair_at[0], 2), (1, pair_at[1], 2)])

        @pl.when(joint_single)
        def _():
            fixed_steps([(0, single_at[0], 1), (1, single_at[1], 1)])

    for hh in range(2):
        n_off = n_offs[hh]
        fixed_ok = (codes[hh] & 1) == 1

        @pl.when(fixed_ok)
        def _(hh=hh, n_off=n_off):
            @pl.when(jnp.logical_not(both_fixed))
            def _():
                fixed_diag((hh,))

            def body(u, carry):
                fixed_steps([(hh, i - 1 - 4 * u, 4)])
                return carry

            lax.fori_loop(joint_quads, n_off >> 2, body, 0)

            @pl.when(((n_off & 2) == 2) & jnp.logical_not(joint_pair))
            def _():
                fixed_steps([(hh, pair_at[hh], 2)])

            @pl.when(((n_off & 1) == 1) & jnp.logical_not(joint_single))
            def _():
                fixed_steps([(hh, single_at[hh], 1)])

        @pl.when(jnp.logical_not(fixed_ok))
        def _(hh=hh, n_off=n_off):
            online_diag(hh)

            def body(jj, carry):
                online_step(hh, i - 1 - jj)
                return carry

            lax.fori_loop(0, n_off, body, 0)

    l0 = jnp.sum(l_ref[0], axis=1, keepdims=True)
    l1 = jnp.sum(l_ref[1], axis=1, keepdims=True)
    o_ref[...] = jnp.where(lo_half, acc_ref[0] / l0, acc_ref[1] / l1)


def _kv_counts(qn_tiles, kn_tiles, cmax_chunks, cmin_chunks):
    nq = qn_tiles.shape[0]
    per_q = cmax_chunks.shape[0] // nq
    qn = jnp.sqrt(qn_tiles[:, 0, :FOX_HEADS])
    kn = jnp.sqrt(jnp.max(kn_tiles[:, :FOX_HEADS, 0], axis=0))
    gate = slice(FOX_HEADS, 2 * FOX_HEADS)
    cmax = jnp.max(cmax_chunks[:, 0, gate].reshape(nq, per_q, FOX_HEADS), axis=1) * LOG2E
    cmin = jnp.min(cmin_chunks[:, 0, gate].reshape(nq, per_q, FOX_HEADS), axis=1) * LOG2E
    pmin = lax.cummin(cmin, axis=0)
    qk = NORM_MARGIN * qn * kn[None, :] + 1.0
    bound = (2.0 * qk + cmax)[:, None, :] - pmin[None, :, :]
    ii = lax.broadcasted_iota(jnp.int32, bound.shape, 0)
    jj = lax.broadcasted_iota(jnp.int32, bound.shape, 1)
    skip = (bound < SKIP_LOG2) & (jj < ii)
    jstar = jnp.max(jnp.where(skip, jj, -1), axis=1)
    n_off = jnp.arange(nq, dtype=jnp.int32)[:, None] - 1 - jstar
    pmin_prev = jnp.concatenate([jnp.full((1, FOX_HEADS), jnp.inf, F32), pmin[:-1]], axis=0)
    fixed_ok = (qk + jnp.maximum(cmax - pmin_prev, 0.0)) < FIXED_MAX_LOG2
    code = 2 * n_off + fixed_ok.astype(jnp.int32)
    return code.T.reshape(-1).astype(jnp.int32)


def _fox(nkv, q, kt3, v, qaug, cumt):
    s = q.shape[0]
    tq, tk = ATT_TQ, ATT_TK
    nb = s // tk
    grid_spec = pltpu.PrefetchScalarGridSpec(
        num_scalar_prefetch=1,
        grid=(FOX_HEADS // 2, s // tq),
        in_specs=[
            pl.BlockSpec((tq, LANES), lambda p, i, n: (i, p)),
            pl.BlockSpec((nb, LANES, tk), lambda p, i, n: (0, p, 0)),
            pl.BlockSpec((s, LANES), lambda p, i, n: (0, p)),
            pl.BlockSpec((tq, LANES), lambda p, i, n: (i, p)),
            pl.BlockSpec((16, s), lambda p, i, n: (0, 0)),
        ],
        out_specs=pl.BlockSpec((tq, LANES), lambda p, i, n: (i, p)),
        scratch_shapes=[
            pltpu.VMEM((2, nb, LANES, tk), BF16),
            pltpu.VMEM((2, tq, LANES), F32),
            pltpu.VMEM((2, tq, LANES), F32),
            pltpu.VMEM((2, tq, LANES), F32),
        ],
    )
    return pl.pallas_call(
        _fox_kernel,
        grid_spec=grid_spec,
        out_shape=jax.ShapeDtypeStruct((s, FOX_WIDTH), F32),
        compiler_params=pltpu.CompilerParams(
            dimension_semantics=("arbitrary", "arbitrary"),
            vmem_limit_bytes=VMEM_LIMIT),
        name="fox_attn",
    )(nkv, q, kt3, v, qaug, cumt)


def _pad_lanes(vec_dt, vec_f):
    out = jnp.zeros((1, LANES), F32)
    out = out.at[0, 0:SSM_HEADS].set(vec_dt.astype(F32))
    if vec_f is not None:
        out = out.at[0, SSM_HEADS:SSM_HEADS + FOX_HEADS].set(vec_f.astype(F32))
    return out


def _layer(x, ffn1_norm, ffn1_w_in, ffn1_w_out, mix_norm, w_in, conv_w, conv_b, dt_bias,
           a_log, d_skip, ssm_norm, f_bias, fox_norm, w_out, ffn2_norm, ffn2_w_in,
           ffn2_w_out, final_g, *, final):
    ones = jnp.ones((1, D_MODEL), F32)
    row = lambda a: a.reshape(1, -1).astype(F32)

    o0 = SSM_WIDTH
    o1 = o0 + SSM_XBC
    o2 = o1 + SSM_HEADS
    o3 = o2 + 3 * FOX_WIDTH
    proj_cols = ((0, o0), (o0, o1), (o2, o2 + FOX_WIDTH), (o2 + FOX_WIDTH, o2 + 2 * FOX_WIDTH),
                 (o2 + 2 * FOX_WIDTH, o3))
    x1, ffn2_wi, ffn2_wo, wz, wxbc, wq, wkt, wv, wo = _ffn(
        x, row(ffn1_norm), ffn1_w_in.astype(BF16), ffn1_w_out.astype(BF16), ones,
        final=False,
        casts=((ffn2_w_in, ((0, 2 * D_FF),)), (ffn2_w_out, ((0, D_MODEL),)),
               (w_in, proj_cols), (w_out, ((0, D_MODEL),))))
    ws = jnp.concatenate(
        [w_in[:, o1:o2], w_in[:, o3:],
         jnp.zeros((D_MODEL, LANES - SSM_HEADS - FOX_HEADS), w_in.dtype)], axis=1)
    zs, xs, bc, q, kt3, v, small, qn2, kn2 = _inproj(
        x1, row(mix_norm), wz, wxbc, wq, wkt, wv, ws, conv_w.astype(F32), row(conv_b))

    bias_pad = _pad_lanes(dt_bias, f_bias)
    a_pad = _pad_lanes(-jnp.exp(a_log.astype(F32)), None)
    dskip_e = jnp.repeat(d_skip.astype(F32), HEAD_DIM).reshape(1, SSM_WIDTH)
    tri = jnp.asarray(np.tril(np.ones((SSD_T, SSD_T), np.float32)), dtype=BF16)
    expand_np = np.zeros((LANES, SSM_WIDTH), np.float32)
    for hd in range(SSM_HEADS):
        expand_np[hd, hd * HEAD_DIM:(hd + 1) * HEAD_DIM] = 1.0
    y_ssd, qaug, cumt, cmax_c, cmin_c = _ssd(zs, xs, bc, small, bias_pad, a_pad,
                            dskip_e, row(ssm_norm), tri, jnp.asarray(expand_np, dtype=BF16))

    o_fox = _fox(_kv_counts(qn2, kn2, cmax_c, cmin_c), q, kt3, v, qaug, cumt)

    mix = (y_ssd, o_fox, row(fox_norm), wo[:SSM_WIDTH], wo[SSM_WIDTH:])
    out, = _ffn(x1, row(ffn2_norm), ffn2_wi, ffn2_wo, row(final_g), final=final, mix=mix)
    return out


def kernel(x, ffn1_norm, ffn1_w_in, ffn1_w_out, mix_norm, w_in, conv_w, conv_b, dt_bias, a_log,
           d_skip, ssm_norm, f_bias, fox_norm, w_out, ffn2_norm, ffn2_w_in, ffn2_w_out, final_norm):
    b, s, d = x.shape
    depth = ffn1_norm.shape[0]
    outs = []
    for bi in range(b):
        xb = x[bi]
        for l in range(depth):
            xb = _layer(xb, ffn1_norm[l], ffn1_w_in[l], ffn1_w_out[l], mix_norm[l], w_in[l],
                        conv_w[l], conv_b[l], dt_bias[l], a_log[l], d_skip[l], ssm_norm[l],
                        f_bias[l], fox_norm[l], w_out[l], ffn2_norm[l], ffn2_w_in[l],
                        ffn2_w_out[l], final_norm, final=(l == depth - 1))
        outs.append(xb)
    return jnp.stack(outs, axis=0)
```

```python
import functools

import jax
import jax.numpy as jnp
import numpy as np
from jax import lax
from jax.experimental import pallas as pl
from jax.experimental.pallas import tpu as pltpu

F32 = jnp.float32
BF16 = jnp.bfloat16

D_MODEL = 1024
HEAD_DIM = 64
SSM_WIDTH = 512
SSM_HEADS = 8
SSM_GROUPS = 2
SSM_STATE = 128
CONV_WIDTH = 4
SSM_XBC = SSM_WIDTH + 2 * SSM_GROUPS * SSM_STATE
FOX_WIDTH = 512
FOX_HEADS = 8
D_FF = 2816
EPS = 1e-6

LANES = 128
VMEM_LIMIT = 56 * 1024 * 1024

FFN_TM = 1024
FFN_TF = 256
PROJ_TM = 512
SSD_T = 128
SSD_STEP = 1024
ATT_TQ = 512
ATT_TK = 512
NEG_INF = float("-inf")
LOG2E = 1.4426950408889634
SKIP_LOG2 = -150.0
NORM_MARGIN = 1.05
FIXED_MAX_LOG2 = 64.0


def _rms(x, g):
    ms = jnp.mean(x * x, axis=-1, keepdims=True)
    return x * lax.rsqrt(ms + EPS) * g


def _silu(x):
    return x * (1.0 / (1.0 + jnp.exp(-x)))


def _softplus(x):
    return jnp.maximum(x, 0.0) + jnp.log1p(jnp.exp(-jnp.abs(x)))


def _split3(x):
    hi = x.astype(BF16).astype(F32)
    r = x - hi
    mid = r.astype(BF16).astype(F32)
    lo = (r - mid).astype(BF16).astype(F32)
    return hi, mid, lo


def _dot_f32_rhs(a_bf16, x):
    hi, mid, lo = _split3(x)
    d = lambda piece: jnp.dot(a_bf16, piece.astype(BF16), preferred_element_type=F32)
    return (d(lo) + d(mid)) + d(hi)


def _dot_f32_lhs(x, b_bf16):
    hi, mid, lo = _split3(x)
    d = lambda piece: jnp.dot(piece.astype(BF16), b_bf16, preferred_element_type=F32)
    return (d(lo) + d(mid)) + d(hi)


def _ffn_kernel(*refs, mix, final, cast_cols):
    if cast_cols:
        n_src = len(cast_cols)
        n_dst = sum(len(ranges) for ranges in cast_cols)
        n_in = len(refs) - 1 - n_src - n_dst
        dst_refs = iter(refs[n_in + n_src + 1:])
        for src_ref, ranges in zip(refs[n_in:n_in + n_src], cast_cols):
            for c0, c1 in ranges:
                next(dst_refs)[...] = src_ref[:, c0:c1].astype(BF16)
        refs = refs[:n_in] + (refs[n_in + n_src],)
    if mix:
        (x_ref, ys_ref, of_ref, mg_ref, wm1_ref, wm2_ref,
         g_ref, wg_ref, wu_ref, wo_ref, fg_ref, o_ref) = refs
        yf = _rms(of_ref[...], mg_ref[...]).astype(BF16)
        x = (x_ref[...]
             + jnp.dot(ys_ref[...], wm1_ref[...], preferred_element_type=F32)
             + jnp.dot(yf, wm2_ref[...], preferred_element_type=F32))
    else:
        x_ref, g_ref, wg_ref, wu_ref, wo_ref, fg_ref, o_ref = refs
        x = x_ref[...]
    h = _rms(x, g_ref[...]).astype(BF16)
    o_ref[...] = 2.0 * x
    nf = D_FF // FFN_TF

    def gate_up(f):
        cols = slice(f * FFN_TF, (f + 1) * FFN_TF)
        return (jnp.dot(h, wg_ref[:, cols], preferred_element_type=F32),
                jnp.dot(h, wu_ref[:, cols], preferred_element_type=F32))

    pending = gate_up(0)
    for f in range(nf):
        gate, up = pending
        if f + 1 < nf:
            pending = gate_up(f + 1)
        act = (_silu(gate) * up).astype(BF16)
        o_ref[...] += jnp.dot(act, wo_ref[f * FFN_TF:(f + 1) * FFN_TF, :],
                              preferred_element_type=F32)
    r = 0.5 * o_ref[...]
    if final:
        r = _rms(r, fg_ref[...])
    o_ref[...] = r


def _ffn(x, g, w_in, w_out, fg, *, final, mix=None, casts=()):
    s = x.shape[0]
    tm = FFN_TM
    steps = s // tm
    row = lambda i: (i, 0)
    const = lambda i: (0, 0)
    resident = functools.partial(pl.BlockSpec, pipeline_mode=pl.Buffered(1))
    x_spec = pl.BlockSpec((tm, D_MODEL), row)
    vec_spec = pl.BlockSpec((1, D_MODEL), const)
    ffn_specs = [
        vec_spec,
        resident((D_MODEL, D_FF), lambda i: (0, 0)),
        resident((D_MODEL, D_FF), lambda i: (0, 1)),
        resident((D_FF, D_MODEL), const),
        vec_spec,
    ]
    ffn_args = (g, w_in, w_in, w_out, fg)
    if mix is None:
        in_specs = [x_spec] + ffn_specs
        args = (x,) + ffn_args
    else:
        ys, of, mg, wm1, wm2 = mix
        in_specs = [
            x_spec,
            pl.BlockSpec((tm, SSM_WIDTH), row),
            pl.BlockSpec((tm, FOX_WIDTH), row),
            pl.BlockSpec((1, FOX_WIDTH), const),
            resident((SSM_WIDTH, D_MODEL), const),
            resident((FOX_WIDTH, D_MODEL), const),
        ] + ffn_specs
        args = (x, ys, of, mg, wm1, wm2) + ffn_args
    out_specs = [pl.BlockSpec((tm, D_MODEL), row)]
    out_shape = [jax.ShapeDtypeStruct((s, D_MODEL), F32)]
    for w, ranges in casts:
        slab_rows = w.shape[0] // steps
        assert slab_rows * steps == w.shape[0] and slab_rows % 16 == 0, w.shape
        in_specs.append(pl.BlockSpec((slab_rows, w.shape[1]), row))
        for c0, c1 in ranges:
            out_specs.append(pl.BlockSpec((slab_rows, c1 - c0), row))
            out_shape.append(jax.ShapeDtypeStruct((w.shape[0], c1 - c0), BF16))
    return pl.pallas_call(
        functools.partial(_ffn_kernel, mix=mix is not None, final=final,
                          cast_cols=tuple(tuple(ranges) for _, ranges in casts)),
        grid=(steps,),
        in_specs=in_specs,
        out_specs=out_specs,
        out_shape=out_shape,
        compiler_params=pltpu.CompilerParams(
            dimension_semantics=("parallel",),
            vmem_limit_bytes=VMEM_LIMIT),
        name="ffn_final" if final else "ffn",
    )(*args, *(w for w, _ in casts))


def _inproj_kernel(x_ref, g_ref, wz_ref, wxbc_ref, wq_ref, wkt_ref, wv_ref, ws_ref,
                   ind_ref, indt_ref, cw_ref, cb_ref,
                   zs_ref, xs_ref, bc_ref, q_ref, kt_ref, v_ref, small_ref, qn_ref, kn_ref,
                   halo_ref):
    tm = PROJ_TM
    i = pl.program_id(0)

    @pl.when(i == 0)
    def _():
        halo_ref[...] = jnp.zeros_like(halo_ref)

    h = _rms(x_ref[...], g_ref[...]).astype(BF16)
    xb = jnp.dot(h, wxbc_ref[...], preferred_element_type=F32)
    zs_ref[...] = _silu(jnp.dot(h, wz_ref[...], preferred_element_type=F32))
    q = jnp.dot(h, wq_ref[...], preferred_element_type=F32)
    qb = (q * (HEAD_DIM ** -0.5 * LOG2E)).astype(BF16)
    q_ref[...] = qb
    kt = lax.dot_general(wkt_ref[...], h, (((0,), (1,)), ((), ())),
                         preferred_element_type=F32)
    ktb = kt.astype(BF16)
    kt_ref[0] = ktb
    v_ref[...] = jnp.dot(h, wv_ref[...], preferred_element_type=F32).astype(BF16)
    small_ref[...] = jnp.dot(h, ws_ref[...].astype(BF16), preferred_element_type=F32)
    qf = qb.astype(F32)
    qn = jnp.dot((qf * qf).astype(BF16), ind_ref[...], preferred_element_type=F32)
    qn_ref[0] = jnp.broadcast_to(jnp.max(qn, axis=0, keepdims=True), (8, LANES))
    kf = ktb.astype(F32)
    kn = jnp.dot(indt_ref[...], (kf * kf).astype(BF16), preferred_element_type=F32)
    kn_ref[0] = jnp.broadcast_to(jnp.max(kn, axis=1, keepdims=True), (16, LANES))
    prev = halo_ref[...]
    row8 = lax.broadcasted_iota(jnp.int32, (8, SSM_XBC), 0)
    conv = cb_ref[...] + xb * cw_ref[CONV_WIDTH - 1:CONV_WIDTH, :]
    for s in range(1, CONV_WIDTH):
        rolled = pltpu.roll(xb, s, axis=0)
        head = jnp.where(row8 < s, pltpu.roll(prev, s, axis=0), rolled[0:8])
        shifted = jnp.concatenate([head, rolled[8:]], axis=0)
        k = CONV_WIDTH - 1 - s
        conv = conv + shifted * cw_ref[k:k + 1, :]
    halo_ref[...] = xb[tm - 8:tm]
    u = _silu(conv)
    xs_ref[...] = u[:, :SSM_WIDTH]
    bc_ref[...] = u[:, SSM_WIDTH:].astype(BF16)


def _inproj(x, g, wz, wxbc, wq, wkt, wv, ws, cw, cb):
    s = x.shape[0]
    tm = PROJ_TM
    nb = s // tm
    const = lambda i: (0, 0)
    row = lambda i: (i, 0)
    ind_np = np.zeros((FOX_WIDTH, LANES), np.float32)
    for hd in range(FOX_HEADS):
        ind_np[hd * HEAD_DIM:(hd + 1) * HEAD_DIM, hd] = 1.0
    ind = jnp.asarray(ind_np, dtype=BF16)
    indt = jnp.asarray(ind_np.T[:16], dtype=BF16)
    return pl.pallas_call(
        _inproj_kernel,
        grid=(nb,),
        in_specs=[
            pl.BlockSpec((tm, D_MODEL), row),
            pl.BlockSpec((1, D_MODEL), const),
            pl.BlockSpec((D_MODEL, SSM_WIDTH), const),
            pl.BlockSpec((D_MODEL, SSM_XBC), const),
            pl.BlockSpec((D_MODEL, FOX_WIDTH), const),
            pl.BlockSpec((D_MODEL, FOX_WIDTH), const),
            pl.BlockSpec((D_MODEL, FOX_WIDTH), const),
            pl.BlockSpec((D_MODEL, LANES), const),
            pl.BlockSpec((FOX_WIDTH, LANES), const),
            pl.BlockSpec((16, FOX_WIDTH), const),
            pl.BlockSpec((CONV_WIDTH, SSM_XBC), const),
            pl.BlockSpec((1, SSM_XBC), const),
        ],
        out_specs=[
            pl.BlockSpec((tm, SSM_WIDTH), row),
            pl.BlockSpec((tm, SSM_WIDTH), row),
            pl.BlockSpec((tm, SSM_XBC - SSM_WIDTH), row),
            pl.BlockSpec((tm, FOX_WIDTH), row),
            pl.BlockSpec((1, FOX_WIDTH, tm), lambda i: (i, 0, 0)),
            pl.BlockSpec((tm, FOX_WIDTH), row),
            pl.BlockSpec((tm, LANES), row),
            pl.BlockSpec((1, 8, LANES), lambda i: (i, 0, 0)),
            pl.BlockSpec((1, 16, LANES), lambda i: (i, 0, 0)),
        ],
        out_shape=[
            jax.ShapeDtypeStruct((s, SSM_WIDTH), F32),
            jax.ShapeDtypeStruct((s, SSM_WIDTH), F32),
            jax.ShapeDtypeStruct((s, SSM_XBC - SSM_WIDTH), BF16),
            jax.ShapeDtypeStruct((s, FOX_WIDTH), BF16),
            jax.ShapeDtypeStruct((nb, FOX_WIDTH, tm), BF16),
            jax.ShapeDtypeStruct((s, FOX_WIDTH), BF16),
            jax.ShapeDtypeStruct((s, LANES), F32),
            jax.ShapeDtypeStruct((nb, 8, LANES), F32),
            jax.ShapeDtypeStruct((nb, 16, LANES), F32),
        ],
        scratch_shapes=[pltpu.VMEM((8, SSM_XBC), F32)],
        compiler_params=pltpu.CompilerParams(
            dimension_semantics=("arbitrary",),
            vmem_limit_bytes=VMEM_LIMIT),
        name="in_proj",
    )(x, g, wz, wxbc, wq, wkt, wv, ws, ind, indt, cw, cb)


def _ssd_kernel(zs_ref, xs_ref, bc_ref, small_ref, bias_ref, apad_ref,
                dskip_ref, ng_ref, tri_ref, expand_ref, place_ref,
                y_ref, qaug_ref, cumt_ref, cmax_ref, cmin_ref,
                state_ref, carry_ref):
    t_rows = SSD_T
    i = pl.program_id(0)

    @pl.when(i == 0)
    def _():
        state_ref[...] = jnp.zeros_like(state_ref)
        carry_ref[...] = jnp.zeros_like(carry_ref)

    chunks = range(SSD_STEP // t_rows)
    rows = [slice(c * t_rows, (c + 1) * t_rows) for c in chunks]
    gw = SSM_WIDTH // SSM_GROUPS
    hpg = SSM_HEADS // SSM_GROUPS
    groups = range(SSM_GROUPS)
    lane = lax.broadcasted_iota(jnp.int32, (t_rows, LANES), 1)
    is_dt = lane < SSM_HEADS
    lo = lane < HEAD_DIM

    sp, v = [], []
    for c in chunks:
        t = small_ref[rows[c], :] + bias_ref[...]
        sp.append(_softplus(jnp.where(is_dt, t, -t)))
        v.append(jnp.where(is_dt, sp[c] * apad_ref[...], -sp[c]))
    local = [_dot_f32_rhs(tri_ref[...], v[c]) for c in chunks]
    carry = carry_ref[...]
    cs, cst = [], []
    for c in chunks:
        cs.append(local[c] + carry)
        carry = jnp.where(lane[0:1, :] < SSM_HEADS, 0.0, cs[c][t_rows - 1:t_rows, :])
        cst.append(cs[c].T)
        cumt_ref[:, rows[c]] = cst[c][0:16, :]
        cmax_ref[c] = jnp.broadcast_to(jnp.max(cs[c], axis=0, keepdims=True), (8, LANES))
        cmin_ref[c] = jnp.broadcast_to(jnp.min(cs[c], axis=0, keepdims=True), (8, LANES))
    carry_ref[...] = carry

    placed = [[jnp.dot(piece.astype(BF16), place_ref[...], preferred_element_type=F32)
               for piece in _split3(cs[c] * LOG2E)] for c in chunks]
    expanded = [_dot_f32_lhs(jnp.concatenate([sp[c], cs[c]], axis=0), expand_ref[...])
                for c in chunks]
    lane4 = lax.broadcasted_iota(jnp.int32, (t_rows, FOX_WIDTH), 1) & (HEAD_DIM - 1)
    xs, ea_e, cd_e, xdt_b, wst = [], [], [], [], []
    for c in chunks:
        qaug_ref[rows[c], :] = jnp.where(
            lane4 == 0, placed[c][0],
            jnp.where(lane4 == 1, placed[c][1],
                      jnp.where(lane4 == 2, placed[c][2],
                                jnp.where(lane4 < 6, 1.0, 0.0)))).astype(BF16)
        dt_e = expanded[c][0:t_rows]
        cs_e = expanded[c][t_rows:2 * t_rows]
        ea_e.append(jnp.exp(cs_e))
        de_e = jnp.exp(cs_e[t_rows - 1:t_rows, :] - cs_e)
        cd_e.append(ea_e[c][t_rows - 1:t_rows, :])
        xs.append(xs_ref[rows[c], :])
        xdt = xs[c] * dt_e
        xdt_b.append(xdt.astype(BF16))
        wst.append((xdt * de_e).astype(BF16))

    def cgrp(c, g):
        return bc_ref[rows[c], SSM_GROUPS * SSM_STATE + g * SSM_STATE:
                      SSM_GROUPS * SSM_STATE + (g + 1) * SSM_STATE]

    def bgrp(c, g):
        return bc_ref[rows[c], g * SSM_STATE:(g + 1) * SSM_STATE]

    gmat = [[lax.dot_general(cgrp(c, g), bgrp(c, g), (((1,), (1,)), ((), ())),
                             preferred_element_type=F32) for g in groups]
            for c in chunks]
    r_i = lax.broadcasted_iota(jnp.int32, (t_rows, t_rows), 0)
    c_i = lax.broadcasted_iota(jnp.int32, (t_rows, t_rows), 1)
    causal = r_i >= c_i
    y_parts = []
    for c in chunks:
        parts = []
        for g in groups:
            for pr in range(hpg // 2):
                c0 = g * gw + pr * LANES
                xpair = xdt_b[c][:, c0:c0 + LANES]
                acc = None
                for hh in range(2):
                    h = g * hpg + pr * 2 + hh
                    seg = cs[c][:, h:h + 1] - cst[c][h:h + 1, :]
                    dec = jnp.exp(jnp.where(causal, seg, NEG_INF))
                    m = (gmat[c][g] * dec).astype(BF16)
                    xm = jnp.where(lo if hh == 0 else jnp.logical_not(lo), xpair,
                                   jnp.zeros_like(xpair))
                    part = jnp.dot(m, xm, preferred_element_type=F32)
                    acc = part if acc is None else acc + part
                parts.append(acc)
        y_parts.append(parts)

    upd = [[jnp.dot(bgrp(c, g).astype(F32).T.astype(BF16), wst[c][:, g * gw:(g + 1) * gw],
                    preferred_element_type=F32) for g in groups]
           for c in chunks]

    state = [state_ref[g] for g in groups]
    for c in chunks:
        for g in groups:
            y_off = jnp.dot(cgrp(c, g), state[g].astype(BF16), preferred_element_type=F32)
            for pr in range(hpg // 2):
                cols = slice(g * gw + pr * LANES, g * gw + (pr + 1) * LANES)
                y_parts[c][g * (hpg // 2) + pr] += (
                    y_off[:, pr * LANES:(pr + 1) * LANES] * ea_e[c][:, cols])
            state[g] = state[g] * cd_e[c][:, g * gw:(g + 1) * gw] + upd[c][g]
    for g in groups:
        state_ref[g] = state[g]

    for c in chunks:
        y = jnp.concatenate(y_parts[c], axis=1) + dskip_ref[...] * xs[c]
        y = y * zs_ref[rows[c], :]
        y_ref[rows[c], :] = _rms(y, ng_ref[...]).astype(BF16)


def _ssd(zs, xs, bc, small, bias_pad, a_pad, dskip_e, ng, tri, expand):
    s = zs.shape[0]
    t = SSD_T
    st = SSD_STEP
    nsub = st // t
    const = lambda i: (0, 0)
    row = lambda i: (i, 0)
    place_np = np.zeros((LANES, FOX_WIDTH), np.float32)
    for hd in range(FOX_HEADS):
        pair, odd = divmod(hd, 2)
        c0 = pair * LANES + (0 if odd else HEAD_DIM)
        place_np[FOX_HEADS + hd, c0:c0 + HEAD_DIM] = 1.0
    place = jnp.asarray(place_np, dtype=BF16)
    return pl.pallas_call(
        _ssd_kernel,
        grid=(s // st,),
        in_specs=[
            pl.BlockSpec((st, SSM_WIDTH), row),
            pl.BlockSpec((st, SSM_WIDTH), row),
            pl.BlockSpec((st, SSM_XBC - SSM_WIDTH), row),
            pl.BlockSpec((st, LANES), row),
            pl.BlockSpec((1, LANES), const),
            pl.BlockSpec((1, LANES), const),
            pl.BlockSpec((1, SSM_WIDTH), const),
            pl.BlockSpec((1, SSM_WIDTH), const),
            pl.BlockSpec((t, t), const),
            pl.BlockSpec((LANES, SSM_WIDTH), const),
            pl.BlockSpec((LANES, FOX_WIDTH), const),
        ],
        out_specs=[
            pl.BlockSpec((st, SSM_WIDTH), row),
            pl.BlockSpec((st, FOX_WIDTH), row),
            pl.BlockSpec((16, st), lambda i: (0, i)),
            pl.BlockSpec((nsub, 8, LANES), lambda i: (i, 0, 0)),
            pl.BlockSpec((nsub, 8, LANES), lambda i: (i, 0, 0)),
        ],
        out_shape=[
            jax.ShapeDtypeStruct((s, SSM_WIDTH), BF16),
            jax.ShapeDtypeStruct((s, FOX_WIDTH), BF16),
            jax.ShapeDtypeStruct((16, s), F32),
            jax.ShapeDtypeStruct((s // t, 8, LANES), F32),
            jax.ShapeDtypeStruct((s // t, 8, LANES), F32),
        ],
        scratch_shapes=[
            pltpu.VMEM((SSM_GROUPS, SSM_STATE, SSM_WIDTH // SSM_GROUPS), F32),
            pltpu.VMEM((1, LANES), F32),
        ],
        compiler_params=pltpu.CompilerParams(
            dimension_semantics=("arbitrary",),
            vmem_limit_bytes=VMEM_LIMIT),
        name="ssd",
    )(zs, xs, bc, small, bias_pad, a_pad, dskip_e, ng, tri, expand, place)


def _fox_kernel(nkv_ref, q_ref, kt_ref, v_ref, qaug_ref, cumt_ref, o_ref,
                kaug_ref, m_ref, l_ref, acc_ref):
    tq, tk = ATT_TQ, ATT_TK
    nb = kt_ref.shape[0]
    nq = pl.num_programs(1)
    p = pl.program_id(0)
    i = pl.program_id(1)

    @pl.when(i == 0)
    def _():
        row16 = lax.broadcasted_iota(jnp.int32, (16, tk), 0)
        zeros48 = jnp.zeros((48, tk), BF16)

        def build(j, carry):
            off = pl.multiple_of(j * tk, tk)
            kt = kt_ref[j]
            for hh in range(2):
                ck = cumt_ref[pl.ds(FOX_HEADS + 2 * p + hh, 1), pl.ds(off, tk)] * LOG2E
                hi, mid, lo = _split3(ck)
                bias = jnp.where(row16 < 3, 1.0,
                                 jnp.where(row16 == 3, -hi,
                                           jnp.where(row16 == 4, -mid,
                                                     jnp.where(row16 == 5, -lo, 0.0))))
                bias = bias.astype(BF16)
                if hh == 0:
                    kaug_ref[0, j, 0:64, :] = kt[0:64]
                    kaug_ref[0, j, 64:80, :] = bias
                    kaug_ref[0, j, 80:128, :] = zeros48
                else:
                    kaug_ref[1, j, 0:16, :] = bias
                    kaug_ref[1, j, 16:64, :] = zeros48
                    kaug_ref[1, j, 64:128, :] = kt[64:128]
            return carry

        lax.fori_loop(0, nb, build, 0)

    q = q_ref[...]
    lane = lax.broadcasted_iota(jnp.int32, (tq, LANES), 1)
    lo_half = lane < HEAD_DIM
    r_i = lax.broadcasted_iota(jnp.int32, (tq, tk), 0)
    c_i = lax.broadcasted_iota(jnp.int32, (tq, tk), 1)
    causal = r_i >= c_i

    aug = qaug_ref[...]
    qas = [jnp.where(lo_half, q, aug), jnp.where(lo_half, aug, q)]

    def logits(hh, j):
        return jnp.dot(qas[hh], kaug_ref[hh, j], preferred_element_type=F32)

    def vblock(j):
        return v_ref[pl.ds(pl.multiple_of(j * tk, tk), tk), :]

    def lane_fold(x):
        out = x[:, 0:LANES]
        for c in range(1, x.shape[1] // LANES):
            out = out + x[:, c * LANES:(c + 1) * LANES]
        return out

    def online_diag(hh):
        s = jnp.where(causal, logits(hh, i), NEG_INF)
        m0 = jnp.max(s, axis=1, keepdims=True)
        p0 = jnp.exp2(s - m0)
        m_ref[hh] = jnp.broadcast_to(m0, (tq, LANES))
        l_ref[hh] = lane_fold(p0)
        acc_ref[hh] = jnp.dot(p0.astype(BF16), vblock(i), preferred_element_type=F32)

    def online_step(hh, j):
        s = logits(hh, j)
        m_prev = m_ref[hh]
        m_new = jnp.maximum(m_prev, jnp.max(s, axis=1, keepdims=True))
        alpha = jnp.exp2(m_prev - m_new)
        pj = jnp.exp2(s - m_new[:, 0:1])
        l_ref[hh] = alpha * l_ref[hh] + lane_fold(pj)
        acc_ref[hh] = alpha * acc_ref[hh] + jnp.dot(
            pj.astype(BF16), vblock(j), preferred_element_type=F32)
        m_ref[hh] = m_new

    half = tq // 2

    def fixed_diag(heads):
        vb = vblock(i)
        logit_pairs = [
            (jnp.dot(qas[hh][0:half], kaug_ref[hh, i, :, 0:half],
                     preferred_element_type=F32),
             jnp.dot(qas[hh][half:], kaug_ref[hh, i], preferred_element_type=F32))
            for hh in heads]
        for hh, (s_top, s_bot) in zip(heads, logit_pairs):
            p_top = jnp.exp2(jnp.where(causal[0:half, 0:half], s_top, NEG_INF))
            p_bot = jnp.exp2(jnp.where(causal[half:, :], s_bot, NEG_INF))
            l_ref[hh, 0:half] = lane_fold(p_top)
            l_ref[hh, half:] = lane_fold(p_bot)
            acc_ref[hh, 0:half] = jnp.dot(p_top.astype(BF16), vb[0:half],
                                          preferred_element_type=F32)
            acc_ref[hh, half:] = jnp.dot(p_bot.astype(BF16), vb, preferred_element_type=F32)

    def fixed_steps(jobs):
        all_logits = [jnp.concatenate([logits(hh, j - (n - 1) + b) for b in range(n)], axis=1)
                      for hh, j, n in jobs]
        for (hh, j, n), s in zip(jobs, all_logits):
            pj = jnp.exp2(s)
            l_ref[hh] += lane_fold(pj)
            vn = v_ref[pl.ds(pl.multiple_of((j - (n - 1)) * tk, tk), n * tk), :]
            acc_ref[hh] += jnp.dot(pj.astype(BF16), vn, preferred_element_type=F32)

    codes = [nkv_ref[(2 * p + hh) * nq + i] for hh in range(2)]
    n_offs = [code >> 1 for code in codes]
    both_fixed = (codes[0] & codes[1] & 1) == 1
    joint_quads = jnp.where(both_fixed, jnp.minimum(n_offs[0] >> 2, n_offs[1] >> 2), 0)
    joint_pair = both_fixed & ((n_offs[0] & n_offs[1] & 2) == 2)
    joint_single = both_fixed & ((n_offs[0] & n_offs[1] & 1) == 1)
    pair_at = [i - 1 - 4 * (n >> 2) for n in n_offs]
    single_at = [i - n for n in n_offs]

    @pl.when(both_fixed)
    def _():
        fixed_diag((0, 1))

        def body(u, carry):
            fixed_steps([(0, i - 1 - 4 * u, 4), (1, i - 1 - 4 * u, 4)])
            return carry

        lax.fori_loop(0, joint_quads, body, 0)

        @pl.when(joint_pair)
        def _():
            fixed_steps([(0, pair_at[0], 2), (1, pair_at[1], 2)])

        @pl.when(joint_single)
        def _():
            fixed_steps([(0, single_at[0], 1), (1, single_at[1], 1)])

    for hh in range(2):
        n_off = n_offs[hh]
        fixed_ok = (codes[hh] & 1) == 1

        @pl.when(fixed_ok)
        def _(hh=hh, n_off=n_off):
            @pl.when(jnp.logical_not(both_fixed))
            def _():
                fixed_diag((hh,))

            def body(u, carry):
                fixed_steps([(hh, i - 1 - 4 * u, 4)])
                return carry

            lax.fori_loop(joint_quads, n_off >> 2, body, 0)

            @pl.when(((n_off & 2) == 2) & jnp.logical_not(joint_pair))
            def _():
                fixed_steps([(hh, pair_at[hh], 2)])

            @pl.when(((n_off & 1) == 1) & jnp.logical_not(joint_single))
            def _():
                fixed_steps([(hh, single_at[hh], 1)])

        @pl.when(jnp.logical_not(fixed_ok))
        def _(hh=hh, n_off=n_off):
            online_diag(hh)

            def body(jj, carry):
                online_step(hh, i - 1 - jj)
                return carry

            lax.fori_loop(0, n_off, body, 0)

    l0 = jnp.sum(l_ref[0], axis=1, keepdims=True)
    l1 = jnp.sum(l_ref[1], axis=1, keepdims=True)
    o_ref[...] = jnp.where(lo_half, acc_ref[0] / l0, acc_ref[1] / l1)


def _kv_counts(qn_tiles, kn_tiles, cmax_chunks, cmin_chunks):
    nq = qn_tiles.shape[0]
    per_q = cmax_chunks.shape[0] // nq
    qn = jnp.sqrt(qn_tiles[:, 0, :FOX_HEADS])
    kn = jnp.sqrt(jnp.max(kn_tiles[:, :FOX_HEADS, 0], axis=0))
    gate = slice(FOX_HEADS, 2 * FOX_HEADS)
    cmax = jnp.max(cmax_chunks[:, 0, gate].reshape(nq, per_q, FOX_HEADS), axis=1) * LOG2E
    cmin = jnp.min(cmin_chunks[:, 0, gate].reshape(nq, per_q, FOX_HEADS), axis=1) * LOG2E
    pmin = lax.cummin(cmin, axis=0)
    qk = NORM_MARGIN * qn * kn[None, :] + 1.0
    bound = (2.0 * qk + cmax)[:, None, :] - pmin[None, :, :]
    ii = lax.broadcasted_iota(jnp.int32, bound.shape, 0)
    jj = lax.broadcasted_iota(jnp.int32, bound.shape, 1)
    skip = (bound < SKIP_LOG2) & (jj < ii)
    jstar = jnp.max(jnp.where(skip, jj, -1), axis=1)
    n_off = jnp.arange(nq, dtype=jnp.int32)[:, None] - 1 - jstar
    pmin_prev = jnp.concatenate([jnp.full((1, FOX_HEADS), jnp.inf, F32), pmin[:-1]], axis=0)
    fixed_ok = (qk + jnp.maximum(cmax - pmin_prev, 0.0)) < FIXED_MAX_LOG2
    code = 2 * n_off + fixed_ok.astype(jnp.int32)
    return code.T.reshape(-1).astype(jnp.int32)


def _fox(nkv, q, kt3, v, qaug, cumt):
    s = q.shape[0]
    tq, tk = ATT_TQ, ATT_TK
    nb = s // tk
    grid_spec = pltpu.PrefetchScalarGridSpec(
        num_scalar_prefetch=1,
        grid=(FOX_HEADS // 2, s // tq),
        in_specs=[
            pl.BlockSpec((tq, LANES), lambda p, i, n: (i, p)),
            pl.BlockSpec((nb, LANES, tk), lambda p, i, n: (0, p, 0)),
            pl.BlockSpec((s, LANES), lambda p, i, n: (0, p)),
            pl.BlockSpec((tq, LANES), lambda p, i, n: (i, p)),
            pl.BlockSpec((16, s), lambda p, i, n: (0, 0)),
        ],
        out_specs=pl.BlockSpec((tq, LANES), lambda p, i, n: (i, p)),
        scratch_shapes=[
            pltpu.VMEM((2, nb, LANES, tk), BF16),
            pltpu.VMEM((2, tq, LANES), F32),
            pltpu.VMEM((2, tq, LANES), F32),
            pltpu.VMEM((2, tq, LANES), F32),
        ],
    )
    return pl.pallas_call(
        _fox_kernel,
        grid_spec=grid_spec,
        out_shape=jax.ShapeDtypeStruct((s, FOX_WIDTH), F32),
        compiler_params=pltpu.CompilerParams(
            dimension_semantics=("arbitrary", "arbitrary"),
            vmem_limit_bytes=VMEM_LIMIT),
        name="fox_attn",
    )(nkv, q, kt3, v, qaug, cumt)


def _pad_lanes(vec_dt, vec_f):
    out = jnp.zeros((1, LANES), F32)
    out = out.at[0, 0:SSM_HEADS].set(vec_dt.astype(F32))
    if vec_f is not None:
        out = out.at[0, SSM_HEADS:SSM_HEADS + FOX_HEADS].set(vec_f.astype(F32))
    return out


def _layer(x, ffn1_norm, ffn1_w_in, ffn1_w_out, mix_norm, w_in, conv_w, conv_b, dt_bias,
           a_log, d_skip, ssm_norm, f_bias, fox_norm, w_out, ffn2_norm, ffn2_w_in,
           ffn2_w_out, final_g, *, final):
    ones = jnp.ones((1, D_MODEL), F32)
    row = lambda a: a.reshape(1, -1).astype(F32)

    o0 = SSM_WIDTH
    o1 = o0 + SSM_XBC
    o2 = o1 + SSM_HEADS
    o3 = o2 + 3 * FOX_WIDTH
    proj_cols = ((0, o0), (o0, o1), (o2, o2 + FOX_WIDTH), (o2 + FOX_WIDTH, o2 + 2 * FOX_WIDTH),
                 (o2 + 2 * FOX_WIDTH, o3))
    x1, ffn2_wi, ffn2_wo, wz, wxbc, wq, wkt, wv, wo = _ffn(
        x, row(ffn1_norm), ffn1_w_in.astype(BF16), ffn1_w_out.astype(BF16), ones,
        final=False,
        casts=((ffn2_w_in, ((0, 2 * D_FF),)), (ffn2_w_out, ((0, D_MODEL),)),
               (w_in, proj_cols), (w_out, ((0, D_MODEL),))))
    ws = jnp.concatenate(
        [w_in[:, o1:o2], w_in[:, o3:],
         jnp.zeros((D_MODEL, LANES - SSM_HEADS - FOX_HEADS), w_in.dtype)], axis=1)
    zs, xs, bc, q, kt3, v, small, qn2, kn2 = _inproj(
        x1, row(mix_norm), wz, wxbc, wq, wkt, wv, ws, conv_w.astype(F32), row(conv_b))

    bias_pad = _pad_lanes(dt_bias, f_bias)
    a_pad = _pad_lanes(-jnp.exp(a_log.astype(F32)), None)
    dskip_e = jnp.repeat(d_skip.astype(F32), HEAD_DIM).reshape(1, SSM_WIDTH)
    tri = jnp.asarray(np.tril(np.ones((SSD_T, SSD_T), np.float32)), dtype=BF16)
    expand_np = np.zeros((LANES, SSM_WIDTH), np.float32)
    for hd in range(SSM_HEADS):
        expand_np[hd, hd * HEAD_DIM:(hd + 1) * HEAD_DIM] = 1.0
    y_ssd, qaug, cumt, cmax_c, cmin_c = _ssd(zs, xs, bc, small, bias_pad, a_pad,
                            dskip_e, row(ssm_norm), tri, jnp.asarray(expand_np, dtype=BF16))

    o_fox = _fox(_kv_counts(qn2, kn2, cmax_c, cmin_c), q, kt3, v, qaug, cumt)

    mix = (y_ssd, o_fox, row(fox_norm), wo[:SSM_WIDTH], wo[SSM_WIDTH:])
    out, = _ffn(x1, row(ffn2_norm), ffn2_wi, ffn2_wo, row(final_g), final=final, mix=mix)
    return out


def kernel(x, ffn1_norm, ffn1_w_in, ffn1_w_out, mix_norm, w_in, conv_w, conv_b, dt_bias, a_log,
           d_skip, ssm_norm, f_bias, fox_norm, w_out, ffn2_norm, ffn2_w_in, ffn2_w_out, final_norm):
    b, s, d = x.shape
    depth = ffn1_norm.shape[0]
    outs = []
    for bi in range(b):
        xb = x[bi]
        for l in range(depth):
            xb = _layer(xb, ffn1_norm[l], ffn1_w_in[l], ffn1_w_out[l], mix_norm[l], w_in[l],
                        conv_w[l], conv_b[l], dt_bias[l], a_log[l], d_skip[l], ssm_norm[l],
                        f_bias[l], fox_norm[l], w_out[l], ffn2_norm[l], ffn2_w_in[l],
                        ffn2_w_out[l], final_norm, final=(l == depth - 1))
        outs.append(xb)
    return jnp.stack(outs, axis=0)
```

```python
import functools

import jax
import jax.numpy as jnp
import numpy as np
from jax import lax
from jax.experimental import pallas as pl
from jax.experimental.pallas import tpu as pltpu

F32 = jnp.float32
BF16 = jnp.bfloat16

D_MODEL = 1024
HEAD_DIM = 64
SSM_WIDTH = 512
SSM_HEADS = 8
SSM_GROUPS = 2
SSM_STATE = 128
CONV_WIDTH = 4
SSM_XBC = SSM_WIDTH + 2 * SSM_GROUPS * SSM_STATE
FOX_WIDTH = 512
FOX_HEADS = 8
D_FF = 2816
EPS = 1e-6

LANES = 128
VMEM_LIMIT = 56 * 1024 * 1024

FFN_TM = 1024
FFN_SUB = 512
FFN_TF = 256
PROJ_TM = 512
SSD_T = 128
SSD_STEP = 1024
ATT_TQ = 512
ATT_TK = 512
NEG_INF = float("-inf")
LOG2E = 1.4426950408889634
SKIP_LOG2 = -150.0
NORM_MARGIN = 1.05
FIXED_MAX_LOG2 = 64.0


def _rms(x, g):
    ms = jnp.mean(x * x, axis=-1, keepdims=True)
    return x * lax.rsqrt(ms + EPS) * g


def _silu(x):
    return x * (1.0 / (1.0 + jnp.exp(-x)))


def _softplus(x):
    return jnp.maximum(x, 0.0) + jnp.log1p(jnp.exp(-jnp.abs(x)))


def _split3(x):
    hi = x.astype(BF16).astype(F32)
    r = x - hi
    mid = r.astype(BF16).astype(F32)
    lo = (r - mid).astype(BF16).astype(F32)
    return hi, mid, lo


def _dot_f32_rhs(a_bf16, x):
    hi, mid, lo = _split3(x)
    d = lambda piece: jnp.dot(a_bf16, piece.astype(BF16), preferred_element_type=F32)
    return (d(lo) + d(mid)) + d(hi)


def _dot_f32_lhs(x, b_bf16):
    hi, mid, lo = _split3(x)
    d = lambda piece: jnp.dot(piece.astype(BF16), b_bf16, preferred_element_type=F32)
    return (d(lo) + d(mid)) + d(hi)


def _ffn_kernel(*refs, mix, final, cast_cols):
    if cast_cols:
        n_src = len(cast_cols)
        n_dst = sum(len(ranges) for ranges in cast_cols)
        n_in = len(refs) - 1 - n_src - n_dst
        dst_refs = iter(refs[n_in + n_src + 1:])
        for src_ref, ranges in zip(refs[n_in:n_in + n_src], cast_cols):
            for c0, c1 in ranges:
                next(dst_refs)[...] = src_ref[:, c0:c1].astype(BF16)
        refs = refs[:n_in] + (refs[n_in + n_src],)
    if mix:
        (x_ref, ys_ref, of_ref, mg_ref, wm1_ref, wm2_ref,
         g_ref, wg_ref, wu_ref, wo_ref, fg_ref, o_ref) = refs
    else:
        x_ref, g_ref, wg_ref, wu_ref, wo_ref, fg_ref, o_ref = refs
    nf = D_FF // FFN_TF
    for r0 in range(0, FFN_TM, FFN_SUB):
        rows = slice(r0, r0 + FFN_SUB)
        if mix:
            yf = _rms(of_ref[rows, :], mg_ref[...]).astype(BF16)
            x = (x_ref[rows, :]
                 + jnp.dot(ys_ref[rows, :], wm1_ref[...], preferred_element_type=F32)
                 + jnp.dot(yf, wm2_ref[...], preferred_element_type=F32))
        else:
            x = x_ref[rows, :]
        h = _rms(x, g_ref[...]).astype(BF16)
        o_ref[rows, :] = 2.0 * x

        def gate_up(f, h=h):
            cols = slice(f * FFN_TF, (f + 1) * FFN_TF)
            return (jnp.dot(h, wg_ref[:, cols], preferred_element_type=F32),
                    jnp.dot(h, wu_ref[:, cols], preferred_element_type=F32))

        pending = gate_up(0)
        for f in range(nf):
            gate, up = pending
            if f + 1 < nf:
                pending = gate_up(f + 1)
            act = (_silu(gate) * up).astype(BF16)
            o_ref[rows, :] += jnp.dot(act, wo_ref[f * FFN_TF:(f + 1) * FFN_TF, :],
                                      preferred_element_type=F32)
        r = 0.5 * o_ref[rows, :]
        if final:
            r = _rms(r, fg_ref[...])
        o_ref[rows, :] = r


def _ffn(x, g, w_in, w_out, fg, *, final, mix=None, casts=()):
    s = x.shape[0]
    tm = FFN_TM
    steps = s // tm
    row = lambda i: (i, 0)
    const = lambda i: (0, 0)
    resident = functools.partial(pl.BlockSpec, pipeline_mode=pl.Buffered(1))
    x_spec = pl.BlockSpec((tm, D_MODEL), row)
    vec_spec = pl.BlockSpec((1, D_MODEL), const)
    ffn_specs = [
        vec_spec,
        resident((D_MODEL, D_FF), lambda i: (0, 0)),
        resident((D_MODEL, D_FF), lambda i: (0, 1)),
        resident((D_FF, D_MODEL), const),
        vec_spec,
    ]
    ffn_args = (g, w_in, w_in, w_out, fg)
    if mix is None:
        in_specs = [x_spec] + ffn_specs
        args = (x,) + ffn_args
    else:
        ys, of, mg, wm1, wm2 = mix
        in_specs = [
            x_spec,
            pl.BlockSpec((tm, SSM_WIDTH), row),
            pl.BlockSpec((tm, FOX_WIDTH), row),
            pl.BlockSpec((1, FOX_WIDTH), const),
            resident((SSM_WIDTH, D_MODEL), const),
            resident((FOX_WIDTH, D_MODEL), const),
        ] + ffn_specs
        args = (x, ys, of, mg, wm1, wm2) + ffn_args
    out_specs = [pl.BlockSpec((tm, D_MODEL), row)]
    out_shape = [jax.ShapeDtypeStruct((s, D_MODEL), F32)]
    for w, ranges in casts:
        slab_rows = w.shape[0] // steps
        assert slab_rows * steps == w.shape[0] and slab_rows % 16 == 0, w.shape
        in_specs.append(pl.BlockSpec((slab_rows, w.shape[1]), row))
        for c0, c1 in ranges:
            out_specs.append(pl.BlockSpec((slab_rows, c1 - c0), row))
            out_shape.append(jax.ShapeDtypeStruct((w.shape[0], c1 - c0), BF16))
    return pl.pallas_call(
        functools.partial(_ffn_kernel, mix=mix is not None, final=final,
                          cast_cols=tuple(tuple(ranges) for _, ranges in casts)),
        grid=(steps,),
        in_specs=in_specs,
        out_specs=out_specs,
        out_shape=out_shape,
        compiler_params=pltpu.CompilerParams(
            dimension_semantics=("parallel",),
            vmem_limit_bytes=VMEM_LIMIT),
        name="ffn_final" if final else "ffn",
    )(*args, *(w for w, _ in casts))


def _inproj_kernel(x_ref, g_ref, wz_ref, wxbc_ref, wq_ref, wkt_ref, wv_ref, ws_ref,
                   ind_ref, indt_ref, cw_ref, cb_ref,
                   zs_ref, xs_ref, bc_ref, q_ref, kt_ref, v_ref, small_ref, qn_ref, kn_ref,
                   halo_ref):
    tm = PROJ_TM
    i = pl.program_id(0)

    @pl.when(i == 0)
    def _():
        halo_ref[...] = jnp.zeros_like(halo_ref)

    h = _rms(x_ref[...], g_ref[...]).astype(BF16)
    xb = jnp.dot(h, wxbc_ref[...], preferred_element_type=F32)
    zs_ref[...] = _silu(jnp.dot(h, wz_ref[...], preferred_element_type=F32))
    q = jnp.dot(h, wq_ref[...], preferred_element_type=F32)
    qb = (q * (HEAD_DIM ** -0.5 * LOG2E)).astype(BF16)
    q_ref[...] = qb
    kt = lax.dot_general(wkt_ref[...], h, (((0,), (1,)), ((), ())),
                         preferred_element_type=F32)
    ktb = kt.astype(BF16)
    kt_ref[0] = ktb
    v_ref[...] = jnp.dot(h, wv_ref[...], preferred_element_type=F32).astype(BF16)
    small_ref[...] = jnp.dot(h, ws_ref[...].astype(BF16), preferred_element_type=F32)
    qf = qb.astype(F32)
    qn = jnp.dot((qf * qf).astype(BF16), ind_ref[...], preferred_element_type=F32)
    qn_ref[0] = jnp.broadcast_to(jnp.max(qn, axis=0, keepdims=True), (8, LANES))
    kf = ktb.astype(F32)
    kn = jnp.dot(indt_ref[...], (kf * kf).astype(BF16), preferred_element_type=F32)
    kn_ref[0] = jnp.broadcast_to(jnp.max(kn, axis=1, keepdims=True), (16, LANES))
    prev = halo_ref[...]
    row8 = lax.broadcasted_iota(jnp.int32, (8, SSM_XBC), 0)
    conv = cb_ref[...] + xb * cw_ref[CONV_WIDTH - 1:CONV_WIDTH, :]
    for s in range(1, CONV_WIDTH):
        rolled = pltpu.roll(xb, s, axis=0)
        head = jnp.where(row8 < s, pltpu.roll(prev, s, axis=0), rolled[0:8])
        shifted = jnp.concatenate([head, rolled[8:]], axis=0)
        k = CONV_WIDTH - 1 - s
        conv = conv + shifted * cw_ref[k:k + 1, :]
    halo_ref[...] = xb[tm - 8:tm]
    u = _silu(conv)
    xs_ref[...] = u[:, :SSM_WIDTH]
    bc_ref[...] = u[:, SSM_WIDTH:].astype(BF16)


def _inproj(x, g, wz, wxbc, wq, wkt, wv, ws, cw, cb):
    s = x.shape[0]
    tm = PROJ_TM
    nb = s // tm
    const = lambda i: (0, 0)
    row = lambda i: (i, 0)
    ind_np = np.zeros((FOX_WIDTH, LANES), np.float32)
    for hd in range(FOX_HEADS):
        ind_np[hd * HEAD_DIM:(hd + 1) * HEAD_DIM, hd] = 1.0
    ind = jnp.asarray(ind_np, dtype=BF16)
    indt = jnp.asarray(ind_np.T[:16], dtype=BF16)
    return pl.pallas_call(
        _inproj_kernel,
        grid=(nb,),
        in_specs=[
            pl.BlockSpec((tm, D_MODEL), row),
            pl.BlockSpec((1, D_MODEL), const),
            pl.BlockSpec((D_MODEL, SSM_WIDTH), const),
            pl.BlockSpec((D_MODEL, SSM_XBC), const),
            pl.BlockSpec((D_MODEL, FOX_WIDTH), const),
            pl.BlockSpec((D_MODEL, FOX_WIDTH), const),
            pl.BlockSpec((D_MODEL, FOX_WIDTH), const),
            pl.BlockSpec((D_MODEL, LANES), const),
            pl.BlockSpec((FOX_WIDTH, LANES), const),
            pl.BlockSpec((16, FOX_WIDTH), const),
            pl.BlockSpec((CONV_WIDTH, SSM_XBC), const),
            pl.BlockSpec((1, SSM_XBC), const),
        ],
        out_specs=[
            pl.BlockSpec((tm, SSM_WIDTH), row),
            pl.BlockSpec((tm, SSM_WIDTH), row),
            pl.BlockSpec((tm, SSM_XBC - SSM_WIDTH), row),
            pl.BlockSpec((tm, FOX_WIDTH), row),
            pl.BlockSpec((1, FOX_WIDTH, tm), lambda i: (i, 0, 0)),
            pl.BlockSpec((tm, FOX_WIDTH), row),
            pl.BlockSpec((tm, LANES), row),
            pl.BlockSpec((1, 8, LANES), lambda i: (i, 0, 0)),
            pl.BlockSpec((1, 16, LANES), lambda i: (i, 0, 0)),
        ],
        out_shape=[
            jax.ShapeDtypeStruct((s, SSM_WIDTH), F32),
            jax.ShapeDtypeStruct((s, SSM_WIDTH), F32),
            jax.ShapeDtypeStruct((s, SSM_XBC - SSM_WIDTH), BF16),
            jax.ShapeDtypeStruct((s, FOX_WIDTH), BF16),
            jax.ShapeDtypeStruct((nb, FOX_WIDTH, tm), BF16),
            jax.ShapeDtypeStruct((s, FOX_WIDTH), BF16),
            jax.ShapeDtypeStruct((s, LANES), F32),
            jax.ShapeDtypeStruct((nb, 8, LANES), F32),
            jax.ShapeDtypeStruct((nb, 16, LANES), F32),
        ],
        scratch_shapes=[pltpu.VMEM((8, SSM_XBC), F32)],
        compiler_params=pltpu.CompilerParams(
            dimension_semantics=("arbitrary",),
            vmem_limit_bytes=VMEM_LIMIT),
        name="in_proj",
    )(x, g, wz, wxbc, wq, wkt, wv, ws, ind, indt, cw, cb)


def _ssd_kernel(zs_ref, xs_ref, bc_ref, small_ref, bias_ref, apad_ref,
                dskip_ref, ng_ref, tri_ref, expand_ref, place_ref,
                y_ref, qaug_ref, cumt_ref, cmax_ref, cmin_ref,
                state_ref, carry_ref):
    t_rows = SSD_T
    i = pl.program_id(0)

    @pl.when(i == 0)
    def _():
        state_ref[...] = jnp.zeros_like(state_ref)
        carry_ref[...] = jnp.zeros_like(carry_ref)

    chunks = range(SSD_STEP // t_rows)
    rows = [slice(c * t_rows, (c + 1) * t_rows) for c in chunks]
    gw = SSM_WIDTH // SSM_GROUPS
    hpg = SSM_HEADS // SSM_GROUPS
    groups = range(SSM_GROUPS)
    lane = lax.broadcasted_iota(jnp.int32, (t_rows, LANES), 1)
    is_dt = lane < SSM_HEADS
    lo = lane < HEAD_DIM

    sp, v = [], []
    for c in chunks:
        t = small_ref[rows[c], :] + bias_ref[...]
        sp.append(_softplus(jnp.where(is_dt, t, -t)))
        v.append(jnp.where(is_dt, sp[c] * apad_ref[...], -sp[c]))
    local = [_dot_f32_rhs(tri_ref[...], v[c]) for c in chunks]
    carry = carry_ref[...]
    cs, cst = [], []
    for c in chunks:
        cs.append(local[c] + carry)
        carry = jnp.where(lane[0:1, :] < SSM_HEADS, 0.0, cs[c][t_rows - 1:t_rows, :])
        cst.append(cs[c].T)
        cumt_ref[:, rows[c]] = cst[c][0:16, :]
        cmax_ref[c] = jnp.broadcast_to(jnp.max(cs[c], axis=0, keepdims=True), (8, LANES))
        cmin_ref[c] = jnp.broadcast_to(jnp.min(cs[c], axis=0, keepdims=True), (8, LANES))
    carry_ref[...] = carry

    placed = [[jnp.dot(piece.astype(BF16), place_ref[...], preferred_element_type=F32)
               for piece in _split3(cs[c] * LOG2E)] for c in chunks]
    expanded = [_dot_f32_lhs(jnp.concatenate([sp[c], cs[c]], axis=0), expand_ref[...])
                for c in chunks]
    lane4 = lax.broadcasted_iota(jnp.int32, (t_rows, FOX_WIDTH), 1) & (HEAD_DIM - 1)
    xs, ea_e, cd_e, xdt_b, wst = [], [], [], [], []
    for c in chunks:
        qaug_ref[rows[c], :] = jnp.where(
            lane4 == 0, placed[c][0],
            jnp.where(lane4 == 1, placed[c][1],
                      jnp.where(lane4 == 2, placed[c][2],
                                jnp.where(lane4 < 6, 1.0, 0.0)))).astype(BF16)
        dt_e = expanded[c][0:t_rows]
        cs_e = expanded[c][t_rows:2 * t_rows]
        ea_e.append(jnp.exp(cs_e))
        de_e = jnp.exp(cs_e[t_rows - 1:t_rows, :] - cs_e)
        cd_e.append(ea_e[c][t_rows - 1:t_rows, :])
        xs.append(xs_ref[rows[c], :])
        xdt = xs[c] * dt_e
        xdt_b.append(xdt.astype(BF16))
        wst.append((xdt * de_e).astype(BF16))

    def cgrp(c, g):
        return bc_ref[rows[c], SSM_GROUPS * SSM_STATE + g * SSM_STATE:
                      SSM_GROUPS * SSM_STATE + (g + 1) * SSM_STATE]

    def bgrp(c, g):
        return bc_ref[rows[c], g * SSM_STATE:(g + 1) * SSM_STATE]

    gmat = [[lax.dot_general(cgrp(c, g), bgrp(c, g), (((1,), (1,)), ((), ())),
                             preferred_element_type=F32) for g in groups]
            for c in chunks]
    r_i = lax.broadcasted_iota(jnp.int32, (t_rows, t_rows), 0)
    c_i = lax.broadcasted_iota(jnp.int32, (t_rows, t_rows), 1)
    causal = r_i >= c_i
    y_parts = []
    for c in chunks:
        parts = []
        for g in groups:
            for pr in range(hpg // 2):
                c0 = g * gw + pr * LANES
                xpair = xdt_b[c][:, c0:c0 + LANES]
                acc = None
                for hh in range(2):
                    h = g * hpg + pr * 2 + hh
                    seg = cs[c][:, h:h + 1] - cst[c][h:h + 1, :]
                    dec = jnp.exp(jnp.where(causal, seg, NEG_INF))
                    m = (gmat[c][g] * dec).astype(BF16)
                    xm = jnp.where(lo if hh == 0 else jnp.logical_not(lo), xpair,
                                   jnp.zeros_like(xpair))
                    part = jnp.dot(m, xm, preferred_element_type=F32)
                    acc = part if acc is None else acc + part
                parts.append(acc)
        y_parts.append(parts)

    upd = [[jnp.dot(bgrp(c, g).astype(F32).T.astype(BF16), wst[c][:, g * gw:(g + 1) * gw],
                    preferred_element_type=F32) for g in groups]
           for c in chunks]

    state = [state_ref[g] for g in groups]
    for c in chunks:
        for g in groups:
            y_off = jnp.dot(cgrp(c, g), state[g].astype(BF16), preferred_element_type=F32)
            for pr in range(hpg // 2):
                cols = slice(g * gw + pr * LANES, g * gw + (pr + 1) * LANES)
                y_parts[c][g * (hpg // 2) + pr] += (
                    y_off[:, pr * LANES:(pr + 1) * LANES] * ea_e[c][:, cols])
            state[g] = state[g] * cd_e[c][:, g * gw:(g + 1) * gw] + upd[c][g]
    for g in groups:
        state_ref[g] = state[g]

    for c in chunks:
        y = jnp.concatenate(y_parts[c], axis=1) + dskip_ref[...] * xs[c]
        y = y * zs_ref[rows[c], :]
        y_ref[rows[c], :] = _rms(y, ng_ref[...]).astype(BF16)


def _ssd(zs, xs, bc, small, bias_pad, a_pad, dskip_e, ng, tri, expand):
    s = zs.shape[0]
    t = SSD_T
    st = SSD_STEP
    nsub = st // t
    const = lambda i: (0, 0)
    row = lambda i: (i, 0)
    place_np = np.zeros((LANES, FOX_WIDTH), np.float32)
    for hd in range(FOX_HEADS):
        pair, odd = divmod(hd, 2)
        c0 = pair * LANES + (0 if odd else HEAD_DIM)
        place_np[FOX_HEADS + hd, c0:c0 + HEAD_DIM] = 1.0
    place = jnp.asarray(place_np, dtype=BF16)
    return pl.pallas_call(
        _ssd_kernel,
        grid=(s // st,),
        in_specs=[
            pl.BlockSpec((st, SSM_WIDTH), row),
            pl.BlockSpec((st, SSM_WIDTH), row),
            pl.BlockSpec((st, SSM_XBC - SSM_WIDTH), row),
            pl.BlockSpec((st, LANES), row),
            pl.BlockSpec((1, LANES), const),
            pl.BlockSpec((1, LANES), const),
            pl.BlockSpec((1, SSM_WIDTH), const),
            pl.BlockSpec((1, SSM_WIDTH), const),
            pl.BlockSpec((t, t), const),
            pl.BlockSpec((LANES, SSM_WIDTH), const),
            pl.BlockSpec((LANES, FOX_WIDTH), const),
        ],
        out_specs=[
            pl.BlockSpec((st, SSM_WIDTH), row),
            pl.BlockSpec((st, FOX_WIDTH), row),
            pl.BlockSpec((16, st), lambda i: (0, i)),
            pl.BlockSpec((nsub, 8, LANES), lambda i: (i, 0, 0)),
            pl.BlockSpec((nsub, 8, LANES), lambda i: (i, 0, 0)),
        ],
        out_shape=[
            jax.ShapeDtypeStruct((s, SSM_WIDTH), BF16),
            jax.ShapeDtypeStruct((s, FOX_WIDTH), BF16),
            jax.ShapeDtypeStruct((16, s), F32),
            jax.ShapeDtypeStruct((s // t, 8, LANES), F32),
            jax.ShapeDtypeStruct((s // t, 8, LANES), F32),
        ],
        scratch_shapes=[
            pltpu.VMEM((SSM_GROUPS, SSM_STATE, SSM_WIDTH // SSM_GROUPS), F32),
            pltpu.VMEM((1, LANES), F32),
        ],
        compiler_params=pltpu.CompilerParams(
            dimension_semantics=("arbitrary",),
            vmem_limit_bytes=VMEM_LIMIT),
        name="ssd",
    )(zs, xs, bc, small, bias_pad, a_pad, dskip_e, ng, tri, expand, place)


def _fox_kernel(nkv_ref, q_ref, kt_ref, v_ref, qaug_ref, cumt_ref, o_ref,
                kaug_ref, m_ref, l_ref, acc_ref):
    tq, tk = ATT_TQ, ATT_TK
    nb = kt_ref.shape[0]
    nq = pl.num_programs(1)
    p = pl.program_id(0)
    i = pl.program_id(1)

    @pl.when(i == 0)
    def _():
        row16 = lax.broadcasted_iota(jnp.int32, (16, tk), 0)
        zeros48 = jnp.zeros((48, tk), BF16)

        def build(j, carry):
            off = pl.multiple_of(j * tk, tk)
            kt = kt_ref[j]
            for hh in range(2):
                ck = cumt_ref[pl.ds(FOX_HEADS + 2 * p + hh, 1), pl.ds(off, tk)] * LOG2E
                hi, mid, lo = _split3(ck)
                bias = jnp.where(row16 < 3, 1.0,
                                 jnp.where(row16 == 3, -hi,
                                           jnp.where(row16 == 4, -mid,
                                                     jnp.where(row16 == 5, -lo, 0.0))))
                bias = bias.astype(BF16)
                if hh == 0:
                    kaug_ref[0, j, 0:64, :] = kt[0:64]
                    kaug_ref[0, j, 64:80, :] = bias
                    kaug_ref[0, j, 80:128, :] = zeros48
                else:
                    kaug_ref[1, j, 0:16, :] = bias
                    kaug_ref[1, j, 16:64, :] = zeros48
                    kaug_ref[1, j, 64:128, :] = kt[64:128]
            return carry

        lax.fori_loop(0, nb, build, 0)

    q = q_ref[...]
    lane = lax.broadcasted_iota(jnp.int32, (tq, LANES), 1)
    lo_half = lane < HEAD_DIM
    r_i = lax.broadcasted_iota(jnp.int32, (tq, tk), 0)
    c_i = lax.broadcasted_iota(jnp.int32, (tq, tk), 1)
    causal = r_i >= c_i

    aug = qaug_ref[...]
    qas = [jnp.where(lo_half, q, aug), jnp.where(lo_half, aug, q)]

    def logits(hh, j):
        return jnp.dot(qas[hh], kaug_ref[hh, j], preferred_element_type=F32)

    def vblock(j):
        return v_ref[pl.ds(pl.multiple_of(j * tk, tk), tk), :]

    def lane_fold(x):
        out = x[:, 0:LANES]
        for c in range(1, x.shape[1] // LANES):
            out = out + x[:, c * LANES:(c + 1) * LANES]
        return out

    def online_diag(hh):
        s = jnp.where(causal, logits(hh, i), NEG_INF)
        m0 = jnp.max(s, axis=1, keepdims=True)
        p0 = jnp.exp2(s - m0)
        m_ref[hh] = jnp.broadcast_to(m0, (tq, LANES))
        l_ref[hh] = lane_fold(p0)
        acc_ref[hh] = jnp.dot(p0.astype(BF16), vblock(i), preferred_element_type=F32)

    def online_step(hh, j):
        s = logits(hh, j)
        m_prev = m_ref[hh]
        m_new = jnp.maximum(m_prev, jnp.max(s, axis=1, keepdims=True))
        alpha = jnp.exp2(m_prev - m_new)
        pj = jnp.exp2(s - m_new[:, 0:1])
        l_ref[hh] = alpha * l_ref[hh] + lane_fold(pj)
        acc_ref[hh] = alpha * acc_ref[hh] + jnp.dot(
            pj.astype(BF16), vblock(j), preferred_element_type=F32)
        m_ref[hh] = m_new

    half = tq // 2

    def fixed_diag(heads):
        vb = vblock(i)
        logit_pairs = [
            (jnp.dot(qas[hh][0:half], kaug_ref[hh, i, :, 0:half],
                     preferred_element_type=F32),
             jnp.dot(qas[hh][half:], kaug_ref[hh, i], preferred_element_type=F32))
            for hh in heads]
        for hh, (s_top, s_bot) in zip(heads, logit_pairs):
            p_top = jnp.exp2(jnp.where(causal[0:half, 0:half], s_top, NEG_INF))
            p_bot = jnp.exp2(jnp.where(causal[half:, :], s_bot, NEG_INF))
            l_ref[hh, 0:half] = lane_fold(p_top)
            l_ref[hh, half:] = lane_fold(p_bot)
            acc_ref[hh, 0:half] = jnp.dot(p_top.astype(BF16), vb[0:half],
                                          preferred_element_type=F32)
            acc_ref[hh, half:] = jnp.dot(p_bot.astype(BF16), vb, preferred_element_type=F32)

    def fixed_steps(jobs):
        all_logits = [jnp.concatenate([logits(hh, j - (n - 1) + b) for b in range(n)], axis=1)
                      for hh, j, n in jobs]
        for (hh, j, n), s in zip(jobs, all_logits):
            pj = jnp.exp2(s)
            l_ref[hh] += lane_fold(pj)
            vn = v_ref[pl.ds(pl.multiple_of((j - (n - 1)) * tk, tk), n * tk), :]
            acc_ref[hh] += jnp.dot(pj.astype(BF16), vn, preferred_element_type=F32)

    codes = [nkv_ref[(2 * p + hh) * nq + i] for hh in range(2)]
    n_offs = [code >> 1 for code in codes]
    both_fixed = (codes[0] & codes[1] & 1) == 1
    joint_quads = jnp.where(both_fixed, jnp.minimum(n_offs[0] >> 2, n_offs[1] >> 2), 0)
    joint_pair = both_fixed & ((n_offs[0] & n_offs[1] & 2) == 2)
    joint_single = both_fixed & ((n_offs[0] & n_offs[1] & 1) == 1)
    pair_at = [i - 1 - 4 * (n >> 2) for n in n_offs]
    single_at = [i - n for n in n_offs]

    @pl.when(both_fixed)
    def _():
        fixed_diag((0, 1))

        def body(u, carry):
            fixed_steps([(0, i - 1 - 4 * u, 4), (1, i - 1 - 4 * u, 4)])
            return carry

        lax.fori_loop(0, joint_quads, body, 0)

        @pl.when(joint_pair)
        def _():
            fixed_steps([(0, pair_at[0], 2), (1, pair_at[1], 2)])

        @pl.when(joint_single)
        def _():
            fixed_steps([(0, single_at[0], 1), (1, single_at[1], 1)])

    for hh in range(2):
        n_off = n_offs[hh]
        fixed_ok = (codes[hh] & 1) == 1

        @pl.when(fixed_ok)
        def _(hh=hh, n_off=n_off):
            @pl.when(jnp.logical_not(both_fixed))
            def _():
                fixed_diag((hh,))

            def body(u, carry):
                fixed_steps([(hh, i - 1 - 4 * u, 4)])
                return carry

            lax.fori_loop(joint_quads, n_off >> 2, body, 0)

            @pl.when(((n_off & 2) == 2) & jnp.logical_not(joint_pair))
            def _():
                fixed_steps([(hh, pair_at[hh], 2)])

            @pl.when(((n_off & 1) == 1) & jnp.logical_not(joint_single))
            def _():
                fixed_steps([(hh, single_at[hh], 1)])

        @pl.when(jnp.logical_not(fixed_ok))
        def _(hh=hh, n_off=n_off):
            online_diag(hh)

            def body(jj, carry):
                online_step(hh, i - 1 - jj)
                return carry

            lax.fori_loop(0, n_off, body, 0)

    l0 = jnp.sum(l_ref[0], axis=1, keepdims=True)
    l1 = jnp.sum(l_ref[1], axis=1, keepdims=True)
    o_ref[...] = jnp.where(lo_half, acc_ref[0] / l0, acc_ref[1] / l1)


def _kv_counts(qn_tiles, kn_tiles, cmax_chunks, cmin_chunks):
    nq = qn_tiles.shape[0]
    per_q = cmax_chunks.shape[0] // nq
    qn = jnp.sqrt(qn_tiles[:, 0, :FOX_HEADS])
    kn = jnp.sqrt(jnp.max(kn_tiles[:, :FOX_HEADS, 0], axis=0))
    gate = slice(FOX_HEADS, 2 * FOX_HEADS)
    cmax = jnp.max(cmax_chunks[:, 0, gate].reshape(nq, per_q, FOX_HEADS), axis=1) * LOG2E
    cmin = jnp.min(cmin_chunks[:, 0, gate].reshape(nq, per_q, FOX_HEADS), axis=1) * LOG2E
    pmin = lax.cummin(cmin, axis=0)
    qk = NORM_MARGIN * qn * kn[None, :] + 1.0
    bound = (2.0 * qk + cmax)[:, None, :] - pmin[None, :, :]
    ii = lax.broadcasted_iota(jnp.int32, bound.shape, 0)
    jj = lax.broadcasted_iota(jnp.int32, bound.shape, 1)
    skip = (bound < SKIP_LOG2) & (jj < ii)
    jstar = jnp.max(jnp.where(skip, jj, -1), axis=1)
    n_off = jnp.arange(nq, dtype=jnp.int32)[:, None] - 1 - jstar
    pmin_prev = jnp.concatenate([jnp.full((1, FOX_HEADS), jnp.inf, F32), pmin[:-1]], axis=0)
    fixed_ok = (qk + jnp.maximum(cmax - pmin_prev, 0.0)) < FIXED_MAX_LOG2
    code = 2 * n_off + fixed_ok.astype(jnp.int32)
    return code.T.reshape(-1).astype(jnp.int32)


def _fox(nkv, q, kt3, v, qaug, cumt):
    s = q.shape[0]
    tq, tk = ATT_TQ, ATT_TK
    nb = s // tk
    grid_spec = pltpu.PrefetchScalarGridSpec(
        num_scalar_prefetch=1,
        grid=(FOX_HEADS // 2, s // tq),
        in_specs=[
            pl.BlockSpec((tq, LANES), lambda p, i, n: (i, p)),
            pl.BlockSpec((nb, LANES, tk), lambda p, i, n: (0, p, 0)),
            pl.BlockSpec((s, LANES), lambda p, i, n: (0, p)),
            pl.BlockSpec((tq, LANES), lambda p, i, n: (i, p)),
            pl.BlockSpec((16, s), lambda p, i, n: (0, 0)),
        ],
        out_specs=pl.BlockSpec((tq, LANES), lambda p, i, n: (i, p)),
        scratch_shapes=[
            pltpu.VMEM((2, nb, LANES, tk), BF16),
            pltpu.VMEM((2, tq, LANES), F32),
            pltpu.VMEM((2, tq, LANES), F32),
            pltpu.VMEM((2, tq, LANES), F32),
        ],
    )
    return pl.pallas_call(
        _fox_kernel,
        grid_spec=grid_spec,
        out_shape=jax.ShapeDtypeStruct((s, FOX_WIDTH), F32),
        compiler_params=pltpu.CompilerParams(
            dimension_semantics=("arbitrary", "arbitrary"),
            vmem_limit_bytes=VMEM_LIMIT),
        name="fox_attn",
    )(nkv, q, kt3, v, qaug, cumt)


def _pad_lanes(vec_dt, vec_f):
    out = jnp.zeros((1, LANES), F32)
    out = out.at[0, 0:SSM_HEADS].set(vec_dt.astype(F32))
    if vec_f is not None:
        out = out.at[0, SSM_HEADS:SSM_HEADS + FOX_HEADS].set(vec_f.astype(F32))
    return out


def _layer(x, ffn1_norm, ffn1_w_in, ffn1_w_out, mix_norm, w_in, conv_w, conv_b, dt_bias,
           a_log, d_skip, ssm_norm, f_bias, fox_norm, w_out, ffn2_norm, ffn2_w_in,
           ffn2_w_out, final_g, *, final):
    ones = jnp.ones((1, D_MODEL), F32)
    row = lambda a: a.reshape(1, -1).astype(F32)

    o0 = SSM_WIDTH
    o1 = o0 + SSM_XBC
    o2 = o1 + SSM_HEADS
    o3 = o2 + 3 * FOX_WIDTH
    proj_cols = ((0, o0), (o0, o1), (o2, o2 + FOX_WIDTH), (o2 + FOX_WIDTH, o2 + 2 * FOX_WIDTH),
                 (o2 + 2 * FOX_WIDTH, o3))
    x1, ffn2_wi, ffn2_wo, wz, wxbc, wq, wkt, wv, wo = _ffn(
        x, row(ffn1_norm), ffn1_w_in.astype(BF16), ffn1_w_out.astype(BF16), ones,
        final=False,
        casts=((ffn2_w_in, ((0, 2 * D_FF),)), (ffn2_w_out, ((0, D_MODEL),)),
               (w_in, proj_cols), (w_out, ((0, D_MODEL),))))
    ws = jnp.concatenate(
        [w_in[:, o1:o2], w_in[:, o3:],
         jnp.zeros((D_MODEL, LANES - SSM_HEADS - FOX_HEADS), w_in.dtype)], axis=1)
    zs, xs, bc, q, kt3, v, small, qn2, kn2 = _inproj(
        x1, row(mix_norm), wz, wxbc, wq, wkt, wv, ws, conv_w.astype(F32), row(conv_b))

    bias_pad = _pad_lanes(dt_bias, f_bias)
    a_pad = _pad_lanes(-jnp.exp(a_log.astype(F32)), None)
    dskip_e = jnp.repeat(d_skip.astype(F32), HEAD_DIM).reshape(1, SSM_WIDTH)
    tri = jnp.asarray(np.tril(np.ones((SSD_T, SSD_T), np.float32)), dtype=BF16)
    expand_np = np.zeros((LANES, SSM_WIDTH), np.float32)
    for hd in range(SSM_HEADS):
        expand_np[hd, hd * HEAD_DIM:(hd + 1) * HEAD_DIM] = 1.0
    y_ssd, qaug, cumt, cmax_c, cmin_c = _ssd(zs, xs, bc, small, bias_pad, a_pad,
                            dskip_e, row(ssm_norm), tri, jnp.asarray(expand_np, dtype=BF16))

    o_fox = _fox(_kv_counts(qn2, kn2, cmax_c, cmin_c), q, kt3, v, qaug, cumt)

    mix = (y_ssd, o_fox, row(fox_norm), wo[:SSM_WIDTH], wo[SSM_WIDTH:])
    out, = _ffn(x1, row(ffn2_norm), ffn2_wi, ffn2_wo, row(final_g), final=final, mix=mix)
    return out


def kernel(x, ffn1_norm, ffn1_w_in, ffn1_w_out, mix_norm, w_in, conv_w, conv_b, dt_bias, a_log,
           d_skip, ssm_norm, f_bias, fox_norm, w_out, ffn2_norm, ffn2_w_in, ffn2_w_out, final_norm):
    b, s, d = x.shape
    depth = ffn1_norm.shape[0]
    outs = []
    for bi in range(b):
        xb = x[bi]
        for l in range(depth):
            xb = _layer(xb, ffn1_norm[l], ffn1_w_in[l], ffn1_w_out[l], mix_norm[l], w_in[l],
                        conv_w[l], conv_b[l], dt_bias[l], a_log[l], d_skip[l], ssm_norm[l],
                        f_bias[l], fox_norm[l], w_out[l], ffn2_norm[l], ffn2_w_in[l],
                        ffn2_w_out[l], final_norm, final=(l == depth - 1))
        outs.append(xb)
    return jnp.stack(outs, axis=0)
```

```python
import functools

import jax
import jax.numpy as jnp
import numpy as np
from jax import lax
from jax.experimental import pallas as pl
from jax.experimental.pallas import tpu as pltpu

F32 = jnp.float32
BF16 = jnp.bfloat16

D_MODEL = 1024
HEAD_DIM = 64
SSM_WIDTH = 512
SSM_HEADS = 8
SSM_GROUPS = 2
SSM_STATE = 128
CONV_WIDTH = 4
SSM_XBC = SSM_WIDTH + 2 * SSM_GROUPS * SSM_STATE
FOX_WIDTH = 512
FOX_HEADS = 8
D_FF = 2816
EPS = 1e-6

LANES = 128
VMEM_LIMIT = 56 * 1024 * 1024

FFN_TM = 1024
FFN_SUB = 512
FFN_TF = 256
PROJ_TM = 512
SSD_T = 128
SSD_STEP = 1024
ATT_TQ = 512
ATT_TK = 512
NEG_INF = float("-inf")
LOG2E = 1.4426950408889634
SKIP_LOG2 = -150.0
NORM_MARGIN = 1.05
FIXED_MAX_LOG2 = 64.0


def _rms(x, g):
    ms = jnp.mean(x * x, axis=-1, keepdims=True)
    return x * lax.rsqrt(ms + EPS) * g


def _silu(x):
    return x * (1.0 / (1.0 + jnp.exp(-x)))


def _softplus(x):
    return jnp.maximum(x, 0.0) + jnp.log1p(jnp.exp(-jnp.abs(x)))


def _split3(x):
    hi = x.astype(BF16).astype(F32)
    r = x - hi
    mid = r.astype(BF16).astype(F32)
    lo = (r - mid).astype(BF16).astype(F32)
    return hi, mid, lo


def _dot_f32_rhs(a_bf16, x):
    hi, mid, lo = _split3(x)
    d = lambda piece: jnp.dot(a_bf16, piece.astype(BF16), preferred_element_type=F32)
    return (d(lo) + d(mid)) + d(hi)


def _dot_f32_lhs(x, b_bf16):
    hi, mid, lo = _split3(x)
    d = lambda piece: jnp.dot(piece.astype(BF16), b_bf16, preferred_element_type=F32)
    return (d(lo) + d(mid)) + d(hi)


def _ffn_kernel(*refs, mix, final, cast_cols):
    if cast_cols:
        n_src = len(cast_cols)
        n_dst = sum(len(ranges) for ranges in cast_cols)
        n_in = len(refs) - 1 - n_src - n_dst
        dst_refs = iter(refs[n_in + n_src + 1:])
        for src_ref, ranges in zip(refs[n_in:n_in + n_src], cast_cols):
            for c0, c1 in ranges:
                next(dst_refs)[...] = src_ref[:, c0:c1].astype(BF16)
        refs = refs[:n_in] + (refs[n_in + n_src],)
    if mix:
        (x_ref, ys_ref, of_ref, mg_ref, wm1_ref, wm2_ref,
         g_ref, wg_ref, wu_ref, wo_ref, fg_ref, o_ref) = refs
    else:
        x_ref, g_ref, wg_ref, wu_ref, wo_ref, fg_ref, o_ref = refs
    nf = D_FF // FFN_TF
    for r0 in range(0, FFN_TM, FFN_SUB):
        rows = slice(r0, r0 + FFN_SUB)
        if mix:
            yf = _rms(of_ref[rows, :], mg_ref[...]).astype(BF16)
            x = (x_ref[rows, :]
                 + jnp.dot(ys_ref[rows, :], wm1_ref[...], preferred_element_type=F32)
                 + jnp.dot(yf, wm2_ref[...], preferred_element_type=F32))
        else:
            x = x_ref[rows, :]
        h = _rms(x, g_ref[...]).astype(BF16)
        o_ref[rows, :] = 2.0 * x

        def gate_up(f, h=h):
            cols = slice(f * FFN_TF, (f + 1) * FFN_TF)
            return (jnp.dot(h, wg_ref[:, cols], preferred_element_type=F32),
                    jnp.dot(h, wu_ref[:, cols], preferred_element_type=F32))

        pending = gate_up(0)
        for f in range(nf):
            gate, up = pending
            if f + 1 < nf:
                pending = gate_up(f + 1)
            act = (_silu(gate) * up).astype(BF16)
            o_ref[rows, :] += jnp.dot(act, wo_ref[f * FFN_TF:(f + 1) * FFN_TF, :],
                                      preferred_element_type=F32)
        r = 0.5 * o_ref[rows, :]
        if final:
            r = _rms(r, fg_ref[...])
        o_ref[rows, :] = r


def _ffn(x, g, w_in, w_out, fg, *, final, mix=None, casts=()):
    s = x.shape[0]
    tm = FFN_TM
    steps = s // tm
    row = lambda i: (i, 0)
    const = lambda i: (0, 0)
    resident = functools.partial(pl.BlockSpec, pipeline_mode=pl.Buffered(1))
    x_spec = pl.BlockSpec((tm, D_MODEL), row)
    vec_spec = pl.BlockSpec((1, D_MODEL), const)
    ffn_specs = [
        vec_spec,
        resident((D_MODEL, D_FF), lambda i: (0, 0)),
        resident((D_MODEL, D_FF), lambda i: (0, 1)),
        resident((D_FF, D_MODEL), const),
        vec_spec,
    ]
    ffn_args = (g, w_in, w_in, w_out, fg)
    if mix is None:
        in_specs = [x_spec] + ffn_specs
        args = (x,) + ffn_args
    else:
        ys, of, mg, wm1, wm2 = mix
        in_specs = [
            x_spec,
            pl.BlockSpec((tm, SSM_WIDTH), row),
            pl.BlockSpec((tm, FOX_WIDTH), row),
            pl.BlockSpec((1, FOX_WIDTH), const),
            resident((SSM_WIDTH, D_MODEL), const),
            resident((FOX_WIDTH, D_MODEL), const),
        ] + ffn_specs
        args = (x, ys, of, mg, wm1, wm2) + ffn_args
    out_specs = [pl.BlockSpec((tm, D_MODEL), row)]
    out_shape = [jax.ShapeDtypeStruct((s, D_MODEL), F32)]
    for w, ranges in casts:
        slab_rows = w.shape[0] // steps
        assert slab_rows * steps == w.shape[0] and slab_rows % 16 == 0, w.shape
        in_specs.append(pl.BlockSpec((slab_rows, w.shape[1]), row))
        for c0, c1 in ranges:
            out_specs.append(pl.BlockSpec((slab_rows, c1 - c0), row))
            out_shape.append(jax.ShapeDtypeStruct((w.shape[0], c1 - c0), BF16))
    return pl.pallas_call(
        functools.partial(_ffn_kernel, mix=mix is not None, final=final,
                          cast_cols=tuple(tuple(ranges) for _, ranges in casts)),
        grid=(steps,),
        in_specs=in_specs,
        out_specs=out_specs,
        out_shape=out_shape,
        compiler_params=pltpu.CompilerParams(
            dimension_semantics=("parallel",),
            vmem_limit_bytes=VMEM_LIMIT),
        name="ffn_final" if final else "ffn",
    )(*args, *(w for w, _ in casts))


def _inproj_kernel(x_ref, g_ref, wz_ref, wxbc_ref, wq_ref, wkt_ref, wv_ref, ws_ref,
                   ind_ref, indt_ref, cw_ref, cb_ref,
                   zs_ref, xs_ref, bc_ref, q_ref, kt_ref, v_ref, small_ref, qn_ref, kn_ref,
                   halo_ref):
    tm = PROJ_TM
    i = pl.program_id(0)

    @pl.when(i == 0)
    def _():
        halo_ref[...] = jnp.zeros_like(halo_ref)

    h = _rms(x_ref[...], g_ref[...]).astype(BF16)
    xb = jnp.dot(h, wxbc_ref[...], preferred_element_type=F32)
    zs_ref[...] = _silu(jnp.dot(h, wz_ref[...], preferred_element_type=F32))
    q = jnp.dot(h, wq_ref[...], preferred_element_type=F32)
    qb = (q * (HEAD_DIM ** -0.5 * LOG2E)).astype(BF16)
    q_ref[...] = qb
    kt = lax.dot_general(wkt_ref[...], h, (((0,), (1,)), ((), ())),
                         preferred_element_type=F32)
    ktb = kt.astype(BF16)
    kt_ref[0] = ktb
    v_ref[...] = jnp.dot(h, wv_ref[...], preferred_element_type=F32).astype(BF16)
    small_ref[...] = jnp.dot(h, ws_ref[...].astype(BF16), preferred_element_type=F32)
    qf = qb.astype(F32)
    qn = jnp.dot((qf * qf).astype(BF16), ind_ref[...], preferred_element_type=F32)
    qn_ref[0] = jnp.broadcast_to(jnp.max(qn, axis=0, keepdims=True), (8, LANES))
    kf = ktb.astype(F32)
    kn = jnp.dot(indt_ref[...], (kf * kf).astype(BF16), preferred_element_type=F32)
    kn_ref[0] = jnp.broadcast_to(jnp.max(kn, axis=1, keepdims=True), (16, LANES))
    prev = halo_ref[...]
    row8 = lax.broadcasted_iota(jnp.int32, (8, SSM_XBC), 0)
    conv = cb_ref[...] + xb * cw_ref[CONV_WIDTH - 1:CONV_WIDTH, :]
    for s in range(1, CONV_WIDTH):
        rolled = pltpu.roll(xb, s, axis=0)
        head = jnp.where(row8 < s, pltpu.roll(prev, s, axis=0), rolled[0:8])
        shifted = jnp.concatenate([head, rolled[8:]], axis=0)
        k = CONV_WIDTH - 1 - s
        conv = conv + shifted * cw_ref[k:k + 1, :]
    halo_ref[...] = xb[tm - 8:tm]
    u = _silu(conv)
    xs_ref[...] = u[:, :SSM_WIDTH]
    bc_ref[...] = u[:, SSM_WIDTH:].astype(BF16)


def _inproj(x, g, wz, wxbc, wq, wkt, wv, ws, cw, cb):
    s = x.shape[0]
    tm = PROJ_TM
    nb = s // tm
    const = lambda i: (0, 0)
    row = lambda i: (i, 0)
    ind_np = np.zeros((FOX_WIDTH, LANES), np.float32)
    for hd in range(FOX_HEADS):
        ind_np[hd * HEAD_DIM:(hd + 1) * HEAD_DIM, hd] = 1.0
    ind = jnp.asarray(ind_np, dtype=BF16)
    indt = jnp.asarray(ind_np.T[:16], dtype=BF16)
    return pl.pallas_call(
        _inproj_kernel,
        grid=(nb,),
        in_specs=[
            pl.BlockSpec((tm, D_MODEL), row),
            pl.BlockSpec((1, D_MODEL), const),
            pl.BlockSpec((D_MODEL, SSM_WIDTH), const),
            pl.BlockSpec((D_MODEL, SSM_XBC), const),
            pl.BlockSpec((D_MODEL, FOX_WIDTH), const),
            pl.BlockSpec((D_MODEL, FOX_WIDTH), const),
            pl.BlockSpec((D_MODEL, FOX_WIDTH), const),
            pl.BlockSpec((D_MODEL, LANES), const),
            pl.BlockSpec((FOX_WIDTH, LANES), const),
            pl.BlockSpec((16, FOX_WIDTH), const),
            pl.BlockSpec((CONV_WIDTH, SSM_XBC), const),
            pl.BlockSpec((1, SSM_XBC), const),
        ],
        out_specs=[
            pl.BlockSpec((tm, SSM_WIDTH), row),
            pl.BlockSpec((tm, SSM_WIDTH), row),
            pl.BlockSpec((tm, SSM_XBC - SSM_WIDTH), row),
            pl.BlockSpec((tm, FOX_WIDTH), row),
            pl.BlockSpec((1, FOX_WIDTH, tm), lambda i: (i, 0, 0)),
            pl.BlockSpec((tm, FOX_WIDTH), row),
            pl.BlockSpec((tm, LANES), row),
            pl.BlockSpec((1, 8, LANES), lambda i: (i, 0, 0)),
            pl.BlockSpec((1, 16, LANES), lambda i: (i, 0, 0)),
        ],
        out_shape=[
            jax.ShapeDtypeStruct((s, SSM_WIDTH), F32),
            jax.ShapeDtypeStruct((s, SSM_WIDTH), F32),
            jax.ShapeDtypeStruct((s, SSM_XBC - SSM_WIDTH), BF16),
            jax.ShapeDtypeStruct((s, FOX_WIDTH), BF16),
            jax.ShapeDtypeStruct((nb, FOX_WIDTH, tm), BF16),
            jax.ShapeDtypeStruct((s, FOX_WIDTH), BF16),
            jax.ShapeDtypeStruct((s, LANES), F32),
            jax.ShapeDtypeStruct((nb, 8, LANES), F32),
            jax.ShapeDtypeStruct((nb, 16, LANES), F32),
        ],
        scratch_shapes=[pltpu.VMEM((8, SSM_XBC), F32)],
        compiler_params=pltpu.CompilerParams(
            dimension_semantics=("arbitrary",),
            vmem_limit_bytes=VMEM_LIMIT),
        name="in_proj",
    )(x, g, wz, wxbc, wq, wkt, wv, ws, ind, indt, cw, cb)


def _ssd_kernel(zs_ref, xs_ref, bc_ref, small_ref, bias_ref, apad_ref,
                dskip_ref, ng_ref, tri_ref, expand_ref, place_ref,
                y_ref, qaug_ref, cumt_ref, cmax_ref, cmin_ref,
                state_ref, carry_ref):
    t_rows = SSD_T
    i = pl.program_id(0)

    @pl.when(i == 0)
    def _():
        state_ref[...] = jnp.zeros_like(state_ref)
        carry_ref[...] = jnp.zeros_like(carry_ref)

    chunks = range(SSD_STEP // t_rows)
    rows = [slice(c * t_rows, (c + 1) * t_rows) for c in chunks]
    gw = SSM_WIDTH // SSM_GROUPS
    hpg = SSM_HEADS // SSM_GROUPS
    groups = range(SSM_GROUPS)
    lane = lax.broadcasted_iota(jnp.int32, (t_rows, LANES), 1)
    is_dt = lane < SSM_HEADS
    lo = lane < HEAD_DIM

    sp, v = [], []
    for c in chunks:
        t = small_ref[rows[c], :] + bias_ref[...]
        sp.append(_softplus(jnp.where(is_dt, t, -t)))
        v.append(jnp.where(is_dt, sp[c] * apad_ref[...], -sp[c]))
    local = [_dot_f32_rhs(tri_ref[...], v[c]) for c in chunks]
    carry = carry_ref[...]
    cs, cst = [], []
    for c in chunks:
        cs.append(local[c] + carry)
        carry = jnp.where(lane[0:1, :] < SSM_HEADS, 0.0, cs[c][t_rows - 1:t_rows, :])
        cst.append(cs[c].T)
        cumt_ref[:, rows[c]] = cst[c][0:16, :]
        cmax_ref[c] = jnp.broadcast_to(jnp.max(cs[c], axis=0, keepdims=True), (8, LANES))
        cmin_ref[c] = jnp.broadcast_to(jnp.min(cs[c], axis=0, keepdims=True), (8, LANES))
    carry_ref[...] = carry

    placed = [[jnp.dot(piece.astype(BF16), place_ref[...], preferred_element_type=F32)
               for piece in _split3(cs[c] * LOG2E)] for c in chunks]
    expanded = [_dot_f32_lhs(jnp.concatenate([sp[c], cs[c]], axis=0), expand_ref[...])
                for c in chunks]
    lane4 = lax.broadcasted_iota(jnp.int32, (t_rows, FOX_WIDTH), 1) & (HEAD_DIM - 1)
    xs, ea_e, cd_e, xdt_b, wst = [], [], [], [], []
    for c in chunks:
        qaug_ref[rows[c], :] = jnp.where(
            lane4 == 0, placed[c][0],
            jnp.where(lane4 == 1, placed[c][1],
                      jnp.where(lane4 == 2, placed[c][2],
                                jnp.where(lane4 < 6, 1.0, 0.0)))).astype(BF16)
        dt_e = expanded[c][0:t_rows]
        cs_e = expanded[c][t_rows:2 * t_rows]
        ea_e.append(jnp.exp(cs_e))
        de_e = jnp.exp(cs_e[t_rows - 1:t_rows, :] - cs_e)
        cd_e.append(ea_e[c][t_rows - 1:t_rows, :])
        xs.append(xs_ref[rows[c], :])
        xdt = xs[c] * dt_e
        xdt_b.append(xdt.astype(BF16))
        wst.append((xdt * de_e).astype(BF16))

    def cgrp(c, g):
        return bc_ref[rows[c], SSM_GROUPS * SSM_STATE + g * SSM_STATE:
                      SSM_GROUPS * SSM_STATE + (g + 1) * SSM_STATE]

    def bgrp(c, g):
        return bc_ref[rows[c], g * SSM_STATE:(g + 1) * SSM_STATE]

    gmat = [[lax.dot_general(cgrp(c, g), bgrp(c, g), (((1,), (1,)), ((), ())),
                             preferred_element_type=F32) for g in groups]
            for c in chunks]
    r_i = lax.broadcasted_iota(jnp.int32, (t_rows, t_rows), 0)
    c_i = lax.broadcasted_iota(jnp.int32, (t_rows, t_rows), 1)
    causal = r_i >= c_i
    y_parts = []
    for c in chunks:
        parts = []
        for g in groups:
            for pr in range(hpg // 2):
                c0 = g * gw + pr * LANES
                xpair = xdt_b[c][:, c0:c0 + LANES]
                acc = None
                for hh in range(2):
                    h = g * hpg + pr * 2 + hh
                    seg = cs[c][:, h:h + 1] - cst[c][h:h + 1, :]
                    dec = jnp.exp(jnp.where(causal, seg, NEG_INF))
                    m = (gmat[c][g] * dec).astype(BF16)
                    xm = jnp.where(lo if hh == 0 else jnp.logical_not(lo), xpair,
                                   jnp.zeros_like(xpair))
                    part = jnp.dot(m, xm, preferred_element_type=F32)
                    acc = part if acc is None else acc + part
                parts.append(acc)
        y_parts.append(parts)

    upd = [[jnp.dot(bgrp(c, g).astype(F32).T.astype(BF16), wst[c][:, g * gw:(g + 1) * gw],
                    preferred_element_type=F32) for g in groups]
           for c in chunks]

    state = [state_ref[g] for g in groups]
    for c in chunks:
        for g in groups:
            y_off = jnp.dot(cgrp(c, g), state[g].astype(BF16), preferred_element_type=F32)
            for pr in range(hpg // 2):
                cols = slice(g * gw + pr * LANES, g * gw + (pr + 1) * LANES)
                y_parts[c][g * (hpg // 2) + pr] += (
                    y_off[:, pr * LANES:(pr + 1) * LANES] * ea_e[c][:, cols])
            state[g] = state[g] * cd_e[c][:, g * gw:(g + 1) * gw] + upd[c][g]
    for g in groups:
        state_ref[g] = state[g]

    for c in chunks:
        y = jnp.concatenate(y_parts[c], axis=1) + dskip_ref[...] * xs[c]
        y = y * zs_ref[rows[c], :]
        y_ref[rows[c], :] = _rms(y, ng_ref[...]).astype(BF16)


def _ssd(zs, xs, bc, small, bias_pad, a_pad, dskip_e, ng, tri, expand):
    s = zs.shape[0]
    t = SSD_T
    st = SSD_STEP
    nsub = st // t
    const = lambda i: (0, 0)
    row = lambda i: (i, 0)
    place_np = np.zeros((LANES, FOX_WIDTH), np.float32)
    for hd in range(FOX_HEADS):
        pair, odd = divmod(hd, 2)
        c0 = pair * LANES + (0 if odd else HEAD_DIM)
        place_np[FOX_HEADS + hd, c0:c0 + HEAD_DIM] = 1.0
    place = jnp.asarray(place_np, dtype=BF16)
    return pl.pallas_call(
        _ssd_kernel,
        grid=(s // st,),
        in_specs=[
            pl.BlockSpec((st, SSM_WIDTH), row),
            pl.BlockSpec((st, SSM_WIDTH), row),
            pl.BlockSpec((st, SSM_XBC - SSM_WIDTH), row),
            pl.BlockSpec((st, LANES), row),
            pl.BlockSpec((1, LANES), const),
            pl.BlockSpec((1, LANES), const),
            pl.BlockSpec((1, SSM_WIDTH), const),
            pl.BlockSpec((1, SSM_WIDTH), const),
            pl.BlockSpec((t, t), const),
            pl.BlockSpec((LANES, SSM_WIDTH), const),
            pl.BlockSpec((LANES, FOX_WIDTH), const),
        ],
        out_specs=[
            pl.BlockSpec((st, SSM_WIDTH), row),
            pl.BlockSpec((st, FOX_WIDTH), row),
            pl.BlockSpec((16, st), lambda i: (0, i)),
            pl.BlockSpec((nsub, 8, LANES), lambda i: (i, 0, 0)),
            pl.BlockSpec((nsub, 8, LANES), lambda i: (i, 0, 0)),
        ],
        out_shape=[
            jax.ShapeDtypeStruct((s, SSM_WIDTH), BF16),
            jax.ShapeDtypeStruct((s, FOX_WIDTH), BF16),
            jax.ShapeDtypeStruct((16, s), F32),
            jax.ShapeDtypeStruct((s // t, 8, LANES), F32),
            jax.ShapeDtypeStruct((s // t, 8, LANES), F32),
        ],
        scratch_shapes=[
            pltpu.VMEM((SSM_GROUPS, SSM_STATE, SSM_WIDTH // SSM_GROUPS), F32),
            pltpu.VMEM((1, LANES), F32),
        ],
        compiler_params=pltpu.CompilerParams(
            dimension_semantics=("arbitrary",),
            vmem_limit_bytes=VMEM_LIMIT),
        name="ssd",
    )(zs, xs, bc, small, bias_pad, a_pad, dskip_e, ng, tri, expand, place)


def _fox_kernel(nkv_ref, q_ref, kt_ref, v_ref, qaug_ref, cumt_ref, o_ref,
                kaug_ref, m_ref, l_ref, acc_ref):
    tq, tk = ATT_TQ, ATT_TK
    nb = kt_ref.shape[0]
    nq = pl.num_programs(1)
    p = pl.program_id(0)
    i = pl.program_id(1)

    @pl.when(i == 0)
    def _():
        row16 = lax.broadcasted_iota(jnp.int32, (16, tk), 0)
        zeros48 = jnp.zeros((48, tk), BF16)

        def build(j, carry):
            off = pl.multiple_of(j * tk, tk)
            kt = kt_ref[j]
            for hh in range(2):
                ck = cumt_ref[pl.ds(FOX_HEADS + 2 * p + hh, 1), pl.ds(off, tk)] * LOG2E
                hi, mid, lo = _split3(ck)
                bias = jnp.where(row16 < 3, 1.0,
                                 jnp.where(row16 == 3, -hi,
                                           jnp.where(row16 == 4, -mid,
                                                     jnp.where(row16 == 5, -lo, 0.0))))
                bias = bias.astype(BF16)
                if hh == 0:
                    kaug_ref[0, j, 0:64, :] = kt[0:64]
                    kaug_ref[0, j, 64:80, :] = bias
                    kaug_ref[0, j, 80:128, :] = zeros48
                else:
                    kaug_ref[1, j, 0:16, :] = bias
                    kaug_ref[1, j, 16:64, :] = zeros48
                    kaug_ref[1, j, 64:128, :] = kt[64:128]
            return carry

        lax.fori_loop(0, nb, build, 0)

    q = q_ref[...]
    lane = lax.broadcasted_iota(jnp.int32, (tq, LANES), 1)
    lo_half = lane < HEAD_DIM

    def lower_tri(n):
        return (lax.broadcasted_iota(jnp.int32, (n, n), 0)
                >= lax.broadcasted_iota(jnp.int32, (n, n), 1))

    aug = qaug_ref[...]
    qas = [jnp.where(lo_half, q, aug), jnp.where(lo_half, aug, q)]

    def logits(hh, j):
        return jnp.dot(qas[hh], kaug_ref[hh, j], preferred_element_type=F32)

    def vblock(j):
        return v_ref[pl.ds(pl.multiple_of(j * tk, tk), tk), :]

    def lane_fold(x):
        out = x[:, 0:LANES]
        for c in range(1, x.shape[1] // LANES):
            out = out + x[:, c * LANES:(c + 1) * LANES]
        return out

    def online_diag(hh):
        s = jnp.where(lower_tri(tq), logits(hh, i), NEG_INF)
        m0 = jnp.max(s, axis=1, keepdims=True)
        p0 = jnp.exp2(s - m0)
        m_ref[hh] = jnp.broadcast_to(m0, (tq, LANES))
        l_ref[hh] = lane_fold(p0)
        acc_ref[hh] = jnp.dot(p0.astype(BF16), vblock(i), preferred_element_type=F32)

    def online_step(hh, j):
        s = logits(hh, j)
        m_prev = m_ref[hh]
        m_new = jnp.maximum(m_prev, jnp.max(s, axis=1, keepdims=True))
        alpha = jnp.exp2(m_prev - m_new)
        pj = jnp.exp2(s - m_new[:, 0:1])
        l_ref[hh] = alpha * l_ref[hh] + lane_fold(pj)
        acc_ref[hh] = alpha * acc_ref[hh] + jnp.dot(
            pj.astype(BF16), vblock(j), preferred_element_type=F32)
        m_ref[hh] = m_new

    half = tq // 2

    def fixed_diag(heads):
        vb = vblock(i)
        tri = lower_tri(half)
        logit_pairs = [
            (jnp.dot(qas[hh][0:half], kaug_ref[hh, i, :, 0:half],
                     preferred_element_type=F32),
             jnp.dot(qas[hh][half:], kaug_ref[hh, i], preferred_element_type=F32))
            for hh in heads]
        for hh, (s_top, s_bot) in zip(heads, logit_pairs):
            p_top = jnp.exp2(jnp.where(tri, s_top, NEG_INF))
            p_bot = jnp.exp2(jnp.concatenate(
                [s_bot[:, 0:half], jnp.where(tri, s_bot[:, half:], NEG_INF)], axis=1))
            l_ref[hh, 0:half] = lane_fold(p_top)
            l_ref[hh, half:] = lane_fold(p_bot)
            acc_ref[hh, 0:half] = jnp.dot(p_top.astype(BF16), vb[0:half],
                                          preferred_element_type=F32)
            acc_ref[hh, half:] = jnp.dot(p_bot.astype(BF16), vb, preferred_element_type=F32)

    def fixed_steps(jobs):
        all_logits = [jnp.concatenate([logits(hh, j - (n - 1) + b) for b in range(n)], axis=1)
                      for hh, j, n in jobs]
        for (hh, j, n), s in zip(jobs, all_logits):
            pj = jnp.exp2(s)
            l_ref[hh] += lane_fold(pj)
            vn = v_ref[pl.ds(pl.multiple_of((j - (n - 1)) * tk, tk), n * tk), :]
            acc_ref[hh] += jnp.dot(pj.astype(BF16), vn, preferred_element_type=F32)

    codes = [nkv_ref[(2 * p + hh) * nq + i] for hh in range(2)]
    n_offs = [code >> 1 for code in codes]
    both_fixed = (codes[0] & codes[1] & 1) == 1
    joint_quads = jnp.where(both_fixed, jnp.minimum(n_offs[0] >> 2, n_offs[1] >> 2), 0)
    joint_pair = both_fixed & ((n_offs[0] & n_offs[1] & 2) == 2)
    joint_single = both_fixed & ((n_offs[0] & n_offs[1] & 1) == 1)
    pair_at = [i - 1 - 4 * (n >> 2) for n in n_offs]
    single_at = [i - n for n in n_offs]

    @pl.when(both_fixed)
    def _():
        fixed_diag((0, 1))

        def body(u, carry):
            fixed_steps([(0, i - 1 - 4 * u, 4), (1, i - 1 - 4 * u, 4)])
            return carry

        lax.fori_loop(0, joint_quads, body, 0)

        @pl.when(joint_pair)
        def _():
            fixed_steps([(0, pair_at[0], 2), (1, pair_at[1], 2)])

        @pl.when(joint_single)
        def _():
            fixed_steps([(0, single_at[0], 1), (1, single_at[1], 1)])

    for hh in range(2):
        n_off = n_offs[hh]
        fixed_ok = (codes[hh] & 1) == 1

        @pl.when(fixed_ok)
        def _(hh=hh, n_off=n_off):
            @pl.when(jnp.logical_not(both_fixed))
            def _():
                fixed_diag((hh,))

            def body(u, carry):
                fixed_steps([(hh, i - 1 - 4 * u, 4)])
                return carry

            lax.fori_loop(joint_quads, n_off >> 2, body, 0)

            @pl.when(((n_off & 2) == 2) & jnp.logical_not(joint_pair))
            def _():
                fixed_steps([(hh, pair_at[hh], 2)])

            @pl.when(((n_off & 1) == 1) & jnp.logical_not(joint_single))
            def _():
                fixed_steps([(hh, single_at[hh], 1)])

        @pl.when(jnp.logical_not(fixed_ok))
        def _(hh=hh, n_off=n_off):
            online_diag(hh)

            def body(jj, carry):
                online_step(hh, i - 1 - jj)
                return carry

            lax.fori_loop(0, n_off, body, 0)

    l0 = jnp.sum(l_ref[0], axis=1, keepdims=True)
    l1 = jnp.sum(l_ref[1], axis=1, keepdims=True)
    o_ref[...] = jnp.where(lo_half, acc_ref[0] / l0, acc_ref[1] / l1)


def _kv_counts(qn_tiles, kn_tiles, cmax_chunks, cmin_chunks):
    nq = qn_tiles.shape[0]
    per_q = cmax_chunks.shape[0] // nq
    qn = jnp.sqrt(qn_tiles[:, 0, :FOX_HEADS])
    kn = jnp.sqrt(jnp.max(kn_tiles[:, :FOX_HEADS, 0], axis=0))
    gate = slice(FOX_HEADS, 2 * FOX_HEADS)
    cmax = jnp.max(cmax_chunks[:, 0, gate].reshape(nq, per_q, FOX_HEADS), axis=1) * LOG2E
    cmin = jnp.min(cmin_chunks[:, 0, gate].reshape(nq, per_q, FOX_HEADS), axis=1) * LOG2E
    pmin = lax.cummin(cmin, axis=0)
    qk = NORM_MARGIN * qn * kn[None, :] + 1.0
    bound = (2.0 * qk + cmax)[:, None, :] - pmin[None, :, :]
    ii = lax.broadcasted_iota(jnp.int32, bound.shape, 0)
    jj = lax.broadcasted_iota(jnp.int32, bound.shape, 1)
    skip = (bound < SKIP_LOG2) & (jj < ii)
    jstar = jnp.max(jnp.where(skip, jj, -1), axis=1)
    n_off = jnp.arange(nq, dtype=jnp.int32)[:, None] - 1 - jstar
    pmin_prev = jnp.concatenate([jnp.full((1, FOX_HEADS), jnp.inf, F32), pmin[:-1]], axis=0)
    fixed_ok = (qk + jnp.maximum(cmax - pmin_prev, 0.0)) < FIXED_MAX_LOG2
    code = 2 * n_off + fixed_ok.astype(jnp.int32)
    return code.T.reshape(-1).astype(jnp.int32)


def _fox(nkv, q, kt3, v, qaug, cumt):
    s = q.shape[0]
    tq, tk = ATT_TQ, ATT_TK
    nb = s // tk
    grid_spec = pltpu.PrefetchScalarGridSpec(
        num_scalar_prefetch=1,
        grid=(FOX_HEADS // 2, s // tq),
        in_specs=[
            pl.BlockSpec((tq, LANES), lambda p, i, n: (i, p)),
            pl.BlockSpec((nb, LANES, tk), lambda p, i, n: (0, p, 0)),
            pl.BlockSpec((s, LANES), lambda p, i, n: (0, p)),
            pl.BlockSpec((tq, LANES), lambda p, i, n: (i, p)),
            pl.BlockSpec((16, s), lambda p, i, n: (0, 0)),
        ],
        out_specs=pl.BlockSpec((tq, LANES), lambda p, i, n: (i, p)),
        scratch_shapes=[
            pltpu.VMEM((2, nb, LANES, tk), BF16),
            pltpu.VMEM((2, tq, LANES), F32),
            pltpu.VMEM((2, tq, LANES), F32),
            pltpu.VMEM((2, tq, LANES), F32),
        ],
    )
    return pl.pallas_call(
        _fox_kernel,
        grid_spec=grid_spec,
        out_shape=jax.ShapeDtypeStruct((s, FOX_WIDTH), F32),
        compiler_params=pltpu.CompilerParams(
            dimension_semantics=("arbitrary", "arbitrary"),
            vmem_limit_bytes=VMEM_LIMIT),
        name="fox_attn",
    )(nkv, q, kt3, v, qaug, cumt)


def _pad_lanes(vec_dt, vec_f):
    out = jnp.zeros((1, LANES), F32)
    out = out.at[0, 0:SSM_HEADS].set(vec_dt.astype(F32))
    if vec_f is not None:
        out = out.at[0, SSM_HEADS:SSM_HEADS + FOX_HEADS].set(vec_f.astype(F32))
    return out


def _layer(x, ffn1_norm, ffn1_w_in, ffn1_w_out, mix_norm, w_in, conv_w, conv_b, dt_bias,
           a_log, d_skip, ssm_norm, f_bias, fox_norm, w_out, ffn2_norm, ffn2_w_in,
           ffn2_w_out, final_g, *, final):
    ones = jnp.ones((1, D_MODEL), F32)
    row = lambda a: a.reshape(1, -1).astype(F32)

    o0 = SSM_WIDTH
    o1 = o0 + SSM_XBC
    o2 = o1 + SSM_HEADS
    o3 = o2 + 3 * FOX_WIDTH
    proj_cols = ((0, o0), (o0, o1), (o2, o2 + FOX_WIDTH), (o2 + FOX_WIDTH, o2 + 2 * FOX_WIDTH),
                 (o2 + 2 * FOX_WIDTH, o3))
    x1, ffn2_wi, ffn2_wo, wz, wxbc, wq, wkt, wv, wo = _ffn(
        x, row(ffn1_norm), ffn1_w_in.astype(BF16), ffn1_w_out.astype(BF16), ones,
        final=False,
        casts=((ffn2_w_in, ((0, 2 * D_FF),)), (ffn2_w_out, ((0, D_MODEL),)),
               (w_in, proj_cols), (w_out, ((0, D_MODEL),))))
    ws = jnp.concatenate(
        [w_in[:, o1:o2], w_in[:, o3:],
         jnp.zeros((D_MODEL, LANES - SSM_HEADS - FOX_HEADS), w_in.dtype)], axis=1)
    zs, xs, bc, q, kt3, v, small, qn2, kn2 = _inproj(
        x1, row(mix_norm), wz, wxbc, wq, wkt, wv, ws, conv_w.astype(F32), row(conv_b))

    bias_pad = _pad_lanes(dt_bias, f_bias)
    a_pad = _pad_lanes(-jnp.exp(a_log.astype(F32)), None)
    dskip_e = jnp.repeat(d_skip.astype(F32), HEAD_DIM).reshape(1, SSM_WIDTH)
    tri = jnp.asarray(np.tril(np.ones((SSD_T, SSD_T), np.float32)), dtype=BF16)
    expand_np = np.zeros((LANES, SSM_WIDTH), np.float32)
    for hd in range(SSM_HEADS):
        expand_np[hd, hd * HEAD_DIM:(hd + 1) * HEAD_DIM] = 1.0
    y_ssd, qaug, cumt, cmax_c, cmin_c = _ssd(zs, xs, bc, small, bias_pad, a_pad,
                            dskip_e, row(ssm_norm), tri, jnp.asarray(expand_np, dtype=BF16))

    o_fox = _fox(_kv_counts(qn2, kn2, cmax_c, cmin_c), q, kt3, v, qaug, cumt)

    mix = (y_ssd, o_fox, row(fox_norm), wo[:SSM_WIDTH], wo[SSM_WIDTH:])
    out, = _ffn(x1, row(ffn2_norm), ffn2_wi, ffn2_wo, row(final_g), final=final, mix=mix)
    return out


def kernel(x, ffn1_norm, ffn1_w_in, ffn1_w_out, mix_norm, w_in, conv_w, conv_b, dt_bias, a_log,
           d_skip, ssm_norm, f_bias, fox_norm, w_out, ffn2_norm, ffn2_w_in, ffn2_w_out, final_norm):
    b, s, d = x.shape
    depth = ffn1_norm.shape[0]
    outs = []
    for bi in range(b):
        xb = x[bi]
        for l in range(depth):
            xb = _layer(xb, ffn1_norm[l], ffn1_w_in[l], ffn1_w_out[l], mix_norm[l], w_in[l],
                        conv_w[l], conv_b[l], dt_bias[l], a_log[l], d_skip[l], ssm_norm[l],
                        f_bias[l], fox_norm[l], w_out[l], ffn2_norm[l], ffn2_w_in[l],
                        ffn2_w_out[l], final_norm, final=(l == depth - 1))
        outs.append(xb)
    return jnp.stack(outs, axis=0)
```

```python
import functools

import jax
import jax.numpy as jnp
import numpy as np
from jax import lax
from jax.experimental import pallas as pl
from jax.experimental.pallas import tpu as pltpu

F32 = jnp.float32
BF16 = jnp.bfloat16

D_MODEL = 1024
HEAD_DIM = 64
SSM_WIDTH = 512
SSM_HEADS = 8
SSM_GROUPS = 2
SSM_STATE = 128
CONV_WIDTH = 4
SSM_XBC = SSM_WIDTH + 2 * SSM_GROUPS * SSM_STATE
FOX_WIDTH = 512
FOX_HEADS = 8
D_FF = 2816
EPS = 1e-6

LANES = 128
VMEM_LIMIT = 56 * 1024 * 1024

FFN_TM = 1024
FFN_TF = 256
PROJ_TM = 512
SSD_T = 128
SSD_STEP = 1024
ATT_TQ = 512
ATT_TK = 512
NEG_INF = float("-inf")
LOG2E = 1.4426950408889634
SKIP_LOG2 = -150.0
NORM_MARGIN = 1.05
FIXED_MAX_LOG2 = 64.0


def _rms(x, g):
    ms = jnp.mean(x * x, axis=-1, keepdims=True)
    return x * lax.rsqrt(ms + EPS) * g


def _silu(x):
    return x * (1.0 / (1.0 + jnp.exp(-x)))


def _softplus(x):
    return jnp.maximum(x, 0.0) + jnp.log1p(jnp.exp(-jnp.abs(x)))


def _split3(x):
    hi = x.astype(BF16).astype(F32)
    r = x - hi
    mid = r.astype(BF16).astype(F32)
    lo = (r - mid).astype(BF16).astype(F32)
    return hi, mid, lo


def _dot_f32_rhs(a_bf16, x):
    hi, mid, lo = _split3(x)
    d = lambda piece: jnp.dot(a_bf16, piece.astype(BF16), preferred_element_type=F32)
    return (d(lo) + d(mid)) + d(hi)


def _dot_f32_lhs(x, b_bf16):
    hi, mid, lo = _split3(x)
    d = lambda piece: jnp.dot(piece.astype(BF16), b_bf16, preferred_element_type=F32)
    return (d(lo) + d(mid)) + d(hi)


def _ffn_kernel(*refs, mix, final, cast_cols):
    if cast_cols:
        n_src = len(cast_cols)
        n_dst = sum(len(ranges) for ranges in cast_cols)
        n_in = len(refs) - 1 - n_src - n_dst
        dst_refs = iter(refs[n_in + n_src + 1:])
        for src_ref, ranges in zip(refs[n_in:n_in + n_src], cast_cols):
            for c0, c1 in ranges:
                next(dst_refs)[...] = src_ref[:, c0:c1].astype(BF16)
        refs = refs[:n_in] + (refs[n_in + n_src],)
    if mix:
        (x_ref, ys_ref, of_ref, mg_ref, wm1_ref, wm2_ref,
         g_ref, wg_ref, wu_ref, wo_ref, fg_ref, o_ref) = refs
        yf = _rms(of_ref[...], mg_ref[...]).astype(BF16)
        x = (x_ref[...]
             + jnp.dot(ys_ref[...], wm1_ref[...], preferred_element_type=F32)
             + jnp.dot(yf, wm2_ref[...], preferred_element_type=F32))
    else:
        x_ref, g_ref, wg_ref, wu_ref, wo_ref, fg_ref, o_ref = refs
        x = x_ref[...]
    h = _rms(x, g_ref[...]).astype(BF16)
    o_ref[...] = 2.0 * x
    nf = D_FF // FFN_TF

    def gate_up(f):
        cols = slice(f * FFN_TF, (f + 1) * FFN_TF)
        return (jnp.dot(h, wg_ref[:, cols], preferred_element_type=F32),
                jnp.dot(h, wu_ref[:, cols], preferred_element_type=F32))

    pending = gate_up(0)
    for f in range(nf):
        gate, up = pending
        if f + 1 < nf:
            pending = gate_up(f + 1)
        act = (_silu(gate) * up).astype(BF16)
        o_ref[...] += jnp.dot(act, wo_ref[f * FFN_TF:(f + 1) * FFN_TF, :],
                              preferred_element_type=F32)
    r = 0.5 * o_ref[...]
    if final:
        r = _rms(r, fg_ref[...])
    o_ref[...] = r


def _ffn(x, g, w_in, w_out, fg, *, final, mix=None, casts=()):
    s = x.shape[0]
    tm = FFN_TM
    steps = s // tm
    row = lambda i: (i, 0)
    const = lambda i: (0, 0)
    resident = functools.partial(pl.BlockSpec, pipeline_mode=pl.Buffered(1))
    x_spec = pl.BlockSpec((tm, D_MODEL), row)
    vec_spec = pl.BlockSpec((1, D_MODEL), const)
    ffn_specs = [
        vec_spec,
        resident((D_MODEL, D_FF), lambda i: (0, 0)),
        resident((D_MODEL, D_FF), lambda i: (0, 1)),
        resident((D_FF, D_MODEL), const),
        vec_spec,
    ]
    ffn_args = (g, w_in, w_in, w_out, fg)
    if mix is None:
        in_specs = [x_spec] + ffn_specs
        args = (x,) + ffn_args
    else:
        ys, of, mg, wm1, wm2 = mix
        in_specs = [
            x_spec,
            pl.BlockSpec((tm, SSM_WIDTH), row),
            pl.BlockSpec((tm, FOX_WIDTH), row),
            pl.BlockSpec((1, FOX_WIDTH), const),
            resident((SSM_WIDTH, D_MODEL), const),
            resident((FOX_WIDTH, D_MODEL), const),
        ] + ffn_specs
        args = (x, ys, of, mg, wm1, wm2) + ffn_args
    out_specs = [pl.BlockSpec((tm, D_MODEL), row)]
    out_shape = [jax.ShapeDtypeStruct((s, D_MODEL), F32)]
    for w, ranges in casts:
        slab_rows = w.shape[0] // steps
        assert slab_rows * steps == w.shape[0] and slab_rows % 16 == 0, w.shape
        in_specs.append(pl.BlockSpec((slab_rows, w.shape[1]), row))
        for c0, c1 in ranges:
            out_specs.append(pl.BlockSpec((slab_rows, c1 - c0), row))
            out_shape.append(jax.ShapeDtypeStruct((w.shape[0], c1 - c0), BF16))
    return pl.pallas_call(
        functools.partial(_ffn_kernel, mix=mix is not None, final=final,
                          cast_cols=tuple(tuple(ranges) for _, ranges in casts)),
        grid=(steps,),
        in_specs=in_specs,
        out_specs=out_specs,
        out_shape=out_shape,
        compiler_params=pltpu.CompilerParams(
            dimension_semantics=("parallel",),
            vmem_limit_bytes=VMEM_LIMIT),
        name="ffn_final" if final else "ffn",
    )(*args, *(w for w, _ in casts))


def _inproj_kernel(x_ref, g_ref, wz_ref, wxbc_ref, wq_ref, wkt_ref, wv_ref, ws_ref,
                   ind_ref, indt_ref, cw_ref, cb_ref,
                   zs_ref, xs_ref, bc_ref, q_ref, kt_ref, v_ref, small_ref, qn_ref, kn_ref,
                   halo_ref):
    tm = PROJ_TM
    i = pl.program_id(0)

    @pl.when(i == 0)
    def _():
        halo_ref[...] = jnp.zeros_like(halo_ref)

    h = _rms(x_ref[...], g_ref[...]).astype(BF16)
    xb = jnp.dot(h, wxbc_ref[...], preferred_element_type=F32)
    zs_ref[...] = _silu(jnp.dot(h, wz_ref[...], preferred_element_type=F32))
    q = jnp.dot(h, wq_ref[...], preferred_element_type=F32)
    qb = (q * (HEAD_DIM ** -0.5 * LOG2E)).astype(BF16)
    q_ref[...] = qb
    kt = lax.dot_general(wkt_ref[...], h, (((0,), (1,)), ((), ())),
                         preferred_element_type=F32)
    ktb = kt.astype(BF16)
    kt_ref[0] = ktb
    v_ref[...] = jnp.dot(h, wv_ref[...], preferred_element_type=F32).astype(BF16)
    small_ref[...] = jnp.dot(h, ws_ref[...].astype(BF16), preferred_element_type=F32)
    qf = qb.astype(F32)
    qn = jnp.dot((qf * qf).astype(BF16), ind_ref[...], preferred_element_type=F32)
    qn_ref[0] = jnp.broadcast_to(jnp.max(qn, axis=0, keepdims=True), (8, LANES))
    kf = ktb.astype(F32)
    kn = jnp.dot(indt_ref[...], (kf * kf).astype(BF16), preferred_element_type=F32)
    kn_ref[0] = jnp.broadcast_to(jnp.max(kn, axis=1, keepdims=True), (16, LANES))
    prev = halo_ref[...]
    row8 = lax.broadcasted_iota(jnp.int32, (8, SSM_XBC), 0)
    conv = cb_ref[...] + xb * cw_ref[CONV_WIDTH - 1:CONV_WIDTH, :]
    for s in range(1, CONV_WIDTH):
        rolled = pltpu.roll(xb, s, axis=0)
        head = jnp.where(row8 < s, pltpu.roll(prev, s, axis=0), rolled[0:8])
        shifted = jnp.concatenate([head, rolled[8:]], axis=0)
        k = CONV_WIDTH - 1 - s
        conv = conv + shifted * cw_ref[k:k + 1, :]
    halo_ref[...] = xb[tm - 8:tm]
    u = _silu(conv)
    xs_ref[...] = u[:, :SSM_WIDTH]
    bc_ref[...] = u[:, SSM_WIDTH:].astype(BF16)


def _inproj(x, g, wz, wxbc, wq, wkt, wv, ws, cw, cb):
    s = x.shape[0]
    tm = PROJ_TM
    nb = s // tm
    const = lambda i: (0, 0)
    row = lambda i: (i, 0)
    ind_np = np.zeros((FOX_WIDTH, LANES), np.float32)
    for hd in range(FOX_HEADS):
        ind_np[hd * HEAD_DIM:(hd + 1) * HEAD_DIM, hd] = 1.0
    ind = jnp.asarray(ind_np, dtype=BF16)
    indt = jnp.asarray(ind_np.T[:16], dtype=BF16)
    return pl.pallas_call(
        _inproj_kernel,
        grid=(nb,),
        in_specs=[
            pl.BlockSpec((tm, D_MODEL), row),
            pl.BlockSpec((1, D_MODEL), const),
            pl.BlockSpec((D_MODEL, SSM_WIDTH), const),
            pl.BlockSpec((D_MODEL, SSM_XBC), const),
            pl.BlockSpec((D_MODEL, FOX_WIDTH), const),
            pl.BlockSpec((D_MODEL, FOX_WIDTH), const),
            pl.BlockSpec((D_MODEL, FOX_WIDTH), const),
            pl.BlockSpec((D_MODEL, LANES), const),
            pl.BlockSpec((FOX_WIDTH, LANES), const),
            pl.BlockSpec((16, FOX_WIDTH), const),
            pl.BlockSpec((CONV_WIDTH, SSM_XBC), const),
            pl.BlockSpec((1, SSM_XBC), const),
        ],
        out_specs=[
            pl.BlockSpec((tm, SSM_WIDTH), row),
            pl.BlockSpec((tm, SSM_WIDTH), row),
            pl.BlockSpec((tm, SSM_XBC - SSM_WIDTH), row),
            pl.BlockSpec((tm, FOX_WIDTH), row),
            pl.BlockSpec((1, FOX_WIDTH, tm), lambda i: (i, 0, 0)),
            pl.BlockSpec((tm, FOX_WIDTH), row),
            pl.BlockSpec((tm, LANES), row),
            pl.BlockSpec((1, 8, LANES), lambda i: (i, 0, 0)),
            pl.BlockSpec((1, 16, LANES), lambda i: (i, 0, 0)),
        ],
        out_shape=[
            jax.ShapeDtypeStruct((s, SSM_WIDTH), F32),
            jax.ShapeDtypeStruct((s, SSM_WIDTH), F32),
            jax.ShapeDtypeStruct((s, SSM_XBC - SSM_WIDTH), BF16),
            jax.ShapeDtypeStruct((s, FOX_WIDTH), BF16),
            jax.ShapeDtypeStruct((nb, FOX_WIDTH, tm), BF16),
            jax.ShapeDtypeStruct((s, FOX_WIDTH), BF16),
            jax.ShapeDtypeStruct((s, LANES), F32),
            jax.ShapeDtypeStruct((nb, 8, LANES), F32),
            jax.ShapeDtypeStruct((nb, 16, LANES), F32),
        ],
        scratch_shapes=[pltpu.VMEM((8, SSM_XBC), F32)],
        compiler_params=pltpu.CompilerParams(
            dimension_semantics=("arbitrary",),
            vmem_limit_bytes=VMEM_LIMIT),
        name="in_proj",
    )(x, g, wz, wxbc, wq, wkt, wv, ws, ind, indt, cw, cb)


def _ssd_kernel(zs_ref, xs_ref, bc_ref, small_ref, bias_ref, apad_ref,
                dskip_ref, ng_ref, tri_ref, expand_ref, place_ref,
                y_ref, qaug_ref, cumt_ref, cmax_ref, cmin_ref,
                state_ref, carry_ref):
    t_rows = SSD_T
    i = pl.program_id(0)

    @pl.when(i == 0)
    def _():
        state_ref[...] = jnp.zeros_like(state_ref)
        carry_ref[...] = jnp.zeros_like(carry_ref)

    chunks = range(SSD_STEP // t_rows)
    rows = [slice(c * t_rows, (c + 1) * t_rows) for c in chunks]
    gw = SSM_WIDTH // SSM_GROUPS
    hpg = SSM_HEADS // SSM_GROUPS
    groups = range(SSM_GROUPS)
    lane = lax.broadcasted_iota(jnp.int32, (t_rows, LANES), 1)
    is_dt = lane < SSM_HEADS
    lo = lane < HEAD_DIM

    sp, v = [], []
    for c in chunks:
        t = small_ref[rows[c], :] + bias_ref[...]
        sp.append(_softplus(jnp.where(is_dt, t, -t)))
        v.append(jnp.where(is_dt, sp[c] * apad_ref[...], -sp[c]))
    local = [_dot_f32_rhs(tri_ref[...], v[c]) for c in chunks]
    carry = carry_ref[...]
    cs, cst = [], []
    for c in chunks:
        cs.append(local[c] + carry)
        carry = jnp.where(lane[0:1, :] < SSM_HEADS, 0.0, cs[c][t_rows - 1:t_rows, :])
        cst.append(cs[c].T)
        cumt_ref[:, rows[c]] = cst[c][0:16, :]
        cmax_ref[c] = jnp.broadcast_to(jnp.max(cs[c], axis=0, keepdims=True), (8, LANES))
        cmin_ref[c] = jnp.broadcast_to(jnp.min(cs[c], axis=0, keepdims=True), (8, LANES))
    carry_ref[...] = carry

    placed = [[jnp.dot(piece.astype(BF16), place_ref[...], preferred_element_type=F32)
               for piece in _split3(cs[c] * LOG2E)] for c in chunks]
    expanded = [_dot_f32_lhs(jnp.concatenate([sp[c], cs[c]], axis=0), expand_ref[...])
                for c in chunks]
    lane4 = lax.broadcasted_iota(jnp.int32, (t_rows, FOX_WIDTH), 1) & (HEAD_DIM - 1)
    xs, ea_e, cd_e, xdt_b, wst = [], [], [], [], []
    for c in chunks:
        qaug_ref[rows[c], :] = jnp.where(
            lane4 == 0, placed[c][0],
            jnp.where(lane4 == 1, placed[c][1],
                      jnp.where(lane4 == 2, placed[c][2],
                                jnp.where(lane4 < 6, 1.0, 0.0)))).astype(BF16)
        dt_e = expanded[c][0:t_rows]
        cs_e = expanded[c][t_rows:2 * t_rows]
        ea_e.append(jnp.exp(cs_e))
        de_e = jnp.exp(cs_e[t_rows - 1:t_rows, :] - cs_e)
        cd_e.append(ea_e[c][t_rows - 1:t_rows, :])
        xs.append(xs_ref[rows[c], :])
        xdt = xs[c] * dt_e
        xdt_b.append(xdt.astype(BF16))
        wst.append((xdt * de_e).astype(BF16))

    def cgrp(c, g):
        return bc_ref[rows[c], SSM_GROUPS * SSM_STATE + g * SSM_STATE:
                      SSM_GROUPS * SSM_STATE + (g + 1) * SSM_STATE]

    def bgrp(c, g):
        return bc_ref[rows[c], g * SSM_STATE:(g + 1) * SSM_STATE]

    gmat = [[lax.dot_general(cgrp(c, g), bgrp(c, g), (((1,), (1,)), ((), ())),
                             preferred_element_type=F32) for g in groups]
            for c in chunks]
    r_i = lax.broadcasted_iota(jnp.int32, (t_rows, t_rows), 0)
    c_i = lax.broadcasted_iota(jnp.int32, (t_rows, t_rows), 1)
    causal = r_i >= c_i
    y_parts = []
    for c in chunks:
        parts = []
        for g in groups:
            for pr in range(hpg // 2):
                c0 = g * gw + pr * LANES
                xpair = xdt_b[c][:, c0:c0 + LANES]
                acc = None
                for hh in range(2):
                    h = g * hpg + pr * 2 + hh
                    seg = cs[c][:, h:h + 1] - cst[c][h:h + 1, :]
                    dec = jnp.exp(jnp.where(causal, seg, NEG_INF))
                    m = (gmat[c][g] * dec).astype(BF16)
                    xm = jnp.where(lo if hh == 0 else jnp.logical_not(lo), xpair,
                                   jnp.zeros_like(xpair))
                    part = jnp.dot(m, xm, preferred_element_type=F32)
                    acc = part if acc is None else acc + part
                parts.append(acc)
        y_parts.append(parts)

    upd = [[jnp.dot(bgrp(c, g).astype(F32).T.astype(BF16), wst[c][:, g * gw:(g + 1) * gw],
                    preferred_element_type=F32) for g in groups]
           for c in chunks]

    state = [state_ref[g] for g in groups]
    for c in chunks:
        for g in groups:
            y_off = jnp.dot(cgrp(c, g), state[g].astype(BF16), preferred_element_type=F32)
            for pr in range(hpg // 2):
                cols = slice(g * gw + pr * LANES, g * gw + (pr + 1) * LANES)
                y_parts[c][g * (hpg // 2) + pr] += (
                    y_off[:, pr * LANES:(pr + 1) * LANES] * ea_e[c][:, cols])
            state[g] = state[g] * cd_e[c][:, g * gw:(g + 1) * gw] + upd[c][g]
    for g in groups:
        state_ref[g] = state[g]

    for c in chunks:
        y = jnp.concatenate(y_parts[c], axis=1) + dskip_ref[...] * xs[c]
        y = y * zs_ref[rows[c], :]
        y_ref[rows[c], :] = _rms(y, ng_ref[...]).astype(BF16)


def _ssd(zs, xs, bc, small, bias_pad, a_pad, dskip_e, ng, tri, expand):
    s = zs.shape[0]
    t = SSD_T
    st = SSD_STEP
    nsub = st // t
    const = lambda i: (0, 0)
    row = lambda i: (i, 0)
    place_np = np.zeros((LANES, FOX_WIDTH), np.float32)
    for hd in range(FOX_HEADS):
        pair, odd = divmod(hd, 2)
        c0 = pair * LANES + (0 if odd else HEAD_DIM)
        place_np[FOX_HEADS + hd, c0:c0 + HEAD_DIM] = 1.0
    place = jnp.asarray(place_np, dtype=BF16)
    return pl.pallas_call(
        _ssd_kernel,
        grid=(s // st,),
        in_specs=[
            pl.BlockSpec((st, SSM_WIDTH), row),
            pl.BlockSpec((st, SSM_WIDTH), row),
            pl.BlockSpec((st, SSM_XBC - SSM_WIDTH), row),
            pl.BlockSpec((st, LANES), row),
            pl.BlockSpec((1, LANES), const),
            pl.BlockSpec((1, LANES), const),
            pl.BlockSpec((1, SSM_WIDTH), const),
            pl.BlockSpec((1, SSM_WIDTH), const),
            pl.BlockSpec((t, t), const),
            pl.BlockSpec((LANES, SSM_WIDTH), const),
            pl.BlockSpec((LANES, FOX_WIDTH), const),
        ],
        out_specs=[
            pl.BlockSpec((st, SSM_WIDTH), row),
            pl.BlockSpec((st, FOX_WIDTH), row),
            pl.BlockSpec((16, st), lambda i: (0, i)),
            pl.BlockSpec((nsub, 8, LANES), lambda i: (i, 0, 0)),
            pl.BlockSpec((nsub, 8, LANES), lambda i: (i, 0, 0)),
        ],
        out_shape=[
            jax.ShapeDtypeStruct((s, SSM_WIDTH), BF16),
            jax.ShapeDtypeStruct((s, FOX_WIDTH), BF16),
            jax.ShapeDtypeStruct((16, s), F32),
            jax.ShapeDtypeStruct((s // t, 8, LANES), F32),
            jax.ShapeDtypeStruct((s // t, 8, LANES), F32),
        ],
        scratch_shapes=[
            pltpu.VMEM((SSM_GROUPS, SSM_STATE, SSM_WIDTH // SSM_GROUPS), F32),
            pltpu.VMEM((1, LANES), F32),
        ],
        compiler_params=pltpu.CompilerParams(
            dimension_semantics=("arbitrary",),
            vmem_limit_bytes=VMEM_LIMIT),
        name="ssd",
    )(zs, xs, bc, small, bias_pad, a_pad, dskip_e, ng, tri, expand, place)


def _fox_kernel(nkv_ref, q_ref, kt_ref, v_ref, qaug_ref, cumt_ref, o_ref,
                kaug_ref, m_ref, l_ref, acc_ref):
    tq, tk = ATT_TQ, ATT_TK
    nb = kt_ref.shape[0]
    nq = pl.num_programs(1)
    p = pl.program_id(0)
    i = pl.program_id(1)

    @pl.when(i == 0)
    def _():
        row16 = lax.broadcasted_iota(jnp.int32, (16, tk), 0)
        zeros48 = jnp.zeros((48, tk), BF16)

        def build(j, carry):
            off = pl.multiple_of(j * tk, tk)
            kt = kt_ref[j]
            for hh in range(2):
                ck = cumt_ref[pl.ds(FOX_HEADS + 2 * p + hh, 1), pl.ds(off, tk)] * LOG2E
                hi, mid, lo = _split3(ck)
                bias = jnp.where(row16 < 3, 1.0,
                                 jnp.where(row16 == 3, -hi,
                                           jnp.where(row16 == 4, -mid,
                                                     jnp.where(row16 == 5, -lo, 0.0))))
                bias = bias.astype(BF16)
                if hh == 0:
                    kaug_ref[0, j, 0:64, :] = kt[0:64]
                    kaug_ref[0, j, 64:80, :] = bias
                    kaug_ref[0, j, 80:128, :] = zeros48
                else:
                    kaug_ref[1, j, 0:16, :] = bias
                    kaug_ref[1, j, 16:64, :] = zeros48
                    kaug_ref[1, j, 64:128, :] = kt[64:128]
            return carry

        lax.fori_loop(0, nb, build, 0)

    q = q_ref[...]
    lane = lax.broadcasted_iota(jnp.int32, (tq, LANES), 1)
    lo_half = lane < HEAD_DIM
    r_i = lax.broadcasted_iota(jnp.int32, (tq, tk), 0)
    c_i = lax.broadcasted_iota(jnp.int32, (tq, tk), 1)
    causal = r_i >= c_i

    aug = qaug_ref[...]
    qas = [jnp.where(lo_half, q, aug), jnp.where(lo_half, aug, q)]

    def logits(hh, j):
        return jnp.dot(qas[hh], kaug_ref[hh, j], preferred_element_type=F32)

    def vblock(j):
        return v_ref[pl.ds(pl.multiple_of(j * tk, tk), tk), :]

    def lane_fold(x):
        out = x[:, 0:LANES]
        for c in range(1, x.shape[1] // LANES):
            out = out + x[:, c * LANES:(c + 1) * LANES]
        return out

    def online_diag(hh):
        s = jnp.where(causal, logits(hh, i), NEG_INF)
        m0 = jnp.max(s, axis=1, keepdims=True)
        p0 = jnp.exp2(s - m0)
        m_ref[hh] = jnp.broadcast_to(m0, (tq, LANES))
        l_ref[hh] = lane_fold(p0)
        acc_ref[hh] = jnp.dot(p0.astype(BF16), vblock(i), preferred_element_type=F32)

    def online_step(hh, j):
        s = logits(hh, j)
        m_prev = m_ref[hh]
        m_new = jnp.maximum(m_prev, jnp.max(s, axis=1, keepdims=True))
        alpha = jnp.exp2(m_prev - m_new)
        pj = jnp.exp2(s - m_new[:, 0:1])
        l_ref[hh] = alpha * l_ref[hh] + lane_fold(pj)
        acc_ref[hh] = alpha * acc_ref[hh] + jnp.dot(
            pj.astype(BF16), vblock(j), preferred_element_type=F32)
        m_ref[hh] = m_new

    half = tq // 2

    def fixed_diag(heads):
        vb = vblock(i)
        logit_pairs = [
            (jnp.dot(qas[hh][0:half], kaug_ref[hh, i, :, 0:half],
                     preferred_element_type=F32),
             jnp.dot(qas[hh][half:], kaug_ref[hh, i], preferred_element_type=F32))
            for hh in heads]
        for hh, (s_top, s_bot) in zip(heads, logit_pairs):
            p_top = jnp.exp2(jnp.where(causal[0:half, 0:half], s_top, NEG_INF))
            p_bot = jnp.exp2(jnp.where(causal[half:, :], s_bot, NEG_INF))
            l_ref[hh, 0:half] = lane_fold(p_top)
            l_ref[hh, half:] = lane_fold(p_bot)
            acc_ref[hh, 0:half] = jnp.dot(p_top.astype(BF16), vb[0:half],
                                          preferred_element_type=F32)
            acc_ref[hh, half:] = jnp.dot(p_bot.astype(BF16), vb, preferred_element_type=F32)

    def fixed_steps(jobs):
        all_logits = [jnp.concatenate([logits(hh, j - (n - 1) + b) for b in range(n)], axis=1)
                      for hh, j, n in jobs]
        for (hh, j, n), s in zip(jobs, all_logits):
            pj = jnp.exp2(s)
            l_ref[hh] += lane_fold(pj)
            vn = v_ref[pl.ds(pl.multiple_of((j - (n - 1)) * tk, tk), n * tk), :]
            acc_ref[hh] += jnp.dot(pj.astype(BF16), vn, preferred_element_type=F32)

    codes = [nkv_ref[(2 * p + hh) * nq + i] for hh in range(2)]
    n_offs = [code >> 1 for code in codes]
    both_fixed = (codes[0] & codes[1] & 1) == 1
    joint_quads = jnp.where(both_fixed, jnp.minimum(n_offs[0] >> 2, n_offs[1] >> 2), 0)
    joint_pair = both_fixed & ((n_offs[0] & n_offs[1] & 2) == 2)
    joint_single = both_fixed & ((n_offs[0] & n_offs[1] & 1) == 1)
    pair_at = [i - 1 - 4 * (n >> 2) for n in n_offs]
    single_at = [i - n for n in n_offs]
    mixed = [both_fixed & jnp.logical_not(joint_pair) & ((n_offs[a] >> 2) > joint_quads)
             & ((n_offs[1 - a] & 2) == 2) for a in range(2)]

    @pl.when(both_fixed)
    def _():
        fixed_diag((0, 1))

        def body(u, carry):
            fixed_steps([(0, i - 1 - 4 * u, 4), (1, i - 1 - 4 * u, 4)])
            return carry

        lax.fori_loop(0, joint_quads, body, 0)

        for a in range(2):
            @pl.when(mixed[a])
            def _(a=a):
                fixed_steps([(a, i - 1 - 4 * joint_quads, 4), (1 - a, pair_at[1 - a], 2)])

        @pl.when(joint_pair)
        def _():
            fixed_steps([(0, pair_at[0], 2), (1, pair_at[1], 2)])

        @pl.when(joint_single)
        def _():
            fixed_steps([(0, single_at[0], 1), (1, single_at[1], 1)])

    for hh in range(2):
        n_off = n_offs[hh]
        fixed_ok = (codes[hh] & 1) == 1

        @pl.when(fixed_ok)
        def _(hh=hh, n_off=n_off):
            @pl.when(jnp.logical_not(both_fixed))
            def _():
                fixed_diag((hh,))

            def body(u, carry):
                fixed_steps([(hh, i - 1 - 4 * u, 4)])
                return carry

            lax.fori_loop(joint_quads + mixed[hh].astype(jnp.int32), n_off >> 2, body, 0)

            @pl.when(((n_off & 2) == 2) & jnp.logical_not(joint_pair | mixed[1 - hh]))
            def _():
                fixed_steps([(hh, pair_at[hh], 2)])

            @pl.when(((n_off & 1) == 1) & jnp.logical_not(joint_single))
            def _():
                fixed_steps([(hh, single_at[hh], 1)])

        @pl.when(jnp.logical_not(fixed_ok))
        def _(hh=hh, n_off=n_off):
            online_diag(hh)

            def body(jj, carry):
                online_step(hh, i - 1 - jj)
                return carry

            lax.fori_loop(0, n_off, body, 0)

    l0 = jnp.sum(l_ref[0], axis=1, keepdims=True)
    l1 = jnp.sum(l_ref[1], axis=1, keepdims=True)
    o_ref[...] = jnp.where(lo_half, acc_ref[0] / l0, acc_ref[1] / l1)


def _kv_counts(qn_tiles, kn_tiles, cmax_chunks, cmin_chunks):
    nq = qn_tiles.shape[0]
    per_q = cmax_chunks.shape[0] // nq
    qn = jnp.sqrt(qn_tiles[:, 0, :FOX_HEADS])
    kn = jnp.sqrt(jnp.max(kn_tiles[:, :FOX_HEADS, 0], axis=0))
    gate = slice(FOX_HEADS, 2 * FOX_HEADS)
    cmax = jnp.max(cmax_chunks[:, 0, gate].reshape(nq, per_q, FOX_HEADS), axis=1) * LOG2E
    cmin = jnp.min(cmin_chunks[:, 0, gate].reshape(nq, per_q, FOX_HEADS), axis=1) * LOG2E
    pmin = lax.cummin(cmin, axis=0)
    qk = NORM_MARGIN * qn * kn[None, :] + 1.0
    bound = (2.0 * qk + cmax)[:, None, :] - pmin[None, :, :]
    ii = lax.broadcasted_iota(jnp.int32, bound.shape, 0)
    jj = lax.broadcasted_iota(jnp.int32, bound.shape, 1)
    skip = (bound < SKIP_LOG2) & (jj < ii)
    jstar = jnp.max(jnp.where(skip, jj, -1), axis=1)
    n_off = jnp.arange(nq, dtype=jnp.int32)[:, None] - 1 - jstar
    pmin_prev = jnp.concatenate([jnp.full((1, FOX_HEADS), jnp.inf, F32), pmin[:-1]], axis=0)
    fixed_ok = (qk + jnp.maximum(cmax - pmin_prev, 0.0)) < FIXED_MAX_LOG2
    code = 2 * n_off + fixed_ok.astype(jnp.int32)
    return code.T.reshape(-1).astype(jnp.int32)


def _fox(nkv, q, kt3, v, qaug, cumt):
    s = q.shape[0]
    tq, tk = ATT_TQ, ATT_TK
    nb = s // tk
    grid_spec = pltpu.PrefetchScalarGridSpec(
        num_scalar_prefetch=1,
        grid=(FOX_HEADS // 2, s // tq),
        in_specs=[
            pl.BlockSpec((tq, LANES), lambda p, i, n: (i, p)),
            pl.BlockSpec((nb, LANES, tk), lambda p, i, n: (0, p, 0)),
            pl.BlockSpec((s, LANES), lambda p, i, n: (0, p)),
            pl.BlockSpec((tq, LANES), lambda p, i, n: (i, p)),
            pl.BlockSpec((16, s), lambda p, i, n: (0, 0)),
        ],
        out_specs=pl.BlockSpec((tq, LANES), lambda p, i, n: (i, p)),
        scratch_shapes=[
            pltpu.VMEM((2, nb, LANES, tk), BF16),
            pltpu.VMEM((2, tq, LANES), F32),
            pltpu.VMEM((2, tq, LANES), F32),
            pltpu.VMEM((2, tq, LANES), F32),
        ],
    )
    return pl.pallas_call(
        _fox_kernel,
        grid_spec=grid_spec,
        out_shape=jax.ShapeDtypeStruct((s, FOX_WIDTH), F32),
        compiler_params=pltpu.CompilerParams(
            dimension_semantics=("arbitrary", "arbitrary"),
            vmem_limit_bytes=VMEM_LIMIT),
        name="fox_attn",
    )(nkv, q, kt3, v, qaug, cumt)


def _pad_lanes(vec_dt, vec_f):
    out = jnp.zeros((1, LANES), F32)
    out = out.at[0, 0:SSM_HEADS].set(vec_dt.astype(F32))
    if vec_f is not None:
        out = out.at[0, SSM_HEADS:SSM_HEADS + FOX_HEADS].set(vec_f.astype(F32))
    return out


def _layer(x, ffn1_norm, ffn1_w_in, ffn1_w_out, mix_norm, w_in, conv_w, conv_b, dt_bias,
           a_log, d_skip, ssm_norm, f_bias, fox_norm, w_out, ffn2_norm, ffn2_w_in,
           ffn2_w_out, final_g, *, final):
    ones = jnp.ones((1, D_MODEL), F32)
    row = lambda a: a.reshape(1, -1).astype(F32)

    o0 = SSM_WIDTH
    o1 = o0 + SSM_XBC
    o2 = o1 + SSM_HEADS
    o3 = o2 + 3 * FOX_WIDTH
    proj_cols = ((0, o0), (o0, o1), (o2, o2 + FOX_WIDTH), (o2 + FOX_WIDTH, o2 + 2 * FOX_WIDTH),
                 (o2 + 2 * FOX_WIDTH, o3))
    x1, ffn2_wi, ffn2_wo, wz, wxbc, wq, wkt, wv, wo = _ffn(
        x, row(ffn1_norm), ffn1_w_in.astype(BF16), ffn1_w_out.astype(BF16), ones,
        final=False,
        casts=((ffn2_w_in, ((0, 2 * D_FF),)), (ffn2_w_out, ((0, D_MODEL),)),
               (w_in, proj_cols), (w_out, ((0, D_MODEL),))))
    ws = jnp.concatenate(
        [w_in[:, o1:o2], w_in[:, o3:],
         jnp.zeros((D_MODEL, LANES - SSM_HEADS - FOX_HEADS), w_in.dtype)], axis=1)
    zs, xs, bc, q, kt3, v, small, qn2, kn2 = _inproj(
        x1, row(mix_norm), wz, wxbc, wq, wkt, wv, ws, conv_w.astype(F32), row(conv_b))

    bias_pad = _pad_lanes(dt_bias, f_bias)
    a_pad = _pad_lanes(-jnp.exp(a_log.astype(F32)), None)
    dskip_e = jnp.repeat(d_skip.astype(F32), HEAD_DIM).reshape(1, SSM_WIDTH)
    tri = jnp.asarray(np.tril(np.ones((SSD_T, SSD_T), np.float32)), dtype=BF16)
    expand_np = np.zeros((LANES, SSM_WIDTH), np.float32)
    for hd in range(SSM_HEADS):
        expand_np[hd, hd * HEAD_DIM:(hd + 1) * HEAD_DIM] = 1.0
    y_ssd, qaug, cumt, cmax_c, cmin_c = _ssd(zs, xs, bc, small, bias_pad, a_pad,
                            dskip_e, row(ssm_norm), tri, jnp.asarray(expand_np, dtype=BF16))

    o_fox = _fox(_kv_counts(qn2, kn2, cmax_c, cmin_c), q, kt3, v, qaug, cumt)

    mix = (y_ssd, o_fox, row(fox_norm), wo[:SSM_WIDTH], wo[SSM_WIDTH:])
    out, = _ffn(x1, row(ffn2_norm), ffn2_wi, ffn2_wo, row(final_g), final=final, mix=mix)
    return out


def kernel(x, ffn1_norm, ffn1_w_in, ffn1_w_out, mix_norm, w_in, conv_w, conv_b, dt_bias, a_log,
           d_skip, ssm_norm, f_bias, fox_norm, w_out, ffn2_norm, ffn2_w_in, ffn2_w_out, final_norm):
    b, s, d = x.shape
    depth = ffn1_norm.shape[0]
    outs = []
    for bi in range(b):
        xb = x[bi]
        for l in range(depth):
            xb = _layer(xb, ffn1_norm[l], ffn1_w_in[l], ffn1_w_out[l], mix_norm[l], w_in[l],
                        conv_w[l], conv_b[l], dt_bias[l], a_log[l], d_skip[l], ssm_norm[l],
                        f_bias[l], fox_norm[l], w_out[l], ffn2_norm[l], ffn2_w_in[l],
                        ffn2_w_out[l], final_norm, final=(l == depth - 1))
        outs.append(xb)
    return jnp.stack(outs, axis=0)
```

```python
import functools

import jax
import jax.numpy as jnp
import numpy as np
from jax import lax
from jax.experimental import pallas as pl
from jax.experimental.pallas import tpu as pltpu

F32 = jnp.float32
BF16 = jnp.bfloat16

D_MODEL = 1024
HEAD_DIM = 64
SSM_WIDTH = 512
SSM_HEADS = 8
SSM_GROUPS = 2
SSM_STATE = 128
CONV_WIDTH = 4
SSM_XBC = SSM_WIDTH + 2 * SSM_GROUPS * SSM_STATE
FOX_WIDTH = 512
FOX_HEADS = 8
D_FF = 2816
EPS = 1e-6

LANES = 128
VMEM_LIMIT = 56 * 1024 * 1024

FFN_TM = 1024
FFN_TF = 256
PROJ_TM = 512
SSD_T = 128
SSD_STEP = 1024
ATT_TQ = 512
ATT_TK = 512
NEG_INF = float("-inf")
LOG2E = 1.4426950408889634
SKIP_LOG2 = -150.0
NORM_MARGIN = 1.05
FIXED_MAX_LOG2 = 64.0


def _rms(x, g):
    ms = jnp.mean(x * x, axis=-1, keepdims=True)
    return x * lax.rsqrt(ms + EPS) * g


def _silu(x):
    h = 0.5 * x
    return h + h * jnp.tanh(h)


def _softplus(x):
    return jnp.maximum(x, 0.0) + jnp.log1p(jnp.exp(-jnp.abs(x)))


def _split3(x):
    hi = x.astype(BF16).astype(F32)
    r = x - hi
    mid = r.astype(BF16).astype(F32)
    lo = (r - mid).astype(BF16).astype(F32)
    return hi, mid, lo


def _dot_f32_rhs(a_bf16, x):
    hi, mid, lo = _split3(x)
    d = lambda piece: jnp.dot(a_bf16, piece.astype(BF16), preferred_element_type=F32)
    return (d(lo) + d(mid)) + d(hi)


def _dot_f32_lhs(x, b_bf16):
    hi, mid, lo = _split3(x)
    d = lambda piece: jnp.dot(piece.astype(BF16), b_bf16, preferred_element_type=F32)
    return (d(lo) + d(mid)) + d(hi)


def _ffn_kernel(*refs, mix, final, cast_cols):
    if cast_cols:
        n_src = len(cast_cols)
        n_dst = sum(len(ranges) for ranges in cast_cols)
        n_in = len(refs) - 1 - n_src - n_dst
        dst_refs = iter(refs[n_in + n_src + 1:])
        for src_ref, ranges in zip(refs[n_in:n_in + n_src], cast_cols):
            for c0, c1 in ranges:
                next(dst_refs)[...] = src_ref[:, c0:c1].astype(BF16)
        refs = refs[:n_in] + (refs[n_in + n_src],)
    if mix:
        (x_ref, ys_ref, of_ref, mg_ref, wm1_ref, wm2_ref,
         g_ref, wg_ref, wu_ref, wo_ref, fg_ref, o_ref) = refs
        yf = _rms(of_ref[...], mg_ref[...]).astype(BF16)
        x = (x_ref[...]
             + jnp.dot(ys_ref[...], wm1_ref[...], preferred_element_type=F32)
             + jnp.dot(yf, wm2_ref[...], preferred_element_type=F32))
    else:
        x_ref, g_ref, wg_ref, wu_ref, wo_ref, fg_ref, o_ref = refs
        x = x_ref[...]
    h = _rms(x, g_ref[...]).astype(BF16)
    o_ref[...] = 2.0 * x
    nf = D_FF // FFN_TF

    def gate_up(f):
        cols = slice(f * FFN_TF, (f + 1) * FFN_TF)
        return (jnp.dot(h, wg_ref[:, cols], preferred_element_type=F32),
                jnp.dot(h, wu_ref[:, cols], preferred_element_type=F32))

    pending = gate_up(0)
    for f in range(nf):
        gate, up = pending
        if f + 1 < nf:
            pending = gate_up(f + 1)
        act = (_silu(gate) * up).astype(BF16)
        o_ref[...] += jnp.dot(act, wo_ref[f * FFN_TF:(f + 1) * FFN_TF, :],
                              preferred_element_type=F32)
    r = 0.5 * o_ref[...]
    if final:
        r = _rms(r, fg_ref[...])
    o_ref[...] = r


def _ffn(x, g, w_in, w_out, fg, *, final, mix=None, casts=()):
    s = x.shape[0]
    tm = FFN_TM
    steps = s // tm
    row = lambda i: (i, 0)
    const = lambda i: (0, 0)
    resident = functools.partial(pl.BlockSpec, pipeline_mode=pl.Buffered(1))
    x_spec = pl.BlockSpec((tm, D_MODEL), row)
    vec_spec = pl.BlockSpec((1, D_MODEL), const)
    ffn_specs = [
        vec_spec,
        resident((D_MODEL, D_FF), lambda i: (0, 0)),
        resident((D_MODEL, D_FF), lambda i: (0, 1)),
        resident((D_FF, D_MODEL), const),
        vec_spec,
    ]
    ffn_args = (g, w_in, w_in, w_out, fg)
    if mix is None:
        in_specs = [x_spec] + ffn_specs
        args = (x,) + ffn_args
    else:
        ys, of, mg, wm1, wm2 = mix
        in_specs = [
            x_spec,
            pl.BlockSpec((tm, SSM_WIDTH), row),
            pl.BlockSpec((tm, FOX_WIDTH), row),
            pl.BlockSpec((1, FOX_WIDTH), const),
            resident((SSM_WIDTH, D_MODEL), const),
            resident((FOX_WIDTH, D_MODEL), const),
        ] + ffn_specs
        args = (x, ys, of, mg, wm1, wm2) + ffn_args
    out_specs = [pl.BlockSpec((tm, D_MODEL), row)]
    out_shape = [jax.ShapeDtypeStruct((s, D_MODEL), F32)]
    for w, ranges in casts:
        slab_rows = w.shape[0] // steps
        assert slab_rows * steps == w.shape[0] and slab_rows % 16 == 0, w.shape
        in_specs.append(pl.BlockSpec((slab_rows, w.shape[1]), row))
        for c0, c1 in ranges:
            out_specs.append(pl.BlockSpec((slab_rows, c1 - c0), row))
            out_shape.append(jax.ShapeDtypeStruct((w.shape[0], c1 - c0), BF16))
    return pl.pallas_call(
        functools.partial(_ffn_kernel, mix=mix is not None, final=final,
                          cast_cols=tuple(tuple(ranges) for _, ranges in casts)),
        grid=(steps,),
        in_specs=in_specs,
        out_specs=out_specs,
        out_shape=out_shape,
        compiler_params=pltpu.CompilerParams(
            dimension_semantics=("parallel",),
            vmem_limit_bytes=VMEM_LIMIT),
        name="ffn_final" if final else "ffn",
    )(*args, *(w for w, _ in casts))


def _inproj_kernel(x_ref, g_ref, wz_ref, wxbc_ref, wq_ref, wkt_ref, wv_ref, ws_ref,
                   ind_ref, indt_ref, cw_ref, cb_ref,
                   zs_ref, xs_ref, bc_ref, q_ref, kt_ref, v_ref, small_ref, qn_ref, kn_ref,
                   halo_ref):
    tm = PROJ_TM
    i = pl.program_id(0)

    @pl.when(i == 0)
    def _():
        halo_ref[...] = jnp.zeros_like(halo_ref)

    h = _rms(x_ref[...], g_ref[...]).astype(BF16)
    xb = jnp.dot(h, wxbc_ref[...], preferred_element_type=F32)
    zs_ref[...] = _silu(jnp.dot(h, wz_ref[...], preferred_element_type=F32))
    q = jnp.dot(h, wq_ref[...], preferred_element_type=F32)
    qb = (q * (HEAD_DIM ** -0.5 * LOG2E)).astype(BF16)
    q_ref[...] = qb
    kt = lax.dot_general(wkt_ref[...], h, (((0,), (1,)), ((), ())),
                         preferred_element_type=F32)
    ktb = kt.astype(BF16)
    kt_ref[0] = ktb
    v_ref[...] = jnp.dot(h, wv_ref[...], preferred_element_type=F32).astype(BF16)
    small_ref[...] = jnp.dot(h, ws_ref[...].astype(BF16), preferred_element_type=F32)
    qf = qb.astype(F32)
    qn = jnp.dot((qf * qf).astype(BF16), ind_ref[...], preferred_element_type=F32)
    qn_ref[0] = jnp.broadcast_to(jnp.max(qn, axis=0, keepdims=True), (8, LANES))
    kf = ktb.astype(F32)
    kn = jnp.dot(indt_ref[...], (kf * kf).astype(BF16), preferred_element_type=F32)
    kn_ref[0] = jnp.broadcast_to(jnp.max(kn, axis=1, keepdims=True), (16, LANES))
    prev = halo_ref[...]
    row8 = lax.broadcasted_iota(jnp.int32, (8, SSM_XBC), 0)
    conv = cb_ref[...] + xb * cw_ref[CONV_WIDTH - 1:CONV_WIDTH, :]
    for s in range(1, CONV_WIDTH):
        rolled = pltpu.roll(xb, s, axis=0)
        head = jnp.where(row8 < s, pltpu.roll(prev, s, axis=0), rolled[0:8])
        shifted = jnp.concatenate([head, rolled[8:]], axis=0)
        k = CONV_WIDTH - 1 - s
        conv = conv + shifted * cw_ref[k:k + 1, :]
    halo_ref[...] = xb[tm - 8:tm]
    u = _silu(conv)
    xs_ref[...] = u[:, :SSM_WIDTH]
    bc_ref[...] = u[:, SSM_WIDTH:].astype(BF16)


def _inproj(x, g, wz, wxbc, wq, wkt, wv, ws, cw, cb):
    s = x.shape[0]
    tm = PROJ_TM
    nb = s // tm
    const = lambda i: (0, 0)
    row = lambda i: (i, 0)
    ind_np = np.zeros((FOX_WIDTH, LANES), np.float32)
    for hd in range(FOX_HEADS):
        ind_np[hd * HEAD_DIM:(hd + 1) * HEAD_DIM, hd] = 1.0
    ind = jnp.asarray(ind_np, dtype=BF16)
    indt = jnp.asarray(ind_np.T[:16], dtype=BF16)
    return pl.pallas_call(
        _inproj_kernel,
        grid=(nb,),
        in_specs=[
            pl.BlockSpec((tm, D_MODEL), row),
            pl.BlockSpec((1, D_MODEL), const),
            pl.BlockSpec((D_MODEL, SSM_WIDTH), const),
            pl.BlockSpec((D_MODEL, SSM_XBC), const),
            pl.BlockSpec((D_MODEL, FOX_WIDTH), const),
            pl.BlockSpec((D_MODEL, FOX_WIDTH), const),
            pl.BlockSpec((D_MODEL, FOX_WIDTH), const),
            pl.BlockSpec((D_MODEL, LANES), const),
            pl.BlockSpec((FOX_WIDTH, LANES), const),
            pl.BlockSpec((16, FOX_WIDTH), const),
            pl.BlockSpec((CONV_WIDTH, SSM_XBC), const),
            pl.BlockSpec((1, SSM_XBC), const),
        ],
        out_specs=[
            pl.BlockSpec((tm, SSM_WIDTH), row),
            pl.BlockSpec((tm, SSM_WIDTH), row),
            pl.BlockSpec((tm, SSM_XBC - SSM_WIDTH), row),
            pl.BlockSpec((tm, FOX_WIDTH), row),
            pl.BlockSpec((1, FOX_WIDTH, tm), lambda i: (i, 0, 0)),
            pl.BlockSpec((tm, FOX_WIDTH), row),
            pl.BlockSpec((tm, LANES), row),
            pl.BlockSpec((1, 8, LANES), lambda i: (i, 0, 0)),
            pl.BlockSpec((1, 16, LANES), lambda i: (i, 0, 0)),
        ],
        out_shape=[
            jax.ShapeDtypeStruct((s, SSM_WIDTH), F32),
            jax.ShapeDtypeStruct((s, SSM_WIDTH), F32),
            jax.ShapeDtypeStruct((s, SSM_XBC - SSM_WIDTH), BF16),
            jax.ShapeDtypeStruct((s, FOX_WIDTH), BF16),
            jax.ShapeDtypeStruct((nb, FOX_WIDTH, tm), BF16),
            jax.ShapeDtypeStruct((s, FOX_WIDTH), BF16),
            jax.ShapeDtypeStruct((s, LANES), F32),
            jax.ShapeDtypeStruct((nb, 8, LANES), F32),
            jax.ShapeDtypeStruct((nb, 16, LANES), F32),
        ],
        scratch_shapes=[pltpu.VMEM((8, SSM_XBC), F32)],
        compiler_params=pltpu.CompilerParams(
            dimension_semantics=("arbitrary",),
            vmem_limit_bytes=VMEM_LIMIT),
        name="in_proj",
    )(x, g, wz, wxbc, wq, wkt, wv, ws, ind, indt, cw, cb)


def _ssd_kernel(zs_ref, xs_ref, bc_ref, small_ref, bias_ref, apad_ref,
                dskip_ref, ng_ref, tri_ref, expand_ref, place_ref,
                y_ref, qaug_ref, cumt_ref, cmax_ref, cmin_ref,
                state_ref, carry_ref):
    t_rows = SSD_T
    i = pl.program_id(0)

    @pl.when(i == 0)
    def _():
        state_ref[...] = jnp.zeros_like(state_ref)
        carry_ref[...] = jnp.zeros_like(carry_ref)

    chunks = range(SSD_STEP // t_rows)
    rows = [slice(c * t_rows, (c + 1) * t_rows) for c in chunks]
    gw = SSM_WIDTH // SSM_GROUPS
    hpg = SSM_HEADS // SSM_GROUPS
    groups = range(SSM_GROUPS)
    lane = lax.broadcasted_iota(jnp.int32, (t_rows, LANES), 1)
    is_dt = lane < SSM_HEADS
    lo = lane < HEAD_DIM

    sp, v = [], []
    for c in chunks:
        t = small_ref[rows[c], :] + bias_ref[...]
        sp.append(_softplus(jnp.where(is_dt, t, -t)))
        v.append(jnp.where(is_dt, sp[c] * apad_ref[...], -sp[c]))
    local = [_dot_f32_rhs(tri_ref[...], v[c]) for c in chunks]
    carry = carry_ref[...]
    cs, cst = [], []
    for c in chunks:
        cs.append(local[c] + carry)
        carry = jnp.where(lane[0:1, :] < SSM_HEADS, 0.0, cs[c][t_rows - 1:t_rows, :])
        cst.append(cs[c].T)
        cumt_ref[:, rows[c]] = cst[c][0:16, :]
        cmax_ref[c] = jnp.broadcast_to(jnp.max(cs[c], axis=0, keepdims=True), (8, LANES))
        cmin_ref[c] = jnp.broadcast_to(jnp.min(cs[c], axis=0, keepdims=True), (8, LANES))
    carry_ref[...] = carry

    placed = [[jnp.dot(piece.astype(BF16), place_ref[...], preferred_element_type=F32)
               for piece in _split3(cs[c] * LOG2E)] for c in chunks]
    expanded = [_dot_f32_lhs(jnp.concatenate([sp[c], cs[c]], axis=0), expand_ref[...])
                for c in chunks]
    lane4 = lax.broadcasted_iota(jnp.int32, (t_rows, FOX_WIDTH), 1) & (HEAD_DIM - 1)
    xs, ea_e, cd_e, xdt_b, wst = [], [], [], [], []
    for c in chunks:
        qaug_ref[rows[c], :] = jnp.where(
            lane4 == 0, placed[c][0],
            jnp.where(lane4 == 1, placed[c][1],
                      jnp.where(lane4 == 2, placed[c][2],
                                jnp.where(lane4 < 6, 1.0, 0.0)))).astype(BF16)
        dt_e = expanded[c][0:t_rows]
        cs_e = expanded[c][t_rows:2 * t_rows]
        ea_e.append(jnp.exp(cs_e))
        de_e = jnp.exp(cs_e[t_rows - 1:t_rows, :] - cs_e)
        cd_e.append(ea_e[c][t_rows - 1:t_rows, :])
        xs.append(xs_ref[rows[c], :])
        xdt = xs[c] * dt_e
        xdt_b.append(xdt.astype(BF16))
        wst.append((xdt * de_e).astype(BF16))

    def cgrp(c, g):
        return bc_ref[rows[c], SSM_GROUPS * SSM_STATE + g * SSM_STATE:
                      SSM_GROUPS * SSM_STATE + (g + 1) * SSM_STATE]

    def bgrp(c, g):
        return bc_ref[rows[c], g * SSM_STATE:(g + 1) * SSM_STATE]

    gmat = [[lax.dot_general(cgrp(c, g), bgrp(c, g), (((1,), (1,)), ((), ())),
                             preferred_element_type=F32) for g in groups]
            for c in chunks]
    r_i = lax.broadcasted_iota(jnp.int32, (t_rows, t_rows), 0)
    c_i = lax.broadcasted_iota(jnp.int32, (t_rows, t_rows), 1)
    causal = r_i >= c_i
    y_parts = []
    for c in chunks:
        parts = []
        for g in groups:
            for pr in range(hpg // 2):
                c0 = g * gw + pr * LANES
                xpair = xdt_b[c][:, c0:c0 + LANES]
                acc = None
                for hh in range(2):
                    h = g * hpg + pr * 2 + hh
                    seg = cs[c][:, h:h + 1] - cst[c][h:h + 1, :]
                    dec = jnp.exp(jnp.where(causal, seg, NEG_INF))
                    m = (gmat[c][g] * dec).astype(BF16)
                    xm = jnp.where(lo if hh == 0 else jnp.logical_not(lo), xpair,
                                   jnp.zeros_like(xpair))
                    part = jnp.dot(m, xm, preferred_element_type=F32)
                    acc = part if acc is None else acc + part
                parts.append(acc)
        y_parts.append(parts)

    upd = [[jnp.dot(bgrp(c, g).astype(F32).T.astype(BF16), wst[c][:, g * gw:(g + 1) * gw],
                    preferred_element_type=F32) for g in groups]
           for c in chunks]

    state = [state_ref[g] for g in groups]
    for c in chunks:
        for g in groups:
            y_off = jnp.dot(cgrp(c, g), state[g].astype(BF16), preferred_element_type=F32)
            for pr in range(hpg // 2):
                cols = slice(g * gw + pr * LANES, g * gw + (pr + 1) * LANES)
                y_parts[c][g * (hpg // 2) + pr] += (
                    y_off[:, pr * LANES:(pr + 1) * LANES] * ea_e[c][:, cols])
            state[g] = state[g] * cd_e[c][:, g * gw:(g + 1) * gw] + upd[c][g]
    for g in groups:
        state_ref[g] = state[g]

    for c in chunks:
        y = jnp.concatenate(y_parts[c], axis=1) + dskip_ref[...] * xs[c]
        y = y * zs_ref[rows[c], :]
        y_ref[rows[c], :] = _rms(y, ng_ref[...]).astype(BF16)


def _ssd(zs, xs, bc, small, bias_pad, a_pad, dskip_e, ng, tri, expand):
    s = zs.shape[0]
    t = SSD_T
    st = SSD_STEP
    nsub = st // t
    const = lambda i: (0, 0)
    row = lambda i: (i, 0)
    place_np = np.zeros((LANES, FOX_WIDTH), np.float32)
    for hd in range(FOX_HEADS):
        pair, odd = divmod(hd, 2)
        c0 = pair * LANES + (0 if odd else HEAD_DIM)
        place_np[FOX_HEADS + hd, c0:c0 + HEAD_DIM] = 1.0
    place = jnp.asarray(place_np, dtype=BF16)
    return pl.pallas_call(
        _ssd_kernel,
        grid=(s // st,),
        in_specs=[
            pl.BlockSpec((st, SSM_WIDTH), row),
            pl.BlockSpec((st, SSM_WIDTH), row),
            pl.BlockSpec((st, SSM_XBC - SSM_WIDTH), row),
            pl.BlockSpec((st, LANES), row),
            pl.BlockSpec((1, LANES), const),
            pl.BlockSpec((1, LANES), const),
            pl.BlockSpec((1, SSM_WIDTH), const),
            pl.BlockSpec((1, SSM_WIDTH), const),
            pl.BlockSpec((t, t), const),
            pl.BlockSpec((LANES, SSM_WIDTH), const),
            pl.BlockSpec((LANES, FOX_WIDTH), const),
        ],
        out_specs=[
            pl.BlockSpec((st, SSM_WIDTH), row),
            pl.BlockSpec((st, FOX_WIDTH), row),
            pl.BlockSpec((16, st), lambda i: (0, i)),
            pl.BlockSpec((nsub, 8, LANES), lambda i: (i, 0, 0)),
            pl.BlockSpec((nsub, 8, LANES), lambda i: (i, 0, 0)),
        ],
        out_shape=[
            jax.ShapeDtypeStruct((s, SSM_WIDTH), BF16),
            jax.ShapeDtypeStruct((s, FOX_WIDTH), BF16),
            jax.ShapeDtypeStruct((16, s), F32),
            jax.ShapeDtypeStruct((s // t, 8, LANES), F32),
            jax.ShapeDtypeStruct((s // t, 8, LANES), F32),
        ],
        scratch_shapes=[
            pltpu.VMEM((SSM_GROUPS, SSM_STATE, SSM_WIDTH // SSM_GROUPS), F32),
            pltpu.VMEM((1, LANES), F32),
        ],
        compiler_params=pltpu.CompilerParams(
            dimension_semantics=("arbitrary",),
            vmem_limit_bytes=VMEM_LIMIT),
        name="ssd",
    )(zs, xs, bc, small, bias_pad, a_pad, dskip_e, ng, tri, expand, place)


def _fox_kernel(nkv_ref, q_ref, kt_ref, v_ref, qaug_ref, cumt_ref, o_ref,
                kaug_ref, m_ref, l_ref, acc_ref):
    tq, tk = ATT_TQ, ATT_TK
    nb = kt_ref.shape[0]
    nq = pl.num_programs(1)
    p = pl.program_id(0)
    i = pl.program_id(1)

    @pl.when(i == 0)
    def _():
        row16 = lax.broadcasted_iota(jnp.int32, (16, tk), 0)
        zeros48 = jnp.zeros((48, tk), BF16)

        def build(j, carry):
            off = pl.multiple_of(j * tk, tk)
            kt = kt_ref[j]
            for hh in range(2):
                ck = cumt_ref[pl.ds(FOX_HEADS + 2 * p + hh, 1), pl.ds(off, tk)] * LOG2E
                hi, mid, lo = _split3(ck)
                bias = jnp.where(row16 < 3, 1.0,
                                 jnp.where(row16 == 3, -hi,
                                           jnp.where(row16 == 4, -mid,
                                                     jnp.where(row16 == 5, -lo, 0.0))))
                bias = bias.astype(BF16)
                if hh == 0:
                    kaug_ref[0, j, 0:64, :] = kt[0:64]
                    kaug_ref[0, j, 64:80, :] = bias
                    kaug_ref[0, j, 80:128, :] = zeros48
                else:
                    kaug_ref[1, j, 0:16, :] = bias
                    kaug_ref[1, j, 16:64, :] = zeros48
                    kaug_ref[1, j, 64:128, :] = kt[64:128]
            return carry

        lax.fori_loop(0, nb, build, 0)

    q = q_ref[...]
    lane = lax.broadcasted_iota(jnp.int32, (tq, LANES), 1)
    lo_half = lane < HEAD_DIM
    r_i = lax.broadcasted_iota(jnp.int32, (tq, tk), 0)
    c_i = lax.broadcasted_iota(jnp.int32, (tq, tk), 1)
    causal = r_i >= c_i

    aug = qaug_ref[...]
    qas = [jnp.where(lo_half, q, aug), jnp.where(lo_half, aug, q)]

    def logits(hh, j):
        return jnp.dot(qas[hh], kaug_ref[hh, j], preferred_element_type=F32)

    def vblock(j):
        return v_ref[pl.ds(pl.multiple_of(j * tk, tk), tk), :]

    def lane_fold(x):
        out = x[:, 0:LANES]
        for c in range(1, x.shape[1] // LANES):
            out = out + x[:, c * LANES:(c + 1) * LANES]
        return out

    def online_diag(hh):
        s = jnp.where(causal, logits(hh, i), NEG_INF)
        m0 = jnp.max(s, axis=1, keepdims=True)
        p0 = jnp.exp2(s - m0)
        m_ref[hh] = jnp.broadcast_to(m0, (tq, LANES))
        l_ref[hh] = lane_fold(p0)
        acc_ref[hh] = jnp.dot(p0.astype(BF16), vblock(i), preferred_element_type=F32)

    def online_step(hh, j):
        s = logits(hh, j)
        m_prev = m_ref[hh]
        m_new = jnp.maximum(m_prev, jnp.max(s, axis=1, keepdims=True))
        alpha = jnp.exp2(m_prev - m_new)
        pj = jnp.exp2(s - m_new[:, 0:1])
        l_ref[hh] = alpha * l_ref[hh] + lane_fold(pj)
        acc_ref[hh] = alpha * acc_ref[hh] + jnp.dot(
            pj.astype(BF16), vblock(j), preferred_element_type=F32)
        m_ref[hh] = m_new

    half = tq // 2

    def fixed_diag(heads):
        vb = vblock(i)
        logit_pairs = [
            (jnp.dot(qas[hh][0:half], kaug_ref[hh, i, :, 0:half],
                     preferred_element_type=F32),
             jnp.dot(qas[hh][half:], kaug_ref[hh, i], preferred_element_type=F32))
            for hh in heads]
        for hh, (s_top, s_bot) in zip(heads, logit_pairs):
            p_top = jnp.exp2(jnp.where(causal[0:half, 0:half], s_top, NEG_INF))
            p_bot = jnp.exp2(jnp.where(causal[half:, :], s_bot, NEG_INF))
            l_ref[hh, 0:half] = lane_fold(p_top)
            l_ref[hh, half:] = lane_fold(p_bot)
            acc_ref[hh, 0:half] = jnp.dot(p_top.astype(BF16), vb[0:half],
                                          preferred_element_type=F32)
            acc_ref[hh, half:] = jnp.dot(p_bot.astype(BF16), vb, preferred_element_type=F32)

    def fixed_steps(jobs):
        all_logits = [jnp.concatenate([logits(hh, j - (n - 1) + b) for b in range(n)], axis=1)
                      for hh, j, n in jobs]
        for (hh, j, n), s in zip(jobs, all_logits):
            pj = jnp.exp2(s)
            l_ref[hh] += lane_fold(pj)
            vn = v_ref[pl.ds(pl.multiple_of((j - (n - 1)) * tk, tk), n * tk), :]
            acc_ref[hh] += jnp.dot(pj.astype(BF16), vn, preferred_element_type=F32)

    codes = [nkv_ref[(2 * p + hh) * nq + i] for hh in range(2)]
    n_offs = [code >> 1 for code in codes]
    both_fixed = (codes[0] & codes[1] & 1) == 1
    joint_quads = jnp.where(both_fixed, jnp.minimum(n_offs[0] >> 2, n_offs[1] >> 2), 0)
    joint_pair = both_fixed & ((n_offs[0] & n_offs[1] & 2) == 2)
    joint_single = both_fixed & ((n_offs[0] & n_offs[1] & 1) == 1)
    pair_at = [i - 1 - 4 * (n >> 2) for n in n_offs]
    single_at = [i - n for n in n_offs]

    @pl.when(both_fixed)
    def _():
        fixed_diag((0, 1))

        def body(u, carry):
            fixed_steps([(0, i - 1 - 4 * u, 4), (1, i - 1 - 4 * u, 4)])
            return carry

        lax.fori_loop(0, joint_quads, body, 0)

        @pl.when(joint_pair)
        def _():
            fixed_steps([(0, pair_at[0], 2), (1, pair_at[1], 2)])

        @pl.when(joint_single)
        def _():
            fixed_steps([(0, single_at[0], 1), (1, single_at[1], 1)])

    for hh in range(2):
        n_off = n_offs[hh]
        fixed_ok = (codes[hh] & 1) == 1

        @pl.when(fixed_ok)
        def _(hh=hh, n_off=n_off):
            @pl.when(jnp.logical_not(both_fixed))
            def _():
                fixed_diag((hh,))

            def body(u, carry):
                fixed_steps([(hh, i - 1 - 4 * u, 4)])
                return carry

            lax.fori_loop(joint_quads, n_off >> 2, body, 0)

            @pl.when(((n_off & 2) == 2) & jnp.logical_not(joint_pair))
            def _():
                fixed_steps([(hh, pair_at[hh], 2)])

            @pl.when(((n_off & 1) == 1) & jnp.logical_not(joint_single))
            def _():
                fixed_steps([(hh, single_at[hh], 1)])

        @pl.when(jnp.logical_not(fixed_ok))
        def _(hh=hh, n_off=n_off):
            online_diag(hh)

            def body(jj, carry):
                online_step(hh, i - 1 - jj)
                return carry

            lax.fori_loop(0, n_off, body, 0)

    l0 = jnp.sum(l_ref[0], axis=1, keepdims=True)
    l1 = jnp.sum(l_ref[1], axis=1, keepdims=True)
    o_ref[...] = jnp.where(lo_half, acc_ref[0] / l0, acc_ref[1] / l1)


def _kv_counts(qn_tiles, kn_tiles, cmax_chunks, cmin_chunks):
    nq = qn_tiles.shape[0]
    per_q = cmax_chunks.shape[0] // nq
    qn = jnp.sqrt(qn_tiles[:, 0, :FOX_HEADS])
    kn = jnp.sqrt(jnp.max(kn_tiles[:, :FOX_HEADS, 0], axis=0))
    gate = slice(FOX_HEADS, 2 * FOX_HEADS)
    cmax = jnp.max(cmax_chunks[:, 0, gate].reshape(nq, per_q, FOX_HEADS), axis=1) * LOG2E
    cmin = jnp.min(cmin_chunks[:, 0, gate].reshape(nq, per_q, FOX_HEADS), axis=1) * LOG2E
    pmin = lax.cummin(cmin, axis=0)
    qk = NORM_MARGIN * qn * kn[None, :] + 1.0
    bound = (2.0 * qk + cmax)[:, None, :] - pmin[None, :, :]
    ii = lax.broadcasted_iota(jnp.int32, bound.shape, 0)
    jj = lax.broadcasted_iota(jnp.int32, bound.shape, 1)
    skip = (bound < SKIP_LOG2) & (jj < ii)
    jstar = jnp.max(jnp.where(skip, jj, -1), axis=1)
    n_off = jnp.arange(nq, dtype=jnp.int32)[:, None] - 1 - jstar
    pmin_prev = jnp.concatenate([jnp.full((1, FOX_HEADS), jnp.inf, F32), pmin[:-1]], axis=0)
    fixed_ok = (qk + jnp.maximum(cmax - pmin_prev, 0.0)) < FIXED_MAX_LOG2
    code = 2 * n_off + fixed_ok.astype(jnp.int32)
    return code.T.reshape(-1).astype(jnp.int32)


def _fox(nkv, q, kt3, v, qaug, cumt):
    s = q.shape[0]
    tq, tk = ATT_TQ, ATT_TK
    nb = s // tk
    grid_spec = pltpu.PrefetchScalarGridSpec(
        num_scalar_prefetch=1,
        grid=(FOX_HEADS // 2, s // tq),
        in_specs=[
            pl.BlockSpec((tq, LANES), lambda p, i, n: (i, p)),
            pl.BlockSpec((nb, LANES, tk), lambda p, i, n: (0, p, 0)),
            pl.BlockSpec((s, LANES), lambda p, i, n: (0, p)),
            pl.BlockSpec((tq, LANES), lambda p, i, n: (i, p)),
            pl.BlockSpec((16, s), lambda p, i, n: (0, 0)),
        ],
        out_specs=pl.BlockSpec((tq, LANES), lambda p, i, n: (i, p)),
        scratch_shapes=[
            pltpu.VMEM((2, nb, LANES, tk), BF16),
            pltpu.VMEM((2, tq, LANES), F32),
            pltpu.VMEM((2, tq, LANES), F32),
            pltpu.VMEM((2, tq, LANES), F32),
        ],
    )
    return pl.pallas_call(
        _fox_kernel,
        grid_spec=grid_spec,
        out_shape=jax.ShapeDtypeStruct((s, FOX_WIDTH), F32),
        compiler_params=pltpu.CompilerParams(
            dimension_semantics=("arbitrary", "arbitrary"),
            vmem_limit_bytes=VMEM_LIMIT),
        name="fox_attn",
    )(nkv, q, kt3, v, qaug, cumt)


def _pad_lanes(vec_dt, vec_f):
    out = jnp.zeros((1, LANES), F32)
    out = out.at[0, 0:SSM_HEADS].set(vec_dt.astype(F32))
    if vec_f is not None:
        out = out.at[0, SSM_HEADS:SSM_HEADS + FOX_HEADS].set(vec_f.astype(F32))
    return out


def _layer(x, ffn1_norm, ffn1_w_in, ffn1_w_out, mix_norm, w_in, conv_w, conv_b, dt_bias,
           a_log, d_skip, ssm_norm, f_bias, fox_norm, w_out, ffn2_norm, ffn2_w_in,
           ffn2_w_out, final_g, *, final):
    ones = jnp.ones((1, D_MODEL), F32)
    row = lambda a: a.reshape(1, -1).astype(F32)

    o0 = SSM_WIDTH
    o1 = o0 + SSM_XBC
    o2 = o1 + SSM_HEADS
    o3 = o2 + 3 * FOX_WIDTH
    proj_cols = ((0, o0), (o0, o1), (o2, o2 + FOX_WIDTH), (o2 + FOX_WIDTH, o2 + 2 * FOX_WIDTH),
                 (o2 + 2 * FOX_WIDTH, o3))
    x1, ffn2_wi, ffn2_wo, wz, wxbc, wq, wkt, wv, wo = _ffn(
        x, row(ffn1_norm), ffn1_w_in.astype(BF16), ffn1_w_out.astype(BF16), ones,
        final=False,
        casts=((ffn2_w_in, ((0, 2 * D_FF),)), (ffn2_w_out, ((0, D_MODEL),)),
               (w_in, proj_cols), (w_out, ((0, D_MODEL),))))
    ws = jnp.concatenate(
        [w_in[:, o1:o2], w_in[:, o3:],
         jnp.zeros((D_MODEL, LANES - SSM_HEADS - FOX_HEADS), w_in.dtype)], axis=1)
    zs, xs, bc, q, kt3, v, small, qn2, kn2 = _inproj(
        x1, row(mix_norm), wz, wxbc, wq, wkt, wv, ws, conv_w.astype(F32), row(conv_b))

    bias_pad = _pad_lanes(dt_bias, f_bias)
    a_pad = _pad_lanes(-jnp.exp(a_log.astype(F32)), None)
    dskip_e = jnp.repeat(d_skip.astype(F32), HEAD_DIM).reshape(1, SSM_WIDTH)
    tri = jnp.asarray(np.tril(np.ones((SSD_T, SSD_T), np.float32)), dtype=BF16)
    expand_np = np.zeros((LANES, SSM_WIDTH), np.float32)
    for hd in range(SSM_HEADS):
        expand_np[hd, hd * HEAD_DIM:(hd + 1) * HEAD_DIM] = 1.0
    y_ssd, qaug, cumt, cmax_c, cmin_c = _ssd(zs, xs, bc, small, bias_pad, a_pad,
                            dskip_e, row(ssm_norm), tri, jnp.asarray(expand_np, dtype=BF16))

    o_fox = _fox(_kv_counts(qn2, kn2, cmax_c, cmin_c), q, kt3, v, qaug, cumt)

    mix = (y_ssd, o_fox, row(fox_norm), wo[:SSM_WIDTH], wo[SSM_WIDTH:])
    out, = _ffn(x1, row(ffn2_norm), ffn2_wi, ffn2_wo, row(final_g), final=final, mix=mix)
    return out


def kernel(x, ffn1_norm, ffn1_w_in, ffn1_w_out, mix_norm, w_in, conv_w, conv_b, dt_bias, a_log,
           d_skip, ssm_norm, f_bias, fox_norm, w_out, ffn2_norm, ffn2_w_in, ffn2_w_out, final_norm):
    b, s, d = x.shape
    depth = ffn1_norm.shape[0]
    outs = []
    for bi in range(b):
        xb = x[bi]
        for l in range(depth):
            xb = _layer(xb, ffn1_norm[l], ffn1_w_in[l], ffn1_w_out[l], mix_norm[l], w_in[l],
                        conv_w[l], conv_b[l], dt_bias[l], a_log[l], d_skip[l], ssm_norm[l],
                        f_bias[l], fox_norm[l], w_out[l], ffn2_norm[l], ffn2_w_in[l],
                        ffn2_w_out[l], final_norm, final=(l == depth - 1))
        outs.append(xb)
    return jnp.stack(outs, axis=0)
```
